```python
import math
import jax
import jax.numpy as jnp
from jax import lax
import numpy as np

D_MODEL = 1024
BATCH = 4
SEQ = 4096
DEPTH = 4
DEC_BATCH = 32
DEC_SEQ = 4
PAST_LEN = 8192
PAGE_SIZE = 128

N_MIXERS = 4
N_A = (DEPTH + 3) // 4
N_B = (DEPTH + 2) // 4
N_C = (DEPTH + 1) // 4
N_D = DEPTH // 4
D_A = D_MODEL
W_A = 31
H_B = 16
DH_B = 64
KV_B = 4
G_B = H_B // KV_B
H_IDX = 8
D_IDX = 64
TOPK = 256
Q_BLOCK = 128
N_BUCKETS = 32
MAX_DIST = 128
B_SPLITS = (H_B * DH_B, H_B * DH_B + KV_B * DH_B, H_B * DH_B + 2 * KV_B * DH_B,
            H_B * DH_B + 2 * KV_B * DH_B + H_IDX * D_IDX,
            H_B * DH_B + 2 * KV_B * DH_B + H_IDX * D_IDX + D_IDX)
B_IN = B_SPLITS[-1] + H_IDX
D_IN_C = 2 * D_MODEL
P_C = 64
H_C = D_IN_C // P_C
G_C = 4
N_SSM = 128
W_C = 4
CONV_C = D_IN_C + 2 * G_C * N_SSM
C_IN = D_IN_C + CONV_C + H_C
CHUNK_C = 128
W_D = 3
N_MEM = 256
H_M = 4
DH_M = D_MODEL // H_M
D_FF = 2816
W_F = 3
EPS = 1e-6

kernel_name = 'hybrid_conformer_dsa_ssd_shortconv_step'


def rms_norm(x, g):
    xf = x.astype(jnp.float32)
    y = xf * lax.rsqrt(jnp.mean(xf * xf, axis=-1, keepdims=True) + EPS)
    return (y * g.astype(jnp.float32)).astype(x.dtype)


def layer_norm(x, g, b):
    xf = x.astype(jnp.float32)
    mu = jnp.mean(xf, axis=-1, keepdims=True)
    xc = xf - mu
    y = xc * lax.rsqrt(jnp.mean(xc * xc, axis=-1, keepdims=True) + EPS)
    return (y * g.astype(jnp.float32) + b.astype(jnp.float32)).astype(x.dtype)


def causal_dwconv(buf, x, w, b=None):
    width = w.shape[0]
    ext = jnp.concatenate([buf.astype(x.dtype), x], axis=1)
    y = lax.conv_general_dilated(ext, w[:, None, :].astype(x.dtype), window_strides=(1,), padding='VALID',
                                 dimension_numbers=('NWC', 'WIO', 'NWC'), feature_group_count=x.shape[-1])
    if b is not None:
        y = y + b
    return y, ext[:, ext.shape[1] - (width - 1):]


def t5_bias(table, dist):
    max_exact = N_BUCKETS // 2
    df = jnp.maximum(dist, 1).astype(jnp.float32)
    large = max_exact + (jnp.log(df / max_exact) / math.log(MAX_DIST / max_exact)
                         * (N_BUCKETS - max_exact)).astype(jnp.int32)
    large = jnp.minimum(large, N_BUCKETS - 1)
    bucket = jnp.where(dist < max_exact, dist, large)
    return table[bucket].astype(jnp.float32)


def mixer_a(h, buf, w_in, b_in, w_conv, b_conv, ln_g, ln_b, w_out):
    a, g = jnp.split(h @ w_in + b_in, 2, axis=-1)
    u = a * jax.nn.sigmoid(g)
    u, new_buf = causal_dwconv(buf, u, w_conv, b_conv)
    u = jax.nn.silu(layer_norm(u, ln_g, ln_b))
    return u @ w_out, new_buf


def b_project(h, w_in, q_g, k_g, ki_g):
    n, l = h.shape[0], h.shape[1]
    q, k, v, qi, ki, wi = jnp.split(h @ w_in, B_SPLITS, axis=-1)
    q = rms_norm(q.reshape(n, l, H_B, DH_B), q_g)
    k = rms_norm(k.reshape(n, l, KV_B, DH_B), k_g)
    v = v.reshape(n, l, KV_B, DH_B)
    qi = qi.reshape(n, l, H_IDX, D_IDX)
    ki = rms_norm(ki, ki_g)
    return q, k, v, qi, ki, wi


def indexer_scores(qi, wi, ki, pos_q, pos_k):
    s = jnp.einsum('bthd,bsd->bths', qi.astype(jnp.float32), ki.astype(jnp.float32))
    wsc = wi.astype(jnp.float32) * (H_IDX ** -0.5) * (D_IDX ** -0.5)
    sc = jnp.einsum('bths,bth->bts', jax.nn.relu(s), wsc)
    return jnp.where(pos_k[None, None, :] <= pos_q[None, :, None], sc, -jnp.inf)


def attend_selected(q, k_sel, v_sel, idx, pos_q, bias_table):
    n, t = q.shape[0], q.shape[1]
    qg = q.reshape(n, t, KV_B, G_B, DH_B)
    logits = jnp.einsum('btngd,btknd->btngk', qg.astype(jnp.float32), k_sel.astype(jnp.float32)) * (DH_B ** -0.5)
    dist = pos_q[None, :, None] - idx
    bias = t5_bias(bias_table, jnp.maximum(dist, 0))
    bias = bias.reshape(n, t, -1, KV_B, G_B).transpose(0, 1, 3, 4, 2)
    logits = jnp.where((dist >= 0)[:, :, None, None, :], logits + bias, -jnp.inf)
    p = jax.nn.softmax(logits, axis=-1).astype(v_sel.dtype)
    o = jnp.einsum('btngk,btknd->btngd', p, v_sel)
    return o.reshape(n, t, H_B * DH_B)


def mixer_b_prompt(h, w_in, w_out, q_g, k_g, ki_g, bias_table):
    n, l = h.shape[0], h.shape[1]
    q, k, v, qi, ki, wi = b_project(h, w_in, q_g, k_g, ki_g)
    topk = min(TOPK, l // 4)
    pos_k = jnp.arange(l)
    bidx = jnp.arange(n)[:, None, None]

    def block(i):
        s0 = i * Q_BLOCK
        pos_q = s0 + jnp.arange(Q_BLOCK)
        qb = lax.dynamic_slice_in_dim(q, s0, Q_BLOCK, axis=1)
        qib = lax.dynamic_slice_in_dim(qi, s0, Q_BLOCK, axis=1)
        wib = lax.dynamic_slice_in_dim(wi, s0, Q_BLOCK, axis=1)
        _, idx = lax.top_k(indexer_scores(qib, wib, ki, pos_q, pos_k), topk)
        return attend_selected(qb, k[bidx, idx], v[bidx, idx], idx, pos_q, bias_table)

    o = lax.map(block, jnp.arange(l // Q_BLOCK))
    o = o.transpose(1, 0, 2, 3).reshape(n, l, H_B * DH_B)
    return o @ w_out, k, v, ki


def mixer_b_sample(h, k_pool, v_pool, ki_pool, page_table, w_in, w_out, q_g, k_g, ki_g, bias_table):
    n, t = h.shape[0], h.shape[1]
    q, k, v, qi, ki, wi = b_project(h, w_in, q_g, k_g, ki_g)
    l = PAST_LEN + t
    topk = min(TOPK, l // 4)
    ki_past = ki_pool[page_table].reshape(n, PAST_LEN, D_IDX)
    ki_all = jnp.concatenate([ki_past.astype(ki.dtype), ki], axis=1)
    pos_q = PAST_LEN + jnp.arange(t)
    _, idx = lax.top_k(indexer_scores(qi, wi, ki_all, pos_q, jnp.arange(l)), topk)
    bidx = jnp.arange(n)[:, None, None]
    in_cache = (idx < PAST_LEN)[..., None, None]
    ic = jnp.minimum(idx, PAST_LEN - 1)
    phys = page_table[bidx, ic // PAGE_SIZE]
    off = ic % PAGE_SIZE
    inew = jnp.clip(idx - PAST_LEN, 0, t - 1)
    k_sel = jnp.where(in_cache, k_pool[phys, off].astype(k.dtype), k[bidx, inew])
    v_sel = jnp.where(in_cache, v_pool[phys, off].astype(v.dtype), v[bidx, inew])
    o = attend_selected(q, k_sel, v_sel, idx, pos_q, bias_table)
    return o @ w_out, k, v, ki


def segsum(x):
    tlen = x.shape[-1]
    cs = jnp.cumsum(x, axis=-1)
    d = cs[..., :, None] - cs[..., None, :]
    return jnp.where(jnp.tril(jnp.ones((tlen, tlen), dtype=bool)), d, -jnp.inf)


def ssd(x, da, bm, cm, h0, q):
    n, l, nh, p = x.shape
    g, ns = bm.shape[2], bm.shape[3]
    j = nh // g
    c = l // q
    x = x.reshape(n, c, q, g, j, p)
    da = da.reshape(n, c, q, g, j).transpose(0, 3, 4, 1, 2)
    bm = bm.reshape(n, c, q, g, ns)
    cm = cm.reshape(n, c, q, g, ns)
    a_cs = jnp.cumsum(da, axis=-1)
    lmat = jnp.exp(segsum(da))
    cb = jnp.einsum('bclgn,bcsgn->bcgls', cm, bm)
    y_diag = jnp.einsum('bcgls,bgjcls,bcsgjp->bclgjp', cb, lmat, x)
    decay_states = jnp.exp(a_cs[..., -1:] - a_cs)
    states = jnp.einsum('bcsgn,bgjcs,bcsgjp->bcgjpn', bm, decay_states, x)
    states = jnp.concatenate([h0.reshape(n, g, j, p, ns)[:, None], states], axis=1)
    chunk_last = jnp.pad(a_cs[..., -1], ((0, 0), (0, 0), (0, 0), (1, 0)))
    decay_chunk = jnp.exp(segsum(chunk_last))
    new_states = jnp.einsum('bgjzc,bcgjpn->bzgjpn', decay_chunk, states)
    y_off = jnp.einsum('bclgn,bcgjpn,bgjcl->bclgjp', cm, new_states[:, :-1], jnp.exp(a_cs))
    y = (y_diag + y_off).reshape(n, l, nh, p)
    return y, new_states[:, -1].reshape(n, nh, p, ns)


def mixer_c(h, conv_buf, ssm0, w_in, w_conv, b_conv, dt_bias, a_log, d_skip, norm_g, w_out):
    n, l = h.shape[0], h.shape[1]
    chunk = CHUNK_C if l % CHUNK_C == 0 else l
    z, xbc, dt = jnp.split(h @ w_in, [D_IN_C, D_IN_C + CONV_C], axis=-1)
    xbc, new_conv = causal_dwconv(conv_buf, xbc, w_conv, b_conv)
    xbc = jax.nn.silu(xbc)
    xs, bm, cm = jnp.split(xbc, [D_IN_C, D_IN_C + G_C * N_SSM], axis=-1)
    xs = xs.reshape(n, l, H_C, P_C).astype(jnp.float32)
    bm = bm.reshape(n, l, G_C, N_SSM).astype(jnp.float32)
    cm = cm.reshape(n, l, G_C, N_SSM).astype(jnp.float32)
    dt = jax.nn.softplus(dt.astype(jnp.float32) + dt_bias.astype(jnp.float32))
    a = -jnp.exp(a_log.astype(jnp.float32))
    y, ssm_new = ssd(xs * dt[..., None], dt * a, bm, cm, ssm0.astype(jnp.float32), chunk)
    y = y + xs * d_skip.astype(jnp.float32)[:, None]
    y = y.reshape(n, l, G_C, D_IN_C // G_C) * jax.nn.silu(z.astype(jnp.float32)).reshape(n, l, G_C, D_IN_C // G_C)
    y = rms_norm(y, norm_g.reshape(G_C, D_IN_C // G_C)).reshape(n, l, D_IN_C).astype(h.dtype)
    return y @ w_out, new_conv, ssm_new.astype(ssm0.dtype)


def mixer_d(h, buf, w_in, w_conv, w_out):
    bg, cg, hh = jnp.split(h @ w_in, 3, axis=-1)
    y, new_buf = causal_dwconv(buf, cg * hh, w_conv)
    return (bg * y) @ w_out, new_buf


def mem_kv(mem, mem_g, w_kv, k_g):
    n = mem.shape[0]
    k, v = jnp.split(rms_norm(mem, mem_g) @ w_kv, 2, axis=-1)
    k = rms_norm(k.reshape(n, N_MEM, H_M, DH_M), k_g)
    return k, v.reshape(n, N_MEM, H_M, DH_M)


def mem_attend(h, k, v, w_q, q_g, w_o):
    n, l = h.shape[0], h.shape[1]
    q = rms_norm((h @ w_q).reshape(n, l, H_M, DH_M), q_g)
    logits = jnp.einsum('blhd,bmhd->bhlm', q.astype(jnp.float32), k.astype(jnp.float32)) * (DH_M ** -0.5)
    p = jax.nn.softmax(logits, axis=-1).astype(v.dtype)
    o = jnp.einsum('bhlm,bmhd->blhd', p, v).reshape(n, l, H_M * DH_M)
    return o @ w_o


def conv_ffn(h, buf, w_in, w_conv, b_conv, w_out):
    a, g = jnp.split(h @ w_in, 2, axis=-1)
    a, new_buf = causal_dwconv(buf, a, w_conv, b_conv)
    return (jax.nn.silu(a) * g) @ w_out, new_buf


def setup_inputs(seed: int = 0) -> dict:
    key = jax.random.key(seed)
    ks = iter(jax.random.split(key, 80))

    def nrm(shape, scale):
        return jax.random.normal(next(ks), shape, jnp.float32) * scale

    def gain(shape):
        return 1.0 + nrm(shape, 0.02)

    n_pages = PAST_LEN // PAGE_SIZE
    n_used = DEC_BATCH * n_pages
    n_pool = n_used + n_used // 4
    page_table = jax.random.permutation(next(ks), n_pool)[:n_used].reshape(DEC_BATCH, n_pages).astype(jnp.int32)
    dt0 = jnp.exp(jax.random.uniform(next(ks), (N_C, H_C), jnp.float32, math.log(1e-3), math.log(1e-1)))
    c_dt_bias = dt0 + jnp.log(-jnp.expm1(-dt0))
    c_a_log = jnp.log(jax.random.uniform(next(ks), (N_C, H_C), jnp.float32, 1.0, 16.0))
    d = D_MODEL
    return {
        'x_prompt': nrm((BATCH, SEQ, d), 1.0),
        'x_sample': nrm((DEC_BATCH, DEC_SEQ, d), 1.0),
        'mem_prompt': nrm((BATCH, N_MEM, d), 1.0),
        'state_a_conv': nrm((N_A, DEC_BATCH, W_A - 1, D_A), 0.5),
        'cache_b_k': nrm((N_B, n_pool, PAGE_SIZE, KV_B, DH_B), 1.0),
        'cache_b_v': nrm((N_B, n_pool, PAGE_SIZE, KV_B, DH_B), 1.0),
        'cache_b_kidx': nrm((N_B, n_pool, PAGE_SIZE, D_IDX), 1.0),
        'state_c_conv': nrm((N_C, DEC_BATCH, W_C - 1, CONV_C), 1.0),
        'state_c_ssm': nrm((N_C, DEC_BATCH, H_C, P_C, N_SSM), 0.1),
        'state_d_conv': nrm((N_D, DEC_BATCH, W_D - 1, d), 1.0),
        'state_ffn_conv': nrm((DEPTH, DEC_BATCH, W_F - 1, D_FF), 1.0),
        'cache_mem_k': nrm((DEPTH, DEC_BATCH, N_MEM, H_M, DH_M), 1.0),
        'cache_mem_v': nrm((DEPTH, DEC_BATCH, N_MEM, H_M, DH_M), 1.0),
        'page_table': page_table,
        'rel_bias': nrm((N_BUCKETS, H_B), 0.1),
        'norm_mix': gain((DEPTH, d)),
        'norm_mem': gain((DEPTH, d)),
        'norm_ffn': gain((DEPTH, d)),
        'norm_memtok': gain((DEPTH, d)),
        'a_w_in': nrm((N_A, d, 2 * D_A), d ** -0.5),
        'a_b_in': nrm((N_A, 2 * D_A), 0.02),
        'a_w_conv': nrm((N_A, W_A, D_A), W_A ** -0.5),
        'a_b_conv': nrm((N_A, D_A), 0.02),
        'a_ln_g': gain((N_A, D_A)),
        'a_ln_b': nrm((N_A, D_A), 0.02),
        'a_w_out': nrm((N_A, D_A, d), D_A ** -0.5),
        'b_w_in': nrm((N_B, d, B_IN), d ** -0.5),
        'b_w_out': nrm((N_B, H_B * DH_B, d), (H_B * DH_B) ** -0.5),
        'b_q_norm': gain((N_B, DH_B)),
        'b_k_norm': gain((N_B, DH_B)),
        'b_kidx_norm': gain((N_B, D_IDX)),
        'c_w_in': nrm((N_C, d, C_IN), d ** -0.5),
        'c_w_conv': nrm((N_C, W_C, CONV_C), W_C ** -0.5),
        'c_b_conv': nrm((N_C, CONV_C), 0.02),
        'c_dt_bias': c_dt_bias,
        'c_a_log': c_a_log,
        'c_d_skip': 1.0 + nrm((N_C, H_C), 0.1),
        'c_norm': gain((N_C, D_IN_C)),
        'c_w_out': nrm((N_C, D_IN_C, d), D_IN_C ** -0.5),
        'd_w_in': nrm((N_D, d, 3 * d), d ** -0.5),
        'd_w_conv': nrm((N_D, W_D, d), W_D ** -0.5),
        'd_w_out': nrm((N_D, d, d), d ** -0.5),
        'm_w_q': nrm((DEPTH, d, H_M * DH_M), d ** -0.5),
        'm_w_kv': nrm((DEPTH, d, 2 * H_M * DH_M), d ** -0.5),
        'm_w_o': nrm((DEPTH, H_M * DH_M, d), (H_M * DH_M) ** -0.5),
        'm_q_norm': gain((DEPTH, DH_M)),
        'm_k_norm': gain((DEPTH, DH_M)),
        'f_w_in': nrm((DEPTH, d, 2 * D_FF), d ** -0.5),
        'f_w_conv': nrm((DEPTH, W_F, D_FF), W_F ** -0.5),
        'f_b_conv': nrm((DEPTH, D_FF), 0.02),
        'f_w_out': nrm((DEPTH, D_FF, d), D_FF ** -0.5),
    }


def reference(x_prompt, x_sample, mem_prompt, state_a_conv, cache_b_k, cache_b_v, cache_b_kidx,
              state_c_conv, state_c_ssm, state_d_conv, state_ffn_conv, cache_mem_k, cache_mem_v,
              page_table, rel_bias, norm_mix, norm_mem, norm_ffn, norm_memtok,
              a_w_in, a_b_in, a_w_conv, a_b_conv, a_ln_g, a_ln_b, a_w_out,
              b_w_in, b_w_out, b_q_norm, b_k_norm, b_kidx_norm,
              c_w_in, c_w_conv, c_b_conv, c_dt_bias, c_a_log, c_d_skip, c_norm, c_w_out,
              d_w_in, d_w_conv, d_w_out,
              m_w_q, m_w_kv, m_w_o, m_q_norm, m_k_norm,
              f_w_in, f_w_conv, f_b_conv, f_w_out):
    yp, ys = x_prompt, x_sample
    bp = yp.shape[0]
    a_cp, a_cs = [], []
    b_kp, b_vp, b_kip, b_ks, b_vs, b_kis = [], [], [], [], [], []
    c_cp, c_sp, c_cs, c_ss = [], [], [], []
    d_cp, d_cs = [], []
    f_cp, f_cs = [], []
    m_kp, m_vp = [], []
    for i in range(DEPTH):
        j = i // N_MIXERS
        kind = i % N_MIXERS
        hp = rms_norm(yp, norm_mix[i])
        hs = rms_norm(ys, norm_mix[i])
        if kind == 0:
            wa = (a_w_in[j], a_b_in[j], a_w_conv[j], a_b_conv[j], a_ln_g[j], a_ln_b[j], a_w_out[j])
            op, sp = mixer_a(hp, jnp.zeros((bp, W_A - 1, D_A), yp.dtype), *wa)
            os_, ss = mixer_a(hs, state_a_conv[j], *wa)
            a_cp.append(sp)
            a_cs.append(ss)
        elif kind == 1:
            wb = (b_w_in[j], b_w_out[j], b_q_norm[j], b_k_norm[j], b_kidx_norm[j], rel_bias)
            op, kp, vp, kip = mixer_b_prompt(hp, *wb)
            os_, ks_, vs_, kis = mixer_b_sample(hs, cache_b_k[j], cache_b_v[j], cache_b_kidx[j], page_table, *wb)
            b_kp.append(kp)
            b_vp.append(vp)
            b_kip.append(kip)
            b_ks.append(ks_)
            b_vs.append(vs_)
            b_kis.append(kis)
        elif kind == 2:
            wc = (c_w_in[j], c_w_conv[j], c_b_conv[j], c_dt_bias[j], c_a_log[j], c_d_skip[j], c_norm[j], c_w_out[j])
            op, cvp, ssp = mixer_c(hp, jnp.zeros((bp, W_C - 1, CONV_C), yp.dtype),
                                   jnp.zeros((bp, H_C, P_C, N_SSM), yp.dtype), *wc)
            os_, cvs, sss = mixer_c(hs, state_c_conv[j], state_c_ssm[j], *wc)
            c_cp.append(cvp)
            c_sp.append(ssp)
            c_cs.append(cvs)
            c_ss.append(sss)
        else:
            wd = (d_w_in[j], d_w_conv[j], d_w_out[j])
            op, sp = mixer_d(hp, jnp.zeros((bp, W_D - 1, D_MODEL), yp.dtype), *wd)
            os_, ss = mixer_d(hs, state_d_conv[j], *wd)
            d_cp.append(sp)
            d_cs.append(ss)
        yp = yp + op
        ys = ys + os_
        mk, mv = mem_kv(mem_prompt, norm_memtok[i], m_w_kv[i], m_k_norm[i])
        m_kp.append(mk)
        m_vp.append(mv)
        yp = yp + mem_attend(rms_norm(yp, norm_mem[i]), mk, mv, m_w_q[i], m_q_norm[i], m_w_o[i])
        ys = ys + mem_attend(rms_norm(ys, norm_mem[i]), cache_mem_k[i], cache_mem_v[i], m_w_q[i], m_q_norm[i], m_w_o[i])
        wf = (f_w_in[i], f_w_conv[i], f_b_conv[i], f_w_out[i])
        fp, fsp = conv_ffn(rms_norm(yp, norm_ffn[i]), jnp.zeros((bp, W_F - 1, D_FF), yp.dtype), *wf)
        fs, fss = conv_ffn(rms_norm(ys, norm_ffn[i]), state_ffn_conv[i], *wf)
        f_cp.append(fsp)
        f_cs.append(fss)
        yp = yp + fp
        ys = ys + fs
    a_conv_p = jnp.stack(a_cp)
    a_conv_s = jnp.stack(a_cs)
    b_k_p = jnp.stack(b_kp)
    b_v_p = jnp.stack(b_vp)
    b_kidx_p = jnp.stack(b_kip)
    b_k_s = jnp.stack(b_ks)
    b_v_s = jnp.stack(b_vs)
    b_kidx_s = jnp.stack(b_kis)
    c_conv_p = jnp.stack(c_cp)
    c_ssm_p = jnp.stack(c_sp)
    c_conv_s = jnp.stack(c_cs)
    c_ssm_s = jnp.stack(c_ss)
    d_conv_p = jnp.stack(d_cp)
    d_conv_s = jnp.stack(d_cs)
    f_conv_p = jnp.stack(f_cp)
    f_conv_s = jnp.stack(f_cs)
    mem_k_p = jnp.stack(m_kp)
    mem_v_p = jnp.stack(m_vp)
    return (yp, ys, a_conv_p, a_conv_s, b_k_p, b_v_p, b_kidx_p, b_k_s, b_v_s, b_kidx_s,
            c_conv_p, c_ssm_p, c_conv_s, c_ssm_s, d_conv_p, d_conv_s, f_conv_p, f_conv_s, mem_k_p, mem_v_p)
```

```python
import functools
import math

import jax
import jax.numpy as jnp
from jax import lax
from jax.experimental import pallas as pl
from jax.experimental.pallas import tpu as pltpu

F32 = jnp.float32
BF16 = jnp.bfloat16
I32 = jnp.int32

D_MODEL = 1024
DEPTH = 4
PAST_LEN = 8192
PAGE_SIZE = 128
W_A = 31
H_B = 16
DH_B = 64
KV_B = 4
G_B = H_B // KV_B
H_IDX = 8
D_IDX = 64
TOPK = 256
N_BUCKETS = 32
MAX_DIST = 128
D_IN_C = 2 * D_MODEL
P_C = 64
H_C = D_IN_C // P_C
G_C = 4
N_SSM = 128
W_C = 4
CHUNK_C = 128
W_D = 3
N_MEM = 256
H_M = 4
DH_M = D_MODEL // H_M
D_FF = 2816
W_F = 3
EPS = 1e-6

V7X_SUBLANES = 8
V7X_LANES = 128
V7X_VMEM_LIMIT_BYTES = 56 * 1024 * 1024

SAMPLE_PAD = V7X_SUBLANES
NEG_BIG = -1e30
INT_MIN = -(2 ** 31)
PAGES_PER_STEP = 8


def _cparams(*sem):
    return pltpu.CompilerParams(dimension_semantics=sem, vmem_limit_bytes=V7X_VMEM_LIMIT_BYTES)


def _pick(n, cands):
    for c in cands:
        if n % c == 0:
            return c
    return n


def _rms(x, g):
    y = x * lax.rsqrt(jnp.mean(x * x, axis=-1, keepdims=True) + EPS)
    return y * g


def _dot(a, b):
    return jnp.dot(a, b, preferred_element_type=F32)


def _dot_t(a, b):
    return lax.dot_general(a, b, (((1,), (1,)), ((), ())), preferred_element_type=F32)


def _sigmoid(x):
    return 1.0 / (1.0 + jnp.exp(-x))


def _silu(x):
    return x * _sigmoid(x)


def _group_meansq(x, gsize):
    tm, c = x.shape
    x2 = x * x
    if gsize == c:
        return jnp.mean(x2, axis=-1, keepdims=True)
    if gsize % V7X_LANES == 0:
        parts = []
        for h in range(c // gsize):
            ms = jnp.mean(x2[:, h * gsize:(h + 1) * gsize], axis=-1, keepdims=True)
            parts.append(jnp.broadcast_to(ms, (tm, gsize)))
        return jnp.concatenate(parts, axis=-1)
    shift = int(math.log2(gsize))
    r = lax.shift_right_logical(lax.broadcasted_iota(I32, (c, c), 0), shift)
    q = lax.shift_right_logical(lax.broadcasted_iota(I32, (c, c), 1), shift)
    bd = (r == q).astype(BF16)
    hi = x2.astype(BF16)
    lo = (x2 - hi.astype(F32)).astype(BF16)
    return (_dot(hi, bd) + _dot(lo, bd)) * (1.0 / gsize)


def _group_rms(x, gain, gsize):
    return (x * lax.rsqrt(_group_meansq(x, gsize) + EPS)) * gain


def _linear_body(*refs, norm, bias, res):
    it = iter(refs)
    x_ref = next(it)
    g_ref = next(it) if norm else None
    w_ref = next(it)
    b_ref = next(it) if bias else None
    r_ref = next(it) if res else None
    o_ref = next(it)
    xn_ref = next(it)

    @pl.when(pl.program_id(1) == 0)
    def _():
        x = x_ref[...].astype(F32)
        if norm:
            x = _rms(x, g_ref[...])
        xn_ref[...] = x.astype(BF16)

    acc = _dot(xn_ref[...], w_ref[...])
    if bias:
        acc = acc + b_ref[...]
    if res:
        acc = acc + r_ref[...]
    o_ref[...] = acc.astype(o_ref.dtype)


def linear(x, w, *, g=None, b=None, res=None, out_dtype=F32, name="linear"):
    m, k = x.shape
    n = w.shape[1]
    tm = _pick(m, (1024, 512, 256, 128))
    tn = _pick(n, (1024, 1408, 512, 256, 128))
    in_specs = [pl.BlockSpec((tm, k), lambda i, j: (i, 0))]
    args = [x]
    if g is not None:
        in_specs.append(pl.BlockSpec((1, k), lambda i, j: (0, 0)))
        args.append(g.reshape(1, k).astype(F32))
    in_specs.append(pl.BlockSpec((k, tn), lambda i, j: (0, j)))
    args.append(w)
    if b is not None:
        in_specs.append(pl.BlockSpec((1, tn), lambda i, j: (0, j)))
        args.append(b.reshape(1, n).astype(F32))
    if res is not None:
        in_specs.append(pl.BlockSpec((tm, tn), lambda i, j: (i, j)))
        args.append(res)
    return pl.pallas_call(
        functools.partial(_linear_body, norm=g is not None, bias=b is not None, res=res is not None),
        grid=(m // tm, n // tn),
        in_specs=in_specs,
        out_specs=pl.BlockSpec((tm, tn), lambda i, j: (i, j)),
        out_shape=jax.ShapeDtypeStruct((m, n), out_dtype),
        scratch_shapes=[pltpu.VMEM((tm, k), BF16)],
        compiler_params=_cparams("arbitrary", "arbitrary"),
        name=name,
    )(*args)


def _head_norm_body(x_ref, g_ref, o_ref, ob_ref, *, gsize):
    y = _group_rms(x_ref[...], g_ref[...], gsize)
    o_ref[...] = y
    ob_ref[...] = y.astype(BF16)


def head_norm(x, col0, width, gain, gsize, name="head_norm"):
    m = x.shape[0]
    tm = _pick(m, (512, 256, 128))
    assert col0 % width == 0
    gt = jnp.tile(gain.astype(F32), width // gsize).reshape(1, width)
    return pl.pallas_call(
        functools.partial(_head_norm_body, gsize=gsize),
        grid=(m // tm,),
        in_specs=[pl.BlockSpec((tm, width), lambda i: (i, col0 // width)),
                  pl.BlockSpec((1, width), lambda i: (0, 0))],
        out_specs=[pl.BlockSpec((tm, width), lambda i: (i, 0)),
                   pl.BlockSpec((tm, width), lambda i: (i, 0))],
        out_shape=[jax.ShapeDtypeStruct((m, width), F32), jax.ShapeDtypeStruct((m, width), BF16)],
        compiler_params=_cparams("arbitrary"),
        name=name,
    )(x, gt)


def _seq_conv_body(*refs, n_in, n_par, pre, post, width, tm, lv, halo):
    in_refs = refs[:n_in]
    st_ref = refs[n_in]
    wc_ref = refs[n_in + 1]
    par_refs = refs[n_in + 2:n_in + 2 + n_par]
    o_ref, nst_ref, abuf = refs[n_in + 2 + n_par:]
    t = pl.program_id(1)
    base = halo - (width - 1)

    @pl.when(t == 0)
    def _():
        abuf[base:halo, :] = st_ref[0]

    tiles = [r[0] for r in in_refs]
    pars = [r[...] for r in par_refs]
    abuf[halo:halo + tm, :] = pre(tiles)
    acc = None
    for k in range(width):
        term = wc_ref[k:k + 1, :] * abuf[base + k:base + k + tm, :]
        acc = term if acc is None else acc + term
    o_ref[0] = post(acc, tiles, pars).astype(o_ref.dtype)
    tail = abuf[base + lv:base + lv + width - 1, :]
    abuf[base:halo, :] = tail
    nst_ref[0] = tail


def seq_conv(inputs, state, wconv, params, pre, post, c_conv, c_out, out_dtype, tm, lv, name):
    n_seq, length, _ = inputs[0][0].shape
    width = wconv.shape[0]
    halo = -(-(width - 1) // V7X_SUBLANES) * V7X_SUBLANES
    in_specs = [pl.BlockSpec((1, tm, bw), functools.partial(lambda s, t, ci: (s, t, ci), ci=ci))
                for (_, bw, ci) in inputs]
    in_specs.append(pl.BlockSpec((1, width - 1, c_conv), lambda s, t: (s, 0, 0)))
    in_specs.append(pl.BlockSpec((width, c_conv), lambda s, t: (0, 0)))
    for p in params:
        in_specs.append(pl.BlockSpec((1, p.shape[-1]), lambda s, t: (0, 0)))
    out, nst = pl.pallas_call(
        functools.partial(_seq_conv_body, n_in=len(inputs), n_par=len(params), pre=pre, post=post,
                          width=width, tm=tm, lv=lv, halo=halo),
        grid=(n_seq, length // tm),
        in_specs=in_specs,
        out_specs=[pl.BlockSpec((1, tm, c_out), lambda s, t: (s, t, 0)),
                   pl.BlockSpec((1, width - 1, c_conv), lambda s, t: (s, 0, 0))],
        out_shape=[jax.ShapeDtypeStruct((n_seq, length, c_out), out_dtype),
                   jax.ShapeDtypeStruct((n_seq, width - 1, c_conv), F32)],
        scratch_shapes=[pltpu.VMEM((halo + tm, c_conv), F32)],
        compiler_params=_cparams("arbitrary", "arbitrary"),
        name=name,
    )(*[a for (a, _, _) in inputs], state, wconv.astype(F32),
      *[p.reshape(1, -1).astype(F32) for p in params])
    return out, nst


def _a_pre(tiles):
    a, g = tiles
    return a * _sigmoid(g)


def _a_post(y, tiles, pars):
    bc, lng, lnb = pars
    y = y + bc
    mu = jnp.mean(y, axis=-1, keepdims=True)
    yc = y - mu
    yn = yc * lax.rsqrt(jnp.mean(yc * yc, axis=-1, keepdims=True) + EPS)
    return _silu(yn * lng + lnb)


def _f_pre(tiles):
    return tiles[0]


def _f_post(y, tiles, pars):
    return _silu(y + pars[0]) * tiles[1]


def _d_pre(tiles):
    return tiles[1] * tiles[2]


def _d_post(y, tiles, pars):
    return tiles[0] * y


def _c_pre(tiles):
    return tiles[0]


def _c_post(y, tiles, pars):
    return _silu(y + pars[0])


def _mem_attn_body(q_ref, k_ref, v_ref, g_ref, o_ref):
    q = q_ref[0]
    for h in range(H_M):
        sl = slice(h * DH_M, (h + 1) * DH_M)
        qh = (_rms(q[:, sl], g_ref[...]) * (DH_M ** -0.5)).astype(BF16)
        kh = k_ref[0, :, sl].astype(BF16)
        vh = v_ref[0, :, sl].astype(BF16)
        s = _dot_t(qh, kh)
        p = jnp.exp(s - jnp.max(s, axis=-1, keepdims=True))
        p = p / jnp.sum(p, axis=-1, keepdims=True)
        o_ref[0, :, sl] = _dot(p.astype(BF16), vh).astype(o_ref.dtype)


def mem_attn(q, k, v, q_gain, tm, name="mem_attn"):
    n_seq, length, d = q.shape
    return pl.pallas_call(
        _mem_attn_body,
        grid=(n_seq, length // tm),
        in_specs=[pl.BlockSpec((1, tm, d), lambda s, t: (s, t, 0)),
                  pl.BlockSpec((1, N_MEM, d), lambda s, t: (s, 0, 0)),
                  pl.BlockSpec((1, N_MEM, d), lambda s, t: (s, 0, 0)),
                  pl.BlockSpec((1, DH_M), lambda s, t: (0, 0))],
        out_specs=pl.BlockSpec((1, tm, d), lambda s, t: (s, t, 0)),
        out_shape=jax.ShapeDtypeStruct((n_seq, length, d), BF16),
        compiler_params=_cparams("arbitrary", "arbitrary"),
        name=name,
    )(q, k, v, q_gain.reshape(1, DH_M).astype(F32))


def _t5_tiles_body(tab_ref, o_ref):
    h = pl.program_id(0)
    r = lax.broadcasted_iota(I32, (PAGE_SIZE, 2 * PAGE_SIZE), 0)
    c = lax.broadcasted_iota(I32, (PAGE_SIZE, 2 * PAGE_SIZE), 1)
    dist = jnp.maximum(jnp.where(c < PAGE_SIZE, r - c, 2 * PAGE_SIZE + r - c), 0)
    max_exact = N_BUCKETS // 2
    df = jnp.maximum(dist, 1).astype(F32)
    large = max_exact + (jnp.log(df / max_exact) / math.log(MAX_DIST / max_exact)
                         * (N_BUCKETS - max_exact)).astype(I32)
    large = jnp.minimum(large, N_BUCKETS - 1)
    bucket = jnp.where(dist < max_exact, dist, large)
    acc = jnp.zeros(dist.shape, F32)
    for b in range(N_BUCKETS):
        acc = jnp.where(bucket == b, tab_ref[b, h], acc)
    o_ref[0] = acc


def t5_tiles(table):
    return pl.pallas_call(
        _t5_tiles_body,
        grid=(H_B,),
        in_specs=[pl.BlockSpec(memory_space=pltpu.SMEM)],
        out_specs=pl.BlockSpec((1, PAGE_SIZE, 2 * PAGE_SIZE), lambda h: (h, 0, 0)),
        out_shape=jax.ShapeDtypeStruct((H_B, PAGE_SIZE, 2 * PAGE_SIZE), F32),
        compiler_params=_cparams("arbitrary"),
        name="t5_tiles",
    )(table.astype(F32))


def _indexer_tile(qih, wcol, ki_tile):
    acc = jnp.zeros((qih[0].shape[0], ki_tile.shape[0]), F32)
    for h in range(H_IDX):
        acc = acc + jnp.maximum(_dot_t(qih[h], ki_tile), 0.0) * wcol[h]
    return acc


def _sort_key(score):
    score = jnp.where(score == 0.0, 0.0, score)
    u = lax.bitcast_convert_type(score, I32)
    return jnp.where(u < 0, u ^ 0x7FFFFFFF, u)


def _kth_largest(count_ge, rows, topk):
    def try_cand(cand, cur):
        return jnp.where(count_ge(cand) >= topk, cand, cur)

    t0 = try_cand(jnp.zeros((rows, 1), I32), jnp.full((rows, 1), INT_MIN, I32))

    def step(b, cur):
        cand = cur + lax.shift_left(jnp.int32(1), jnp.int32(30) - b)
        return try_cand(cand, cur)

    return lax.fori_loop(0, 31, step, t0)


def _strict_upper(n):
    r = lax.broadcasted_iota(I32, (n, n), 0)
    c = lax.broadcasted_iota(I32, (n, n), 1)
    return (r < c).astype(BF16)


def _dsa_prompt_body(q_ref, qi_ref, wi_ref, k_ref, v_ref, ki_ref, bt_ref, o_ref,
                     key_ref, am_ref, lg_ref, *, topk):
    i = pl.program_id(1)
    nj = i + 1
    qb = PAGE_SIZE
    rows = lax.broadcasted_iota(I32, (qb, qb), 0)
    cols = lax.broadcasted_iota(I32, (qb, qb), 1)

    def kslice(j):
        return pl.ds(pl.multiple_of(j * qb, qb), qb)

    qi = qi_ref[0]
    wsc = (wi_ref[0] * (H_IDX ** -0.5)) * (D_IDX ** -0.5)
    qih = [qi[:, h * D_IDX:(h + 1) * D_IDX].astype(BF16) for h in range(H_IDX)]
    wcol = [wsc[:, h:h + 1] for h in range(H_IDX)]

    def idx_step(j, carry):
        sc = _indexer_tile(qih, wcol, ki_ref[0, kslice(j), :])
        valid = (j < i) | (cols <= rows)
        key_ref[j] = jnp.where(valid, _sort_key(sc), INT_MIN)
        return carry

    lax.fori_loop(0, nj, idx_step, 0)

    def count_ge(cand):
        def cstep(j, c):
            return c + jnp.where(key_ref[j] >= cand, 1.0, 0.0)
        c = lax.fori_loop(0, nj, cstep, jnp.zeros((qb, qb), F32))
        return jnp.sum(c, axis=-1, keepdims=True)

    thr = _kth_largest(count_ge, qb, topk)

    def gt_step(j, c):
        return c + jnp.where(key_ref[j] > thr, 1.0, 0.0)

    n_gt = jnp.sum(lax.fori_loop(0, nj, gt_step, jnp.zeros((qb, qb), F32)), axis=-1, keepdims=True)
    n_tie = topk - n_gt
    upper = _strict_upper(qb)

    def mask_step(j, run):
        key = key_ref[j]
        valid = (j < i) | (cols <= rows)
        eq = valid & (key == thr)
        eqf = jnp.where(eq, 1.0, 0.0)
        rank = _dot(eqf.astype(BF16), upper) + run
        sel = (key > thr) | (eq & (rank < n_tie))
        am_ref[j] = jnp.where(sel, 0.0, NEG_BIG)
        return run + jnp.sum(eqf, axis=-1, keepdims=True)

    lax.fori_loop(0, nj, mask_step, jnp.zeros((qb, 1), F32))

    q = q_ref[0]
    for n in range(KV_B):
        heads = [n * G_B + hh for hh in range(G_B)]
        qg = jnp.concatenate([q[:, h * DH_B:(h + 1) * DH_B] for h in heads], axis=0)
        qg = (qg * (DH_B ** -0.5)).astype(BF16)
        b_diag = jnp.concatenate([bt_ref[h, :, 0:qb] for h in heads], axis=0)
        b_sub = jnp.concatenate([bt_ref[h, :, qb:2 * qb] for h in heads], axis=0)
        b_far = jnp.concatenate(
            [jnp.broadcast_to(bt_ref[h, 0:1, qb:qb + 1], (qb, qb)) for h in heads], axis=0)
        ksl = slice(n * DH_B, (n + 1) * DH_B)

        def p1(j, mx):
            s = _dot_t(qg, k_ref[0, kslice(j), ksl])
            bias = jnp.where(j == i, b_diag, jnp.where(j == i - 1, b_sub, b_far))
            am = am_ref[j]
            s = s + bias + jnp.concatenate([am] * G_B, axis=0)
            lg_ref[j] = s
            return jnp.maximum(mx, s)

        mx = lax.fori_loop(0, nj, p1, jnp.full((G_B * qb, qb), NEG_BIG, F32))
        m = jnp.max(mx, axis=-1, keepdims=True)

        def p2(j, carry):
            l, acc = carry
            p = jnp.exp(lg_ref[j] - m)
            acc = acc + _dot(p.astype(BF16), v_ref[0, kslice(j), ksl])
            return l + p, acc

        l, acc = lax.fori_loop(0, nj, p2, (jnp.zeros((G_B * qb, qb), F32),
                                           jnp.zeros((G_B * qb, DH_B), F32)))
        o = acc / jnp.sum(l, axis=-1, keepdims=True)
        for hh, h in enumerate(heads):
            o_ref[0, :, h * DH_B:(h + 1) * DH_B] = o[hh * qb:(hh + 1) * qb].astype(o_ref.dtype)


def dsa_prompt(qn, qi, wi, kb, vb, kib, bt):
    n_seq, length, _ = qn.shape
    nkb = length // PAGE_SIZE
    qmap = lambda s, i: (s, i, 0)
    kmap = lambda s, i: (s, 0, 0)
    return pl.pallas_call(
        functools.partial(_dsa_prompt_body, topk=min(TOPK, length // 4)),
        grid=(n_seq, nkb),
        in_specs=[pl.BlockSpec((1, PAGE_SIZE, H_B * DH_B), qmap),
                  pl.BlockSpec((1, PAGE_SIZE, H_IDX * D_IDX), qmap),
                  pl.BlockSpec((1, PAGE_SIZE, H_IDX), qmap),
                  pl.BlockSpec((1, length, KV_B * DH_B), kmap),
                  pl.BlockSpec((1, length, KV_B * DH_B), kmap),
                  pl.BlockSpec((1, length, D_IDX), kmap),
                  pl.BlockSpec((H_B, PAGE_SIZE, 2 * PAGE_SIZE), lambda s, i: (0, 0, 0))],
        out_specs=pl.BlockSpec((1, PAGE_SIZE, H_B * DH_B), qmap),
        out_shape=jax.ShapeDtypeStruct((n_seq, length, H_B * DH_B), BF16),
        scratch_shapes=[pltpu.VMEM((nkb, PAGE_SIZE, PAGE_SIZE), I32),
                        pltpu.VMEM((nkb, PAGE_SIZE, PAGE_SIZE), F32),
                        pltpu.VMEM((nkb, G_B * PAGE_SIZE, PAGE_SIZE), F32)],
        compiler_params=_cparams("arbitrary", "arbitrary"),
        name="dsa_prompt",
    )(qn, qi, wi, kb, vb, kib, bt)


N_PAGES = PAST_LEN // PAGE_SIZE
N_KTILES = N_PAGES + 1
N_KTILES_PAD = -(-N_KTILES // PAGES_PER_STEP) * PAGES_PER_STEP


def _dsa_sample_sel_body(pt_ref, qi_ref, wi_ref, kin_ref, *rest, lv, topk):
    page_refs = rest[:PAGES_PER_STEP]
    am_ref, key_ref = rest[PAGES_PER_STEP:]
    c = pl.program_id(1)
    nsteps = pl.num_programs(1)
    r8 = SAMPLE_PAD
    qi = qi_ref[0]
    wsc = (wi_ref[0] * (H_IDX ** -0.5)) * (D_IDX ** -0.5)
    qih = [qi[:, h * D_IDX:(h + 1) * D_IDX].astype(BF16) for h in range(H_IDX)]
    wcol = [wsc[:, h:h + 1] for h in range(H_IDX)]
    for r in range(PAGES_PER_STEP):
        sc = _indexer_tile(qih, wcol, page_refs[r][0].astype(BF16))
        key_ref[c * PAGES_PER_STEP + r] = _sort_key(sc)

    @pl.when(c == nsteps - 1)
    def _():
        rows = lax.broadcasted_iota(I32, (r8, PAGE_SIZE), 0)
        cols = lax.broadcasted_iota(I32, (r8, PAGE_SIZE), 1)
        new_valid = (cols <= rows) & (cols < lv)
        kin = jnp.concatenate([kin_ref[0], jnp.zeros((PAGE_SIZE - r8, D_IDX), F32)], axis=0)
        sc = _indexer_tile(qih, wcol, kin.astype(BF16))
        key_ref[N_PAGES] = jnp.where(new_valid, _sort_key(sc), INT_MIN)

        def count_ge(cand):
            def cstep(j, acc):
                return acc + jnp.where(key_ref[j] >= cand, 1.0, 0.0)
            acc = lax.fori_loop(0, N_KTILES, cstep, jnp.zeros((r8, PAGE_SIZE), F32))
            return jnp.sum(acc, axis=-1, keepdims=True)

        thr = _kth_largest(count_ge, r8, topk)

        def gt_step(j, acc):
            return acc + jnp.where(key_ref[j] > thr, 1.0, 0.0)

        n_gt = jnp.sum(lax.fori_loop(0, N_KTILES, gt_step, jnp.zeros((r8, PAGE_SIZE), F32)),
                       axis=-1, keepdims=True)
        n_tie = topk - n_gt
        upper = _strict_upper(PAGE_SIZE)

        def mask_step(j, run):
            key = key_ref[j]
            valid = (j < N_PAGES) | new_valid
            eq = valid & (key == thr)
            eqf = jnp.where(eq, 1.0, 0.0)
            rank = _dot(eqf.astype(BF16), upper) + run
            sel = (key > thr) | (eq & (rank < n_tie))
            am_ref[0, j] = jnp.where(sel, 0.0, NEG_BIG)
            return run + jnp.sum(eqf, axis=-1, keepdims=True)

        lax.fori_loop(0, N_KTILES, mask_step, jnp.zeros((r8, 1), F32))
        for j in range(N_KTILES, N_KTILES_PAD):
            am_ref[0, j] = jnp.full((r8, PAGE_SIZE), NEG_BIG, F32)


def dsa_sample_select(page_table, qi, wi, kin, kidx_pool, lv):
    n_seq = qi.shape[0]
    r8 = SAMPLE_PAD
    qmap = lambda s, c, pt: (s, 0, 0)
    page_specs = [pl.BlockSpec((1, PAGE_SIZE, D_IDX),
                               functools.partial(lambda s, c, pt, r: (pt[s, c * PAGES_PER_STEP + r], 0, 0), r=r))
                  for r in range(PAGES_PER_STEP)]
    grid_spec = pltpu.PrefetchScalarGridSpec(
        num_scalar_prefetch=1,
        grid=(n_seq, N_PAGES // PAGES_PER_STEP),
        in_specs=[pl.BlockSpec((1, r8, H_IDX * D_IDX), qmap),
                  pl.BlockSpec((1, r8, H_IDX), qmap),
                  pl.BlockSpec((1, r8, D_IDX), qmap)] + page_specs,
        out_specs=pl.BlockSpec((1, N_KTILES_PAD, r8, PAGE_SIZE), lambda s, c, pt: (s, 0, 0, 0)),
        scratch_shapes=[pltpu.VMEM((N_KTILES, r8, PAGE_SIZE), I32)],
    )
    return pl.pallas_call(
        functools.partial(_dsa_sample_sel_body, lv=lv, topk=min(TOPK, (PAST_LEN + lv) // 4)),
        grid_spec=grid_spec,
        out_shape=jax.ShapeDtypeStruct((n_seq, N_KTILES_PAD, r8, PAGE_SIZE), F32),
        compiler_params=_cparams("arbitrary", "arbitrary"),
        name="dsa_sample_select",
    )(page_table, qi, wi, kin, *([kidx_pool] * PAGES_PER_STEP))


def _dsa_sample_attn_body(pt_ref, q_ref, kn_ref, vn_ref, am_ref, amn_ref, bt_ref, *rest):
    kp_refs = rest[:PAGES_PER_STEP]
    vp_refs = rest[PAGES_PER_STEP:2 * PAGES_PER_STEP]
    o_ref, m_ref, l_ref, acc_ref = rest[2 * PAGES_PER_STEP:]
    c = pl.program_id(1)
    nsteps = pl.num_programs(1)
    r8 = SAMPLE_PAD
    gr = G_B * r8

    @pl.when(c == 0)
    def _():
        m_ref[...] = jnp.full(m_ref.shape, NEG_BIG, F32)
        l_ref[...] = jnp.zeros(l_ref.shape, F32)
        acc_ref[...] = jnp.zeros(acc_ref.shape, F32)

    q = q_ref[0]
    qgs, b_subs, b_diags, b_fars = [], [], [], []
    for n in range(KV_B):
        heads = [n * G_B + hh for hh in range(G_B)]
        qg = jnp.concatenate([q[:, h * DH_B:(h + 1) * DH_B] for h in heads], axis=0)
        qgs.append((qg * (DH_B ** -0.5)).astype(BF16))
        b_diags.append(jnp.concatenate([bt_ref[h, :, 0:PAGE_SIZE] for h in heads], axis=0))
        b_subs.append(jnp.concatenate([bt_ref[h, :, PAGE_SIZE:2 * PAGE_SIZE] for h in heads], axis=0))
        b_fars.append(jnp.concatenate(
            [jnp.broadcast_to(bt_ref[h, 0:1, PAGE_SIZE:PAGE_SIZE + 1], (r8, PAGE_SIZE))
             for h in heads], axis=0))

    def update(n, s, v_tile):
        m_old = m_ref[n]
        m_new = jnp.maximum(m_old, jnp.max(s, axis=-1, keepdims=True))
        alpha = jnp.exp(m_old - m_new)
        p = jnp.exp(s - m_new)
        l_ref[n] = alpha * l_ref[n] + jnp.sum(p, axis=-1, keepdims=True)
        acc_ref[n] = alpha[:, 0:DH_B] * acc_ref[n] + _dot(p.astype(BF16), v_tile)
        m_ref[n] = m_new

    for r in range(PAGES_PER_STEP):
        am = am_ref[0, r]
        am4 = jnp.concatenate([am] * G_B, axis=0)
        is_last_page = (c == nsteps - 1) & (r == PAGES_PER_STEP - 1)
        for n in range(KV_B):
            ksl = slice(n * DH_B, (n + 1) * DH_B)
            s = _dot_t(qgs[n], kp_refs[r][0, :, ksl].astype(BF16))
            s = s + jnp.where(is_last_page, b_subs[n], b_fars[n]) + am4
            update(n, s, vp_refs[r][0, :, ksl].astype(BF16))

    @pl.when(c == nsteps - 1)
    def _():
        pad = jnp.zeros((PAGE_SIZE - r8, KV_B * DH_B), F32)
        kn = jnp.concatenate([kn_ref[0], pad], axis=0).astype(BF16)
        vn = jnp.concatenate([vn_ref[0], pad], axis=0).astype(BF16)
        am4 = jnp.concatenate([amn_ref[0, 0]] * G_B, axis=0)
        for n in range(KV_B):
            ksl = slice(n * DH_B, (n + 1) * DH_B)
            s = _dot_t(qgs[n], kn[:, ksl]) + b_diags[n] + am4
            update(n, s, vn[:, ksl])
            o = acc_ref[n] / l_ref[n][:, 0:DH_B]
            for hh in range(G_B):
                h = n * G_B + hh
                o_ref[0, :, h * DH_B:(h + 1) * DH_B] = o[hh * r8:(hh + 1) * r8].astype(o_ref.dtype)


def dsa_sample_attend(page_table, qn, kn, vn, amask, bt, k_pool, v_pool):
    n_seq = qn.shape[0]
    r8 = SAMPLE_PAD
    qmap = lambda s, c, pt: (s, 0, 0)
    page_map = [functools.partial(lambda s, c, pt, r: (pt[s, c * PAGES_PER_STEP + r], 0, 0), r=r)
                for r in range(PAGES_PER_STEP)]
    kv_w = KV_B * DH_B
    grid_spec = pltpu.PrefetchScalarGridSpec(
        num_scalar_prefetch=1,
        grid=(n_seq, N_PAGES // PAGES_PER_STEP),
        in_specs=[pl.BlockSpec((1, r8, H_B * DH_B), qmap),
                  pl.BlockSpec((1, r8, kv_w), qmap),
                  pl.BlockSpec((1, r8, kv_w), qmap),
                  pl.BlockSpec((1, PAGES_PER_STEP, r8, PAGE_SIZE), lambda s, c, pt: (s, c, 0, 0)),
                  pl.BlockSpec((1, PAGES_PER_STEP, r8, PAGE_SIZE),
                               lambda s, c, pt: (s, N_PAGES // PAGES_PER_STEP, 0, 0)),
                  pl.BlockSpec((H_B, r8, 2 * PAGE_SIZE), lambda s, c, pt: (0, 0, 0))]
                 + [pl.BlockSpec((1, PAGE_SIZE, kv_w), m) for m in page_map]
                 + [pl.BlockSpec((1, PAGE_SIZE, kv_w), m) for m in page_map],
        out_specs=pl.BlockSpec((1, r8, H_B * DH_B), qmap),
        scratch_shapes=[pltpu.VMEM((KV_B, G_B * r8, PAGE_SIZE), F32),
                        pltpu.VMEM((KV_B, G_B * r8, PAGE_SIZE), F32),
                        pltpu.VMEM((KV_B, G_B * r8, DH_B), F32)],
    )
    return pl.pallas_call(
        _dsa_sample_attn_body,
        grid_spec=grid_spec,
        out_shape=jax.ShapeDtypeStruct((n_seq, r8, H_B * DH_B), BF16),
        compiler_params=_cparams("arbitrary", "arbitrary"),
        name="dsa_sample_attend",
    )(page_table, qn, kn, vn, amask, amask, bt,
      *([k_pool] * PAGES_PER_STEP), *([v_pool] * PAGES_PER_STEP))


HEADS_PER_GROUP = H_C // G_C
GROUP_W = D_IN_C // G_C


def _cumsum_rows(x):
    n = x.shape[0]
    idx = lax.broadcasted_iota(I32, x.shape, 0)
    d = 1
    while d < n:
        x = x + jnp.where(idx >= d, pltpu.roll(x, d, 0), 0.0)
        d *= 2
    return x


def _cumsum_lanes(x):
    n = x.shape[1]
    idx = lax.broadcasted_iota(I32, x.shape, 1)
    d = 1
    while d < n:
        x = x + jnp.where(idx >= d, pltpu.roll(x, d, 1), 0.0)
        d *= 2
    return x


def _softplus(x):
    return jnp.maximum(x, 0.0) + jnp.log1p(jnp.exp(-jnp.abs(x)))


def _ssd_body(x_ref, xs_ref, bm_ref, cm_ref, z_ref, ng_ref, wdt_ref, wdtt_ref, dtb_ref, dtbt_ref,
              alog_ref, alogt_ref, dsk_ref, cn_ref, wo_ref, h0_ref, o_ref, hst_ref,
              dt_ref, dtt_ref, *, qin, lv):
    c = pl.program_id(1)
    g = pl.program_id(2)
    q = CHUNK_C

    def pad_rows(a):
        if qin == q:
            return a
        return jnp.concatenate([a, jnp.zeros((q - qin, a.shape[1]), a.dtype)], axis=0)

    @pl.when(c == 0)
    def _():
        for j in range(HEADS_PER_GROUP):
            hst_ref[0, g * HEADS_PER_GROUP + j] = h0_ref[0, j]

    @pl.when(g == 0)
    def _():
        xn = _rms(pad_rows(x_ref[0]), ng_ref[...]).astype(BF16)
        dt = _softplus(_dot(xn, wdt_ref[...]) + dtb_ref[...])
        dtt = _softplus(_dot_t(wdtt_ref[...], xn) + dtbt_ref[...])
        if lv < q:
            rows = lax.broadcasted_iota(I32, dt.shape, 0)
            lanes = lax.broadcasted_iota(I32, dtt.shape, 1)
            dt = jnp.where(rows < lv, dt, 0.0)
            dtt = jnp.where(lanes < lv, dtt, 0.0)
        dt_ref[...] = dt
        dtt_ref[...] = dtt

    dt = dt_ref[...]
    dtt = dtt_ref[...]
    acs = _cumsum_rows(dt * (-jnp.exp(alog_ref[...])))
    acst = _cumsum_lanes(dtt * (-jnp.exp(alogt_ref[...])))
    xs = pad_rows(xs_ref[0])
    bm = pad_rows(bm_ref[0])
    cm = pad_rows(cm_ref[0])
    xst = xs.T
    cmb = cm.astype(BF16)
    cb = _dot_t(cmb, bm.astype(BF16))
    rows = lax.broadcasted_iota(I32, (q, q), 0)
    cols = lax.broadcasted_iota(I32, (q, q), 1)
    tril = cols <= rows
    ys = []
    for j in range(HEADS_PER_GROUP):
        hd = g * HEADS_PER_GROUP + j
        onehot_c = lax.broadcasted_iota(I32, (1, H_C), 1) == hd
        onehot_r = lax.broadcasted_iota(I32, (H_C, 1), 0) == hd
        acs_col = jnp.sum(jnp.where(onehot_c, acs, 0.0), axis=1, keepdims=True)
        dt_col = jnp.sum(jnp.where(onehot_c, dt, 0.0), axis=1, keepdims=True)
        acs_row = jnp.sum(jnp.where(onehot_r, acst, 0.0), axis=0, keepdims=True)
        dt_row = jnp.sum(jnp.where(onehot_r, dtt, 0.0), axis=0, keepdims=True)
        a_last = acs_row[:, q - 1:q]
        xh = xs[:, j * P_C:(j + 1) * P_C] * dt_col
        xht = xst[j * P_C:(j + 1) * P_C, :] * dt_row
        lmat = jnp.exp(jnp.where(tril, acs_col - acs_row, -jnp.inf))
        y = _dot((cb * lmat).astype(BF16), xh.astype(BF16))
        h_old = hst_ref[0, hd]
        y = y + _dot_t(cmb, h_old.astype(BF16)) * jnp.exp(acs_col)
        decay = jnp.exp(a_last - acs_col)
        st = _dot(xht.astype(BF16), (bm * decay).astype(BF16))
        hst_ref[0, hd] = jnp.exp(a_last) * h_old + st
        ys.append(y)
    y = jnp.concatenate(ys, axis=1) + xs * dsk_ref[...]
    y = y * _silu(pad_rows(z_ref[0]))
    y = _rms(y, cn_ref[...]).astype(BF16)
    part = _dot(y, wo_ref[...])[0:qin]

    @pl.when(g == 0)
    def _():
        o_ref[0] = x_ref[0] + part

    @pl.when(g != 0)
    def _():
        o_ref[0] = o_ref[0] + part


def ssd_mix(x, xbc, proj, z_col0, h0, norm_g, w_dt, dt_bias, a_log, d_skip, c_norm, w_out, qin, lv):
    n_seq, length, d = x.shape
    nchunk = length // qin
    gw = GROUP_W
    col = lambda a: a.reshape(-1, 1).astype(F32)
    row = lambda a: a.reshape(1, -1).astype(F32)
    cmap = lambda off: (lambda s, c, g: (s, c, off + g))
    full2 = lambda s, c, g: (0, 0)
    in_specs = [
        pl.BlockSpec((1, qin, d), lambda s, c, g: (s, c, 0)),
        pl.BlockSpec((1, qin, gw), cmap(0)),
        pl.BlockSpec((1, qin, N_SSM), cmap(D_IN_C // N_SSM)),
        pl.BlockSpec((1, qin, N_SSM), cmap(D_IN_C // N_SSM + G_C)),
        pl.BlockSpec((1, qin, gw), cmap(z_col0 // gw)),
        pl.BlockSpec((1, d), full2),
        pl.BlockSpec((d, H_C), full2),
        pl.BlockSpec((H_C, d), full2),
        pl.BlockSpec((1, H_C), full2),
        pl.BlockSpec((H_C, 1), full2),
        pl.BlockSpec((1, H_C), full2),
        pl.BlockSpec((H_C, 1), full2),
        pl.BlockSpec((1, gw), lambda s, c, g: (0, g)),
        pl.BlockSpec((1, gw), lambda s, c, g: (0, g)),
        pl.BlockSpec((gw, d), lambda s, c, g: (g, 0)),
        pl.BlockSpec((1, HEADS_PER_GROUP, P_C, N_SSM), lambda s, c, g: (s, g, 0, 0)),
    ]
    out, hst = pl.pallas_call(
        functools.partial(_ssd_body, qin=qin, lv=lv),
        grid=(n_seq, nchunk, G_C),
        in_specs=in_specs,
        out_specs=[pl.BlockSpec((1, qin, d), lambda s, c, g: (s, c, 0)),
                   pl.BlockSpec((1, H_C, P_C, N_SSM), lambda s, c, g: (s, 0, 0, 0))],
        out_shape=[jax.ShapeDtypeStruct((n_seq, length, d), F32),
                   jax.ShapeDtypeStruct((n_seq, H_C, P_C, N_SSM), F32)],
        scratch_shapes=[pltpu.VMEM((CHUNK_C, H_C), F32), pltpu.VMEM((H_C, CHUNK_C), F32)],
        compiler_params=_cparams("arbitrary", "arbitrary", "arbitrary"),
        name="ssd_mix",
    )(x, xbc, xbc, xbc, proj, row(norm_g), w_dt.astype(BF16), w_dt.T.astype(BF16),
      row(dt_bias), col(dt_bias), row(a_log), col(a_log),
      row(jnp.repeat(d_skip, P_C)), row(c_norm), w_out, h0)
    return out, hst


def _flat(a):
    return a.reshape(-1, a.shape[-1])


def _unflat(a, like):
    return a.reshape(like.shape[0], like.shape[1], a.shape[-1])


def kernel(x_prompt, x_sample, mem_prompt, state_a_conv, cache_b_k, cache_b_v, cache_b_kidx, state_c_conv, state_c_ssm, state_d_conv, state_ffn_conv, cache_mem_k, cache_mem_v, page_table, rel_bias, norm_mix, norm_mem, norm_ffn, norm_memtok, a_w_in, a_b_in, a_w_conv, a_b_conv, a_ln_g, a_ln_b, a_w_out, b_w_in, b_w_out, b_q_norm, b_k_norm, b_kidx_norm, c_w_in, c_w_conv, c_b_conv, c_dt_bias, c_a_log, c_d_skip, c_norm, c_w_out, d_w_in, d_w_conv, d_w_out, m_w_q, m_w_kv, m_w_o, m_q_norm, m_k_norm, f_w_in, f_w_conv, f_b_conv, f_w_out):
    n_p, l_p, d = x_prompt.shape
    n_s, l_s, _ = x_sample.shape
    bf = lambda w: w.astype(BF16)
    xs_pad = jnp.pad(x_sample, ((0, 0), (0, SAMPLE_PAD - l_s), (0, 0)))
    groups = [dict(x=x_prompt, tm=256, lv=256, qin=CHUNK_C, lvq=CHUNK_C, prompt=True),
              dict(x=xs_pad, tm=SAMPLE_PAD, lv=l_s, qin=SAMPLE_PAD, lvq=l_s, prompt=False)]
    bt = t5_tiles(rel_bias)
    mem_flat = _flat(mem_prompt)
    outs = {k: [[], []] for k in ("a", "bk", "bv", "bki", "cc", "cs", "d", "f")}
    m_kp, m_vp = [], []

    for i in range(DEPTH):
        j = i // 4
        kind = i % 4
        kv = linear(mem_flat, bf(m_w_kv[i]), g=norm_memtok[i], name="mem_kv")
        mk, _ = head_norm(kv, 0, d, m_k_norm[i], DH_M, name="mem_k_norm")
        mk = mk.reshape(n_p, N_MEM, d)
        mv = kv[:, d:].reshape(n_p, N_MEM, d)
        m_kp.append(mk.reshape(n_p, N_MEM, H_M, DH_M))
        m_vp.append(mv.reshape(n_p, N_MEM, H_M, DH_M))

        for gi, grp in enumerate(groups):
            x = grp["x"]
            n_seq, length, _ = x.shape
            prompt = grp["prompt"]
            zeros_state = lambda w, c: jnp.zeros((n_seq, w - 1, c), F32)
            xf = _flat(x)
            if kind == 0:
                ag = _unflat(linear(xf, bf(a_w_in[j]), g=norm_mix[i], b=a_b_in[j], name="a_in"), x)
                st = zeros_state(W_A, d) if prompt else state_a_conv[j]
                u, nst = seq_conv([(ag, d, 0), (ag, d, 1)], st, a_w_conv[j],
                                  [a_b_conv[j], a_ln_g[j], a_ln_b[j]], _a_pre, _a_post,
                                  d, d, BF16, grp["tm"], grp["lv"], "a_conv")
                outs["a"][gi].append(nst)
                x = _unflat(linear(_flat(u), bf(a_w_out[j]), res=xf, name="a_out"), x)
            elif kind == 1:
                w_in = b_w_in[j]
                n_main = H_B * DH_B + 2 * KV_B * DH_B + H_IDX * D_IDX
                w_tail = jnp.pad(w_in[:, n_main:], ((0, 0), (0, V7X_LANES - (w_in.shape[1] - n_main))))
                pm = linear(xf, bf(w_in[:, :n_main]), g=norm_mix[i], name="b_in")
                ptl = linear(xf, bf(w_tail), g=norm_mix[i], name="b_in_tail")
                kw = KV_B * DH_B
                qn, _ = head_norm(pm, 0, H_B * DH_B, b_q_norm[j], DH_B, name="b_q_norm")
                kn, knb = head_norm(pm, H_B * DH_B, kw, b_k_norm[j], DH_B, name="b_k_norm")
                kin, kinb = head_norm(ptl[:, :D_IDX], 0, D_IDX, b_kidx_norm[j], D_IDX, name="b_ki_norm")
                v = pm[:, H_B * DH_B + kw:H_B * DH_B + 2 * kw]
                qi = pm[:, H_B * DH_B + 2 * kw:]
                wi = ptl[:, D_IDX:D_IDX + H_IDX]
                r3 = lambda a: a.reshape(n_seq, length, a.shape[-1])
                if prompt:
                    o = dsa_prompt(r3(qn), r3(qi), r3(wi), r3(knb), r3(bf(v)), r3(kinb), bt)
                else:
                    am = dsa_sample_select(page_table, r3(qi), r3(wi), r3(kin), cache_b_kidx[j], grp["lv"])
                    n_pool = cache_b_k.shape[1]
                    o = dsa_sample_attend(page_table, r3(qn), r3(kn), r3(v), am, bt,
                                          cache_b_k[j].reshape(n_pool, PAGE_SIZE, kw),
                                          cache_b_v[j].reshape(n_pool, PAGE_SIZE, kw))
                outs["bk"][gi].append(r3(kn))
                outs["bv"][gi].append(r3(v))
                outs["bki"][gi].append(r3(kin))
                x = _unflat(linear(_flat(o), bf(b_w_out[j]), res=xf, name="b_out"), x)
            elif kind == 2:
                w_in = c_w_in[j]
                conv_c = D_IN_C + 2 * G_C * N_SSM
                w_main = jnp.concatenate([w_in[:, D_IN_C:D_IN_C + conv_c], w_in[:, :D_IN_C]], axis=1)
                proj = _unflat(linear(xf, bf(w_main), g=norm_mix[i], name="c_in"), x)
                st = zeros_state(W_C, conv_c) if prompt else state_c_conv[j]
                xbc, nst = seq_conv([(proj, conv_c, 0)], st, c_w_conv[j], [c_b_conv[j]], _c_pre, _c_post,
                                    conv_c, conv_c, F32, grp["tm"], grp["lv"], "c_conv")
                h0 = jnp.zeros((n_seq, H_C, P_C, N_SSM), F32) if prompt else state_c_ssm[j]
                x, hst = ssd_mix(x, xbc, proj, conv_c, h0, norm_mix[i], w_in[:, D_IN_C + conv_c:],
                                 c_dt_bias[j], c_a_log[j], c_d_skip[j], c_norm[j], bf(c_w_out[j]),
                                 grp["qin"], grp["lvq"])
                outs["cc"][gi].append(nst)
                outs["cs"][gi].append(hst)
            else:
                p3 = _unflat(linear(xf, bf(d_w_in[j]), g=norm_mix[i], name="d_in"), x)
                st = zeros_state(W_D, d) if prompt else state_d_conv[j]
                u, nst = seq_conv([(p3, d, 0), (p3, d, 1), (p3, d, 2)], st, d_w_conv[j], [],
                                  _d_pre, _d_post, d, d, BF16, grp["tm"], grp["lv"], "d_conv")
                outs["d"][gi].append(nst)
                x = _unflat(linear(_flat(u), bf(d_w_out[j]), res=xf, name="d_out"), x)

            xf = _flat(x)
            qm = _unflat(linear(xf, bf(m_w_q[i]), g=norm_mem[i], name="mem_q"), x)
            if prompt:
                kk, vv, tma = mk, mv, 512
            else:
                kk = cache_mem_k[i].reshape(n_seq, N_MEM, d)
                vv = cache_mem_v[i].reshape(n_seq, N_MEM, d)
                tma = SAMPLE_PAD
            om = mem_attn(qm, kk, vv, m_q_norm[i], tma)
            x = _unflat(linear(_flat(om), bf(m_w_o[i]), res=xf, name="mem_o"), x)

            xf = _flat(x)
            hf = _unflat(linear(xf, bf(f_w_in[i]), g=norm_ffn[i], name="ffn_in"), x)
            st = zeros_state(W_F, D_FF) if prompt else state_ffn_conv[i]
            uf, nst = seq_conv([(hf, D_FF, 0), (hf, D_FF, 1)], st, f_w_conv[i], [f_b_conv[i]],
                               _f_pre, _f_post, D_FF, D_FF, BF16, grp["tm"], grp["lv"], "ffn_conv")
            outs["f"][gi].append(nst)
            x = _unflat(linear(_flat(uf), bf(f_w_out[i]), res=xf, name="ffn_out"), x)
            grp["x"] = x

    yp = groups[0]["x"]
    ys = groups[1]["x"][:, :l_s]
    st = lambda key, gi: jnp.stack(outs[key][gi])
    kvshape = lambda a, n, l: a.reshape(a.shape[0], n, -1, a.shape[-1])[:, :, :l]
    b_k_p = kvshape(st("bk", 0), n_p, l_p).reshape(-1, n_p, l_p, KV_B, DH_B)
    b_v_p = kvshape(st("bv", 0), n_p, l_p).reshape(-1, n_p, l_p, KV_B, DH_B)
    b_ki_p = kvshape(st("bki", 0), n_p, l_p)
    b_k_s = kvshape(st("bk", 1), n_s, l_s).reshape(-1, n_s, l_s, KV_B, DH_B)
    b_v_s = kvshape(st("bv", 1), n_s, l_s).reshape(-1, n_s, l_s, KV_B, DH_B)
    b_ki_s = kvshape(st("bki", 1), n_s, l_s)
    return (yp, ys, st("a", 0), st("a", 1), b_k_p, b_v_p, b_ki_p, b_k_s, b_v_s, b_ki_s,
            st("cc", 0), st("cs", 0), st("cc", 1), st("cs", 1), st("d", 0), st("d", 1),
            st("f", 0), st("f", 1), jnp.stack(m_kp), jnp.stack(m_vp))
```

```python
import functools
import math

import jax
import jax.numpy as jnp
from jax import lax
from jax.experimental import pallas as pl
from jax.experimental.pallas import tpu as pltpu

F32 = jnp.float32
BF16 = jnp.bfloat16
I32 = jnp.int32

D_MODEL = 1024
DEPTH = 4
PAST_LEN = 8192
PAGE_SIZE = 128
W_A = 31
H_B = 16
DH_B = 64
KV_B = 4
G_B = H_B // KV_B
H_IDX = 8
D_IDX = 64
TOPK = 256
N_BUCKETS = 32
MAX_DIST = 128
D_IN_C = 2 * D_MODEL
P_C = 64
H_C = D_IN_C // P_C
G_C = 4
N_SSM = 128
W_C = 4
CHUNK_C = 128
W_D = 3
N_MEM = 256
H_M = 4
DH_M = D_MODEL // H_M
D_FF = 2816
W_F = 3
EPS = 1e-6

V7X_SUBLANES = 8
V7X_LANES = 128
V7X_VMEM_LIMIT_BYTES = 56 * 1024 * 1024

SAMPLE_PAD = V7X_SUBLANES
NEG_BIG = -1e30
INT_MIN = -(2 ** 31)
PAGES_PER_STEP = 16


def _cparams(*sem):
    return pltpu.CompilerParams(dimension_semantics=sem, vmem_limit_bytes=V7X_VMEM_LIMIT_BYTES)


def _pick(n, cands):
    for c in cands:
        if n % c == 0:
            return c
    return n


def _rms(x, g):
    y = x * lax.rsqrt(jnp.mean(x * x, axis=-1, keepdims=True) + EPS)
    return y * g


def _dot(a, b):
    return jnp.dot(a, b, preferred_element_type=F32)


def _dot_t(a, b):
    return lax.dot_general(a, b, (((1,), (1,)), ((), ())), preferred_element_type=F32)


def _sigmoid(x):
    return 1.0 / (1.0 + jnp.exp(-x))


def _silu(x):
    return x * _sigmoid(x)


def _group_meansq(x, gsize):
    tm, c = x.shape
    x2 = x * x
    if gsize == c:
        return jnp.mean(x2, axis=-1, keepdims=True)
    if gsize % V7X_LANES == 0:
        parts = []
        for h in range(c // gsize):
            ms = jnp.mean(x2[:, h * gsize:(h + 1) * gsize], axis=-1, keepdims=True)
            parts.append(jnp.broadcast_to(ms, (tm, gsize)))
        return jnp.concatenate(parts, axis=-1)
    shift = int(math.log2(gsize))
    r = lax.shift_right_logical(lax.broadcasted_iota(I32, (c, c), 0), shift)
    q = lax.shift_right_logical(lax.broadcasted_iota(I32, (c, c), 1), shift)
    bd = (r == q).astype(BF16)
    hi = x2.astype(BF16)
    lo = (x2 - hi.astype(F32)).astype(BF16)
    return (_dot(hi, bd) + _dot(lo, bd)) * (1.0 / gsize)


def _group_rms(x, gain, gsize):
    return (x * lax.rsqrt(_group_meansq(x, gsize) + EPS)) * gain


def _linear_body(*refs, norm, bias, res):
    it = iter(refs)
    x_ref = next(it)
    g_ref = next(it) if norm else None
    w_ref = next(it)
    b_ref = next(it) if bias else None
    r_ref = next(it) if res else None
    o_ref = next(it)
    xn_ref = next(it)

    @pl.when(pl.program_id(1) == 0)
    def _():
        x = x_ref[...].astype(F32)
        if norm:
            x = _rms(x, g_ref[...])
        xn_ref[...] = x.astype(BF16)

    acc = _dot(xn_ref[...], w_ref[...])
    if bias:
        acc = acc + b_ref[...]
    if res:
        acc = acc + r_ref[...]
    o_ref[...] = acc.astype(o_ref.dtype)


def linear(x, w, *, g=None, b=None, res=None, out_dtype=F32, name="linear"):
    m, k = x.shape
    n = w.shape[1]
    tm = _pick(m, (1024, 512, 256, 128))
    tn = _pick(n, (1024, 1408, 512, 256, 128))
    in_specs = [pl.BlockSpec((tm, k), lambda i, j: (i, 0))]
    args = [x]
    if g is not None:
        in_specs.append(pl.BlockSpec((1, k), lambda i, j: (0, 0)))
        args.append(g.reshape(1, k).astype(F32))
    in_specs.append(pl.BlockSpec((k, tn), lambda i, j: (0, j)))
    args.append(w)
    if b is not None:
        in_specs.append(pl.BlockSpec((1, tn), lambda i, j: (0, j)))
        args.append(b.reshape(1, n).astype(F32))
    if res is not None:
        in_specs.append(pl.BlockSpec((tm, tn), lambda i, j: (i, j)))
        args.append(res)
    return pl.pallas_call(
        functools.partial(_linear_body, norm=g is not None, bias=b is not None, res=res is not None),
        grid=(m // tm, n // tn),
        in_specs=in_specs,
        out_specs=pl.BlockSpec((tm, tn), lambda i, j: (i, j)),
        out_shape=jax.ShapeDtypeStruct((m, n), out_dtype),
        scratch_shapes=[pltpu.VMEM((tm, k), BF16)],
        compiler_params=_cparams("arbitrary", "arbitrary"),
        name=name,
    )(*args)


def _head_norm_body(x_ref, g_ref, o_ref, ob_ref, *, gsize):
    y = _group_rms(x_ref[...], g_ref[...], gsize)
    o_ref[...] = y
    ob_ref[...] = y.astype(BF16)


def head_norm(x, col0, width, gain, gsize, name="head_norm"):
    m = x.shape[0]
    tm = _pick(m, (512, 256, 128))
    assert col0 % width == 0
    gt = jnp.tile(gain.astype(F32), width // gsize).reshape(1, width)
    return pl.pallas_call(
        functools.partial(_head_norm_body, gsize=gsize),
        grid=(m // tm,),
        in_specs=[pl.BlockSpec((tm, width), lambda i: (i, col0 // width)),
                  pl.BlockSpec((1, width), lambda i: (0, 0))],
        out_specs=[pl.BlockSpec((tm, width), lambda i: (i, 0)),
                   pl.BlockSpec((tm, width), lambda i: (i, 0))],
        out_shape=[jax.ShapeDtypeStruct((m, width), F32), jax.ShapeDtypeStruct((m, width), BF16)],
        compiler_params=_cparams("arbitrary"),
        name=name,
    )(x, gt)


def _seq_conv_body(*refs, n_in, n_par, pre, post, width, tm, lv, halo):
    in_refs = refs[:n_in]
    st_ref = refs[n_in]
    wc_ref = refs[n_in + 1]
    par_refs = refs[n_in + 2:n_in + 2 + n_par]
    o_ref, nst_ref, abuf = refs[n_in + 2 + n_par:]
    t = pl.program_id(1)
    base = halo - (width - 1)

    @pl.when(t == 0)
    def _():
        abuf[base:halo, :] = st_ref[0]

    tiles = [r[0] for r in in_refs]
    pars = [r[...] for r in par_refs]
    abuf[halo:halo + tm, :] = pre(tiles)
    acc = None
    for k in range(width):
        term = wc_ref[k:k + 1, :] * abuf[base + k:base + k + tm, :]
        acc = term if acc is None else acc + term
    o_ref[0] = post(acc, tiles, pars).astype(o_ref.dtype)
    tail = abuf[base + lv:base + lv + width - 1, :]
    abuf[base:halo, :] = tail
    nst_ref[0] = tail


def seq_conv(inputs, state, wconv, params, pre, post, c_conv, c_out, out_dtype, tm, lv, name):
    n_seq, length, _ = inputs[0][0].shape
    width = wconv.shape[0]
    halo = -(-(width - 1) // V7X_SUBLANES) * V7X_SUBLANES
    in_specs = [pl.BlockSpec((1, tm, bw), functools.partial(lambda s, t, ci: (s, t, ci), ci=ci))
                for (_, bw, ci) in inputs]
    in_specs.append(pl.BlockSpec((1, width - 1, c_conv), lambda s, t: (s, 0, 0)))
    in_specs.append(pl.BlockSpec((width, c_conv), lambda s, t: (0, 0)))
    for p in params:
        in_specs.append(pl.BlockSpec((1, p.shape[-1]), lambda s, t: (0, 0)))
    out, nst = pl.pallas_call(
        functools.partial(_seq_conv_body, n_in=len(inputs), n_par=len(params), pre=pre, post=post,
                          width=width, tm=tm, lv=lv, halo=halo),
        grid=(n_seq, length // tm),
        in_specs=in_specs,
        out_specs=[pl.BlockSpec((1, tm, c_out), lambda s, t: (s, t, 0)),
                   pl.BlockSpec((1, width - 1, c_conv), lambda s, t: (s, 0, 0))],
        out_shape=[jax.ShapeDtypeStruct((n_seq, length, c_out), out_dtype),
                   jax.ShapeDtypeStruct((n_seq, width - 1, c_conv), F32)],
        scratch_shapes=[pltpu.VMEM((halo + tm, c_conv), F32)],
        compiler_params=_cparams("arbitrary", "arbitrary"),
        name=name,
    )(*[a for (a, _, _) in inputs], state, wconv.astype(F32),
      *[p.reshape(1, -1).astype(F32) for p in params])
    return out, nst


def _a_pre(tiles):
    a, g = tiles
    return a * _sigmoid(g)


def _a_post(y, tiles, pars):
    bc, lng, lnb = pars
    y = y + bc
    mu = jnp.mean(y, axis=-1, keepdims=True)
    yc = y - mu
    yn = yc * lax.rsqrt(jnp.mean(yc * yc, axis=-1, keepdims=True) + EPS)
    return _silu(yn * lng + lnb)


def _f_pre(tiles):
    return tiles[0]


def _f_post(y, tiles, pars):
    return _silu(y + pars[0]) * tiles[1]


def _d_pre(tiles):
    return tiles[1] * tiles[2]


def _d_post(y, tiles, pars):
    return tiles[0] * y


def _c_pre(tiles):
    return tiles[0]


def _c_post(y, tiles, pars):
    return _silu(y + pars[0])


def _mem_attn_body(q_ref, k_ref, v_ref, g_ref, o_ref):
    q = q_ref[0]
    for h in range(H_M):
        sl = slice(h * DH_M, (h + 1) * DH_M)
        qh = (_rms(q[:, sl], g_ref[...]) * (DH_M ** -0.5)).astype(BF16)
        kh = k_ref[0, :, sl].astype(BF16)
        vh = v_ref[0, :, sl].astype(BF16)
        s = _dot_t(qh, kh)
        p = jnp.exp(s - jnp.max(s, axis=-1, keepdims=True))
        p = p / jnp.sum(p, axis=-1, keepdims=True)
        o_ref[0, :, sl] = _dot(p.astype(BF16), vh).astype(o_ref.dtype)


def mem_attn(q, k, v, q_gain, tm, kv_seq0=0, name="mem_attn"):
    n_seq, length, d = q.shape
    return pl.pallas_call(
        _mem_attn_body,
        grid=(n_seq, length // tm),
        in_specs=[pl.BlockSpec((1, tm, d), lambda s, t: (s, t, 0)),
                  pl.BlockSpec((1, N_MEM, d), lambda s, t: (kv_seq0 + s, 0, 0)),
                  pl.BlockSpec((1, N_MEM, d), lambda s, t: (kv_seq0 + s, 0, 0)),
                  pl.BlockSpec((1, DH_M), lambda s, t: (0, 0))],
        out_specs=pl.BlockSpec((1, tm, d), lambda s, t: (s, t, 0)),
        out_shape=jax.ShapeDtypeStruct((n_seq, length, d), BF16),
        compiler_params=_cparams("arbitrary", "arbitrary"),
        name=name,
    )(q, k, v, q_gain.reshape(1, DH_M).astype(F32))


HEADS_PER_PAIR = 2 * G_B


def _t5_lookup(tab_ref, h, dist):
    dist = jnp.maximum(dist, 0)
    max_exact = N_BUCKETS // 2
    df = jnp.maximum(dist, 1).astype(F32)
    large = max_exact + (jnp.log(df / max_exact) / math.log(MAX_DIST / max_exact)
                         * (N_BUCKETS - max_exact)).astype(I32)
    large = jnp.minimum(large, N_BUCKETS - 1)
    bucket = jnp.where(dist < max_exact, dist, large)
    acc = jnp.zeros(dist.shape, F32)
    for b in range(N_BUCKETS):
        acc = jnp.where(bucket == b, tab_ref[b, h], acc)
    return acc


def _t5_tiles_body(tab_ref, o_ref, ot_ref):
    h = pl.program_id(0)
    r = lax.broadcasted_iota(I32, (PAGE_SIZE, PAGE_SIZE), 0)
    c = lax.broadcasted_iota(I32, (PAGE_SIZE, PAGE_SIZE), 1)
    o_ref[0, :, 0:PAGE_SIZE] = _t5_lookup(tab_ref, h, r - c)
    o_ref[0, :, PAGE_SIZE:2 * PAGE_SIZE] = _t5_lookup(tab_ref, h, PAGE_SIZE + r - c)
    ot_ref[0, 0] = _t5_lookup(tab_ref, h, c - r)
    ot_ref[0, 1] = _t5_lookup(tab_ref, h, PAGE_SIZE + c - r)


def t5_tiles(table):
    return pl.pallas_call(
        _t5_tiles_body,
        grid=(H_B,),
        in_specs=[pl.BlockSpec(memory_space=pltpu.SMEM)],
        out_specs=[pl.BlockSpec((1, PAGE_SIZE, 2 * PAGE_SIZE), lambda h: (h, 0, 0)),
                   pl.BlockSpec((1, 2, PAGE_SIZE, PAGE_SIZE),
                                lambda h: (h // HEADS_PER_PAIR, 0, 0, h % HEADS_PER_PAIR))],
        out_shape=[jax.ShapeDtypeStruct((H_B, PAGE_SIZE, 2 * PAGE_SIZE), F32),
                   jax.ShapeDtypeStruct((H_B // HEADS_PER_PAIR, 2, PAGE_SIZE, HEADS_PER_PAIR * PAGE_SIZE), F32)],
        compiler_params=_cparams("arbitrary"),
        name="t5_tiles",
    )(table.astype(F32))


def _indexer_tile(qih, wcol, ki_tile):
    acc = jnp.zeros((qih[0].shape[0], ki_tile.shape[0]), F32)
    for h in range(H_IDX):
        acc = acc + jnp.maximum(_dot_t(qih[h], ki_tile), 0.0) * wcol[h]
    return acc


def _sort_key(score):
    score = jnp.where(score == 0.0, 0.0, score)
    u = lax.bitcast_convert_type(score, I32)
    return jnp.where(u < 0, u ^ 0x7FFFFFFF, u)


def _kth_largest(count_ge, shape, topk):
    def try_cand(cand, cur):
        return jnp.where(count_ge(cand) >= topk, cand, cur)

    t0 = try_cand(jnp.zeros(shape, I32), jnp.full(shape, INT_MIN, I32))

    def step(b, cur):
        cand = cur + lax.shift_left(jnp.int32(1), jnp.int32(30) - b)
        return try_cand(cand, cur)

    return lax.fori_loop(0, 31, step, t0)


def _strict_upper(n):
    r = lax.broadcasted_iota(I32, (n, n), 0)
    c = lax.broadcasted_iota(I32, (n, n), 1)
    return (r < c).astype(BF16)


def _strict_lower(n):
    r = lax.broadcasted_iota(I32, (n, n), 0)
    c = lax.broadcasted_iota(I32, (n, n), 1)
    return (c < r).astype(BF16)


def _fold_rows(x, op):
    parts = [x[r:r + V7X_SUBLANES] for r in range(0, x.shape[0], V7X_SUBLANES)]
    while len(parts) > 1:
        nxt = [op(parts[k], parts[k + 1]) for k in range(0, len(parts) - 1, 2)]
        if len(parts) % 2:
            nxt.append(parts[-1])
        parts = nxt
    return parts[0]


LOOP_UNROLL = 4


def _loop_tiles(n, body, carry, unroll=LOOP_UNROLL):
    shift = int(math.log2(unroll))
    n_main = lax.shift_right_logical(n, shift)
    carry = lax.fori_loop(0, n_main, lambda t, c: body(t * unroll, unroll, c), carry)
    return lax.fori_loop(n_main * unroll, n, lambda j, c: body(j, 1, c), carry)


def _dsa_prompt_body(q_ref, qi_ref, wi_ref, k_ref, vt_ref, ki_ref, bt_ref, o_ref,
                     key_ref, am_ref, lg_ref, acc_ref, ot_ref, *, topk):
    i = pl.program_id(1)
    nj = i + 1
    qb = PAGE_SIZE
    kpos = lax.broadcasted_iota(I32, (qb, qb), 0)
    qpos = lax.broadcasted_iota(I32, (qb, qb), 1)

    def kslice(j0, u=1):
        return pl.ds(pl.multiple_of(j0 * qb, qb), u * qb)

    qit = qi_ref[0].T
    qit = jnp.concatenate([qit[h * D_IDX:(h + 1) * D_IDX] for h in range(H_IDX)], axis=1).astype(BF16)
    wsc = (wi_ref[0] * (H_IDX ** -0.5)) * (D_IDX ** -0.5)
    wt = jnp.concatenate([wsc, jnp.zeros((qb, qb - H_IDX), F32)], axis=1).T
    wrow = jnp.concatenate([wt[h:h + 1] for h in range(H_IDX)], axis=1)

    def idx_step(j0, u, carry):
        s = jnp.maximum(_dot(ki_ref[0, kslice(j0, u), :], qit), 0.0) * wrow
        sc = s[:, 0:qb]
        for h in range(1, H_IDX):
            sc = sc + s[:, h * qb:(h + 1) * qb]
        key = _sort_key(sc)
        for t in range(u):
            valid = (j0 + t < i) | (kpos <= qpos)
            key_ref[j0 + t] = jnp.where(valid, key[t * qb:(t + 1) * qb], INT_MIN)
        return carry

    _loop_tiles(nj, idx_step, 0)

    def count_where(pred):
        def cstep(j0, u, c):
            for t in range(u):
                c = c + jnp.where(pred(key_ref[j0 + t]), 1.0, 0.0)
            return c
        c = _loop_tiles(nj, cstep, jnp.zeros((qb, qb), F32))
        return jnp.sum(_fold_rows(c, jnp.add), axis=0, keepdims=True)

    thr = _kth_largest(lambda cand: count_where(lambda key: key >= cand), (1, qb), topk)
    n_tie = topk - count_where(lambda key: key > thr)
    lower = _strict_lower(qb)

    def mask_step(j0, u, run):
        for t in range(u):
            j = j0 + t
            key = key_ref[j]
            valid = (j < i) | (kpos <= qpos)
            eq = valid & (key == thr)
            eqf = jnp.where(eq, 1.0, 0.0)
            rank = _dot(lower, eqf.astype(BF16)) + run
            sel = (key > thr) | (eq & (rank < n_tie))
            am_ref[j] = jnp.where(sel, 0.0, NEG_BIG)
            run = run + jnp.sum(_fold_rows(eqf, jnp.add), axis=0, keepdims=True)
        return run

    _loop_tiles(nj, mask_step, jnp.zeros((1, qb), F32))

    qt = q_ref[0].T
    pw = HEADS_PER_PAIR * qb
    gw = G_B * qb
    for pp in range(KV_B // 2):
        def group_qt(n):
            return jnp.concatenate([qt[h * DH_B:(h + 1) * DH_B] for h in range(n * G_B, (n + 1) * G_B)],
                                   axis=1)
        zero = jnp.zeros((DH_B, gw), F32)
        rhs = jnp.concatenate([jnp.concatenate([group_qt(2 * pp), zero], axis=1),
                               jnp.concatenate([zero, group_qt(2 * pp + 1)], axis=1)], axis=0)
        rhs = (rhs * (DH_B ** -0.5)).astype(BF16)
        klanes = slice(pp * 2 * DH_B, (pp + 1) * 2 * DH_B)
        far = bt_ref[pp, 1, 0:1, :]

        def tile_logits(j0, u, bias, mx):
            s = _dot(k_ref[0, kslice(j0, u), klanes], rhs)
            for t in range(u):
                st = s[t * qb:(t + 1) * qb] + bias + jnp.concatenate([am_ref[j0 + t]] * HEADS_PER_PAIR, axis=1)
                lg_ref[j0 + t] = st
                mx = jnp.maximum(mx, _fold_rows(st, jnp.maximum))
            return mx

        neg = jnp.full((V7X_SUBLANES, pw), NEG_BIG, F32)
        mx = _loop_tiles(jnp.maximum(i - 1, 0), lambda j0, u, m: tile_logits(j0, u, far, m), neg)
        mx = lax.cond(i >= 1, lambda m: tile_logits(i - 1, 1, bt_ref[pp, 1], m), lambda m: m, mx)
        mx = tile_logits(i, 1, bt_ref[pp, 0], mx)
        m = jnp.max(mx, axis=0, keepdims=True)
        acc_ref[...] = jnp.zeros(acc_ref.shape, F32)

        def p2(j0, u, l):
            ps = []
            for t in range(u):
                p = jnp.exp(lg_ref[j0 + t] - m)
                l = l + _fold_rows(p, jnp.add)
                ps.append(p.astype(BF16))
            pb = jnp.concatenate(ps, axis=0)
            vt = jnp.concatenate([vt_ref[0, j0 + t] for t in range(u)], axis=1)
            for gg in range(2):
                n = 2 * pp + gg
                acc_ref[gg] += _dot(vt[n * DH_B:(n + 1) * DH_B, :], pb[:, gg * gw:(gg + 1) * gw])
            return l

        l = _loop_tiles(nj, p2, jnp.zeros((V7X_SUBLANES, pw), F32))
        lsum = jnp.sum(l, axis=0, keepdims=True)
        for gg in range(2):
            o_t = acc_ref[gg] / lsum[:, gg * gw:(gg + 1) * gw]
            for hh in range(G_B):
                h = (2 * pp + gg) * G_B + hh
                ot_ref[h * DH_B:(h + 1) * DH_B, :] = o_t[:, hh * qb:(hh + 1) * qb]
    o_ref[0] = ot_ref[...].T.astype(o_ref.dtype)


def dsa_prompt(qn, qi, wi, kb, vtb, kib, btt):
    n_seq, length, _ = qn.shape
    nkb = length // PAGE_SIZE
    qmap = lambda s, i: (s, i, 0)
    kmap = lambda s, i: (s, 0, 0)
    pw = HEADS_PER_PAIR * PAGE_SIZE
    return pl.pallas_call(
        functools.partial(_dsa_prompt_body, topk=min(TOPK, length // 4)),
        grid=(n_seq, nkb),
        in_specs=[pl.BlockSpec((1, PAGE_SIZE, H_B * DH_B), qmap),
                  pl.BlockSpec((1, PAGE_SIZE, H_IDX * D_IDX), qmap),
                  pl.BlockSpec((1, PAGE_SIZE, H_IDX), qmap),
                  pl.BlockSpec((1, length, KV_B * DH_B), kmap),
                  pl.BlockSpec((1, nkb, KV_B * DH_B, PAGE_SIZE), lambda s, i: (s, 0, 0, 0)),
                  pl.BlockSpec((1, length, D_IDX), kmap),
                  pl.BlockSpec((H_B // HEADS_PER_PAIR, 2, PAGE_SIZE, pw), lambda s, i: (0, 0, 0, 0))],
        out_specs=pl.BlockSpec((1, PAGE_SIZE, H_B * DH_B), qmap),
        out_shape=jax.ShapeDtypeStruct((n_seq, length, H_B * DH_B), BF16),
        scratch_shapes=[pltpu.VMEM((nkb, PAGE_SIZE, PAGE_SIZE), I32),
                        pltpu.VMEM((nkb, PAGE_SIZE, PAGE_SIZE), F32),
                        pltpu.VMEM((nkb, PAGE_SIZE, pw), F32),
                        pltpu.VMEM((2, DH_B, G_B * PAGE_SIZE), F32),
                        pltpu.VMEM((H_B * DH_B, PAGE_SIZE), F32)],
        compiler_params=_cparams("arbitrary", "arbitrary"),
        name="dsa_prompt",
    )(qn, qi, wi, kb, vtb, kib, btt)


N_PAGES = PAST_LEN // PAGE_SIZE
N_KTILES = N_PAGES + 1
N_KTILES_PAD = -(-N_KTILES // PAGES_PER_STEP) * PAGES_PER_STEP


def _dsa_sample_sel_body(pt_ref, qi_ref, wi_ref, kin_ref, *rest, lv, topk):
    page_refs = rest[:PAGES_PER_STEP]
    am_ref, key_ref, qs_ref, ws_ref = rest[PAGES_PER_STEP:]
    c = pl.program_id(1)
    nsteps = pl.num_programs(1)
    r8 = SAMPLE_PAD

    @pl.when(c == 0)
    def _():
        qi = qi_ref[0]
        wsc = (wi_ref[0] * (H_IDX ** -0.5)) * (D_IDX ** -0.5)
        qs_ref[...] = jnp.concatenate([qi[:, h * D_IDX:(h + 1) * D_IDX] for h in range(H_IDX)],
                                      axis=0).astype(BF16)
        ws_ref[...] = jnp.concatenate([jnp.broadcast_to(wsc[:, h:h + 1], (r8, PAGE_SIZE))
                                       for h in range(H_IDX)], axis=0)

    def score(ki_tile):
        s = jnp.maximum(_dot_t(qs_ref[...], ki_tile), 0.0) * ws_ref[...]
        return _fold_rows(s, jnp.add)

    for r in range(PAGES_PER_STEP):
        key_ref[c * PAGES_PER_STEP + r] = _sort_key(score(page_refs[r][0].astype(BF16)))

    @pl.when(c == nsteps - 1)
    def _():
        rows = lax.broadcasted_iota(I32, (r8, PAGE_SIZE), 0)
        cols = lax.broadcasted_iota(I32, (r8, PAGE_SIZE), 1)
        new_valid = (cols <= rows) & (cols < lv)
        kin = jnp.concatenate([kin_ref[0], jnp.zeros((PAGE_SIZE - r8, D_IDX), F32)], axis=0)
        key_ref[N_PAGES] = jnp.where(new_valid, _sort_key(score(kin.astype(BF16))), INT_MIN)

        def count_ge(cand):
            def cstep(j, acc):
                return acc + jnp.where(key_ref[j] >= cand, 1.0, 0.0)
            acc = lax.fori_loop(0, N_KTILES, cstep, jnp.zeros((r8, PAGE_SIZE), F32))
            return jnp.sum(acc, axis=-1, keepdims=True)

        thr = _kth_largest(count_ge, (r8, 1), topk)

        def gt_step(j, acc):
            return acc + jnp.where(key_ref[j] > thr, 1.0, 0.0)

        n_gt = jnp.sum(lax.fori_loop(0, N_KTILES, gt_step, jnp.zeros((r8, PAGE_SIZE), F32)),
                       axis=-1, keepdims=True)
        n_tie = topk - n_gt
        upper = _strict_upper(PAGE_SIZE)

        def mask_step(j, run):
            key = key_ref[j]
            valid = (j < N_PAGES) | new_valid
            eq = valid & (key == thr)
            eqf = jnp.where(eq, 1.0, 0.0)
            rank = _dot(eqf.astype(BF16), upper) + run
            sel = (key > thr) | (eq & (rank < n_tie))
            am_ref[0, j] = jnp.where(sel, 0.0, NEG_BIG)
            return run + jnp.sum(eqf, axis=-1, keepdims=True)

        lax.fori_loop(0, N_KTILES, mask_step, jnp.zeros((r8, 1), F32))
        for j in range(N_KTILES, N_KTILES_PAD):
            am_ref[0, j] = jnp.full((r8, PAGE_SIZE), NEG_BIG, F32)


def dsa_sample_select(page_table, qi, wi, kin, kidx_pool, lv):
    n_seq = qi.shape[0]
    r8 = SAMPLE_PAD
    qmap = lambda s, c, pt: (s, 0, 0)
    page_specs = [pl.BlockSpec((1, PAGE_SIZE, D_IDX),
                               functools.partial(lambda s, c, pt, r: (pt[s, c * PAGES_PER_STEP + r], 0, 0), r=r))
                  for r in range(PAGES_PER_STEP)]
    grid_spec = pltpu.PrefetchScalarGridSpec(
        num_scalar_prefetch=1,
        grid=(n_seq, N_PAGES // PAGES_PER_STEP),
        in_specs=[pl.BlockSpec((1, r8, H_IDX * D_IDX), qmap),
                  pl.BlockSpec((1, r8, H_IDX), qmap),
                  pl.BlockSpec((1, r8, D_IDX), qmap)] + page_specs,
        out_specs=pl.BlockSpec((1, N_KTILES_PAD, r8, PAGE_SIZE), lambda s, c, pt: (s, 0, 0, 0)),
        scratch_shapes=[pltpu.VMEM((N_KTILES, r8, PAGE_SIZE), I32),
                        pltpu.VMEM((H_IDX * r8, D_IDX), BF16),
                        pltpu.VMEM((H_IDX * r8, PAGE_SIZE), F32)],
    )
    return pl.pallas_call(
        functools.partial(_dsa_sample_sel_body, lv=lv, topk=min(TOPK, (PAST_LEN + lv) // 4)),
        grid_spec=grid_spec,
        out_shape=jax.ShapeDtypeStruct((n_seq, N_KTILES_PAD, r8, PAGE_SIZE), F32),
        compiler_params=_cparams("arbitrary", "arbitrary"),
        name="dsa_sample_select",
    )(page_table, qi, wi, kin, *([kidx_pool] * PAGES_PER_STEP))


def _dsa_sample_attn_body(pt_ref, q_ref, kn_ref, vn_ref, am_ref, amn_ref, bt_ref, *rest):
    kp_refs = rest[:PAGES_PER_STEP]
    vp_refs = rest[PAGES_PER_STEP:2 * PAGES_PER_STEP]
    o_ref, m_ref, l_ref, acc_ref = rest[2 * PAGES_PER_STEP:]
    c = pl.program_id(1)
    nsteps = pl.num_programs(1)
    r8 = SAMPLE_PAD
    gr = G_B * r8

    @pl.when(c == 0)
    def _():
        m_ref[...] = jnp.full(m_ref.shape, NEG_BIG, F32)
        l_ref[...] = jnp.zeros(l_ref.shape, F32)
        acc_ref[...] = jnp.zeros(acc_ref.shape, F32)

    q = q_ref[0]
    qgs, b_subs, b_diags, b_fars = [], [], [], []
    for n in range(KV_B):
        heads = [n * G_B + hh for hh in range(G_B)]
        qg = jnp.concatenate([q[:, h * DH_B:(h + 1) * DH_B] for h in heads], axis=0)
        qgs.append((qg * (DH_B ** -0.5)).astype(BF16))
        b_diags.append(jnp.concatenate([bt_ref[h, :, 0:PAGE_SIZE] for h in heads], axis=0))
        b_subs.append(jnp.concatenate([bt_ref[h, :, PAGE_SIZE:2 * PAGE_SIZE] for h in heads], axis=0))
        b_fars.append(jnp.concatenate(
            [jnp.broadcast_to(bt_ref[h, 0:1, PAGE_SIZE:PAGE_SIZE + 1], (r8, PAGE_SIZE))
             for h in heads], axis=0))

    def update(n, s, v_tile):
        m_old = m_ref[n]
        m_new = jnp.maximum(m_old, jnp.max(s, axis=-1, keepdims=True))
        alpha = jnp.exp(m_old - m_new)
        p = jnp.exp(s - m_new[:, 0:1])
        l_ref[n] = alpha * l_ref[n] + jnp.sum(p, axis=-1, keepdims=True)
        acc_ref[n] = alpha[:, 0:DH_B] * acc_ref[n] + _dot(p.astype(BF16), v_tile)
        m_ref[n] = m_new

    kcat = jnp.concatenate([kp_refs[r][0] for r in range(PAGES_PER_STEP)], axis=0).astype(BF16)
    vcat = jnp.concatenate([vp_refs[r][0] for r in range(PAGES_PER_STEP)], axis=0).astype(BF16)
    amcat = jnp.concatenate([am_ref[0, r] for r in range(PAGES_PER_STEP)], axis=1)
    am4 = jnp.concatenate([amcat] * G_B, axis=0)
    is_last_step = c == nsteps - 1
    for n in range(KV_B):
        ksl = slice(n * DH_B, (n + 1) * DH_B)
        bias = jnp.concatenate([b_fars[n]] * (PAGES_PER_STEP - 1)
                               + [jnp.where(is_last_step, b_subs[n], b_fars[n])], axis=1)
        s = _dot_t(qgs[n], kcat[:, ksl]) + bias + am4
        update(n, s, vcat[:, ksl])

    @pl.when(c == nsteps - 1)
    def _():
        pad = jnp.zeros((PAGE_SIZE - r8, KV_B * DH_B), F32)
        kn = jnp.concatenate([kn_ref[0], pad], axis=0).astype(BF16)
        vn = jnp.concatenate([vn_ref[0], pad], axis=0).astype(BF16)
        am4 = jnp.concatenate([amn_ref[0, 0]] * G_B, axis=0)
        for n in range(KV_B):
            ksl = slice(n * DH_B, (n + 1) * DH_B)
            s = _dot_t(qgs[n], kn[:, ksl]) + b_diags[n] + am4
            update(n, s, vn[:, ksl])
            o = acc_ref[n] / l_ref[n][:, 0:DH_B]
            for hh in range(G_B):
                h = n * G_B + hh
                o_ref[0, :, h * DH_B:(h + 1) * DH_B] = o[hh * r8:(hh + 1) * r8].astype(o_ref.dtype)


def dsa_sample_attend(page_table, qn, kn, vn, amask, bt, k_pool, v_pool):
    n_seq = qn.shape[0]
    r8 = SAMPLE_PAD
    qmap = lambda s, c, pt: (s, 0, 0)
    page_map = [functools.partial(lambda s, c, pt, r: (pt[s, c * PAGES_PER_STEP + r], 0, 0), r=r)
                for r in range(PAGES_PER_STEP)]
    kv_w = KV_B * DH_B
    grid_spec = pltpu.PrefetchScalarGridSpec(
        num_scalar_prefetch=1,
        grid=(n_seq, N_PAGES // PAGES_PER_STEP),
        in_specs=[pl.BlockSpec((1, r8, H_B * DH_B), qmap),
                  pl.BlockSpec((1, r8, kv_w), qmap),
                  pl.BlockSpec((1, r8, kv_w), qmap),
                  pl.BlockSpec((1, PAGES_PER_STEP, r8, PAGE_SIZE), lambda s, c, pt: (s, c, 0, 0)),
                  pl.BlockSpec((1, PAGES_PER_STEP, r8, PAGE_SIZE),
                               lambda s, c, pt: (s, N_PAGES // PAGES_PER_STEP, 0, 0)),
                  pl.BlockSpec((H_B, r8, 2 * PAGE_SIZE), lambda s, c, pt: (0, 0, 0))]
                 + [pl.BlockSpec((1, PAGE_SIZE, kv_w), m) for m in page_map]
                 + [pl.BlockSpec((1, PAGE_SIZE, kv_w), m) for m in page_map],
        out_specs=pl.BlockSpec((1, r8, H_B * DH_B), qmap),
        scratch_shapes=[pltpu.VMEM((KV_B, G_B * r8, PAGE_SIZE), F32),
                        pltpu.VMEM((KV_B, G_B * r8, PAGE_SIZE), F32),
                        pltpu.VMEM((KV_B, G_B * r8, DH_B), F32)],
    )
    return pl.pallas_call(
        _dsa_sample_attn_body,
        grid_spec=grid_spec,
        out_shape=jax.ShapeDtypeStruct((n_seq, r8, H_B * DH_B), BF16),
        compiler_params=_cparams("arbitrary", "arbitrary"),
        name="dsa_sample_attend",
    )(page_table, qn, kn, vn, amask, amask, bt,
      *([k_pool] * PAGES_PER_STEP), *([v_pool] * PAGES_PER_STEP))


HEADS_PER_GROUP = H_C // G_C
GROUP_W = D_IN_C // G_C


def _cumsum_rows(x):
    n = x.shape[0]
    idx = lax.broadcasted_iota(I32, x.shape, 0)
    d = 1
    while d < n:
        x = x + jnp.where(idx >= d, pltpu.roll(x, d, 0), 0.0)
        d *= 2
    return x


def _cumsum_lanes(x):
    n = x.shape[1]
    idx = lax.broadcasted_iota(I32, x.shape, 1)
    d = 1
    while d < n:
        x = x + jnp.where(idx >= d, pltpu.roll(x, d, 1), 0.0)
        d *= 2
    return x


def _softplus(x):
    return jnp.maximum(x, 0.0) + jnp.log1p(jnp.exp(-jnp.abs(x)))


def _ssd_body(x_ref, xs_ref, bm_ref, cm_ref, z_ref, ng_ref, wdt_ref, wdtt_ref, dtb_ref, dtbt_ref,
              alog_ref, alogt_ref, dsk_ref, cn_ref, wo_ref, h0_ref, o_ref, hst_ref,
              dt_ref, dtt_ref, *, qin, lv):
    c = pl.program_id(1)
    g = pl.program_id(2)
    q = CHUNK_C

    def pad_rows(a):
        if qin == q:
            return a
        return jnp.concatenate([a, jnp.zeros((q - qin, a.shape[1]), a.dtype)], axis=0)

    @pl.when(c == 0)
    def _():
        for j in range(HEADS_PER_GROUP):
            hst_ref[0, g * HEADS_PER_GROUP + j] = h0_ref[0, j]

    @pl.when(g == 0)
    def _():
        xn = _rms(pad_rows(x_ref[0]), ng_ref[...]).astype(BF16)
        dt = _softplus(_dot(xn, wdt_ref[...]) + dtb_ref[...])
        dtt = _softplus(_dot_t(wdtt_ref[...], xn) + dtbt_ref[...])
        if lv < q:
            rows = lax.broadcasted_iota(I32, dt.shape, 0)
            lanes = lax.broadcasted_iota(I32, dtt.shape, 1)
            dt = jnp.where(rows < lv, dt, 0.0)
            dtt = jnp.where(lanes < lv, dtt, 0.0)
        dt_ref[...] = dt
        dtt_ref[...] = dtt

    dt = dt_ref[...]
    dtt = dtt_ref[...]
    acs = _cumsum_rows(dt * (-jnp.exp(alog_ref[...])))
    acst = _cumsum_lanes(dtt * (-jnp.exp(alogt_ref[...])))
    xs = pad_rows(xs_ref[0])
    bm = pad_rows(bm_ref[0])
    cm = pad_rows(cm_ref[0])
    xst = xs.T
    cmb = cm.astype(BF16)
    cb = _dot_t(cmb, bm.astype(BF16))
    rows = lax.broadcasted_iota(I32, (q, q), 0)
    cols = lax.broadcasted_iota(I32, (q, q), 1)
    tril = cols <= rows
    ys = []
    for j in range(HEADS_PER_GROUP):
        hd = g * HEADS_PER_GROUP + j
        onehot_c = lax.broadcasted_iota(I32, (1, H_C), 1) == hd
        onehot_r = lax.broadcasted_iota(I32, (H_C, 1), 0) == hd
        acs_col = jnp.sum(jnp.where(onehot_c, acs, 0.0), axis=1, keepdims=True)
        dt_col = jnp.sum(jnp.where(onehot_c, dt, 0.0), axis=1, keepdims=True)
        acs_row = jnp.sum(jnp.where(onehot_r, acst, 0.0), axis=0, keepdims=True)
        dt_row = jnp.sum(jnp.where(onehot_r, dtt, 0.0), axis=0, keepdims=True)
        a_last = acs_row[:, q - 1:q]
        xh = xs[:, j * P_C:(j + 1) * P_C] * dt_col
        xht = xst[j * P_C:(j + 1) * P_C, :] * dt_row
        lmat = jnp.exp(jnp.where(tril, acs_col - acs_row, -jnp.inf))
        y = _dot((cb * lmat).astype(BF16), xh.astype(BF16))
        h_old = hst_ref[0, hd]
        y = y + _dot_t(cmb, h_old.astype(BF16)) * jnp.exp(acs_col)
        decay = jnp.exp(a_last - acs_col)
        st = _dot(xht.astype(BF16), (bm * decay).astype(BF16))
        hst_ref[0, hd] = jnp.exp(a_last) * h_old + st
        ys.append(y)
    y = jnp.concatenate(ys, axis=1) + xs * dsk_ref[...]
    y = y * _silu(pad_rows(z_ref[0]))
    y = _rms(y, cn_ref[...]).astype(BF16)
    part = _dot(y, wo_ref[...])[0:qin]

    @pl.when(g == 0)
    def _():
        o_ref[0] = x_ref[0] + part

    @pl.when(g != 0)
    def _():
        o_ref[0] = o_ref[0] + part


def ssd_mix(x, xbc, proj, z_col0, h0, norm_g, w_dt, dt_bias, a_log, d_skip, c_norm, w_out, qin, lv):
    n_seq, length, d = x.shape
    nchunk = length // qin
    gw = GROUP_W
    col = lambda a: a.reshape(-1, 1).astype(F32)
    row = lambda a: a.reshape(1, -1).astype(F32)
    cmap = lambda off: (lambda s, c, g: (s, c, off + g))
    full2 = lambda s, c, g: (0, 0)
    in_specs = [
        pl.BlockSpec((1, qin, d), lambda s, c, g: (s, c, 0)),
        pl.BlockSpec((1, qin, gw), cmap(0)),
        pl.BlockSpec((1, qin, N_SSM), cmap(D_IN_C // N_SSM)),
        pl.BlockSpec((1, qin, N_SSM), cmap(D_IN_C // N_SSM + G_C)),
        pl.BlockSpec((1, qin, gw), cmap(z_col0 // gw)),
        pl.BlockSpec((1, d), full2),
        pl.BlockSpec((d, H_C), full2),
        pl.BlockSpec((H_C, d), full2),
        pl.BlockSpec((1, H_C), full2),
        pl.BlockSpec((H_C, 1), full2),
        pl.BlockSpec((1, H_C), full2),
        pl.BlockSpec((H_C, 1), full2),
        pl.BlockSpec((1, gw), lambda s, c, g: (0, g)),
        pl.BlockSpec((1, gw), lambda s, c, g: (0, g)),
        pl.BlockSpec((gw, d), lambda s, c, g: (g, 0)),
        pl.BlockSpec((1, HEADS_PER_GROUP, P_C, N_SSM), lambda s, c, g: (s, g, 0, 0)),
    ]
    out, hst = pl.pallas_call(
        functools.partial(_ssd_body, qin=qin, lv=lv),
        grid=(n_seq, nchunk, G_C),
        in_specs=in_specs,
        out_specs=[pl.BlockSpec((1, qin, d), lambda s, c, g: (s, c, 0)),
                   pl.BlockSpec((1, H_C, P_C, N_SSM), lambda s, c, g: (s, 0, 0, 0))],
        out_shape=[jax.ShapeDtypeStruct((n_seq, length, d), F32),
                   jax.ShapeDtypeStruct((n_seq, H_C, P_C, N_SSM), F32)],
        scratch_shapes=[pltpu.VMEM((CHUNK_C, H_C), F32), pltpu.VMEM((H_C, CHUNK_C), F32)],
        compiler_params=_cparams("arbitrary", "arbitrary", "arbitrary"),
        name="ssd_mix",
    )(x, xbc, xbc, xbc, proj, row(norm_g), w_dt.astype(BF16), w_dt.T.astype(BF16),
      row(dt_bias), col(dt_bias), row(a_log), col(a_log),
      row(jnp.repeat(d_skip, P_C)), row(c_norm), w_out, h0)
    return out, hst


def _flat(a):
    return a.reshape(-1, a.shape[-1])


def _unflat(a, like):
    return a.reshape(like.shape[0], like.shape[1], a.shape[-1])


def kernel(x_prompt, x_sample, mem_prompt, state_a_conv, cache_b_k, cache_b_v, cache_b_kidx, state_c_conv, state_c_ssm, state_d_conv, state_ffn_conv, cache_mem_k, cache_mem_v, page_table, rel_bias, norm_mix, norm_mem, norm_ffn, norm_memtok, a_w_in, a_b_in, a_w_conv, a_b_conv, a_ln_g, a_ln_b, a_w_out, b_w_in, b_w_out, b_q_norm, b_k_norm, b_kidx_norm, c_w_in, c_w_conv, c_b_conv, c_dt_bias, c_a_log, c_d_skip, c_norm, c_w_out, d_w_in, d_w_conv, d_w_out, m_w_q, m_w_kv, m_w_o, m_q_norm, m_k_norm, f_w_in, f_w_conv, f_b_conv, f_w_out):
    n_p, l_p, d = x_prompt.shape
    n_s, l_s, _ = x_sample.shape
    bf = lambda w: w.astype(BF16)
    xs_pad = jnp.pad(x_sample, ((0, 0), (0, SAMPLE_PAD - l_s), (0, 0)))
    groups = [dict(x=x_prompt, tm=256, lv=256, qin=CHUNK_C, lvq=CHUNK_C, prompt=True),
              dict(x=xs_pad, tm=SAMPLE_PAD, lv=l_s, qin=SAMPLE_PAD, lvq=l_s, prompt=False)]
    bt, btt = t5_tiles(rel_bias)
    mem_flat = _flat(mem_prompt)
    outs = {k: [[], []] for k in ("a", "bk", "bv", "bki", "cc", "cs", "d", "f")}
    m_kp, m_vp = [], []

    for i in range(DEPTH):
        j = i // 4
        kind = i % 4
        kv = linear(mem_flat, bf(m_w_kv[i]), g=norm_memtok[i], name="mem_kv")
        mk, _ = head_norm(kv, 0, d, m_k_norm[i], DH_M, name="mem_k_norm")
        mk = mk.reshape(n_p, N_MEM, d)
        mv = kv[:, d:].reshape(n_p, N_MEM, d)
        m_kp.append(mk.reshape(n_p, N_MEM, H_M, DH_M))
        m_vp.append(mv.reshape(n_p, N_MEM, H_M, DH_M))

        for gi, grp in enumerate(groups):
            x = grp["x"]
            n_seq, length, _ = x.shape
            prompt = grp["prompt"]
            zeros_state = lambda w, c: jnp.zeros((n_seq, w - 1, c), F32)
            xf = _flat(x)
            if kind == 0:
                ag = _unflat(linear(xf, bf(a_w_in[j]), g=norm_mix[i], b=a_b_in[j], name="a_in"), x)
                st = zeros_state(W_A, d) if prompt else state_a_conv[j]
                u, nst = seq_conv([(ag, d, 0), (ag, d, 1)], st, a_w_conv[j],
                                  [a_b_conv[j], a_ln_g[j], a_ln_b[j]], _a_pre, _a_post,
                                  d, d, BF16, grp["tm"], grp["lv"], "a_conv")
                outs["a"][gi].append(nst)
                x = _unflat(linear(_flat(u), bf(a_w_out[j]), res=xf, name="a_out"), x)
            elif kind == 1:
                w_in = b_w_in[j]
                n_main = H_B * DH_B + 2 * KV_B * DH_B + H_IDX * D_IDX
                w_tail = jnp.pad(w_in[:, n_main:], ((0, 0), (0, V7X_LANES - (w_in.shape[1] - n_main))))
                pm = linear(xf, bf(w_in[:, :n_main]), g=norm_mix[i], name="b_in")
                ptl = linear(xf, bf(w_tail), g=norm_mix[i], name="b_in_tail")
                kw = KV_B * DH_B
                qn, _ = head_norm(pm, 0, H_B * DH_B, b_q_norm[j], DH_B, name="b_q_norm")
                kn, knb = head_norm(pm, H_B * DH_B, kw, b_k_norm[j], DH_B, name="b_k_norm")
                kin, kinb = head_norm(ptl[:, :D_IDX], 0, D_IDX, b_kidx_norm[j], D_IDX, name="b_ki_norm")
                v = pm[:, H_B * DH_B + kw:H_B * DH_B + 2 * kw]
                qi = pm[:, H_B * DH_B + 2 * kw:]
                wi = ptl[:, D_IDX:D_IDX + H_IDX]
                r3 = lambda a: a.reshape(n_seq, length, a.shape[-1])
                if prompt:
                    vtb = bf(v).reshape(n_seq, length // PAGE_SIZE, PAGE_SIZE, kw).transpose(0, 1, 3, 2)
                    o = dsa_prompt(r3(qn), r3(qi), r3(wi), r3(knb), vtb, r3(kinb), btt)
                else:
                    am = dsa_sample_select(page_table, r3(qi), r3(wi), r3(kin), cache_b_kidx[j], grp["lv"])
                    n_pool = cache_b_k.shape[1]
                    o = dsa_sample_attend(page_table, r3(qn), r3(kn), r3(v), am, bt,
                                          cache_b_k[j].reshape(n_pool, PAGE_SIZE, kw),
                                          cache_b_v[j].reshape(n_pool, PAGE_SIZE, kw))
                outs["bk"][gi].append(r3(kn))
                outs["bv"][gi].append(r3(v))
                outs["bki"][gi].append(r3(kin))
                x = _unflat(linear(_flat(o), bf(b_w_out[j]), res=xf, name="b_out"), x)
            elif kind == 2:
                w_in = c_w_in[j]
                conv_c = D_IN_C + 2 * G_C * N_SSM
                w_main = jnp.concatenate([w_in[:, D_IN_C:D_IN_C + conv_c], w_in[:, :D_IN_C]], axis=1)
                proj = _unflat(linear(xf, bf(w_main), g=norm_mix[i], name="c_in"), x)
                st = zeros_state(W_C, conv_c) if prompt else state_c_conv[j]
                xbc, nst = seq_conv([(proj, conv_c, 0)], st, c_w_conv[j], [c_b_conv[j]], _c_pre, _c_post,
                                    conv_c, conv_c, F32, grp["tm"], grp["lv"], "c_conv")
                h0 = jnp.zeros((n_seq, H_C, P_C, N_SSM), F32) if prompt else state_c_ssm[j]
                x, hst = ssd_mix(x, xbc, proj, conv_c, h0, norm_mix[i], w_in[:, D_IN_C + conv_c:],
                                 c_dt_bias[j], c_a_log[j], c_d_skip[j], c_norm[j], bf(c_w_out[j]),
                                 grp["qin"], grp["lvq"])
                outs["cc"][gi].append(nst)
                outs["cs"][gi].append(hst)
            else:
                p3 = _unflat(linear(xf, bf(d_w_in[j]), g=norm_mix[i], name="d_in"), x)
                st = zeros_state(W_D, d) if prompt else state_d_conv[j]
                u, nst = seq_conv([(p3, d, 0), (p3, d, 1), (p3, d, 2)], st, d_w_conv[j], [],
                                  _d_pre, _d_post, d, d, BF16, grp["tm"], grp["lv"], "d_conv")
                outs["d"][gi].append(nst)
                x = _unflat(linear(_flat(u), bf(d_w_out[j]), res=xf, name="d_out"), x)

            xf = _flat(x)
            qm = _unflat(linear(xf, bf(m_w_q[i]), g=norm_mem[i], name="mem_q"), x)
            if prompt:
                kk, vv, tma, seq0 = mk, mv, 512, 0
            else:
                kk = cache_mem_k.reshape(DEPTH * n_seq, N_MEM, d)
                vv = cache_mem_v.reshape(DEPTH * n_seq, N_MEM, d)
                tma, seq0 = SAMPLE_PAD, i * n_seq
            om = mem_attn(qm, kk, vv, m_q_norm[i], tma, seq0)
            x = _unflat(linear(_flat(om), bf(m_w_o[i]), res=xf, name="mem_o"), x)

            xf = _flat(x)
            hf = _unflat(linear(xf, bf(f_w_in[i]), g=norm_ffn[i], name="ffn_in"), x)
            st = zeros_state(W_F, D_FF) if prompt else state_ffn_conv[i]
            uf, nst = seq_conv([(hf, D_FF, 0), (hf, D_FF, 1)], st, f_w_conv[i], [f_b_conv[i]],
                               _f_pre, _f_post, D_FF, D_FF, BF16, grp["tm"], grp["lv"], "ffn_conv")
            outs["f"][gi].append(nst)
            x = _unflat(linear(_flat(uf), bf(f_w_out[i]), res=xf, name="ffn_out"), x)
            grp["x"] = x

    yp = groups[0]["x"]
    ys = groups[1]["x"][:, :l_s]
    st = lambda key, gi: jnp.stack(outs[key][gi])
    kvshape = lambda a, n, l: a.reshape(a.shape[0], n, -1, a.shape[-1])[:, :, :l]
    b_k_p = kvshape(st("bk", 0), n_p, l_p).reshape(-1, n_p, l_p, KV_B, DH_B)
    b_v_p = kvshape(st("bv", 0), n_p, l_p).reshape(-1, n_p, l_p, KV_B, DH_B)
    b_ki_p = kvshape(st("bki", 0), n_p, l_p)
    b_k_s = kvshape(st("bk", 1), n_s, l_s).reshape(-1, n_s, l_s, KV_B, DH_B)
    b_v_s = kvshape(st("bv", 1), n_s, l_s).reshape(-1, n_s, l_s, KV_B, DH_B)
    b_ki_s = kvshape(st("bki", 1), n_s, l_s)
    return (yp, ys, st("a", 0), st("a", 1), b_k_p, b_v_p, b_ki_p, b_k_s, b_v_s, b_ki_s,
            st("cc", 0), st("cs", 0), st("cc", 1), st("cs", 1), st("d", 0), st("d", 1),
            st("f", 0), st("f", 1), jnp.stack(m_kp), jnp.stack(m_vp))
```

```python
import functools
import math

import jax
import jax.numpy as jnp
from jax import lax
from jax.experimental import pallas as pl
from jax.experimental.pallas import tpu as pltpu

F32 = jnp.float32
BF16 = jnp.bfloat16
I32 = jnp.int32

D_MODEL = 1024
DEPTH = 4
PAST_LEN = 8192
PAGE_SIZE = 128
W_A = 31
H_B = 16
DH_B = 64
KV_B = 4
G_B = H_B // KV_B
H_IDX = 8
D_IDX = 64
TOPK = 256
N_BUCKETS = 32
MAX_DIST = 128
D_IN_C = 2 * D_MODEL
P_C = 64
H_C = D_IN_C // P_C
G_C = 4
N_SSM = 128
W_C = 4
CHUNK_C = 128
W_D = 3
N_MEM = 256
H_M = 4
DH_M = D_MODEL // H_M
D_FF = 2816
W_F = 3
EPS = 1e-6

V7X_SUBLANES = 8
V7X_LANES = 128
V7X_VMEM_LIMIT_BYTES = 56 * 1024 * 1024

SAMPLE_PAD = V7X_SUBLANES
NEG_BIG = -1e30
INT_MIN = -(2 ** 31)
PAGES_PER_STEP = 16


def _cparams(*sem):
    return pltpu.CompilerParams(dimension_semantics=sem, vmem_limit_bytes=V7X_VMEM_LIMIT_BYTES)


def _pick(n, cands):
    for c in cands:
        if n % c == 0:
            return c
    return n


def _rms(x, g):
    y = x * lax.rsqrt(jnp.mean(x * x, axis=-1, keepdims=True) + EPS)
    return y * g


def _dot(a, b):
    return jnp.dot(a, b, preferred_element_type=F32)


def _dot_t(a, b):
    return lax.dot_general(a, b, (((1,), (1,)), ((), ())), preferred_element_type=F32)


def _sigmoid(x):
    return 1.0 / (1.0 + jnp.exp(-x))


def _silu(x):
    return x * _sigmoid(x)


def _group_meansq(x, gsize):
    tm, c = x.shape
    x2 = x * x
    if gsize == c:
        return jnp.mean(x2, axis=-1, keepdims=True)
    if gsize % V7X_LANES == 0:
        parts = []
        for h in range(c // gsize):
            ms = jnp.mean(x2[:, h * gsize:(h + 1) * gsize], axis=-1, keepdims=True)
            parts.append(jnp.broadcast_to(ms, (tm, gsize)))
        return jnp.concatenate(parts, axis=-1)
    shift = int(math.log2(gsize))
    r = lax.shift_right_logical(lax.broadcasted_iota(I32, (c, c), 0), shift)
    q = lax.shift_right_logical(lax.broadcasted_iota(I32, (c, c), 1), shift)
    bd = (r == q).astype(BF16)
    hi = x2.astype(BF16)
    lo = (x2 - hi.astype(F32)).astype(BF16)
    return (_dot(hi, bd) + _dot(lo, bd)) * (1.0 / gsize)


def _group_rms(x, gain, gsize):
    return (x * lax.rsqrt(_group_meansq(x, gsize) + EPS)) * gain


def _linear_body(*refs, norm, bias, res):
    it = iter(refs)
    x_ref = next(it)
    g_ref = next(it) if norm else None
    w_ref = next(it)
    b_ref = next(it) if bias else None
    r_ref = next(it) if res else None
    o_ref = next(it)
    xn_ref = next(it)

    @pl.when(pl.program_id(1) == 0)
    def _():
        x = x_ref[...].astype(F32)
        if norm:
            x = _rms(x, g_ref[...])
        xn_ref[...] = x.astype(BF16)

    acc = _dot(xn_ref[...], w_ref[...])
    if bias:
        acc = acc + b_ref[...]
    if res:
        acc = acc + r_ref[...]
    o_ref[...] = acc.astype(o_ref.dtype)


def linear(x, w, *, g=None, b=None, res=None, out_dtype=F32, name="linear"):
    m, k = x.shape
    n = w.shape[1]
    tm = _pick(m, (1024, 512, 256, 128))
    tn = _pick(n, (1024, 1408, 512, 256, 128))
    in_specs = [pl.BlockSpec((tm, k), lambda i, j: (i, 0))]
    args = [x]
    if g is not None:
        in_specs.append(pl.BlockSpec((1, k), lambda i, j: (0, 0)))
        args.append(g.reshape(1, k).astype(F32))
    in_specs.append(pl.BlockSpec((k, tn), lambda i, j: (0, j)))
    args.append(w)
    if b is not None:
        in_specs.append(pl.BlockSpec((1, tn), lambda i, j: (0, j)))
        args.append(b.reshape(1, n).astype(F32))
    if res is not None:
        in_specs.append(pl.BlockSpec((tm, tn), lambda i, j: (i, j)))
        args.append(res)
    return pl.pallas_call(
        functools.partial(_linear_body, norm=g is not None, bias=b is not None, res=res is not None),
        grid=(m // tm, n // tn),
        in_specs=in_specs,
        out_specs=pl.BlockSpec((tm, tn), lambda i, j: (i, j)),
        out_shape=jax.ShapeDtypeStruct((m, n), out_dtype),
        scratch_shapes=[pltpu.VMEM((tm, k), BF16)],
        compiler_params=_cparams("arbitrary", "arbitrary"),
        name=name,
    )(*args)


def _head_norm_body(x_ref, g_ref, o_ref, ob_ref, *, gsize):
    y = _group_rms(x_ref[...], g_ref[...], gsize)
    o_ref[...] = y
    ob_ref[...] = y.astype(BF16)


def head_norm(x, col0, width, gain, gsize, name="head_norm"):
    m = x.shape[0]
    tm = _pick(m, (512, 256, 128))
    assert col0 % width == 0
    gt = jnp.tile(gain.astype(F32), width // gsize).reshape(1, width)
    return pl.pallas_call(
        functools.partial(_head_norm_body, gsize=gsize),
        grid=(m // tm,),
        in_specs=[pl.BlockSpec((tm, width), lambda i: (i, col0 // width)),
                  pl.BlockSpec((1, width), lambda i: (0, 0))],
        out_specs=[pl.BlockSpec((tm, width), lambda i: (i, 0)),
                   pl.BlockSpec((tm, width), lambda i: (i, 0))],
        out_shape=[jax.ShapeDtypeStruct((m, width), F32), jax.ShapeDtypeStruct((m, width), BF16)],
        compiler_params=_cparams("arbitrary"),
        name=name,
    )(x, gt)


def _seq_conv_body(*refs, n_in, n_par, pre, post, width, tm, lv, halo):
    in_refs = refs[:n_in]
    st_ref = refs[n_in]
    wc_ref = refs[n_in + 1]
    par_refs = refs[n_in + 2:n_in + 2 + n_par]
    o_ref, nst_ref, abuf = refs[n_in + 2 + n_par:]
    t = pl.program_id(1)
    base = halo - (width - 1)

    @pl.when(t == 0)
    def _():
        abuf[base:halo, :] = st_ref[0]

    tiles = [r[0] for r in in_refs]
    pars = [r[...] for r in par_refs]
    abuf[halo:halo + tm, :] = pre(tiles)
    acc = None
    for k in range(width):
        term = wc_ref[k:k + 1, :] * abuf[base + k:base + k + tm, :]
        acc = term if acc is None else acc + term
    o_ref[0] = post(acc, tiles, pars).astype(o_ref.dtype)
    tail = abuf[base + lv:base + lv + width - 1, :]
    abuf[base:halo, :] = tail
    nst_ref[0] = tail


def seq_conv(inputs, state, wconv, params, pre, post, c_conv, c_out, out_dtype, tm, lv, name):
    n_seq, length, _ = inputs[0][0].shape
    width = wconv.shape[0]
    halo = -(-(width - 1) // V7X_SUBLANES) * V7X_SUBLANES
    in_specs = [pl.BlockSpec((1, tm, bw), functools.partial(lambda s, t, ci: (s, t, ci), ci=ci))
                for (_, bw, ci) in inputs]
    in_specs.append(pl.BlockSpec((1, width - 1, c_conv), lambda s, t: (s, 0, 0)))
    in_specs.append(pl.BlockSpec((width, c_conv), lambda s, t: (0, 0)))
    for p in params:
        in_specs.append(pl.BlockSpec((1, p.shape[-1]), lambda s, t: (0, 0)))
    out, nst = pl.pallas_call(
        functools.partial(_seq_conv_body, n_in=len(inputs), n_par=len(params), pre=pre, post=post,
                          width=width, tm=tm, lv=lv, halo=halo),
        grid=(n_seq, length // tm),
        in_specs=in_specs,
        out_specs=[pl.BlockSpec((1, tm, c_out), lambda s, t: (s, t, 0)),
                   pl.BlockSpec((1, width - 1, c_conv), lambda s, t: (s, 0, 0))],
        out_shape=[jax.ShapeDtypeStruct((n_seq, length, c_out), out_dtype),
                   jax.ShapeDtypeStruct((n_seq, width - 1, c_conv), F32)],
        scratch_shapes=[pltpu.VMEM((halo + tm, c_conv), F32)],
        compiler_params=_cparams("arbitrary", "arbitrary"),
        name=name,
    )(*[a for (a, _, _) in inputs], state, wconv.astype(F32),
      *[p.reshape(1, -1).astype(F32) for p in params])
    return out, nst


def _a_pre(tiles):
    a, g = tiles
    return a * _sigmoid(g)


def _a_post(y, tiles, pars):
    bc, lng, lnb = pars
    y = y + bc
    mu = jnp.mean(y, axis=-1, keepdims=True)
    yc = y - mu
    yn = yc * lax.rsqrt(jnp.mean(yc * yc, axis=-1, keepdims=True) + EPS)
    return _silu(yn * lng + lnb)


def _f_pre(tiles):
    return tiles[0]


def _f_post(y, tiles, pars):
    return _silu(y + pars[0]) * tiles[1]


def _d_pre(tiles):
    return tiles[1] * tiles[2]


def _d_post(y, tiles, pars):
    return tiles[0] * y


def _c_pre(tiles):
    return tiles[0]


def _c_post(y, tiles, pars):
    return _silu(y + pars[0])


def _softmax_rows(s):
    p = jnp.exp(s - jnp.max(s, axis=-1, keepdims=True))
    return p / jnp.sum(p, axis=-1, keepdims=True)


def _mem_attn_body(q_ref, k_ref, v_ref, g_ref, o_ref, *, token_major):
    q = q_ref[0]
    tm = q.shape[0]
    hsl = [slice(h * DH_M, (h + 1) * DH_M) for h in range(H_M)]
    qh = [(_rms(q[:, sl], g_ref[...]) * (DH_M ** -0.5)).astype(BF16) for sl in hsl]
    if not token_major:
        for h in range(H_M):
            p = _softmax_rows(_dot_t(qh[h], k_ref[0, h].astype(BF16)))
            o_ref[0, :, hsl[h]] = _dot(p.astype(BF16), v_ref[0, h].astype(BF16)).astype(o_ref.dtype)
    else:
        k_all = k_ref[0].reshape(N_MEM * H_M, DH_M).astype(BF16)
        v_all = v_ref[0].reshape(N_MEM * H_M, DH_M).astype(BF16)
        s = _dot_t(jnp.concatenate(qh, axis=0), k_all)
        col_head = lax.broadcasted_iota(I32, s.shape, 1) & (H_M - 1)
        row_head = lax.shift_right_logical(lax.broadcasted_iota(I32, s.shape, 0), int(math.log2(tm)))
        p = _softmax_rows(jnp.where(col_head == row_head, s, NEG_BIG))
        o = _dot(p.astype(BF16), v_all)
        for h in range(H_M):
            o_ref[0, :, hsl[h]] = o[h * tm:(h + 1) * tm].astype(o_ref.dtype)


def mem_attn(q, k, v, q_gain, tm, token_major, kv_seq0=0, name="mem_attn"):
    n_seq, length, d = q.shape
    kv_block = (1,) + k.shape[1:]
    return pl.pallas_call(
        functools.partial(_mem_attn_body, token_major=token_major),
        grid=(n_seq, length // tm),
        in_specs=[pl.BlockSpec((1, tm, d), lambda s, t: (s, t, 0)),
                  pl.BlockSpec(kv_block, lambda s, t: (kv_seq0 + s, 0, 0, 0)),
                  pl.BlockSpec(kv_block, lambda s, t: (kv_seq0 + s, 0, 0, 0)),
                  pl.BlockSpec((1, DH_M), lambda s, t: (0, 0))],
        out_specs=pl.BlockSpec((1, tm, d), lambda s, t: (s, t, 0)),
        out_shape=jax.ShapeDtypeStruct((n_seq, length, d), BF16),
        compiler_params=_cparams("arbitrary", "arbitrary"),
        name=name,
    )(q, k, v, q_gain.reshape(1, DH_M).astype(F32))


HEADS_PER_PAIR = 2 * G_B


def _t5_lookup(tab_ref, h, dist):
    dist = jnp.maximum(dist, 0)
    max_exact = N_BUCKETS // 2
    df = jnp.maximum(dist, 1).astype(F32)
    large = max_exact + (jnp.log(df / max_exact) / math.log(MAX_DIST / max_exact)
                         * (N_BUCKETS - max_exact)).astype(I32)
    large = jnp.minimum(large, N_BUCKETS - 1)
    bucket = jnp.where(dist < max_exact, dist, large)
    acc = jnp.zeros(dist.shape, F32)
    for b in range(N_BUCKETS):
        acc = jnp.where(bucket == b, tab_ref[b, h], acc)
    return acc


def _t5_tiles_body(tab_ref, o_ref, ot_ref):
    h = pl.program_id(0)
    r = lax.broadcasted_iota(I32, (PAGE_SIZE, PAGE_SIZE), 0)
    c = lax.broadcasted_iota(I32, (PAGE_SIZE, PAGE_SIZE), 1)
    o_ref[0, :, 0:PAGE_SIZE] = _t5_lookup(tab_ref, h, r - c)
    o_ref[0, :, PAGE_SIZE:2 * PAGE_SIZE] = _t5_lookup(tab_ref, h, PAGE_SIZE + r - c)
    ot_ref[0, 0] = _t5_lookup(tab_ref, h, c - r)
    ot_ref[0, 1] = _t5_lookup(tab_ref, h, PAGE_SIZE + c - r)


def t5_tiles(table):
    return pl.pallas_call(
        _t5_tiles_body,
        grid=(H_B,),
        in_specs=[pl.BlockSpec(memory_space=pltpu.SMEM)],
        out_specs=[pl.BlockSpec((1, PAGE_SIZE, 2 * PAGE_SIZE), lambda h: (h, 0, 0)),
                   pl.BlockSpec((1, 2, PAGE_SIZE, PAGE_SIZE),
                                lambda h: (h // HEADS_PER_PAIR, 0, 0, h % HEADS_PER_PAIR))],
        out_shape=[jax.ShapeDtypeStruct((H_B, PAGE_SIZE, 2 * PAGE_SIZE), F32),
                   jax.ShapeDtypeStruct((H_B // HEADS_PER_PAIR, 2, PAGE_SIZE, HEADS_PER_PAIR * PAGE_SIZE), F32)],
        compiler_params=_cparams("arbitrary"),
        name="t5_tiles",
    )(table.astype(F32))


def _indexer_tile(qih, wcol, ki_tile):
    acc = jnp.zeros((qih[0].shape[0], ki_tile.shape[0]), F32)
    for h in range(H_IDX):
        acc = acc + jnp.maximum(_dot_t(qih[h], ki_tile), 0.0) * wcol[h]
    return acc


def _sort_key(score):
    score = jnp.where(score == 0.0, 0.0, score)
    u = lax.bitcast_convert_type(score, I32)
    return jnp.where(u < 0, u ^ 0x7FFFFFFF, u)


def _kth_largest(count_ge, shape, topk):
    def try_cand(cand, cur):
        return jnp.where(count_ge(cand) >= topk, cand, cur)

    t0 = try_cand(jnp.zeros(shape, I32), jnp.full(shape, INT_MIN, I32))

    def step(b, cur):
        cand = cur + lax.shift_left(jnp.int32(1), jnp.int32(30) - b)
        return try_cand(cand, cur)

    return lax.fori_loop(0, 31, step, t0)


def _strict_upper(n):
    r = lax.broadcasted_iota(I32, (n, n), 0)
    c = lax.broadcasted_iota(I32, (n, n), 1)
    return (r < c).astype(BF16)


def _strict_lower(n):
    r = lax.broadcasted_iota(I32, (n, n), 0)
    c = lax.broadcasted_iota(I32, (n, n), 1)
    return (c < r).astype(BF16)


def _tree(parts, op):
    parts = list(parts)
    while len(parts) > 1:
        nxt = [op(parts[k], parts[k + 1]) for k in range(0, len(parts) - 1, 2)]
        if len(parts) % 2:
            nxt.append(parts[-1])
        parts = nxt
    return parts[0]


def _fold_rows(x, op):
    return _tree([x[r:r + V7X_SUBLANES] for r in range(0, x.shape[0], V7X_SUBLANES)], op)


LOOP_UNROLL = 4


def _loop_tiles(n, body, carry, unroll=LOOP_UNROLL):
    shift = int(math.log2(unroll))
    n_main = lax.shift_right_logical(n, shift)
    carry = lax.fori_loop(0, n_main, lambda t, c: body(t * unroll, unroll, c), carry)
    return lax.fori_loop(n_main * unroll, n, lambda j, c: body(j, 1, c), carry)


def _dsa_prompt_body(q_ref, qi_ref, wi_ref, k_ref, vt_ref, ki_ref, bt_ref, o_ref,
                     key_ref, am_ref, lg_ref, acc_ref, ot_ref, *, topk):
    i = pl.program_id(1)
    nj = i + 1
    qb = PAGE_SIZE
    kpos = lax.broadcasted_iota(I32, (qb, qb), 0)
    qpos = lax.broadcasted_iota(I32, (qb, qb), 1)

    def kslice(j0, u=1):
        return pl.ds(pl.multiple_of(j0 * qb, qb), u * qb)

    qit = qi_ref[0].T
    qit = jnp.concatenate([qit[h * D_IDX:(h + 1) * D_IDX] for h in range(H_IDX)], axis=1).astype(BF16)
    wsc = (wi_ref[0] * (H_IDX ** -0.5)) * (D_IDX ** -0.5)
    wt = jnp.concatenate([wsc, jnp.zeros((qb, qb - H_IDX), F32)], axis=1).T
    wrow = jnp.concatenate([wt[h:h + 1] for h in range(H_IDX)], axis=1)

    def idx_step(j0, u, carry):
        s = jnp.maximum(_dot(ki_ref[0, kslice(j0, u), :], qit), 0.0) * wrow
        sc = s[:, 0:qb]
        for h in range(1, H_IDX):
            sc = sc + s[:, h * qb:(h + 1) * qb]
        key = _sort_key(sc)
        for t in range(u):
            valid = (j0 + t < i) | (kpos <= qpos)
            key_ref[j0 + t] = jnp.where(valid, key[t * qb:(t + 1) * qb], INT_MIN)
        return carry

    _loop_tiles(nj, idx_step, 0)

    def count_where(pred):
        def cstep(j0, u, c):
            for t in range(u):
                c = c + jnp.where(pred(key_ref[j0 + t]), 1.0, 0.0)
            return c
        c = _loop_tiles(nj, cstep, jnp.zeros((qb, qb), F32))
        return jnp.sum(_fold_rows(c, jnp.add), axis=0, keepdims=True)

    thr = _kth_largest(lambda cand: count_where(lambda key: key >= cand), (1, qb), topk)
    n_tie = topk - count_where(lambda key: key > thr)
    n_eq = count_where(lambda key: (key == thr) & (key != INT_MIN))
    need_rank = jnp.max(n_eq - n_tie) > 0.0

    @pl.when(jnp.logical_not(need_rank))
    def _():
        def mask_step(j0, u, carry):
            for t in range(u):
                key = key_ref[j0 + t]
                am_ref[j0 + t] = jnp.where((key >= thr) & (key != INT_MIN), 0.0, NEG_BIG)
            return carry
        _loop_tiles(nj, mask_step, 0)

    @pl.when(need_rank)
    def _():
        lower = _strict_lower(qb)

        def mask_step(j, run):
            key = key_ref[j]
            eq = (key == thr) & (key != INT_MIN)
            eqf = jnp.where(eq, 1.0, 0.0)
            rank = _dot(lower, eqf.astype(BF16)) + run
            sel = (key > thr) | (eq & (rank < n_tie))
            am_ref[j] = jnp.where(sel, 0.0, NEG_BIG)
            return run + jnp.sum(_fold_rows(eqf, jnp.add), axis=0, keepdims=True)

        lax.fori_loop(0, nj, mask_step, jnp.zeros((1, qb), F32))

    qt = q_ref[0].T
    pw = HEADS_PER_PAIR * qb
    gw = G_B * qb
    for pp in range(KV_B // 2):
        def group_qt(n):
            return jnp.concatenate([qt[h * DH_B:(h + 1) * DH_B] for h in range(n * G_B, (n + 1) * G_B)],
                                   axis=1)
        zero = jnp.zeros((DH_B, gw), F32)
        rhs = jnp.concatenate([jnp.concatenate([group_qt(2 * pp), zero], axis=1),
                               jnp.concatenate([zero, group_qt(2 * pp + 1)], axis=1)], axis=0)
        rhs = (rhs * (DH_B ** -0.5)).astype(BF16)
        klanes = slice(pp * 2 * DH_B, (pp + 1) * 2 * DH_B)
        far = bt_ref[pp, 1, 0:1, :]

        def tile_logits(j0, u, bias, mx):
            s = _dot(k_ref[0, kslice(j0, u), klanes], rhs)
            for t in range(u):
                st = s[t * qb:(t + 1) * qb] + bias + jnp.concatenate([am_ref[j0 + t]] * HEADS_PER_PAIR, axis=1)
                lg_ref[j0 + t] = st
                mx = jnp.maximum(mx, _fold_rows(st, jnp.maximum))
            return mx

        neg = jnp.full((V7X_SUBLANES, pw), NEG_BIG, F32)
        mx = _loop_tiles(jnp.maximum(i - 1, 0), lambda j0, u, m: tile_logits(j0, u, far, m), neg)
        mx = lax.cond(i >= 1, lambda m: tile_logits(i - 1, 1, bt_ref[pp, 1], m), lambda m: m, mx)
        mx = tile_logits(i, 1, bt_ref[pp, 0], mx)
        m = jnp.max(mx, axis=0, keepdims=True)
        acc_ref[...] = jnp.zeros(acc_ref.shape, F32)

        def p2(j0, u, l):
            ps = []
            for t in range(u):
                p = jnp.exp(lg_ref[j0 + t] - m)
                l = l + _fold_rows(p, jnp.add)
                ps.append(p.astype(BF16))
            pb = jnp.concatenate(ps, axis=0)
            vt = jnp.concatenate([vt_ref[0, j0 + t] for t in range(u)], axis=1)
            for gg in range(2):
                n = 2 * pp + gg
                acc_ref[gg] += _dot(vt[n * DH_B:(n + 1) * DH_B, :], pb[:, gg * gw:(gg + 1) * gw])
            return l

        l = _loop_tiles(nj, p2, jnp.zeros((V7X_SUBLANES, pw), F32))
        lsum = jnp.sum(l, axis=0, keepdims=True)
        for gg in range(2):
            o_t = acc_ref[gg] / lsum[:, gg * gw:(gg + 1) * gw]
            for hh in range(G_B):
                h = (2 * pp + gg) * G_B + hh
                ot_ref[h * DH_B:(h + 1) * DH_B, :] = o_t[:, hh * qb:(hh + 1) * qb]
    o_ref[0] = ot_ref[...].T.astype(o_ref.dtype)


def dsa_prompt(qn, qi, wi, kb, vtb, kib, btt):
    n_seq, length, _ = qn.shape
    nkb = length // PAGE_SIZE
    qmap = lambda s, i: (s, i, 0)
    kmap = lambda s, i: (s, 0, 0)
    pw = HEADS_PER_PAIR * PAGE_SIZE
    return pl.pallas_call(
        functools.partial(_dsa_prompt_body, topk=min(TOPK, length // 4)),
        grid=(n_seq, nkb),
        in_specs=[pl.BlockSpec((1, PAGE_SIZE, H_B * DH_B), qmap),
                  pl.BlockSpec((1, PAGE_SIZE, H_IDX * D_IDX), qmap),
                  pl.BlockSpec((1, PAGE_SIZE, H_IDX), qmap),
                  pl.BlockSpec((1, length, KV_B * DH_B), kmap),
                  pl.BlockSpec((1, nkb, KV_B * DH_B, PAGE_SIZE), lambda s, i: (s, 0, 0, 0)),
                  pl.BlockSpec((1, length, D_IDX), kmap),
                  pl.BlockSpec((H_B // HEADS_PER_PAIR, 2, PAGE_SIZE, pw), lambda s, i: (0, 0, 0, 0))],
        out_specs=pl.BlockSpec((1, PAGE_SIZE, H_B * DH_B), qmap),
        out_shape=jax.ShapeDtypeStruct((n_seq, length, H_B * DH_B), BF16),
        scratch_shapes=[pltpu.VMEM((nkb, PAGE_SIZE, PAGE_SIZE), I32),
                        pltpu.VMEM((nkb, PAGE_SIZE, PAGE_SIZE), F32),
                        pltpu.VMEM((nkb, PAGE_SIZE, pw), F32),
                        pltpu.VMEM((2, DH_B, G_B * PAGE_SIZE), F32),
                        pltpu.VMEM((H_B * DH_B, PAGE_SIZE), F32)],
        compiler_params=_cparams("arbitrary", "arbitrary"),
        name="dsa_prompt",
    )(qn, qi, wi, kb, vtb, kib, btt)


N_PAGES = PAST_LEN // PAGE_SIZE
N_KTILES = N_PAGES + 1
N_KTILES_PAD = -(-N_KTILES // PAGES_PER_STEP) * PAGES_PER_STEP


def _dsa_sample_sel_body(pt_ref, qi_ref, wi_ref, kin_ref, *rest, lv, topk):
    page_refs = rest[:PAGES_PER_STEP]
    am_ref, key_ref, qs_ref, ws_ref = rest[PAGES_PER_STEP:]
    c = pl.program_id(1)
    nsteps = pl.num_programs(1)
    r8 = SAMPLE_PAD

    @pl.when(c == 0)
    def _():
        qi = qi_ref[0]
        wsc = (wi_ref[0] * (H_IDX ** -0.5)) * (D_IDX ** -0.5)
        qs_ref[...] = jnp.concatenate([qi[:, h * D_IDX:(h + 1) * D_IDX] for h in range(H_IDX)],
                                      axis=0).astype(BF16)
        ws_ref[...] = jnp.concatenate([jnp.broadcast_to(wsc[:, h:h + 1], (r8, PAGE_SIZE))
                                       for h in range(H_IDX)], axis=0)

    def score(ki_tile):
        s = jnp.maximum(_dot_t(qs_ref[...], ki_tile), 0.0) * ws_ref[...]
        return _fold_rows(s, jnp.add)

    for r in range(PAGES_PER_STEP):
        key_ref[c * PAGES_PER_STEP + r] = _sort_key(score(page_refs[r][0].astype(BF16)))

    @pl.when(c == nsteps - 1)
    def _():
        rows = lax.broadcasted_iota(I32, (r8, PAGE_SIZE), 0)
        cols = lax.broadcasted_iota(I32, (r8, PAGE_SIZE), 1)
        new_valid = (cols <= rows) & (cols < lv)
        kin = jnp.concatenate([kin_ref[0], jnp.zeros((PAGE_SIZE - r8, D_IDX), F32)], axis=0)
        key_ref[N_PAGES] = jnp.where(new_valid, _sort_key(score(kin.astype(BF16))), INT_MIN)

        def tile_valid(j):
            return new_valid if j == N_PAGES else None

        def count_where(pred):
            parts = []
            for j in range(N_KTILES):
                hit = pred(key_ref[j])
                if tile_valid(j) is not None:
                    hit = hit & tile_valid(j)
                parts.append(jnp.where(hit, 1.0, 0.0))
            return jnp.sum(_tree(parts, jnp.add), axis=-1, keepdims=True)

        thr = _kth_largest(lambda cand: count_where(lambda key: key >= cand), (r8, 1), topk)
        n_tie = topk - count_where(lambda key: key > thr)
        n_eq = count_where(lambda key: key == thr)
        need_rank = jnp.max(n_eq - n_tie) > 0.0

        @pl.when(jnp.logical_not(need_rank))
        def _():
            for j in range(N_KTILES):
                sel = key_ref[j] >= thr
                if tile_valid(j) is not None:
                    sel = sel & tile_valid(j)
                am_ref[0, j] = jnp.where(sel, 0.0, NEG_BIG)

        @pl.when(need_rank)
        def _():
            upper = _strict_upper(PAGE_SIZE)

            def mask_step(j, run):
                key = key_ref[j]
                valid = (j < N_PAGES) | new_valid
                eq = valid & (key == thr)
                eqf = jnp.where(eq, 1.0, 0.0)
                rank = _dot(eqf.astype(BF16), upper) + run
                sel = (key > thr) | (eq & (rank < n_tie))
                am_ref[0, j] = jnp.where(sel, 0.0, NEG_BIG)
                return run + jnp.sum(eqf, axis=-1, keepdims=True)

            lax.fori_loop(0, N_KTILES, mask_step, jnp.zeros((r8, 1), F32))

        for j in range(N_KTILES, N_KTILES_PAD):
            am_ref[0, j] = jnp.full((r8, PAGE_SIZE), NEG_BIG, F32)


def dsa_sample_select(page_table, qi, wi, kin, kidx_pool, lv):
    n_seq = qi.shape[0]
    r8 = SAMPLE_PAD
    qmap = lambda s, c, pt: (s, 0, 0)
    page_specs = [pl.BlockSpec((1, PAGE_SIZE, D_IDX),
                               functools.partial(lambda s, c, pt, r: (pt[s, c * PAGES_PER_STEP + r], 0, 0), r=r))
                  for r in range(PAGES_PER_STEP)]
    grid_spec = pltpu.PrefetchScalarGridSpec(
        num_scalar_prefetch=1,
        grid=(n_seq, N_PAGES // PAGES_PER_STEP),
        in_specs=[pl.BlockSpec((1, r8, H_IDX * D_IDX), qmap),
                  pl.BlockSpec((1, r8, H_IDX), qmap),
                  pl.BlockSpec((1, r8, D_IDX), qmap)] + page_specs,
        out_specs=pl.BlockSpec((1, N_KTILES_PAD, r8, PAGE_SIZE), lambda s, c, pt: (s, 0, 0, 0)),
        scratch_shapes=[pltpu.VMEM((N_KTILES, r8, PAGE_SIZE), I32),
                        pltpu.VMEM((H_IDX * r8, D_IDX), BF16),
                        pltpu.VMEM((H_IDX * r8, PAGE_SIZE), F32)],
    )
    return pl.pallas_call(
        functools.partial(_dsa_sample_sel_body, lv=lv, topk=min(TOPK, (PAST_LEN + lv) // 4)),
        grid_spec=grid_spec,
        out_shape=jax.ShapeDtypeStruct((n_seq, N_KTILES_PAD, r8, PAGE_SIZE), F32),
        compiler_params=_cparams("arbitrary", "arbitrary"),
        name="dsa_sample_select",
    )(page_table, qi, wi, kin, *([kidx_pool] * PAGES_PER_STEP))


def _dsa_sample_attn_body(pt_ref, q_ref, kn_ref, vn_ref, am_ref, amn_ref, bt_ref, *rest):
    kp_refs = rest[:PAGES_PER_STEP]
    vp_refs = rest[PAGES_PER_STEP:2 * PAGES_PER_STEP]
    o_ref, m_ref, l_ref, acc_ref, kbuf, vbuf = rest[2 * PAGES_PER_STEP:]
    c = pl.program_id(1)
    nsteps = pl.num_programs(1)
    r8 = SAMPLE_PAD
    gr = G_B * r8

    @pl.when(c == 0)
    def _():
        m_ref[...] = jnp.full(m_ref.shape, NEG_BIG, F32)
        l_ref[...] = jnp.zeros(l_ref.shape, F32)
        acc_ref[...] = jnp.zeros(acc_ref.shape, F32)

    q = q_ref[0]
    qgs, b_subs, b_diags, b_fars = [], [], [], []
    for n in range(KV_B):
        heads = [n * G_B + hh for hh in range(G_B)]
        qg = jnp.concatenate([q[:, h * DH_B:(h + 1) * DH_B] for h in heads], axis=0)
        qgs.append((qg * (DH_B ** -0.5)).astype(BF16))
        b_diags.append(jnp.concatenate([bt_ref[h, :, 0:PAGE_SIZE] for h in heads], axis=0))
        b_subs.append(jnp.concatenate([bt_ref[h, :, PAGE_SIZE:2 * PAGE_SIZE] for h in heads], axis=0))
        b_fars.append(jnp.concatenate(
            [jnp.broadcast_to(bt_ref[h, 0:1, PAGE_SIZE:PAGE_SIZE + 1], (r8, PAGE_SIZE))
             for h in heads], axis=0))

    def update(n, s, v_tile):
        m_old = m_ref[n]
        m_new = jnp.maximum(m_old, jnp.max(s, axis=-1, keepdims=True))
        alpha = jnp.exp(m_old - m_new)
        p = jnp.exp(s - m_new[:, 0:1])
        l_ref[n] = alpha * l_ref[n] + jnp.sum(p, axis=-1, keepdims=True)
        acc_ref[n] = alpha[:, 0:DH_B] * acc_ref[n] + _dot(p.astype(BF16), v_tile)
        m_ref[n] = m_new

    for r in range(PAGES_PER_STEP):
        kbuf[r] = kp_refs[r][0].reshape(PAGE_SIZE * KV_B, DH_B)
        vbuf[r] = vp_refs[r][0].reshape(PAGE_SIZE * KV_B, DH_B)

    def group_rows(buf, n):
        return jnp.concatenate([buf[r, pl.ds(n, PAGE_SIZE, stride=KV_B), :] for r in range(PAGES_PER_STEP)],
                               axis=0).astype(BF16)

    amcat = jnp.concatenate([am_ref[0, r] for r in range(PAGES_PER_STEP)], axis=1)
    am4 = jnp.concatenate([amcat] * G_B, axis=0)
    is_last_step = c == nsteps - 1
    for n in range(KV_B):
        bias = jnp.concatenate([b_fars[n]] * (PAGES_PER_STEP - 1)
                               + [jnp.where(is_last_step, b_subs[n], b_fars[n])], axis=1)
        s = _dot_t(qgs[n], group_rows(kbuf, n)) + bias + am4
        update(n, s, group_rows(vbuf, n))

    @pl.when(c == nsteps - 1)
    def _():
        pad = jnp.zeros((PAGE_SIZE - r8, KV_B * DH_B), F32)
        kn = jnp.concatenate([kn_ref[0], pad], axis=0).astype(BF16)
        vn = jnp.concatenate([vn_ref[0], pad], axis=0).astype(BF16)
        am4 = jnp.concatenate([amn_ref[0, 0]] * G_B, axis=0)
        for n in range(KV_B):
            ksl = slice(n * DH_B, (n + 1) * DH_B)
            s = _dot_t(qgs[n], kn[:, ksl]) + b_diags[n] + am4
            update(n, s, vn[:, ksl])
            o = acc_ref[n] / l_ref[n][:, 0:DH_B]
            for hh in range(G_B):
                h = n * G_B + hh
                o_ref[0, :, h * DH_B:(h + 1) * DH_B] = o[hh * r8:(hh + 1) * r8].astype(o_ref.dtype)


def dsa_sample_attend(page_table, qn, kn, vn, amask, bt, k_pool, v_pool):
    n_seq = qn.shape[0]
    r8 = SAMPLE_PAD
    qmap = lambda s, c, pt: (s, 0, 0)
    page_map = [functools.partial(lambda s, c, pt, r: (pt[s, c * PAGES_PER_STEP + r], 0, 0, 0), r=r)
                for r in range(PAGES_PER_STEP)]
    kv_w = KV_B * DH_B
    page_block = (1, PAGE_SIZE, KV_B, DH_B)
    grid_spec = pltpu.PrefetchScalarGridSpec(
        num_scalar_prefetch=1,
        grid=(n_seq, N_PAGES // PAGES_PER_STEP),
        in_specs=[pl.BlockSpec((1, r8, H_B * DH_B), qmap),
                  pl.BlockSpec((1, r8, kv_w), qmap),
                  pl.BlockSpec((1, r8, kv_w), qmap),
                  pl.BlockSpec((1, PAGES_PER_STEP, r8, PAGE_SIZE), lambda s, c, pt: (s, c, 0, 0)),
                  pl.BlockSpec((1, PAGES_PER_STEP, r8, PAGE_SIZE),
                               lambda s, c, pt: (s, N_PAGES // PAGES_PER_STEP, 0, 0)),
                  pl.BlockSpec((H_B, r8, 2 * PAGE_SIZE), lambda s, c, pt: (0, 0, 0))]
                 + [pl.BlockSpec(page_block, m) for m in page_map]
                 + [pl.BlockSpec(page_block, m) for m in page_map],
        out_specs=pl.BlockSpec((1, r8, H_B * DH_B), qmap),
        scratch_shapes=[pltpu.VMEM((KV_B, G_B * r8, PAGE_SIZE), F32),
                        pltpu.VMEM((KV_B, G_B * r8, PAGE_SIZE), F32),
                        pltpu.VMEM((KV_B, G_B * r8, DH_B), F32),
                        pltpu.VMEM((PAGES_PER_STEP, PAGE_SIZE * KV_B, DH_B), F32),
                        pltpu.VMEM((PAGES_PER_STEP, PAGE_SIZE * KV_B, DH_B), F32)],
    )
    return pl.pallas_call(
        _dsa_sample_attn_body,
        grid_spec=grid_spec,
        out_shape=jax.ShapeDtypeStruct((n_seq, r8, H_B * DH_B), BF16),
        compiler_params=_cparams("arbitrary", "arbitrary"),
        name="dsa_sample_attend",
    )(page_table, qn, kn, vn, amask, amask, bt,
      *([k_pool] * PAGES_PER_STEP), *([v_pool] * PAGES_PER_STEP))


HEADS_PER_GROUP = H_C // G_C
GROUP_W = D_IN_C // G_C


def _cumsum_rows(x):
    n = x.shape[0]
    idx = lax.broadcasted_iota(I32, x.shape, 0)
    d = 1
    while d < n:
        x = x + jnp.where(idx >= d, pltpu.roll(x, d, 0), 0.0)
        d *= 2
    return x


def _cumsum_lanes(x):
    n = x.shape[1]
    idx = lax.broadcasted_iota(I32, x.shape, 1)
    d = 1
    while d < n:
        x = x + jnp.where(idx >= d, pltpu.roll(x, d, 1), 0.0)
        d *= 2
    return x


def _softplus(x):
    return jnp.maximum(x, 0.0) + jnp.log1p(jnp.exp(-jnp.abs(x)))


def _split3(x):
    hi = x.astype(BF16)
    r = x - hi.astype(F32)
    mid = r.astype(BF16)
    lo = (r - mid.astype(F32)).astype(BF16)
    return hi, mid, lo


def _ssd_body(x_ref, xs_ref, bm_ref, cm_ref, z_ref, ng_ref, wdt_ref, wdtt_ref, dtb_ref, dtbt_ref,
              alog_ref, alogt_ref, dsk_ref, cn_ref, wo_ref, h0_ref, o_ref, hst_ref,
              dt_ref, dtt_ref, acs_ref, acst_ref, *, qin, lv):
    c = pl.program_id(1)
    g = pl.program_id(2)
    q = CHUNK_C
    hpg = HEADS_PER_GROUP
    rows = lax.broadcasted_iota(I32, (q, q), 0)
    cols = lax.broadcasted_iota(I32, (q, q), 1)
    tril = cols <= rows

    def pad_rows(a):
        if qin == q:
            return a
        return jnp.concatenate([a, jnp.zeros((q - qin, a.shape[1]), a.dtype)], axis=0)

    @pl.when(c == 0)
    def _():
        for j in range(hpg):
            hst_ref[0, g * hpg + j] = h0_ref[0, j]

    @pl.when(g == 0)
    def _():
        xn = _rms(pad_rows(x_ref[0]), ng_ref[...]).astype(BF16)
        dt = _softplus(_dot(xn, wdt_ref[...]) + dtb_ref[...])
        dtt = _softplus(_dot_t(wdtt_ref[...], xn) + dtbt_ref[...])
        if lv < q:
            dt = jnp.where(lax.broadcasted_iota(I32, dt.shape, 0) < lv, dt, 0.0)
            dtt = jnp.where(lax.broadcasted_iota(I32, dtt.shape, 1) < lv, dtt, 0.0)
        acs = _tree([_dot(tril.astype(BF16), p) for p in _split3(dt * (-jnp.exp(alog_ref[...])))], jnp.add)
        acst = _tree([_dot(p, (rows <= cols).astype(BF16))
                      for p in _split3(dtt * (-jnp.exp(alogt_ref[...])))], jnp.add)
        for gg in range(G_C):
            dt_ref[gg] = dt[:, gg * hpg:(gg + 1) * hpg]
            acs_ref[gg] = acs[:, gg * hpg:(gg + 1) * hpg]
            dtt_ref[gg] = dtt[gg * hpg:(gg + 1) * hpg, :]
            acst_ref[gg] = acst[gg * hpg:(gg + 1) * hpg, :]

    dt = dt_ref[g]
    dtt = dtt_ref[g]
    acs = acs_ref[g]
    acst = acst_ref[g]
    xs = pad_rows(xs_ref[0])
    bm = pad_rows(bm_ref[0])
    cm = pad_rows(cm_ref[0])
    xst = xs.T
    cmb = cm.astype(BF16)
    cb = _dot_t(cmb, bm.astype(BF16))
    ys = []
    for j in range(hpg):
        hd = g * hpg + j
        acs_col = acs[:, j:j + 1]
        dt_col = dt[:, j:j + 1]
        acs_row = acst[j:j + 1, :]
        dt_row = dtt[j:j + 1, :]
        a_last = acs_row[:, q - 1:q]
        xh = xs[:, j * P_C:(j + 1) * P_C] * dt_col
        xht = xst[j * P_C:(j + 1) * P_C, :] * dt_row
        lmat = jnp.exp(jnp.where(tril, acs_col - acs_row, -jnp.inf))
        y = _dot((cb * lmat).astype(BF16), xh.astype(BF16))
        h_old = hst_ref[0, hd]
        y = y + _dot_t(cmb, h_old.astype(BF16)) * jnp.exp(acs_col)
        decay = jnp.exp(a_last - acs_col)
        st = _dot(xht.astype(BF16), (bm * decay).astype(BF16))
        hst_ref[0, hd] = jnp.exp(a_last) * h_old + st
        ys.append(y)
    y = jnp.concatenate(ys, axis=1) + xs * dsk_ref[...]
    y = y * _silu(pad_rows(z_ref[0]))
    y = _rms(y, cn_ref[...]).astype(BF16)
    part = _dot(y, wo_ref[...])[0:qin]

    @pl.when(g == 0)
    def _():
        o_ref[0] = x_ref[0] + part

    @pl.when(g != 0)
    def _():
        o_ref[0] = o_ref[0] + part


def ssd_mix(x, xbc, proj, z_col0, h0, norm_g, w_dt, dt_bias, a_log, d_skip, c_norm, w_out, qin, lv):
    n_seq, length, d = x.shape
    nchunk = length // qin
    gw = GROUP_W
    col = lambda a: a.reshape(-1, 1).astype(F32)
    row = lambda a: a.reshape(1, -1).astype(F32)
    cmap = lambda off: (lambda s, c, g: (s, c, off + g))
    full2 = lambda s, c, g: (0, 0)
    in_specs = [
        pl.BlockSpec((1, qin, d), lambda s, c, g: (s, c, 0)),
        pl.BlockSpec((1, qin, gw), cmap(0)),
        pl.BlockSpec((1, qin, N_SSM), cmap(D_IN_C // N_SSM)),
        pl.BlockSpec((1, qin, N_SSM), cmap(D_IN_C // N_SSM + G_C)),
        pl.BlockSpec((1, qin, gw), cmap(z_col0 // gw)),
        pl.BlockSpec((1, d), full2),
        pl.BlockSpec((d, H_C), full2),
        pl.BlockSpec((H_C, d), full2),
        pl.BlockSpec((1, H_C), full2),
        pl.BlockSpec((H_C, 1), full2),
        pl.BlockSpec((1, H_C), full2),
        pl.BlockSpec((H_C, 1), full2),
        pl.BlockSpec((1, gw), lambda s, c, g: (0, g)),
        pl.BlockSpec((1, gw), lambda s, c, g: (0, g)),
        pl.BlockSpec((gw, d), lambda s, c, g: (g, 0)),
        pl.BlockSpec((1, HEADS_PER_GROUP, P_C, N_SSM), lambda s, c, g: (s, g, 0, 0)),
    ]
    out, hst = pl.pallas_call(
        functools.partial(_ssd_body, qin=qin, lv=lv),
        grid=(n_seq, nchunk, G_C),
        in_specs=in_specs,
        out_specs=[pl.BlockSpec((1, qin, d), lambda s, c, g: (s, c, 0)),
                   pl.BlockSpec((1, H_C, P_C, N_SSM), lambda s, c, g: (s, 0, 0, 0))],
        out_shape=[jax.ShapeDtypeStruct((n_seq, length, d), F32),
                   jax.ShapeDtypeStruct((n_seq, H_C, P_C, N_SSM), F32)],
        scratch_shapes=[pltpu.VMEM((G_C, CHUNK_C, HEADS_PER_GROUP), F32),
                        pltpu.VMEM((G_C, HEADS_PER_GROUP, CHUNK_C), F32),
                        pltpu.VMEM((G_C, CHUNK_C, HEADS_PER_GROUP), F32),
                        pltpu.VMEM((G_C, HEADS_PER_GROUP, CHUNK_C), F32)],
        compiler_params=_cparams("arbitrary", "arbitrary", "arbitrary"),
        name="ssd_mix",
    )(x, xbc, xbc, xbc, proj, row(norm_g), w_dt.astype(BF16), w_dt.T.astype(BF16),
      row(dt_bias), col(dt_bias), row(a_log), col(a_log),
      row(jnp.repeat(d_skip, P_C)), row(c_norm), w_out, h0)
    return out, hst


def _flat(a):
    return a.reshape(-1, a.shape[-1])


def _unflat(a, like):
    return a.reshape(like.shape[0], like.shape[1], a.shape[-1])


def kernel(x_prompt, x_sample, mem_prompt, state_a_conv, cache_b_k, cache_b_v, cache_b_kidx, state_c_conv, state_c_ssm, state_d_conv, state_ffn_conv, cache_mem_k, cache_mem_v, page_table, rel_bias, norm_mix, norm_mem, norm_ffn, norm_memtok, a_w_in, a_b_in, a_w_conv, a_b_conv, a_ln_g, a_ln_b, a_w_out, b_w_in, b_w_out, b_q_norm, b_k_norm, b_kidx_norm, c_w_in, c_w_conv, c_b_conv, c_dt_bias, c_a_log, c_d_skip, c_norm, c_w_out, d_w_in, d_w_conv, d_w_out, m_w_q, m_w_kv, m_w_o, m_q_norm, m_k_norm, f_w_in, f_w_conv, f_b_conv, f_w_out):
    n_p, l_p, d = x_prompt.shape
    n_s, l_s, _ = x_sample.shape
    bf = lambda w: w.astype(BF16)
    xs_pad = jnp.pad(x_sample, ((0, 0), (0, SAMPLE_PAD - l_s), (0, 0)))
    groups = [dict(x=x_prompt, tm=256, lv=256, qin=CHUNK_C, lvq=CHUNK_C, prompt=True),
              dict(x=xs_pad, tm=SAMPLE_PAD, lv=l_s, qin=SAMPLE_PAD, lvq=l_s, prompt=False)]
    bt, btt = t5_tiles(rel_bias)
    mem_flat = _flat(mem_prompt)
    outs = {k: [[], []] for k in ("a", "bk", "bv", "bki", "cc", "cs", "d", "f")}
    m_kp, m_vp = [], []

    for i in range(DEPTH):
        j = i // 4
        kind = i % 4
        kv = linear(mem_flat, bf(m_w_kv[i]), g=norm_memtok[i], name="mem_kv")
        mk, _ = head_norm(kv, 0, d, m_k_norm[i], DH_M, name="mem_k_norm")
        m_kp.append(mk.reshape(n_p, N_MEM, H_M, DH_M))
        m_vp.append(kv[:, d:].reshape(n_p, N_MEM, H_M, DH_M))
        mk = m_kp[-1].transpose(0, 2, 1, 3)
        mv = m_vp[-1].transpose(0, 2, 1, 3)

        for gi, grp in enumerate(groups):
            x = grp["x"]
            n_seq, length, _ = x.shape
            prompt = grp["prompt"]
            zeros_state = lambda w, c: jnp.zeros((n_seq, w - 1, c), F32)
            xf = _flat(x)
            if kind == 0:
                ag = _unflat(linear(xf, bf(a_w_in[j]), g=norm_mix[i], b=a_b_in[j], name="a_in"), x)
                st = zeros_state(W_A, d) if prompt else state_a_conv[j]
                u, nst = seq_conv([(ag, d, 0), (ag, d, 1)], st, a_w_conv[j],
                                  [a_b_conv[j], a_ln_g[j], a_ln_b[j]], _a_pre, _a_post,
                                  d, d, BF16, grp["tm"], grp["lv"], "a_conv")
                outs["a"][gi].append(nst)
                x = _unflat(linear(_flat(u), bf(a_w_out[j]), res=xf, name="a_out"), x)
            elif kind == 1:
                w_in = b_w_in[j]
                n_main = H_B * DH_B + 2 * KV_B * DH_B + H_IDX * D_IDX
                w_tail = jnp.pad(w_in[:, n_main:], ((0, 0), (0, V7X_LANES - (w_in.shape[1] - n_main))))
                pm = linear(xf, bf(w_in[:, :n_main]), g=norm_mix[i], name="b_in")
                ptl = linear(xf, bf(w_tail), g=norm_mix[i], name="b_in_tail")
                kw = KV_B * DH_B
                qn, _ = head_norm(pm, 0, H_B * DH_B, b_q_norm[j], DH_B, name="b_q_norm")
                kn, knb = head_norm(pm, H_B * DH_B, kw, b_k_norm[j], DH_B, name="b_k_norm")
                kin, kinb = head_norm(ptl[:, :D_IDX], 0, D_IDX, b_kidx_norm[j], D_IDX, name="b_ki_norm")
                v = pm[:, H_B * DH_B + kw:H_B * DH_B + 2 * kw]
                qi = pm[:, H_B * DH_B + 2 * kw:]
                wi = ptl[:, D_IDX:D_IDX + H_IDX]
                r3 = lambda a: a.reshape(n_seq, length, a.shape[-1])
                if prompt:
                    vtb = bf(v).reshape(n_seq, length // PAGE_SIZE, PAGE_SIZE, kw).transpose(0, 1, 3, 2)
                    o = dsa_prompt(r3(qn), r3(qi), r3(wi), r3(knb), vtb, r3(kinb), btt)
                else:
                    am = dsa_sample_select(page_table, r3(qi), r3(wi), r3(kin), cache_b_kidx[j], grp["lv"])
                    o = dsa_sample_attend(page_table, r3(qn), r3(kn), r3(v), am, bt, cache_b_k[j], cache_b_v[j])
                outs["bk"][gi].append(r3(kn))
                outs["bv"][gi].append(r3(v))
                outs["bki"][gi].append(r3(kin))
                x = _unflat(linear(_flat(o), bf(b_w_out[j]), res=xf, name="b_out"), x)
            elif kind == 2:
                w_in = c_w_in[j]
                conv_c = D_IN_C + 2 * G_C * N_SSM
                w_main = jnp.concatenate([w_in[:, D_IN_C:D_IN_C + conv_c], w_in[:, :D_IN_C]], axis=1)
                proj = _unflat(linear(xf, bf(w_main), g=norm_mix[i], name="c_in"), x)
                st = zeros_state(W_C, conv_c) if prompt else state_c_conv[j]
                xbc, nst = seq_conv([(proj, conv_c, 0)], st, c_w_conv[j], [c_b_conv[j]], _c_pre, _c_post,
                                    conv_c, conv_c, F32, grp["tm"], grp["lv"], "c_conv")
                h0 = jnp.zeros((n_seq, H_C, P_C, N_SSM), F32) if prompt else state_c_ssm[j]
                x, hst = ssd_mix(x, xbc, proj, conv_c, h0, norm_mix[i], w_in[:, D_IN_C + conv_c:],
                                 c_dt_bias[j], c_a_log[j], c_d_skip[j], c_norm[j], bf(c_w_out[j]),
                                 grp["qin"], grp["lvq"])
                outs["cc"][gi].append(nst)
                outs["cs"][gi].append(hst)
            else:
                p3 = _unflat(linear(xf, bf(d_w_in[j]), g=norm_mix[i], name="d_in"), x)
                st = zeros_state(W_D, d) if prompt else state_d_conv[j]
                u, nst = seq_conv([(p3, d, 0), (p3, d, 1), (p3, d, 2)], st, d_w_conv[j], [],
                                  _d_pre, _d_post, d, d, BF16, grp["tm"], grp["lv"], "d_conv")
                outs["d"][gi].append(nst)
                x = _unflat(linear(_flat(u), bf(d_w_out[j]), res=xf, name="d_out"), x)

            xf = _flat(x)
            qm = _unflat(linear(xf, bf(m_w_q[i]), g=norm_mem[i], name="mem_q"), x)
            if prompt:
                kk, vv, tma, seq0 = mk, mv, 512, 0
            else:
                kk = cache_mem_k.reshape(DEPTH * n_seq, N_MEM, H_M, DH_M)
                vv = cache_mem_v.reshape(DEPTH * n_seq, N_MEM, H_M, DH_M)
                tma, seq0 = SAMPLE_PAD, i * n_seq
            om = mem_attn(qm, kk, vv, m_q_norm[i], tma, not prompt, seq0)
            x = _unflat(linear(_flat(om), bf(m_w_o[i]), res=xf, name="mem_o"), x)

            xf = _flat(x)
            hf = _unflat(linear(xf, bf(f_w_in[i]), g=norm_ffn[i], name="ffn_in"), x)
            st = zeros_state(W_F, D_FF) if prompt else state_ffn_conv[i]
            uf, nst = seq_conv([(hf, D_FF, 0), (hf, D_FF, 1)], st, f_w_conv[i], [f_b_conv[i]],
                               _f_pre, _f_post, D_FF, D_FF, BF16, grp["tm"], grp["lv"], "ffn_conv")
            outs["f"][gi].append(nst)
            x = _unflat(linear(_flat(uf), bf(f_w_out[i]), res=xf, name="ffn_out"), x)
            grp["x"] = x

    yp = groups[0]["x"]
    ys = groups[1]["x"][:, :l_s]
    st = lambda key, gi: jnp.stack(outs[key][gi])
    kvshape = lambda a, n, l: a.reshape(a.shape[0], n, -1, a.shape[-1])[:, :, :l]
    b_k_p = kvshape(st("bk", 0), n_p, l_p).reshape(-1, n_p, l_p, KV_B, DH_B)
    b_v_p = kvshape(st("bv", 0), n_p, l_p).reshape(-1, n_p, l_p, KV_B, DH_B)
    b_ki_p = kvshape(st("bki", 0), n_p, l_p)
    b_k_s = kvshape(st("bk", 1), n_s, l_s).reshape(-1, n_s, l_s, KV_B, DH_B)
    b_v_s = kvshape(st("bv", 1), n_s, l_s).reshape(-1, n_s, l_s, KV_B, DH_B)
    b_ki_s = kvshape(st("bki", 1), n_s, l_s)
    return (yp, ys, st("a", 0), st("a", 1), b_k_p, b_v_p, b_ki_p, b_k_s, b_v_s, b_ki_s,
            st("cc", 0), st("cs", 0), st("cc", 1), st("cs", 1), st("d", 0), st("d", 1),
            st("f", 0), st("f", 1), jnp.stack(m_kp), jnp.stack(m_vp))
```

```python
import functools
import math

import jax
import jax.numpy as jnp
from jax import lax
from jax.experimental import pallas as pl
from jax.experimental.pallas import tpu as pltpu

F32 = jnp.float32
BF16 = jnp.bfloat16
I32 = jnp.int32

D_MODEL = 1024
DEPTH = 4
PAST_LEN = 8192
PAGE_SIZE = 128
W_A = 31
H_B = 16
DH_B = 64
KV_B = 4
G_B = H_B // KV_B
H_IDX = 8
D_IDX = 64
TOPK = 256
N_BUCKETS = 32
MAX_DIST = 128
D_IN_C = 2 * D_MODEL
P_C = 64
H_C = D_IN_C // P_C
G_C = 4
N_SSM = 128
W_C = 4
CHUNK_C = 128
W_D = 3
N_MEM = 256
H_M = 4
DH_M = D_MODEL // H_M
D_FF = 2816
W_F = 3
EPS = 1e-6

V7X_SUBLANES = 8
V7X_LANES = 128
V7X_VMEM_LIMIT_BYTES = 56 * 1024 * 1024

SAMPLE_PAD = V7X_SUBLANES
NEG_BIG = -1e30
INT_MIN = -(2 ** 31)
PAGES_PER_STEP = 16


def _cparams(*sem):
    return pltpu.CompilerParams(dimension_semantics=sem, vmem_limit_bytes=V7X_VMEM_LIMIT_BYTES)


def _pick(n, cands):
    for c in cands:
        if n % c == 0:
            return c
    return n


def _rms(x, g):
    y = x * lax.rsqrt(jnp.mean(x * x, axis=-1, keepdims=True) + EPS)
    return y * g


def _dot(a, b):
    return jnp.dot(a, b, preferred_element_type=F32)


def _dot_t(a, b):
    return lax.dot_general(a, b, (((1,), (1,)), ((), ())), preferred_element_type=F32)


def _sigmoid(x):
    return 1.0 / (1.0 + jnp.exp(-x))


def _silu(x):
    return x * _sigmoid(x)


def _group_meansq(x, gsize):
    tm, c = x.shape
    x2 = x * x
    if gsize == c:
        return jnp.mean(x2, axis=-1, keepdims=True)
    if gsize % V7X_LANES == 0:
        parts = []
        for h in range(c // gsize):
            ms = jnp.mean(x2[:, h * gsize:(h + 1) * gsize], axis=-1, keepdims=True)
            parts.append(jnp.broadcast_to(ms, (tm, gsize)))
        return jnp.concatenate(parts, axis=-1)
    shift = int(math.log2(gsize))
    r = lax.shift_right_logical(lax.broadcasted_iota(I32, (c, c), 0), shift)
    q = lax.shift_right_logical(lax.broadcasted_iota(I32, (c, c), 1), shift)
    bd = (r == q).astype(BF16)
    hi = x2.astype(BF16)
    lo = (x2 - hi.astype(F32)).astype(BF16)
    return (_dot(hi, bd) + _dot(lo, bd)) * (1.0 / gsize)


def _group_rms(x, gain, gsize):
    return (x * lax.rsqrt(_group_meansq(x, gsize) + EPS)) * gain


def _linear_body(*refs, norm, bias, res):
    it = iter(refs)
    x_ref = next(it)
    g_ref = next(it) if norm else None
    w_ref = next(it)
    b_ref = next(it) if bias else None
    r_ref = next(it) if res else None
    o_ref = next(it)
    xn_ref = next(it)

    @pl.when(pl.program_id(1) == 0)
    def _():
        x = x_ref[...].astype(F32)
        if norm:
            x = _rms(x, g_ref[...])
        xn_ref[...] = x.astype(BF16)

    acc = _dot(xn_ref[...], w_ref[...])
    if bias:
        acc = acc + b_ref[...]
    if res:
        acc = acc + r_ref[...]
    o_ref[...] = acc.astype(o_ref.dtype)


def linear(x, w, *, g=None, b=None, res=None, out_dtype=F32, name="linear"):
    m, k = x.shape
    n = w.shape[1]
    tm = _pick(m, (1024, 512, 256, 128))
    tn = _pick(n, (1024, 1408, 512, 256, 128))
    in_specs = [pl.BlockSpec((tm, k), lambda i, j: (i, 0))]
    args = [x]
    if g is not None:
        in_specs.append(pl.BlockSpec((1, k), lambda i, j: (0, 0)))
        args.append(g.reshape(1, k).astype(F32))
    in_specs.append(pl.BlockSpec((k, tn), lambda i, j: (0, j)))
    args.append(w)
    if b is not None:
        in_specs.append(pl.BlockSpec((1, tn), lambda i, j: (0, j)))
        args.append(b.reshape(1, n).astype(F32))
    if res is not None:
        in_specs.append(pl.BlockSpec((tm, tn), lambda i, j: (i, j)))
        args.append(res)
    return pl.pallas_call(
        functools.partial(_linear_body, norm=g is not None, bias=b is not None, res=res is not None),
        grid=(m // tm, n // tn),
        in_specs=in_specs,
        out_specs=pl.BlockSpec((tm, tn), lambda i, j: (i, j)),
        out_shape=jax.ShapeDtypeStruct((m, n), out_dtype),
        scratch_shapes=[pltpu.VMEM((tm, k), BF16)],
        compiler_params=_cparams("arbitrary", "arbitrary"),
        name=name,
    )(*args)


def _head_norm_body(x_ref, g_ref, o_ref, ob_ref, *, gsize):
    y = _group_rms(x_ref[...], g_ref[...], gsize)
    o_ref[...] = y
    ob_ref[...] = y.astype(BF16)


def head_norm(x, col0, width, gain, gsize, name="head_norm"):
    m = x.shape[0]
    tm = _pick(m, (512, 256, 128))
    assert col0 % width == 0
    gt = jnp.tile(gain.astype(F32), width // gsize).reshape(1, width)
    return pl.pallas_call(
        functools.partial(_head_norm_body, gsize=gsize),
        grid=(m // tm,),
        in_specs=[pl.BlockSpec((tm, width), lambda i: (i, col0 // width)),
                  pl.BlockSpec((1, width), lambda i: (0, 0))],
        out_specs=[pl.BlockSpec((tm, width), lambda i: (i, 0)),
                   pl.BlockSpec((tm, width), lambda i: (i, 0))],
        out_shape=[jax.ShapeDtypeStruct((m, width), F32), jax.ShapeDtypeStruct((m, width), BF16)],
        compiler_params=_cparams("arbitrary"),
        name=name,
    )(x, gt)


def _seq_conv_body(*refs, n_in, n_par, pre, post, width, tm, lv, halo):
    in_refs = refs[:n_in]
    st_ref = refs[n_in]
    wc_ref = refs[n_in + 1]
    par_refs = refs[n_in + 2:n_in + 2 + n_par]
    o_ref, nst_ref, abuf, rbuf = refs[n_in + 2 + n_par:]
    t = pl.program_id(1)
    base = halo - (width - 1)

    @pl.when(t == 0)
    def _():
        abuf[base:halo, :] = st_ref[0]

    tiles = [r[0] for r in in_refs]
    pars = [r[...] for r in par_refs]
    abuf[halo:halo + tm, :] = pre(tiles)
    acc = None
    for r in range(V7X_SUBLANES):
        taps = [k for k in range(width) if (base + k) % V7X_SUBLANES == r]
        if not taps:
            continue
        qmax = max((base + k) // V7X_SUBLANES for k in taps)
        rows = V7X_SUBLANES * qmax + tm
        for k in taps:
            q0 = V7X_SUBLANES * ((base + k) // V7X_SUBLANES)
            if r == 0 or len(taps) == 1:
                shifted = abuf[r + q0:r + q0 + tm, :]
            else:
                if k == taps[0]:
                    rbuf[r, 0:rows, :] = abuf[r:r + rows, :]
                shifted = rbuf[r, q0:q0 + tm, :]
            term = wc_ref[k:k + 1, :] * shifted
            acc = term if acc is None else acc + term
    o_ref[0] = post(acc, tiles, pars).astype(o_ref.dtype)
    tail = abuf[base + lv:base + lv + width - 1, :]
    abuf[base:halo, :] = tail
    nst_ref[0] = tail


def seq_conv(inputs, state, wconv, params, pre, post, c_conv, c_out, out_dtype, tm, lv, name):
    n_seq, length, _ = inputs[0][0].shape
    width = wconv.shape[0]
    halo = -(-(width - 1) // V7X_SUBLANES) * V7X_SUBLANES
    in_specs = [pl.BlockSpec((1, tm, bw), functools.partial(lambda s, t, ci: (s, t, ci), ci=ci))
                for (_, bw, ci) in inputs]
    in_specs.append(pl.BlockSpec((1, width - 1, c_conv), lambda s, t: (s, 0, 0)))
    in_specs.append(pl.BlockSpec((width, c_conv), lambda s, t: (0, 0)))
    for p in params:
        in_specs.append(pl.BlockSpec((1, p.shape[-1]), lambda s, t: (0, 0)))
    out, nst = pl.pallas_call(
        functools.partial(_seq_conv_body, n_in=len(inputs), n_par=len(params), pre=pre, post=post,
                          width=width, tm=tm, lv=lv, halo=halo),
        grid=(n_seq, length // tm),
        in_specs=in_specs,
        out_specs=[pl.BlockSpec((1, tm, c_out), lambda s, t: (s, t, 0)),
                   pl.BlockSpec((1, width - 1, c_conv), lambda s, t: (s, 0, 0))],
        out_shape=[jax.ShapeDtypeStruct((n_seq, length, c_out), out_dtype),
                   jax.ShapeDtypeStruct((n_seq, width - 1, c_conv), F32)],
        scratch_shapes=[pltpu.VMEM((halo + tm, c_conv), F32),
                        pltpu.VMEM((V7X_SUBLANES, halo + tm, c_conv) if width > V7X_SUBLANES
                                   else (1, V7X_SUBLANES, V7X_LANES), F32)],
        compiler_params=_cparams("arbitrary", "arbitrary"),
        name=name,
    )(*[a for (a, _, _) in inputs], state, wconv.astype(F32),
      *[p.reshape(1, -1).astype(F32) for p in params])
    return out, nst


def _a_pre(tiles):
    a, g = tiles
    return a * _sigmoid(g)


def _a_post(y, tiles, pars):
    bc, lng, lnb = pars
    y = y + bc
    mu = jnp.mean(y, axis=-1, keepdims=True)
    yc = y - mu
    yn = yc * lax.rsqrt(jnp.mean(yc * yc, axis=-1, keepdims=True) + EPS)
    return _silu(yn * lng + lnb)


def _f_pre(tiles):
    return tiles[0]


def _f_post(y, tiles, pars):
    return _silu(y + pars[0]) * tiles[1]


def _d_pre(tiles):
    return tiles[1] * tiles[2]


def _d_post(y, tiles, pars):
    return tiles[0] * y


def _c_pre(tiles):
    return tiles[0]


def _c_post(y, tiles, pars):
    return _silu(y + pars[0])


FFN_CHUNK = 2 * V7X_LANES


def _ffn_body(x_ref, ng_ref, win_ref, st_ref, wc_ref, bc_ref, wout_ref, o_ref, nst_ref, abuf, *, sb, tm, lv, halo):
    t = pl.program_id(1)
    base = halo - (W_F - 1)
    d = x_ref.shape[-1]

    @pl.when(t == 0)
    def _():
        abuf[:, base:halo, :] = st_ref[...]

    x = x_ref[...].reshape(sb * tm, d)
    xn = _rms(x, ng_ref[...]).astype(BF16)
    out = x
    for c0 in range(0, D_FF, FFN_CHUNK):
        cs = slice(c0, c0 + FFN_CHUNK)
        abuf[:, halo:halo + tm, cs] = _dot(xn, win_ref[:, cs]).reshape(sb, tm, FFN_CHUNK)
        g = _dot(xn, win_ref[:, D_FF + c0:D_FF + c0 + FFN_CHUNK])
        y = None
        for k in range(W_F):
            term = wc_ref[k:k + 1, cs] * abuf[:, base + k:base + k + tm, cs]
            y = term if y is None else y + term
        y = y.reshape(sb * tm, FFN_CHUNK) + bc_ref[:, cs]
        out = out + _dot((_silu(y) * g).astype(BF16), wout_ref[cs, :])
    o_ref[...] = out.reshape(sb, tm, d)
    tail = abuf[:, base + lv:base + lv + W_F - 1, :]
    abuf[:, base:halo, :] = tail
    nst_ref[...] = tail


def ffn_fused(x, state, norm_g, w_in, w_conv, b_conv, w_out, sb, tm, lv):
    n_seq, length, d = x.shape
    halo = V7X_SUBLANES
    const = lambda s, t: (0, 0)
    resident = dict(pipeline_mode=pl.Buffered(1))
    return pl.pallas_call(
        functools.partial(_ffn_body, sb=sb, tm=tm, lv=lv, halo=halo),
        grid=(n_seq // sb, length // tm),
        in_specs=[pl.BlockSpec((sb, tm, d), lambda s, t: (s, t, 0)),
                  pl.BlockSpec((1, d), const),
                  pl.BlockSpec((d, 2 * D_FF), const, **resident),
                  pl.BlockSpec((sb, W_F - 1, D_FF), lambda s, t: (s, 0, 0)),
                  pl.BlockSpec((W_F, D_FF), const),
                  pl.BlockSpec((1, D_FF), const),
                  pl.BlockSpec((D_FF, d), const, **resident)],
        out_specs=[pl.BlockSpec((sb, tm, d), lambda s, t: (s, t, 0)),
                   pl.BlockSpec((sb, W_F - 1, D_FF), lambda s, t: (s, 0, 0))],
        out_shape=[jax.ShapeDtypeStruct((n_seq, length, d), F32),
                   jax.ShapeDtypeStruct((n_seq, W_F - 1, D_FF), F32)],
        scratch_shapes=[pltpu.VMEM((sb, halo + tm, D_FF), F32)],
        compiler_params=_cparams("arbitrary", "arbitrary"),
        name="ffn",
    )(x, norm_g.reshape(1, d).astype(F32), w_in, state, w_conv.astype(F32),
      b_conv.reshape(1, D_FF).astype(F32), w_out)


def _softmax_rows(s):
    p = jnp.exp(s - jnp.max(s, axis=-1, keepdims=True))
    return p / jnp.sum(p, axis=-1, keepdims=True)


def _mem_attn_body(q_ref, k_ref, v_ref, g_ref, o_ref, *, token_major):
    q = q_ref[0]
    tm = q.shape[0]
    hsl = [slice(h * DH_M, (h + 1) * DH_M) for h in range(H_M)]
    qh = [(_rms(q[:, sl], g_ref[...]) * (DH_M ** -0.5)).astype(BF16) for sl in hsl]
    if not token_major:
        for h in range(H_M):
            p = _softmax_rows(_dot_t(qh[h], k_ref[0, h].astype(BF16)))
            o_ref[0, :, hsl[h]] = _dot(p.astype(BF16), v_ref[0, h].astype(BF16)).astype(o_ref.dtype)
    else:
        k_all = k_ref[0].reshape(N_MEM * H_M, DH_M).astype(BF16)
        v_all = v_ref[0].reshape(N_MEM * H_M, DH_M).astype(BF16)
        s = _dot_t(jnp.concatenate(qh, axis=0), k_all)
        col_head = lax.broadcasted_iota(I32, s.shape, 1) & (H_M - 1)
        row_head = lax.shift_right_logical(lax.broadcasted_iota(I32, s.shape, 0), int(math.log2(tm)))
        p = _softmax_rows(jnp.where(col_head == row_head, s, NEG_BIG))
        o = _dot(p.astype(BF16), v_all)
        for h in range(H_M):
            o_ref[0, :, hsl[h]] = o[h * tm:(h + 1) * tm].astype(o_ref.dtype)


def mem_attn(q, k, v, q_gain, tm, token_major, kv_seq0=0, name="mem_attn"):
    n_seq, length, d = q.shape
    kv_block = (1,) + k.shape[1:]
    return pl.pallas_call(
        functools.partial(_mem_attn_body, token_major=token_major),
        grid=(n_seq, length // tm),
        in_specs=[pl.BlockSpec((1, tm, d), lambda s, t: (s, t, 0)),
                  pl.BlockSpec(kv_block, lambda s, t: (kv_seq0 + s, 0, 0, 0)),
                  pl.BlockSpec(kv_block, lambda s, t: (kv_seq0 + s, 0, 0, 0)),
                  pl.BlockSpec((1, DH_M), lambda s, t: (0, 0))],
        out_specs=pl.BlockSpec((1, tm, d), lambda s, t: (s, t, 0)),
        out_shape=jax.ShapeDtypeStruct((n_seq, length, d), BF16),
        compiler_params=_cparams("arbitrary", "arbitrary"),
        name=name,
    )(q, k, v, q_gain.reshape(1, DH_M).astype(F32))


HEADS_PER_PAIR = 2 * G_B


def _t5_lookup(tab_ref, h, dist):
    dist = jnp.maximum(dist, 0)
    max_exact = N_BUCKETS // 2
    df = jnp.maximum(dist, 1).astype(F32)
    large = max_exact + (jnp.log(df / max_exact) / math.log(MAX_DIST / max_exact)
                         * (N_BUCKETS - max_exact)).astype(I32)
    large = jnp.minimum(large, N_BUCKETS - 1)
    bucket = jnp.where(dist < max_exact, dist, large)
    acc = jnp.zeros(dist.shape, F32)
    for b in range(N_BUCKETS):
        acc = jnp.where(bucket == b, tab_ref[b, h], acc)
    return acc


def _t5_tiles_body(tab_ref, o_ref, ot_ref):
    h = pl.program_id(0)
    r = lax.broadcasted_iota(I32, (PAGE_SIZE, PAGE_SIZE), 0)
    c = lax.broadcasted_iota(I32, (PAGE_SIZE, PAGE_SIZE), 1)
    o_ref[0, :, 0:PAGE_SIZE] = _t5_lookup(tab_ref, h, r - c)
    o_ref[0, :, PAGE_SIZE:2 * PAGE_SIZE] = _t5_lookup(tab_ref, h, PAGE_SIZE + r - c)
    ot_ref[0, 0] = _t5_lookup(tab_ref, h, c - r)
    ot_ref[0, 1] = _t5_lookup(tab_ref, h, PAGE_SIZE + c - r)


def t5_tiles(table):
    return pl.pallas_call(
        _t5_tiles_body,
        grid=(H_B,),
        in_specs=[pl.BlockSpec(memory_space=pltpu.SMEM)],
        out_specs=[pl.BlockSpec((1, PAGE_SIZE, 2 * PAGE_SIZE), lambda h: (h, 0, 0)),
                   pl.BlockSpec((1, 2, PAGE_SIZE, PAGE_SIZE),
                                lambda h: (h // HEADS_PER_PAIR, 0, 0, h % HEADS_PER_PAIR))],
        out_shape=[jax.ShapeDtypeStruct((H_B, PAGE_SIZE, 2 * PAGE_SIZE), F32),
                   jax.ShapeDtypeStruct((H_B // HEADS_PER_PAIR, 2, PAGE_SIZE, HEADS_PER_PAIR * PAGE_SIZE), F32)],
        compiler_params=_cparams("arbitrary"),
        name="t5_tiles",
    )(table.astype(F32))


def _indexer_tile(qih, wcol, ki_tile):
    acc = jnp.zeros((qih[0].shape[0], ki_tile.shape[0]), F32)
    for h in range(H_IDX):
        acc = acc + jnp.maximum(_dot_t(qih[h], ki_tile), 0.0) * wcol[h]
    return acc


def _sort_key(score):
    score = jnp.where(score == 0.0, 0.0, score)
    u = lax.bitcast_convert_type(score, I32)
    return jnp.where(u < 0, u ^ 0x7FFFFFFF, u)


def _kth_largest(count_ge, shape, topk):
    def try_cand(cand, cur):
        return jnp.where(count_ge(cand) >= topk, cand, cur)

    t0 = try_cand(jnp.zeros(shape, I32), jnp.full(shape, INT_MIN, I32))

    def step(b, cur):
        cand = cur + lax.shift_left(jnp.int32(1), jnp.int32(30) - b)
        return try_cand(cand, cur)

    return lax.fori_loop(0, 31, step, t0)


def _strict_upper(n):
    r = lax.broadcasted_iota(I32, (n, n), 0)
    c = lax.broadcasted_iota(I32, (n, n), 1)
    return (r < c).astype(BF16)


def _strict_lower(n):
    r = lax.broadcasted_iota(I32, (n, n), 0)
    c = lax.broadcasted_iota(I32, (n, n), 1)
    return (c < r).astype(BF16)


def _tree(parts, op):
    parts = list(parts)
    while len(parts) > 1:
        nxt = [op(parts[k], parts[k + 1]) for k in range(0, len(parts) - 1, 2)]
        if len(parts) % 2:
            nxt.append(parts[-1])
        parts = nxt
    return parts[0]


def _fold_rows(x, op):
    return _tree([x[r:r + V7X_SUBLANES] for r in range(0, x.shape[0], V7X_SUBLANES)], op)


LOOP_UNROLL = 4


def _loop_tiles(n, body, carry, unroll=LOOP_UNROLL):
    shift = int(math.log2(unroll))
    n_main = lax.shift_right_logical(n, shift)
    carry = lax.fori_loop(0, n_main, lambda t, c: body(t * unroll, unroll, c), carry)
    return lax.fori_loop(n_main * unroll, n, lambda j, c: body(j, 1, c), carry)


def _dsa_prompt_body(q_ref, qi_ref, wi_ref, k_ref, vt_ref, ki_ref, bt_ref, o_ref,
                     key_ref, am_ref, lg_ref, acc_ref, ot_ref, *, topk):
    i = pl.program_id(1)
    nj = i + 1
    qb = PAGE_SIZE
    kpos = lax.broadcasted_iota(I32, (qb, qb), 0)
    qpos = lax.broadcasted_iota(I32, (qb, qb), 1)

    def kslice(j0, u=1):
        return pl.ds(pl.multiple_of(j0 * qb, qb), u * qb)

    qit = qi_ref[0].T
    qit = jnp.concatenate([qit[h * D_IDX:(h + 1) * D_IDX] for h in range(H_IDX)], axis=1).astype(BF16)
    wsc = (wi_ref[0] * (H_IDX ** -0.5)) * (D_IDX ** -0.5)
    wt = jnp.concatenate([wsc, jnp.zeros((qb, qb - H_IDX), F32)], axis=1).T
    wrow = jnp.concatenate([wt[h:h + 1] for h in range(H_IDX)], axis=1)

    def idx_step(j0, u, carry):
        s = jnp.maximum(_dot(ki_ref[0, kslice(j0, u), :], qit), 0.0) * wrow
        sc = s[:, 0:qb]
        for h in range(1, H_IDX):
            sc = sc + s[:, h * qb:(h + 1) * qb]
        key = _sort_key(sc)
        for t in range(u):
            valid = (j0 + t < i) | (kpos <= qpos)
            key_ref[j0 + t] = jnp.where(valid, key[t * qb:(t + 1) * qb], INT_MIN)
        return carry

    _loop_tiles(nj, idx_step, 0)

    def count_where(pred):
        def cstep(j0, u, c):
            for t in range(u):
                c = c + jnp.where(pred(key_ref[j0 + t]), 1.0, 0.0)
            return c
        c = _loop_tiles(nj, cstep, jnp.zeros((qb, qb), F32))
        return jnp.sum(_fold_rows(c, jnp.add), axis=0, keepdims=True)

    thr = _kth_largest(lambda cand: count_where(lambda key: key >= cand), (1, qb), topk)
    n_tie = topk - count_where(lambda key: key > thr)
    n_eq = count_where(lambda key: (key == thr) & (key != INT_MIN))
    need_rank = jnp.max(n_eq - n_tie) > 0.0

    @pl.when(jnp.logical_not(need_rank))
    def _():
        def mask_step(j0, u, carry):
            for t in range(u):
                key = key_ref[j0 + t]
                am_ref[j0 + t] = jnp.where((key >= thr) & (key != INT_MIN), 0.0, NEG_BIG)
            return carry
        _loop_tiles(nj, mask_step, 0)

    @pl.when(need_rank)
    def _():
        lower = _strict_lower(qb)

        def mask_step(j, run):
            key = key_ref[j]
            eq = (key == thr) & (key != INT_MIN)
            eqf = jnp.where(eq, 1.0, 0.0)
            rank = _dot(lower, eqf.astype(BF16)) + run
            sel = (key > thr) | (eq & (rank < n_tie))
            am_ref[j] = jnp.where(sel, 0.0, NEG_BIG)
            return run + jnp.sum(_fold_rows(eqf, jnp.add), axis=0, keepdims=True)

        lax.fori_loop(0, nj, mask_step, jnp.zeros((1, qb), F32))

    qt = q_ref[0].T
    pw = HEADS_PER_PAIR * qb
    gw = G_B * qb
    for pp in range(KV_B // 2):
        def group_qt(n):
            return jnp.concatenate([qt[h * DH_B:(h + 1) * DH_B] for h in range(n * G_B, (n + 1) * G_B)],
                                   axis=1)
        zero = jnp.zeros((DH_B, gw), F32)
        rhs = jnp.concatenate([jnp.concatenate([group_qt(2 * pp), zero], axis=1),
                               jnp.concatenate([zero, group_qt(2 * pp + 1)], axis=1)], axis=0)
        rhs = (rhs * (DH_B ** -0.5)).astype(BF16)
        klanes = slice(pp * 2 * DH_B, (pp + 1) * 2 * DH_B)
        far = bt_ref[pp, 1, 0:1, :]

        def tile_logits(j0, u, bias, mx):
            s = _dot(k_ref[0, kslice(j0, u), klanes], rhs)
            for t in range(u):
                st = s[t * qb:(t + 1) * qb] + bias + jnp.concatenate([am_ref[j0 + t]] * HEADS_PER_PAIR, axis=1)
                lg_ref[j0 + t] = st
                mx = jnp.maximum(mx, _fold_rows(st, jnp.maximum))
            return mx

        neg = jnp.full((V7X_SUBLANES, pw), NEG_BIG, F32)
        mx = _loop_tiles(jnp.maximum(i - 1, 0), lambda j0, u, m: tile_logits(j0, u, far, m), neg)
        mx = lax.cond(i >= 1, lambda m: tile_logits(i - 1, 1, bt_ref[pp, 1], m), lambda m: m, mx)
        mx = tile_logits(i, 1, bt_ref[pp, 0], mx)
        m = jnp.max(mx, axis=0, keepdims=True)
        acc_ref[...] = jnp.zeros(acc_ref.shape, F32)

        def p2(j0, u, l):
            ps = []
            for t in range(u):
                p = jnp.exp(lg_ref[j0 + t] - m)
                l = l + _fold_rows(p, jnp.add)
                ps.append(p.astype(BF16))
            pb = jnp.concatenate(ps, axis=0)
            vt = jnp.concatenate([vt_ref[0, j0 + t] for t in range(u)], axis=1)
            for gg in range(2):
                n = 2 * pp + gg
                acc_ref[gg] += _dot(vt[n * DH_B:(n + 1) * DH_B, :], pb[:, gg * gw:(gg + 1) * gw])
            return l

        l = _loop_tiles(nj, p2, jnp.zeros((V7X_SUBLANES, pw), F32))
        lsum = jnp.sum(l, axis=0, keepdims=True)
        for gg in range(2):
            o_t = acc_ref[gg] / lsum[:, gg * gw:(gg + 1) * gw]
            for hh in range(G_B):
                h = (2 * pp + gg) * G_B + hh
                ot_ref[h * DH_B:(h + 1) * DH_B, :] = o_t[:, hh * qb:(hh + 1) * qb]
    o_ref[0] = ot_ref[...].T.astype(o_ref.dtype)


def dsa_prompt(qn, qi, wi, kb, vtb, kib, btt):
    n_seq, length, _ = qn.shape
    nkb = length // PAGE_SIZE
    qmap = lambda s, i: (s, i, 0)
    kmap = lambda s, i: (s, 0, 0)
    pw = HEADS_PER_PAIR * PAGE_SIZE
    return pl.pallas_call(
        functools.partial(_dsa_prompt_body, topk=min(TOPK, length // 4)),
        grid=(n_seq, nkb),
        in_specs=[pl.BlockSpec((1, PAGE_SIZE, H_B * DH_B), qmap),
                  pl.BlockSpec((1, PAGE_SIZE, H_IDX * D_IDX), qmap),
                  pl.BlockSpec((1, PAGE_SIZE, H_IDX), qmap),
                  pl.BlockSpec((1, length, KV_B * DH_B), kmap),
                  pl.BlockSpec((1, nkb, KV_B * DH_B, PAGE_SIZE), lambda s, i: (s, 0, 0, 0)),
                  pl.BlockSpec((1, length, D_IDX), kmap),
                  pl.BlockSpec((H_B // HEADS_PER_PAIR, 2, PAGE_SIZE, pw), lambda s, i: (0, 0, 0, 0))],
        out_specs=pl.BlockSpec((1, PAGE_SIZE, H_B * DH_B), qmap),
        out_shape=jax.ShapeDtypeStruct((n_seq, length, H_B * DH_B), BF16),
        scratch_shapes=[pltpu.VMEM((nkb, PAGE_SIZE, PAGE_SIZE), I32),
                        pltpu.VMEM((nkb, PAGE_SIZE, PAGE_SIZE), F32),
                        pltpu.VMEM((nkb, PAGE_SIZE, pw), F32),
                        pltpu.VMEM((2, DH_B, G_B * PAGE_SIZE), F32),
                        pltpu.VMEM((H_B * DH_B, PAGE_SIZE), F32)],
        compiler_params=_cparams("arbitrary", "arbitrary"),
        name="dsa_prompt",
    )(qn, qi, wi, kb, vtb, kib, btt)


N_PAGES = PAST_LEN // PAGE_SIZE
N_KTILES = N_PAGES + 1
N_KTILES_PAD = -(-N_KTILES // PAGES_PER_STEP) * PAGES_PER_STEP


def _dsa_sample_sel_body(pt_ref, qi_ref, wi_ref, kin_ref, *rest, lv, topk):
    page_refs = rest[:PAGES_PER_STEP]
    am_ref, key_ref, qs_ref, ws_ref = rest[PAGES_PER_STEP:]
    c = pl.program_id(1)
    nsteps = pl.num_programs(1)
    r8 = SAMPLE_PAD

    @pl.when(c == 0)
    def _():
        qi = qi_ref[0]
        wsc = (wi_ref[0] * (H_IDX ** -0.5)) * (D_IDX ** -0.5)
        qs_ref[...] = jnp.concatenate([qi[:, h * D_IDX:(h + 1) * D_IDX] for h in range(H_IDX)],
                                      axis=0).astype(BF16)
        ws_ref[...] = jnp.concatenate([jnp.broadcast_to(wsc[:, h:h + 1], (r8, PAGE_SIZE))
                                       for h in range(H_IDX)], axis=0)

    def score(ki_t):
        s = jnp.maximum(_dot(qs_ref[...], ki_t), 0.0) * ws_ref[...]
        return _fold_rows(s, jnp.add)

    for r in range(PAGES_PER_STEP):
        key_ref[c * PAGES_PER_STEP + r] = _sort_key(score(page_refs[r][0].astype(BF16)))

    @pl.when(c == nsteps - 1)
    def _():
        rows = lax.broadcasted_iota(I32, (r8, PAGE_SIZE), 0)
        cols = lax.broadcasted_iota(I32, (r8, PAGE_SIZE), 1)
        new_valid = (cols <= rows) & (cols < lv)
        kin = jnp.concatenate([kin_ref[0], jnp.zeros((PAGE_SIZE - r8, D_IDX), F32)], axis=0)
        kin_t = jnp.concatenate([kin, jnp.zeros((PAGE_SIZE, PAGE_SIZE - D_IDX), F32)], axis=1).T[0:D_IDX]
        key_ref[N_PAGES] = jnp.where(new_valid, _sort_key(score(kin_t.astype(BF16))), INT_MIN)

        def tile_valid(j):
            return new_valid if j == N_PAGES else None

        def count_where(pred):
            parts = []
            for j in range(N_KTILES):
                hit = pred(key_ref[j])
                if tile_valid(j) is not None:
                    hit = hit & tile_valid(j)
                parts.append(jnp.where(hit, 1.0, 0.0))
            return jnp.sum(_tree(parts, jnp.add), axis=-1, keepdims=True)

        thr = _kth_largest(lambda cand: count_where(lambda key: key >= cand), (r8, 1), topk)
        n_tie = topk - count_where(lambda key: key > thr)
        n_eq = count_where(lambda key: key == thr)
        need_rank = jnp.max(n_eq - n_tie) > 0.0

        @pl.when(jnp.logical_not(need_rank))
        def _():
            for j in range(N_KTILES):
                sel = key_ref[j] >= thr
                if tile_valid(j) is not None:
                    sel = sel & tile_valid(j)
                am_ref[0, j] = jnp.where(sel, 0.0, NEG_BIG)

        @pl.when(need_rank)
        def _():
            upper = _strict_upper(PAGE_SIZE)

            def mask_step(j, run):
                key = key_ref[j]
                valid = (j < N_PAGES) | new_valid
                eq = valid & (key == thr)
                eqf = jnp.where(eq, 1.0, 0.0)
                rank = _dot(eqf.astype(BF16), upper) + run
                sel = (key > thr) | (eq & (rank < n_tie))
                am_ref[0, j] = jnp.where(sel, 0.0, NEG_BIG)
                return run + jnp.sum(eqf, axis=-1, keepdims=True)

            lax.fori_loop(0, N_KTILES, mask_step, jnp.zeros((r8, 1), F32))

        for j in range(N_KTILES, N_KTILES_PAD):
            am_ref[0, j] = jnp.full((r8, PAGE_SIZE), NEG_BIG, F32)


def dsa_sample_select(page_table, qi, wi, kin, kidx_pool_t, lv):
    n_seq = qi.shape[0]
    r8 = SAMPLE_PAD
    qmap = lambda s, c, pt: (s, 0, 0)
    page_specs = [pl.BlockSpec((1, D_IDX, PAGE_SIZE),
                               functools.partial(lambda s, c, pt, r: (pt[s, c * PAGES_PER_STEP + r], 0, 0), r=r))
                  for r in range(PAGES_PER_STEP)]
    grid_spec = pltpu.PrefetchScalarGridSpec(
        num_scalar_prefetch=1,
        grid=(n_seq, N_PAGES // PAGES_PER_STEP),
        in_specs=[pl.BlockSpec((1, r8, H_IDX * D_IDX), qmap),
                  pl.BlockSpec((1, r8, H_IDX), qmap),
                  pl.BlockSpec((1, r8, D_IDX), qmap)] + page_specs,
        out_specs=pl.BlockSpec((1, N_KTILES_PAD, r8, PAGE_SIZE), lambda s, c, pt: (s, 0, 0, 0)),
        scratch_shapes=[pltpu.VMEM((N_KTILES, r8, PAGE_SIZE), I32),
                        pltpu.VMEM((H_IDX * r8, D_IDX), BF16),
                        pltpu.VMEM((H_IDX * r8, PAGE_SIZE), F32)],
    )
    return pl.pallas_call(
        functools.partial(_dsa_sample_sel_body, lv=lv, topk=min(TOPK, (PAST_LEN + lv) // 4)),
        grid_spec=grid_spec,
        out_shape=jax.ShapeDtypeStruct((n_seq, N_KTILES_PAD, r8, PAGE_SIZE), F32),
        compiler_params=_cparams("arbitrary", "arbitrary"),
        name="dsa_sample_select",
    )(page_table, qi, wi, kin, *([kidx_pool_t] * PAGES_PER_STEP))


def _dsa_sample_attn_body(pt_ref, q_ref, kn_ref, vn_ref, am_ref, amn_ref, bt_ref, *rest):
    kp_refs = rest[:PAGES_PER_STEP]
    vp_refs = rest[PAGES_PER_STEP:2 * PAGES_PER_STEP]
    o_ref, m_ref, l_ref, acc_ref = rest[2 * PAGES_PER_STEP:]
    c = pl.program_id(1)
    nsteps = pl.num_programs(1)
    r8 = SAMPLE_PAD
    gr = G_B * r8

    @pl.when(c == 0)
    def _():
        m_ref[...] = jnp.full(m_ref.shape, NEG_BIG, F32)
        l_ref[...] = jnp.zeros(l_ref.shape, F32)
        acc_ref[...] = jnp.zeros(acc_ref.shape, F32)

    q = q_ref[0]
    qgs, b_subs, b_diags, b_fars = [], [], [], []
    for n in range(KV_B):
        heads = [n * G_B + hh for hh in range(G_B)]
        qg = jnp.concatenate([q[:, h * DH_B:(h + 1) * DH_B] for h in heads], axis=0)
        qgs.append((qg * (DH_B ** -0.5)).astype(BF16))
        b_diags.append(jnp.concatenate([bt_ref[h, :, 0:PAGE_SIZE] for h in heads], axis=0))
        b_subs.append(jnp.concatenate([bt_ref[h, :, PAGE_SIZE:2 * PAGE_SIZE] for h in heads], axis=0))
        b_fars.append(jnp.concatenate(
            [jnp.broadcast_to(bt_ref[h, 0:1, PAGE_SIZE:PAGE_SIZE + 1], (r8, PAGE_SIZE))
             for h in heads], axis=0))

    def update(n, s, pv):
        m_old = m_ref[n]
        m_new = jnp.maximum(m_old, jnp.max(s, axis=-1, keepdims=True))
        alpha = jnp.exp(m_old - m_new)
        p = jnp.exp(s - m_new[:, 0:1])
        l_ref[n] = alpha * l_ref[n] + jnp.sum(p, axis=-1, keepdims=True)
        acc_ref[n] = alpha[:, 0:DH_B] * acc_ref[n] + pv(p.astype(BF16))
        m_ref[n] = m_new

    def group_t(refs, n):
        return jnp.concatenate([ref[0, n] for ref in refs], axis=1).astype(BF16)

    amcat = jnp.concatenate([am_ref[0, r] for r in range(PAGES_PER_STEP)], axis=1)
    am4 = jnp.concatenate([amcat] * G_B, axis=0)
    is_last_step = c == nsteps - 1
    for n in range(KV_B):
        bias = jnp.concatenate([b_fars[n]] * (PAGES_PER_STEP - 1)
                               + [jnp.where(is_last_step, b_subs[n], b_fars[n])], axis=1)
        s = _dot(qgs[n], group_t(kp_refs, n)) + bias + am4
        v_t = group_t(vp_refs, n)
        update(n, s, lambda p: _dot_t(p, v_t))

    @pl.when(c == nsteps - 1)
    def _():
        pad = jnp.zeros((PAGE_SIZE - r8, KV_B * DH_B), F32)
        kn = jnp.concatenate([kn_ref[0], pad], axis=0).astype(BF16)
        vn = jnp.concatenate([vn_ref[0], pad], axis=0).astype(BF16)
        am4 = jnp.concatenate([amn_ref[0, 0]] * G_B, axis=0)
        for n in range(KV_B):
            ksl = slice(n * DH_B, (n + 1) * DH_B)
            s = _dot_t(qgs[n], kn[:, ksl]) + b_diags[n] + am4
            update(n, s, lambda p: _dot(p, vn[:, ksl]))
            o = acc_ref[n] / l_ref[n][:, 0:DH_B]
            for hh in range(G_B):
                h = n * G_B + hh
                o_ref[0, :, h * DH_B:(h + 1) * DH_B] = o[hh * r8:(hh + 1) * r8].astype(o_ref.dtype)


def dsa_sample_attend(page_table, qn, kn, vn, amask, bt, k_pool, v_pool):
    n_seq = qn.shape[0]
    r8 = SAMPLE_PAD
    qmap = lambda s, c, pt: (s, 0, 0)
    page_map = [functools.partial(lambda s, c, pt, r: (pt[s, c * PAGES_PER_STEP + r], 0, 0, 0), r=r)
                for r in range(PAGES_PER_STEP)]
    kv_w = KV_B * DH_B
    page_block = (1, KV_B, DH_B, PAGE_SIZE)
    grid_spec = pltpu.PrefetchScalarGridSpec(
        num_scalar_prefetch=1,
        grid=(n_seq, N_PAGES // PAGES_PER_STEP),
        in_specs=[pl.BlockSpec((1, r8, H_B * DH_B), qmap),
                  pl.BlockSpec((1, r8, kv_w), qmap),
                  pl.BlockSpec((1, r8, kv_w), qmap),
                  pl.BlockSpec((1, PAGES_PER_STEP, r8, PAGE_SIZE), lambda s, c, pt: (s, c, 0, 0)),
                  pl.BlockSpec((1, PAGES_PER_STEP, r8, PAGE_SIZE),
                               lambda s, c, pt: (s, N_PAGES // PAGES_PER_STEP, 0, 0)),
                  pl.BlockSpec((H_B, r8, 2 * PAGE_SIZE), lambda s, c, pt: (0, 0, 0))]
                 + [pl.BlockSpec(page_block, m) for m in page_map]
                 + [pl.BlockSpec(page_block, m) for m in page_map],
        out_specs=pl.BlockSpec((1, r8, H_B * DH_B), qmap),
        scratch_shapes=[pltpu.VMEM((KV_B, G_B * r8, PAGE_SIZE), F32),
                        pltpu.VMEM((KV_B, G_B * r8, PAGE_SIZE), F32),
                        pltpu.VMEM((KV_B, G_B * r8, DH_B), F32)],
    )
    return pl.pallas_call(
        _dsa_sample_attn_body,
        grid_spec=grid_spec,
        out_shape=jax.ShapeDtypeStruct((n_seq, r8, H_B * DH_B), BF16),
        compiler_params=_cparams("arbitrary", "arbitrary"),
        name="dsa_sample_attend",
    )(page_table, qn, kn, vn, amask, amask, bt,
      *([k_pool] * PAGES_PER_STEP), *([v_pool] * PAGES_PER_STEP))


HEADS_PER_GROUP = H_C // G_C
GROUP_W = D_IN_C // G_C


def _cumsum_rows(x):
    n = x.shape[0]
    idx = lax.broadcasted_iota(I32, x.shape, 0)
    d = 1
    while d < n:
        x = x + jnp.where(idx >= d, pltpu.roll(x, d, 0), 0.0)
        d *= 2
    return x


def _cumsum_lanes(x):
    n = x.shape[1]
    idx = lax.broadcasted_iota(I32, x.shape, 1)
    d = 1
    while d < n:
        x = x + jnp.where(idx >= d, pltpu.roll(x, d, 1), 0.0)
        d *= 2
    return x


def _softplus(x):
    return jnp.maximum(x, 0.0) + jnp.log1p(jnp.exp(-jnp.abs(x)))


def _split3(x):
    hi = x.astype(BF16)
    r = x - hi.astype(F32)
    mid = r.astype(BF16)
    lo = (r - mid.astype(F32)).astype(BF16)
    return hi, mid, lo


def _ssd_body(x_ref, xs_ref, bm_ref, cm_ref, z_ref, ng_ref, wdt_ref, wdtt_ref, dtb_ref, dtbt_ref,
              alog_ref, alogt_ref, dsk_ref, cn_ref, wo_ref, h0_ref, o_ref, hst_ref,
              dt_ref, dtt_ref, acs_ref, acst_ref, *, qin, lv):
    c = pl.program_id(1)
    g = pl.program_id(2)
    q = CHUNK_C
    hpg = HEADS_PER_GROUP
    rows = lax.broadcasted_iota(I32, (q, q), 0)
    cols = lax.broadcasted_iota(I32, (q, q), 1)
    tril = cols <= rows

    def pad_rows(a):
        if qin == q:
            return a
        return jnp.concatenate([a, jnp.zeros((q - qin, a.shape[1]), a.dtype)], axis=0)

    @pl.when(c == 0)
    def _():
        for j in range(hpg):
            hst_ref[0, g * hpg + j] = h0_ref[0, j]

    @pl.when(g == 0)
    def _():
        xn = _rms(pad_rows(x_ref[0]), ng_ref[...]).astype(BF16)
        dt = _softplus(_dot(xn, wdt_ref[...]) + dtb_ref[...])
        dtt = _softplus(_dot_t(wdtt_ref[...], xn) + dtbt_ref[...])
        if lv < q:
            dt = jnp.where(lax.broadcasted_iota(I32, dt.shape, 0) < lv, dt, 0.0)
            dtt = jnp.where(lax.broadcasted_iota(I32, dtt.shape, 1) < lv, dtt, 0.0)
        acs = _tree([_dot(tril.astype(BF16), p) for p in _split3(dt * (-jnp.exp(alog_ref[...])))], jnp.add)
        acst = _tree([_dot(p, (rows <= cols).astype(BF16))
                      for p in _split3(dtt * (-jnp.exp(alogt_ref[...])))], jnp.add)
        for gg in range(G_C):
            dt_ref[gg] = dt[:, gg * hpg:(gg + 1) * hpg]
            acs_ref[gg] = acs[:, gg * hpg:(gg + 1) * hpg]
            dtt_ref[gg] = dtt[gg * hpg:(gg + 1) * hpg, :]
            acst_ref[gg] = acst[gg * hpg:(gg + 1) * hpg, :]

    dt = dt_ref[g]
    dtt = dtt_ref[g]
    acs = acs_ref[g]
    acst = acst_ref[g]
    xs = pad_rows(xs_ref[0])
    bm = pad_rows(bm_ref[0])
    cm = pad_rows(cm_ref[0])
    xst = xs.T
    cmb = cm.astype(BF16)
    cb = _dot_t(cmb, bm.astype(BF16))
    ys = []
    for j in range(hpg):
        hd = g * hpg + j
        acs_col = acs[:, j:j + 1]
        dt_col = dt[:, j:j + 1]
        acs_row = acst[j:j + 1, :]
        dt_row = dtt[j:j + 1, :]
        a_last = acs_row[:, q - 1:q]
        xh = xs[:, j * P_C:(j + 1) * P_C] * dt_col
        xht = xst[j * P_C:(j + 1) * P_C, :] * dt_row
        lmat = jnp.exp(jnp.where(tril, acs_col - acs_row, -jnp.inf))
        y = _dot((cb * lmat).astype(BF16), xh.astype(BF16))
        h_old = hst_ref[0, hd]
        y = y + _dot_t(cmb, h_old.astype(BF16)) * jnp.exp(acs_col)
        decay = jnp.exp(a_last - acs_col)
        st = _dot(xht.astype(BF16), (bm * decay).astype(BF16))
        hst_ref[0, hd] = jnp.exp(a_last) * h_old + st
        ys.append(y)
    y = jnp.concatenate(ys, axis=1) + xs * dsk_ref[...]
    y = y * _silu(pad_rows(z_ref[0]))
    y = _rms(y, cn_ref[...]).astype(BF16)
    part = _dot(y, wo_ref[...])[0:qin]

    @pl.when(g == 0)
    def _():
        o_ref[0] = x_ref[0] + part

    @pl.when(g != 0)
    def _():
        o_ref[0] = o_ref[0] + part


def ssd_mix(x, xbc, proj, z_col0, h0, norm_g, w_dt, dt_bias, a_log, d_skip, c_norm, w_out, qin, lv):
    n_seq, length, d = x.shape
    nchunk = length // qin
    gw = GROUP_W
    col = lambda a: a.reshape(-1, 1).astype(F32)
    row = lambda a: a.reshape(1, -1).astype(F32)
    cmap = lambda off: (lambda s, c, g: (s, c, off + g))
    full2 = lambda s, c, g: (0, 0)
    in_specs = [
        pl.BlockSpec((1, qin, d), lambda s, c, g: (s, c, 0)),
        pl.BlockSpec((1, qin, gw), cmap(0)),
        pl.BlockSpec((1, qin, N_SSM), cmap(D_IN_C // N_SSM)),
        pl.BlockSpec((1, qin, N_SSM), cmap(D_IN_C // N_SSM + G_C)),
        pl.BlockSpec((1, qin, gw), cmap(z_col0 // gw)),
        pl.BlockSpec((1, d), full2),
        pl.BlockSpec((d, H_C), full2),
        pl.BlockSpec((H_C, d), full2),
        pl.BlockSpec((1, H_C), full2),
        pl.BlockSpec((H_C, 1), full2),
        pl.BlockSpec((1, H_C), full2),
        pl.BlockSpec((H_C, 1), full2),
        pl.BlockSpec((1, gw), lambda s, c, g: (0, g)),
        pl.BlockSpec((1, gw), lambda s, c, g: (0, g)),
        pl.BlockSpec((gw, d), lambda s, c, g: (g, 0)),
        pl.BlockSpec((1, HEADS_PER_GROUP, P_C, N_SSM), lambda s, c, g: (s, g, 0, 0)),
    ]
    out, hst = pl.pallas_call(
        functools.partial(_ssd_body, qin=qin, lv=lv),
        grid=(n_seq, nchunk, G_C),
        in_specs=in_specs,
        out_specs=[pl.BlockSpec((1, qin, d), lambda s, c, g: (s, c, 0)),
                   pl.BlockSpec((1, H_C, P_C, N_SSM), lambda s, c, g: (s, 0, 0, 0))],
        out_shape=[jax.ShapeDtypeStruct((n_seq, length, d), F32),
                   jax.ShapeDtypeStruct((n_seq, H_C, P_C, N_SSM), F32)],
        scratch_shapes=[pltpu.VMEM((G_C, CHUNK_C, HEADS_PER_GROUP), F32),
                        pltpu.VMEM((G_C, HEADS_PER_GROUP, CHUNK_C), F32),
                        pltpu.VMEM((G_C, CHUNK_C, HEADS_PER_GROUP), F32),
                        pltpu.VMEM((G_C, HEADS_PER_GROUP, CHUNK_C), F32)],
        compiler_params=_cparams("arbitrary", "arbitrary", "arbitrary"),
        name="ssd_mix",
    )(x, xbc, xbc, xbc, proj, row(norm_g), w_dt.astype(BF16), w_dt.T.astype(BF16),
      row(dt_bias), col(dt_bias), row(a_log), col(a_log),
      row(jnp.repeat(d_skip, P_C)), row(c_norm), w_out, h0)
    return out, hst


def _flat(a):
    return a.reshape(-1, a.shape[-1])


def _unflat(a, like):
    return a.reshape(like.shape[0], like.shape[1], a.shape[-1])


def kernel(x_prompt, x_sample, mem_prompt, state_a_conv, cache_b_k, cache_b_v, cache_b_kidx, state_c_conv, state_c_ssm, state_d_conv, state_ffn_conv, cache_mem_k, cache_mem_v, page_table, rel_bias, norm_mix, norm_mem, norm_ffn, norm_memtok, a_w_in, a_b_in, a_w_conv, a_b_conv, a_ln_g, a_ln_b, a_w_out, b_w_in, b_w_out, b_q_norm, b_k_norm, b_kidx_norm, c_w_in, c_w_conv, c_b_conv, c_dt_bias, c_a_log, c_d_skip, c_norm, c_w_out, d_w_in, d_w_conv, d_w_out, m_w_q, m_w_kv, m_w_o, m_q_norm, m_k_norm, f_w_in, f_w_conv, f_b_conv, f_w_out):
    n_p, l_p, d = x_prompt.shape
    n_s, l_s, _ = x_sample.shape
    bf = lambda w: w.astype(BF16)
    xs_pad = jnp.pad(x_sample, ((0, 0), (0, SAMPLE_PAD - l_s), (0, 0)))
    groups = [dict(x=x_prompt, tm=256, tmf=512, lv=256, qin=CHUNK_C, lvq=CHUNK_C, prompt=True),
              dict(x=xs_pad, tm=SAMPLE_PAD, tmf=SAMPLE_PAD, lv=l_s, qin=SAMPLE_PAD, lvq=l_s, prompt=False)]
    bt, btt = t5_tiles(rel_bias)
    mem_flat = _flat(mem_prompt)
    outs = {k: [[], []] for k in ("a", "bk", "bv", "bki", "cc", "cs", "d", "f")}
    m_kp, m_vp = [], []

    for i in range(DEPTH):
        j = i // 4
        kind = i % 4
        kv = linear(mem_flat, bf(m_w_kv[i]), g=norm_memtok[i], name="mem_kv")
        mk, _ = head_norm(kv, 0, d, m_k_norm[i], DH_M, name="mem_k_norm")
        m_kp.append(mk.reshape(n_p, N_MEM, H_M, DH_M))
        m_vp.append(kv[:, d:].reshape(n_p, N_MEM, H_M, DH_M))
        mk = m_kp[-1].transpose(0, 2, 1, 3)
        mv = m_vp[-1].transpose(0, 2, 1, 3)

        for gi, grp in enumerate(groups):
            x = grp["x"]
            n_seq, length, _ = x.shape
            prompt = grp["prompt"]
            zeros_state = lambda w, c: jnp.zeros((n_seq, w - 1, c), F32)
            xf = _flat(x)
            if kind == 0:
                ag = _unflat(linear(xf, bf(a_w_in[j]), g=norm_mix[i], b=a_b_in[j], name="a_in"), x)
                st = zeros_state(W_A, d) if prompt else state_a_conv[j]
                u, nst = seq_conv([(ag, d, 0), (ag, d, 1)], st, a_w_conv[j],
                                  [a_b_conv[j], a_ln_g[j], a_ln_b[j]], _a_pre, _a_post,
                                  d, d, BF16, grp["tm"], grp["lv"], "a_conv")
                outs["a"][gi].append(nst)
                x = _unflat(linear(_flat(u), bf(a_w_out[j]), res=xf, name="a_out"), x)
            elif kind == 1:
                w_in = b_w_in[j]
                n_main = H_B * DH_B + 2 * KV_B * DH_B + H_IDX * D_IDX
                w_tail = jnp.pad(w_in[:, n_main:], ((0, 0), (0, V7X_LANES - (w_in.shape[1] - n_main))))
                pm = linear(xf, bf(w_in[:, :n_main]), g=norm_mix[i], name="b_in")
                ptl = linear(xf, bf(w_tail), g=norm_mix[i], name="b_in_tail")
                kw = KV_B * DH_B
                qn, _ = head_norm(pm, 0, H_B * DH_B, b_q_norm[j], DH_B, name="b_q_norm")
                kn, knb = head_norm(pm, H_B * DH_B, kw, b_k_norm[j], DH_B, name="b_k_norm")
                kin, kinb = head_norm(ptl[:, :D_IDX], 0, D_IDX, b_kidx_norm[j], D_IDX, name="b_ki_norm")
                v = pm[:, H_B * DH_B + kw:H_B * DH_B + 2 * kw]
                qi = pm[:, H_B * DH_B + 2 * kw:]
                wi = ptl[:, D_IDX:D_IDX + H_IDX]
                r3 = lambda a: a.reshape(n_seq, length, a.shape[-1])
                if prompt:
                    vtb = bf(v).reshape(n_seq, length // PAGE_SIZE, PAGE_SIZE, kw).transpose(0, 1, 3, 2)
                    o = dsa_prompt(r3(qn), r3(qi), r3(wi), r3(knb), vtb, r3(kinb), btt)
                else:
                    am = dsa_sample_select(page_table, r3(qi), r3(wi), r3(kin),
                                           cache_b_kidx[j].transpose(0, 2, 1), grp["lv"])
                    o = dsa_sample_attend(page_table, r3(qn), r3(kn), r3(v), am, bt,
                                          cache_b_k[j].transpose(0, 2, 3, 1), cache_b_v[j].transpose(0, 2, 3, 1))
                outs["bk"][gi].append(r3(kn))
                outs["bv"][gi].append(r3(v))
                outs["bki"][gi].append(r3(kin))
                x = _unflat(linear(_flat(o), bf(b_w_out[j]), res=xf, name="b_out"), x)
            elif kind == 2:
                w_in = c_w_in[j]
                conv_c = D_IN_C + 2 * G_C * N_SSM
                w_main = jnp.concatenate([w_in[:, D_IN_C:D_IN_C + conv_c], w_in[:, :D_IN_C]], axis=1)
                proj = _unflat(linear(xf, bf(w_main), g=norm_mix[i], name="c_in"), x)
                st = zeros_state(W_C, conv_c) if prompt else state_c_conv[j]
                xbc, nst = seq_conv([(proj, conv_c, 0)], st, c_w_conv[j], [c_b_conv[j]], _c_pre, _c_post,
                                    conv_c, conv_c, F32, grp["tm"], grp["lv"], "c_conv")
                h0 = jnp.zeros((n_seq, H_C, P_C, N_SSM), F32) if prompt else state_c_ssm[j]
                x, hst = ssd_mix(x, xbc, proj, conv_c, h0, norm_mix[i], w_in[:, D_IN_C + conv_c:],
                                 c_dt_bias[j], c_a_log[j], c_d_skip[j], c_norm[j], bf(c_w_out[j]),
                                 grp["qin"], grp["lvq"])
                outs["cc"][gi].append(nst)
                outs["cs"][gi].append(hst)
            else:
                p3 = _unflat(linear(xf, bf(d_w_in[j]), g=norm_mix[i], name="d_in"), x)
                st = zeros_state(W_D, d) if prompt else state_d_conv[j]
                u, nst = seq_conv([(p3, d, 0), (p3, d, 1), (p3, d, 2)], st, d_w_conv[j], [],
                                  _d_pre, _d_post, d, d, BF16, grp["tm"], grp["lv"], "d_conv")
                outs["d"][gi].append(nst)
                x = _unflat(linear(_flat(u), bf(d_w_out[j]), res=xf, name="d_out"), x)

            xf = _flat(x)
            qm = _unflat(linear(xf, bf(m_w_q[i]), g=norm_mem[i], name="mem_q"), x)
            if prompt:
                kk, vv, tma, seq0 = mk, mv, 512, 0
            else:
                kk = cache_mem_k.reshape(DEPTH * n_seq, N_MEM, H_M, DH_M)
                vv = cache_mem_v.reshape(DEPTH * n_seq, N_MEM, H_M, DH_M)
                tma, seq0 = SAMPLE_PAD, i * n_seq
            om = mem_attn(qm, kk, vv, m_q_norm[i], tma, not prompt, seq0)
            x = _unflat(linear(_flat(om), bf(m_w_o[i]), res=xf, name="mem_o"), x)

            st = zeros_state(W_F, D_FF) if prompt else state_ffn_conv[i]
            x, nst = ffn_fused(x, st, norm_ffn[i], bf(f_w_in[i]), f_w_conv[i], f_b_conv[i], bf(f_w_out[i]),
                               1 if prompt else n_seq, grp["tmf"], grp["tmf"] if prompt else grp["lv"])
            outs["f"][gi].append(nst)
            grp["x"] = x

    yp = groups[0]["x"]
    ys = groups[1]["x"][:, :l_s]
    st = lambda key, gi: jnp.stack(outs[key][gi])
    kvshape = lambda a, n, l: a.reshape(a.shape[0], n, -1, a.shape[-1])[:, :, :l]
    b_k_p = kvshape(st("bk", 0), n_p, l_p).reshape(-1, n_p, l_p, KV_B, DH_B)
    b_v_p = kvshape(st("bv", 0), n_p, l_p).reshape(-1, n_p, l_p, KV_B, DH_B)
    b_ki_p = kvshape(st("bki", 0), n_p, l_p)
    b_k_s = kvshape(st("bk", 1), n_s, l_s).reshape(-1, n_s, l_s, KV_B, DH_B)
    b_v_s = kvshape(st("bv", 1), n_s, l_s).reshape(-1, n_s, l_s, KV_B, DH_B)
    b_ki_s = kvshape(st("bki", 1), n_s, l_s)
    return (yp, ys, st("a", 0), st("a", 1), b_k_p, b_v_p, b_ki_p, b_k_s, b_v_s, b_ki_s,
            st("cc", 0), st("cs", 0), st("cc", 1), st("cs", 1), st("d", 0), st("d", 1),
            st("f", 0), st("f", 1), jnp.stack(m_kp), jnp.stack(m_vp))
```

```python
import functools
import math

import jax
import jax.numpy as jnp
from jax import lax
from jax.experimental import pallas as pl
from jax.experimental.pallas import tpu as pltpu

F32 = jnp.float32
BF16 = jnp.bfloat16
I32 = jnp.int32

D_MODEL = 1024
DEPTH = 4
PAST_LEN = 8192
PAGE_SIZE = 128
W_A = 31
H_B = 16
DH_B = 64
KV_B = 4
G_B = H_B // KV_B
H_IDX = 8
D_IDX = 64
TOPK = 256
N_BUCKETS = 32
MAX_DIST = 128
D_IN_C = 2 * D_MODEL
P_C = 64
H_C = D_IN_C // P_C
G_C = 4
N_SSM = 128
W_C = 4
CHUNK_C = 128
W_D = 3
N_MEM = 256
H_M = 4
DH_M = D_MODEL // H_M
D_FF = 2816
W_F = 3
EPS = 1e-6

V7X_SUBLANES = 8
V7X_LANES = 128
V7X_VMEM_LIMIT_BYTES = 56 * 1024 * 1024

SAMPLE_PAD = V7X_SUBLANES
NEG_BIG = -1e30
INT_MIN = -(2 ** 31)
PAGES_PER_STEP = 16
MEM_SEQS_PER_STEP = 8


def _cparams(*sem):
    return pltpu.CompilerParams(dimension_semantics=sem, vmem_limit_bytes=V7X_VMEM_LIMIT_BYTES)


def _pick(n, cands):
    for c in cands:
        if n % c == 0:
            return c
    return n


def _rms(x, g):
    y = x * lax.rsqrt(jnp.mean(x * x, axis=-1, keepdims=True) + EPS)
    return y * g


def _dot(a, b):
    return jnp.dot(a, b, preferred_element_type=F32)


def _dot_t(a, b):
    return lax.dot_general(a, b, (((1,), (1,)), ((), ())), preferred_element_type=F32)


def _sigmoid(x):
    return 1.0 / (1.0 + jnp.exp(-x))


def _silu(x):
    return x * _sigmoid(x)


def _group_meansq(x, gsize):
    tm, c = x.shape
    x2 = x * x
    if gsize == c:
        return jnp.mean(x2, axis=-1, keepdims=True)
    if gsize % V7X_LANES == 0:
        parts = []
        for h in range(c // gsize):
            ms = jnp.mean(x2[:, h * gsize:(h + 1) * gsize], axis=-1, keepdims=True)
            parts.append(jnp.broadcast_to(ms, (tm, gsize)))
        return jnp.concatenate(parts, axis=-1)
    shift = int(math.log2(gsize))
    r = lax.shift_right_logical(lax.broadcasted_iota(I32, (c, c), 0), shift)
    q = lax.shift_right_logical(lax.broadcasted_iota(I32, (c, c), 1), shift)
    bd = (r == q).astype(BF16)
    hi = x2.astype(BF16)
    lo = (x2 - hi.astype(F32)).astype(BF16)
    return (_dot(hi, bd) + _dot(lo, bd)) * (1.0 / gsize)


def _group_rms(x, gain, gsize):
    return (x * lax.rsqrt(_group_meansq(x, gsize) + EPS)) * gain


def _linear_body(*refs, norm, bias, res):
    it = iter(refs)
    x_ref = next(it)
    g_ref = next(it) if norm else None
    w_ref = next(it)
    b_ref = next(it) if bias else None
    r_ref = next(it) if res else None
    o_ref = next(it)
    xn_ref = next(it)

    @pl.when(pl.program_id(1) == 0)
    def _():
        x = x_ref[...].astype(F32)
        if norm:
            x = _rms(x, g_ref[...])
        xn_ref[...] = x.astype(BF16)

    acc = _dot(xn_ref[...], w_ref[...])
    if bias:
        acc = acc + b_ref[...]
    if res:
        acc = acc + r_ref[...]
    o_ref[...] = acc.astype(o_ref.dtype)


def linear(x, w, *, g=None, b=None, res=None, out_dtype=F32, name="linear"):
    m, k = x.shape
    n = w.shape[1]
    tm = _pick(m, (1024, 512, 256, 128))
    tn = _pick(n, (1024, 1408, 512, 256, 128))
    in_specs = [pl.BlockSpec((tm, k), lambda i, j: (i, 0))]
    args = [x]
    if g is not None:
        in_specs.append(pl.BlockSpec((1, k), lambda i, j: (0, 0)))
        args.append(g.reshape(1, k).astype(F32))
    in_specs.append(pl.BlockSpec((k, tn), lambda i, j: (0, j)))
    args.append(w)
    if b is not None:
        in_specs.append(pl.BlockSpec((1, tn), lambda i, j: (0, j)))
        args.append(b.reshape(1, n).astype(F32))
    if res is not None:
        in_specs.append(pl.BlockSpec((tm, tn), lambda i, j: (i, j)))
        args.append(res)
    return pl.pallas_call(
        functools.partial(_linear_body, norm=g is not None, bias=b is not None, res=res is not None),
        grid=(m // tm, n // tn),
        in_specs=in_specs,
        out_specs=pl.BlockSpec((tm, tn), lambda i, j: (i, j)),
        out_shape=jax.ShapeDtypeStruct((m, n), out_dtype),
        scratch_shapes=[pltpu.VMEM((tm, k), BF16)],
        compiler_params=_cparams("arbitrary", "arbitrary"),
        name=name,
    )(*args)


def _head_norm_body(x_ref, g_ref, o_ref, ob_ref, *, gsize):
    y = _group_rms(x_ref[...], g_ref[...], gsize)
    o_ref[...] = y
    ob_ref[...] = y.astype(BF16)


def head_norm(x, col0, width, gain, gsize, name="head_norm"):
    m = x.shape[0]
    tm = _pick(m, (512, 256, 128))
    assert col0 % width == 0
    gt = jnp.tile(gain.astype(F32), width // gsize).reshape(1, width)
    return pl.pallas_call(
        functools.partial(_head_norm_body, gsize=gsize),
        grid=(m // tm,),
        in_specs=[pl.BlockSpec((tm, width), lambda i: (i, col0 // width)),
                  pl.BlockSpec((1, width), lambda i: (0, 0))],
        out_specs=[pl.BlockSpec((tm, width), lambda i: (i, 0)),
                   pl.BlockSpec((tm, width), lambda i: (i, 0))],
        out_shape=[jax.ShapeDtypeStruct((m, width), F32), jax.ShapeDtypeStruct((m, width), BF16)],
        compiler_params=_cparams("arbitrary"),
        name=name,
    )(x, gt)


def _seq_conv_body(*refs, n_in, n_par, pre, post, width, tm, lv, halo):
    in_refs = refs[:n_in]
    st_ref = refs[n_in]
    wc_ref = refs[n_in + 1]
    par_refs = refs[n_in + 2:n_in + 2 + n_par]
    o_ref, nst_ref, abuf, rbuf = refs[n_in + 2 + n_par:]
    t = pl.program_id(1)
    base = halo - (width - 1)

    @pl.when(t == 0)
    def _():
        abuf[base:halo, :] = st_ref[0]

    tiles = [r[0] for r in in_refs]
    pars = [r[...] for r in par_refs]
    abuf[halo:halo + tm, :] = pre(tiles)
    acc = None
    for r in range(V7X_SUBLANES):
        taps = [k for k in range(width) if (base + k) % V7X_SUBLANES == r]
        if not taps:
            continue
        qmax = max((base + k) // V7X_SUBLANES for k in taps)
        rows = V7X_SUBLANES * qmax + tm
        for k in taps:
            q0 = V7X_SUBLANES * ((base + k) // V7X_SUBLANES)
            if r == 0 or len(taps) == 1:
                shifted = abuf[r + q0:r + q0 + tm, :]
            else:
                if k == taps[0]:
                    rbuf[r, 0:rows, :] = abuf[r:r + rows, :]
                shifted = rbuf[r, q0:q0 + tm, :]
            term = wc_ref[k:k + 1, :] * shifted
            acc = term if acc is None else acc + term
    o_ref[0] = post(acc, tiles, pars).astype(o_ref.dtype)
    tail = abuf[base + lv:base + lv + width - 1, :]
    abuf[base:halo, :] = tail
    nst_ref[0] = tail


def seq_conv(inputs, state, wconv, params, pre, post, c_conv, c_out, out_dtype, tm, lv, name):
    n_seq, length, _ = inputs[0][0].shape
    width = wconv.shape[0]
    halo = -(-(width - 1) // V7X_SUBLANES) * V7X_SUBLANES
    in_specs = [pl.BlockSpec((1, tm, bw), functools.partial(lambda s, t, ci: (s, t, ci), ci=ci))
                for (_, bw, ci) in inputs]
    in_specs.append(pl.BlockSpec((1, width - 1, c_conv), lambda s, t: (s, 0, 0)))
    in_specs.append(pl.BlockSpec((width, c_conv), lambda s, t: (0, 0)))
    for p in params:
        in_specs.append(pl.BlockSpec((1, p.shape[-1]), lambda s, t: (0, 0)))
    out, nst = pl.pallas_call(
        functools.partial(_seq_conv_body, n_in=len(inputs), n_par=len(params), pre=pre, post=post,
                          width=width, tm=tm, lv=lv, halo=halo),
        grid=(n_seq, length // tm),
        in_specs=in_specs,
        out_specs=[pl.BlockSpec((1, tm, c_out), lambda s, t: (s, t, 0)),
                   pl.BlockSpec((1, width - 1, c_conv), lambda s, t: (s, 0, 0))],
        out_shape=[jax.ShapeDtypeStruct((n_seq, length, c_out), out_dtype),
                   jax.ShapeDtypeStruct((n_seq, width - 1, c_conv), F32)],
        scratch_shapes=[pltpu.VMEM((halo + tm, c_conv), F32),
                        pltpu.VMEM((V7X_SUBLANES, halo + tm, c_conv) if width > V7X_SUBLANES
                                   else (1, V7X_SUBLANES, V7X_LANES), F32)],
        compiler_params=_cparams("arbitrary", "arbitrary"),
        name=name,
    )(*[a for (a, _, _) in inputs], state, wconv.astype(F32),
      *[p.reshape(1, -1).astype(F32) for p in params])
    return out, nst


def _a_pre(tiles):
    a, g = tiles
    return a * _sigmoid(g)


def _a_post(y, tiles, pars):
    bc, lng, lnb = pars
    y = y + bc
    mu = jnp.mean(y, axis=-1, keepdims=True)
    yc = y - mu
    yn = yc * lax.rsqrt(jnp.mean(yc * yc, axis=-1, keepdims=True) + EPS)
    return _silu(yn * lng + lnb)


def _f_pre(tiles):
    return tiles[0]


def _f_post(y, tiles, pars):
    return _silu(y + pars[0]) * tiles[1]


def _d_pre(tiles):
    return tiles[1] * tiles[2]


def _d_post(y, tiles, pars):
    return tiles[0] * y


def _c_pre(tiles):
    return tiles[0]


def _c_post(y, tiles, pars):
    return _silu(y + pars[0])


def _conv_mixer_body(*refs, n_par, n_parts, pre, post, width, sb, tm, lv, halo, realign):
    x_ref, ng_ref, win_ref, bin_ref, st_ref, wc_ref = refs[:6]
    par_refs = refs[6:6 + n_par]
    wout_ref, o_ref, nst_ref, abuf, rbuf = refs[6 + n_par:]
    t = pl.program_id(1)
    base = halo - (width - 1)
    d = x_ref.shape[-1]
    c = abuf.shape[-1]

    @pl.when(t == 0)
    def _():
        abuf[:, base:halo, :] = st_ref[...]

    x = x_ref[...].reshape(sb * tm, d)
    proj = _dot(_rms(x, ng_ref[...]).astype(BF16), win_ref[...]) + bin_ref[...]
    parts = [proj[:, p * c:(p + 1) * c] for p in range(n_parts)]
    abuf[:, halo:halo + tm, :] = pre(parts).reshape(sb, tm, c)
    acc = None
    for r in range(V7X_SUBLANES):
        taps = [k for k in range(width) if (base + k) % V7X_SUBLANES == r]
        if not taps:
            continue
        rows = V7X_SUBLANES * max((base + k) // V7X_SUBLANES for k in taps) + tm
        for k in taps:
            q0 = V7X_SUBLANES * ((base + k) // V7X_SUBLANES)
            if r == 0 or len(taps) == 1 or not realign:
                shifted = abuf[:, r + q0:r + q0 + tm, :]
            else:
                if k == taps[0]:
                    rbuf[r, :, 0:rows, :] = abuf[:, r:r + rows, :]
                shifted = rbuf[r, :, q0:q0 + tm, :]
            term = wc_ref[k:k + 1, :] * shifted
            acc = term if acc is None else acc + term
    u = post(acc.reshape(sb * tm, c), parts, [p[...] for p in par_refs]).astype(BF16)
    o_ref[...] = (x + _dot(u, wout_ref[...])).reshape(sb, tm, d)
    tail = abuf[:, base + lv:base + lv + width - 1, :]
    abuf[:, base:halo, :] = tail
    nst_ref[...] = tail


def conv_mixer(x, state, norm_g, w_in, b_in, w_conv, params, w_out, pre, post, sb, tm, lv, name):
    n_seq, length, d = x.shape
    width, c = w_conv.shape
    n_parts = w_in.shape[1] // c
    halo = -(-(width - 1) // V7X_SUBLANES) * V7X_SUBLANES
    realign = width > V7X_SUBLANES and tm >= 8 * V7X_SUBLANES
    const = lambda s, t: (0, 0)
    b_in = jnp.zeros((n_parts * c,), F32) if b_in is None else b_in
    in_specs = [pl.BlockSpec((sb, tm, d), lambda s, t: (s, t, 0)),
                pl.BlockSpec((1, d), const),
                pl.BlockSpec((d, n_parts * c), const),
                pl.BlockSpec((1, n_parts * c), const),
                pl.BlockSpec((sb, width - 1, c), lambda s, t: (s, 0, 0)),
                pl.BlockSpec((width, c), const)]
    in_specs += [pl.BlockSpec((1, c), const) for _ in params]
    in_specs.append(pl.BlockSpec((c, d), const))
    return pl.pallas_call(
        functools.partial(_conv_mixer_body, n_par=len(params), n_parts=n_parts, pre=pre, post=post,
                          width=width, sb=sb, tm=tm, lv=lv, halo=halo, realign=realign),
        grid=(n_seq // sb, length // tm),
        in_specs=in_specs,
        out_specs=[pl.BlockSpec((sb, tm, d), lambda s, t: (s, t, 0)),
                   pl.BlockSpec((sb, width - 1, c), lambda s, t: (s, 0, 0))],
        out_shape=[jax.ShapeDtypeStruct((n_seq, length, d), F32),
                   jax.ShapeDtypeStruct((n_seq, width - 1, c), F32)],
        scratch_shapes=[pltpu.VMEM((sb, halo + tm, c), F32),
                        pltpu.VMEM((V7X_SUBLANES, sb, halo + tm, c) if realign
                                   else (1, 1, V7X_SUBLANES, V7X_LANES), F32)],
        compiler_params=_cparams("arbitrary", "arbitrary"),
        name=name,
    )(x, norm_g.reshape(1, d).astype(F32), w_in, b_in.reshape(1, -1).astype(F32), state, w_conv.astype(F32),
      *[p.reshape(1, c).astype(F32) for p in params], w_out)


FFN_CHUNK = 2 * V7X_LANES


def _ffn_body(x_ref, ng_ref, win_ref, st_ref, wc_ref, bc_ref, wout_ref, o_ref, nst_ref, abuf, *, sb, tm, lv, halo):
    t = pl.program_id(1)
    base = halo - (W_F - 1)
    d = x_ref.shape[-1]

    @pl.when(t == 0)
    def _():
        abuf[:, base:halo, :] = st_ref[...]

    x = x_ref[...].reshape(sb * tm, d)
    xn = _rms(x, ng_ref[...]).astype(BF16)
    out = x
    for c0 in range(0, D_FF, FFN_CHUNK):
        cs = slice(c0, c0 + FFN_CHUNK)
        abuf[:, halo:halo + tm, cs] = _dot(xn, win_ref[:, cs]).reshape(sb, tm, FFN_CHUNK)
        g = _dot(xn, win_ref[:, D_FF + c0:D_FF + c0 + FFN_CHUNK])
        y = None
        for k in range(W_F):
            term = wc_ref[k:k + 1, cs] * abuf[:, base + k:base + k + tm, cs]
            y = term if y is None else y + term
        y = y.reshape(sb * tm, FFN_CHUNK) + bc_ref[:, cs]
        out = out + _dot((_silu(y) * g).astype(BF16), wout_ref[cs, :])
    o_ref[...] = out.reshape(sb, tm, d)
    tail = abuf[:, base + lv:base + lv + W_F - 1, :]
    abuf[:, base:halo, :] = tail
    nst_ref[...] = tail


def ffn_fused(x, state, norm_g, w_in, w_conv, b_conv, w_out, sb, tm, lv):
    n_seq, length, d = x.shape
    halo = V7X_SUBLANES
    const = lambda s, t: (0, 0)
    resident = dict(pipeline_mode=pl.Buffered(1))
    return pl.pallas_call(
        functools.partial(_ffn_body, sb=sb, tm=tm, lv=lv, halo=halo),
        grid=(n_seq // sb, length // tm),
        in_specs=[pl.BlockSpec((sb, tm, d), lambda s, t: (s, t, 0)),
                  pl.BlockSpec((1, d), const),
                  pl.BlockSpec((d, 2 * D_FF), const, **resident),
                  pl.BlockSpec((sb, W_F - 1, D_FF), lambda s, t: (s, 0, 0)),
                  pl.BlockSpec((W_F, D_FF), const),
                  pl.BlockSpec((1, D_FF), const),
                  pl.BlockSpec((D_FF, d), const, **resident)],
        out_specs=[pl.BlockSpec((sb, tm, d), lambda s, t: (s, t, 0)),
                   pl.BlockSpec((sb, W_F - 1, D_FF), lambda s, t: (s, 0, 0))],
        out_shape=[jax.ShapeDtypeStruct((n_seq, length, d), F32),
                   jax.ShapeDtypeStruct((n_seq, W_F - 1, D_FF), F32)],
        scratch_shapes=[pltpu.VMEM((sb, halo + tm, D_FF), F32)],
        compiler_params=_cparams("arbitrary", "arbitrary"),
        name="ffn",
    )(x, norm_g.reshape(1, d).astype(F32), w_in, state, w_conv.astype(F32),
      b_conv.reshape(1, D_FF).astype(F32), w_out)


def _softmax_rows(s):
    p = jnp.exp(s - jnp.max(s, axis=-1, keepdims=True))
    return p / jnp.sum(p, axis=-1, keepdims=True)


def _mem_attend(q, k, v, gain, token_major):
    tm = q.shape[0]
    qh = [(_rms(q[:, h * DH_M:(h + 1) * DH_M], gain) * (DH_M ** -0.5)).astype(BF16) for h in range(H_M)]
    if not token_major:
        outs = []
        for h in range(H_M):
            p = _softmax_rows(_dot_t(qh[h], k[h].astype(BF16)))
            outs.append(_dot(p.astype(BF16), v[h].astype(BF16)))
    else:
        k_all = k.reshape(N_MEM * H_M, DH_M).astype(BF16)
        v_all = v.reshape(N_MEM * H_M, DH_M).astype(BF16)
        s = _dot_t(jnp.concatenate(qh, axis=0), k_all)
        col_head = lax.broadcasted_iota(I32, s.shape, 1) & (H_M - 1)
        row_head = lax.shift_right_logical(lax.broadcasted_iota(I32, s.shape, 0), int(math.log2(tm)))
        p = _softmax_rows(jnp.where(col_head == row_head, s, NEG_BIG))
        o = _dot(p.astype(BF16), v_all)
        outs = [o[h * tm:(h + 1) * tm] for h in range(H_M)]
    return jnp.concatenate(outs, axis=1).astype(BF16)


def _mem_attn_body(q_ref, k_ref, v_ref, g_ref, o_ref, *, sb):
    for s in range(sb):
        o_ref[s] = _mem_attend(q_ref[s], k_ref[s], v_ref[s], g_ref[...], True)


def mem_attn_cached(q, k, v, q_gain, sb, kv_seq0):
    n_seq, tm, d = q.shape
    kv_block = (sb,) + k.shape[1:]
    kv_map = lambda s: (kv_seq0 // sb + s, 0, 0, 0)
    return pl.pallas_call(
        functools.partial(_mem_attn_body, sb=sb),
        grid=(n_seq // sb,),
        in_specs=[pl.BlockSpec((sb, tm, d), lambda s: (s, 0, 0)),
                  pl.BlockSpec(kv_block, kv_map),
                  pl.BlockSpec(kv_block, kv_map),
                  pl.BlockSpec((1, DH_M), lambda s: (0, 0))],
        out_specs=pl.BlockSpec((sb, tm, d), lambda s: (s, 0, 0)),
        out_shape=jax.ShapeDtypeStruct((n_seq, tm, d), BF16),
        compiler_params=_cparams("arbitrary"),
        name="mem_attn",
    )(q, k, v, q_gain.reshape(1, DH_M).astype(F32))


def _mem_fused_body(x_ref, ng_ref, wq_ref, k_ref, v_ref, g_ref, wo_ref, o_ref):
    x = x_ref[0]
    q = _dot(_rms(x, ng_ref[...]).astype(BF16), wq_ref[...])
    o = _mem_attend(q, k_ref[0], v_ref[0], g_ref[...], False)
    o_ref[0] = x + _dot(o, wo_ref[...])


def mem_fused(x, k, v, norm_g, w_q, q_gain, w_o, tm):
    n_seq, length, d = x.shape
    const = lambda s, t: (0, 0)
    kv_block = (1,) + k.shape[1:]
    return pl.pallas_call(
        _mem_fused_body,
        grid=(n_seq, length // tm),
        in_specs=[pl.BlockSpec((1, tm, d), lambda s, t: (s, t, 0)),
                  pl.BlockSpec((1, d), const),
                  pl.BlockSpec((d, d), const),
                  pl.BlockSpec(kv_block, lambda s, t: (s, 0, 0, 0)),
                  pl.BlockSpec(kv_block, lambda s, t: (s, 0, 0, 0)),
                  pl.BlockSpec((1, DH_M), const),
                  pl.BlockSpec((d, d), const)],
        out_specs=pl.BlockSpec((1, tm, d), lambda s, t: (s, t, 0)),
        out_shape=jax.ShapeDtypeStruct((n_seq, length, d), F32),
        compiler_params=_cparams("arbitrary", "arbitrary"),
        name="mem_fused",
    )(x, norm_g.reshape(1, d).astype(F32), w_q, k, v, q_gain.reshape(1, DH_M).astype(F32), w_o)


HEADS_PER_PAIR = 2 * G_B


def _t5_lookup(tab_ref, h, dist):
    dist = jnp.maximum(dist, 0)
    max_exact = N_BUCKETS // 2
    df = jnp.maximum(dist, 1).astype(F32)
    large = max_exact + (jnp.log(df / max_exact) / math.log(MAX_DIST / max_exact)
                         * (N_BUCKETS - max_exact)).astype(I32)
    large = jnp.minimum(large, N_BUCKETS - 1)
    bucket = jnp.where(dist < max_exact, dist, large)
    acc = jnp.zeros(dist.shape, F32)
    for b in range(N_BUCKETS):
        acc = jnp.where(bucket == b, tab_ref[b, h], acc)
    return acc


def _t5_tiles_body(tab_ref, o_ref, ot_ref):
    h = pl.program_id(0)
    r = lax.broadcasted_iota(I32, (PAGE_SIZE, PAGE_SIZE), 0)
    c = lax.broadcasted_iota(I32, (PAGE_SIZE, PAGE_SIZE), 1)
    o_ref[0, :, 0:PAGE_SIZE] = _t5_lookup(tab_ref, h, r - c)
    o_ref[0, :, PAGE_SIZE:2 * PAGE_SIZE] = _t5_lookup(tab_ref, h, PAGE_SIZE + r - c)
    ot_ref[0, 0] = _t5_lookup(tab_ref, h, c - r)
    ot_ref[0, 1] = _t5_lookup(tab_ref, h, PAGE_SIZE + c - r)


def t5_tiles(table):
    return pl.pallas_call(
        _t5_tiles_body,
        grid=(H_B,),
        in_specs=[pl.BlockSpec(memory_space=pltpu.SMEM)],
        out_specs=[pl.BlockSpec((1, PAGE_SIZE, 2 * PAGE_SIZE), lambda h: (h, 0, 0)),
                   pl.BlockSpec((1, 2, PAGE_SIZE, PAGE_SIZE),
                                lambda h: (h // HEADS_PER_PAIR, 0, 0, h % HEADS_PER_PAIR))],
        out_shape=[jax.ShapeDtypeStruct((H_B, PAGE_SIZE, 2 * PAGE_SIZE), F32),
                   jax.ShapeDtypeStruct((H_B // HEADS_PER_PAIR, 2, PAGE_SIZE, HEADS_PER_PAIR * PAGE_SIZE), F32)],
        compiler_params=_cparams("arbitrary"),
        name="t5_tiles",
    )(table.astype(F32))


def _indexer_tile(qih, wcol, ki_tile):
    acc = jnp.zeros((qih[0].shape[0], ki_tile.shape[0]), F32)
    for h in range(H_IDX):
        acc = acc + jnp.maximum(_dot_t(qih[h], ki_tile), 0.0) * wcol[h]
    return acc


def _sort_key(score):
    score = jnp.where(score == 0.0, 0.0, score)
    u = lax.bitcast_convert_type(score, I32)
    return jnp.where(u < 0, u ^ 0x7FFFFFFF, u)


def _kth_largest(count_ge, shape, topk):
    def try_cand(cand, cur):
        return jnp.where(count_ge(cand) >= topk, cand, cur)

    t0 = try_cand(jnp.zeros(shape, I32), jnp.full(shape, INT_MIN, I32))

    def step(b, cur):
        cand = cur + lax.shift_left(jnp.int32(1), jnp.int32(30) - b)
        return try_cand(cand, cur)

    return lax.fori_loop(0, 31, step, t0)


def _strict_upper(n):
    r = lax.broadcasted_iota(I32, (n, n), 0)
    c = lax.broadcasted_iota(I32, (n, n), 1)
    return (r < c).astype(BF16)


def _strict_lower(n):
    r = lax.broadcasted_iota(I32, (n, n), 0)
    c = lax.broadcasted_iota(I32, (n, n), 1)
    return (c < r).astype(BF16)


def _tree(parts, op):
    parts = list(parts)
    while len(parts) > 1:
        nxt = [op(parts[k], parts[k + 1]) for k in range(0, len(parts) - 1, 2)]
        if len(parts) % 2:
            nxt.append(parts[-1])
        parts = nxt
    return parts[0]


def _fold_rows(x, op):
    return _tree([x[r:r + V7X_SUBLANES] for r in range(0, x.shape[0], V7X_SUBLANES)], op)


LOOP_UNROLL = 4


def _loop_tiles(n, body, carry, unroll=LOOP_UNROLL):
    shift = int(math.log2(unroll))
    n_main = lax.shift_right_logical(n, shift)
    carry = lax.fori_loop(0, n_main, lambda t, c: body(t * unroll, unroll, c), carry)
    return lax.fori_loop(n_main * unroll, n, lambda j, c: body(j, 1, c), carry)


def _dsa_prompt_body(q_ref, qi_ref, wi_ref, k_ref, vt_ref, ki_ref, bt_ref, o_ref,
                     key_ref, am_ref, lg_ref, acc_ref, ot_ref, *, topk):
    i = pl.program_id(1)
    nj = i + 1
    qb = PAGE_SIZE
    kpos = lax.broadcasted_iota(I32, (qb, qb), 0)
    qpos = lax.broadcasted_iota(I32, (qb, qb), 1)

    def kslice(j0, u=1):
        return pl.ds(pl.multiple_of(j0 * qb, qb), u * qb)

    qit = qi_ref[0].T
    qit = jnp.concatenate([qit[h * D_IDX:(h + 1) * D_IDX] for h in range(H_IDX)], axis=1).astype(BF16)
    wsc = (wi_ref[0] * (H_IDX ** -0.5)) * (D_IDX ** -0.5)
    wt = jnp.concatenate([wsc, jnp.zeros((qb, qb - H_IDX), F32)], axis=1).T
    wrow = jnp.concatenate([wt[h:h + 1] for h in range(H_IDX)], axis=1)

    def idx_step(j0, u, carry):
        s = jnp.maximum(_dot(ki_ref[0, kslice(j0, u), :], qit), 0.0) * wrow
        sc = s[:, 0:qb]
        for h in range(1, H_IDX):
            sc = sc + s[:, h * qb:(h + 1) * qb]
        key = _sort_key(sc)
        for t in range(u):
            valid = (j0 + t < i) | (kpos <= qpos)
            key_ref[j0 + t] = jnp.where(valid, key[t * qb:(t + 1) * qb], INT_MIN)
        return carry

    _loop_tiles(nj, idx_step, 0)

    def count_where(pred):
        def cstep(j0, u, c):
            for t in range(u):
                c = c + jnp.where(pred(key_ref[j0 + t]), 1.0, 0.0)
            return c
        c = _loop_tiles(nj, cstep, jnp.zeros((qb, qb), F32))
        return jnp.sum(_fold_rows(c, jnp.add), axis=0, keepdims=True)

    thr = _kth_largest(lambda cand: count_where(lambda key: key >= cand), (1, qb), topk)
    n_tie = topk - count_where(lambda key: key > thr)
    n_eq = count_where(lambda key: (key == thr) & (key != INT_MIN))
    need_rank = jnp.max(n_eq - n_tie) > 0.0

    @pl.when(jnp.logical_not(need_rank))
    def _():
        def mask_step(j0, u, carry):
            for t in range(u):
                key = key_ref[j0 + t]
                am_ref[j0 + t] = jnp.where((key >= thr) & (key != INT_MIN), 0.0, NEG_BIG)
            return carry
        _loop_tiles(nj, mask_step, 0)

    @pl.when(need_rank)
    def _():
        lower = _strict_lower(qb)

        def mask_step(j, run):
            key = key_ref[j]
            eq = (key == thr) & (key != INT_MIN)
            eqf = jnp.where(eq, 1.0, 0.0)
            rank = _dot(lower, eqf.astype(BF16)) + run
            sel = (key > thr) | (eq & (rank < n_tie))
            am_ref[j] = jnp.where(sel, 0.0, NEG_BIG)
            return run + jnp.sum(_fold_rows(eqf, jnp.add), axis=0, keepdims=True)

        lax.fori_loop(0, nj, mask_step, jnp.zeros((1, qb), F32))

    qt = q_ref[0].T
    pw = HEADS_PER_PAIR * qb
    gw = G_B * qb
    for pp in range(KV_B // 2):
        def group_qt(n):
            return jnp.concatenate([qt[h * DH_B:(h + 1) * DH_B] for h in range(n * G_B, (n + 1) * G_B)],
                                   axis=1)
        zero = jnp.zeros((DH_B, gw), F32)
        rhs = jnp.concatenate([jnp.concatenate([group_qt(2 * pp), zero], axis=1),
                               jnp.concatenate([zero, group_qt(2 * pp + 1)], axis=1)], axis=0)
        rhs = (rhs * (DH_B ** -0.5)).astype(BF16)
        klanes = slice(pp * 2 * DH_B, (pp + 1) * 2 * DH_B)
        far = bt_ref[pp, 1, 0:1, :]

        def tile_logits(j0, u, bias, mx):
            s = _dot(k_ref[0, kslice(j0, u), klanes], rhs)
            for t in range(u):
                st = s[t * qb:(t + 1) * qb] + bias + jnp.concatenate([am_ref[j0 + t]] * HEADS_PER_PAIR, axis=1)
                lg_ref[j0 + t] = st
                mx = jnp.maximum(mx, _fold_rows(st, jnp.maximum))
            return mx

        neg = jnp.full((V7X_SUBLANES, pw), NEG_BIG, F32)
        mx = _loop_tiles(jnp.maximum(i - 1, 0), lambda j0, u, m: tile_logits(j0, u, far, m), neg)
        mx = lax.cond(i >= 1, lambda m: tile_logits(i - 1, 1, bt_ref[pp, 1], m), lambda m: m, mx)
        mx = tile_logits(i, 1, bt_ref[pp, 0], mx)
        m = jnp.max(mx, axis=0, keepdims=True)
        acc_ref[...] = jnp.zeros(acc_ref.shape, F32)

        def p2(j0, u, l):
            ps = []
            for t in range(u):
                p = jnp.exp(lg_ref[j0 + t] - m)
                l = l + _fold_rows(p, jnp.add)
                ps.append(p.astype(BF16))
            pb = jnp.concatenate(ps, axis=0)
            vt = jnp.concatenate([vt_ref[0, j0 + t] for t in range(u)], axis=1)
            for gg in range(2):
                n = 2 * pp + gg
                acc_ref[gg] += _dot(vt[n * DH_B:(n + 1) * DH_B, :], pb[:, gg * gw:(gg + 1) * gw])
            return l

        l = _loop_tiles(nj, p2, jnp.zeros((V7X_SUBLANES, pw), F32))
        lsum = jnp.sum(l, axis=0, keepdims=True)
        for gg in range(2):
            o_t = acc_ref[gg] / lsum[:, gg * gw:(gg + 1) * gw]
            for hh in range(G_B):
                h = (2 * pp + gg) * G_B + hh
                ot_ref[h * DH_B:(h + 1) * DH_B, :] = o_t[:, hh * qb:(hh + 1) * qb]
    o_ref[0] = ot_ref[...].T.astype(o_ref.dtype)


def dsa_prompt(qn, qi, wi, kb, vtb, kib, btt):
    n_seq, length, _ = qn.shape
    nkb = length // PAGE_SIZE
    qmap = lambda s, i: (s, i, 0)
    kmap = lambda s, i: (s, 0, 0)
    pw = HEADS_PER_PAIR * PAGE_SIZE
    return pl.pallas_call(
        functools.partial(_dsa_prompt_body, topk=min(TOPK, length // 4)),
        grid=(n_seq, nkb),
        in_specs=[pl.BlockSpec((1, PAGE_SIZE, H_B * DH_B), qmap),
                  pl.BlockSpec((1, PAGE_SIZE, H_IDX * D_IDX), qmap),
                  pl.BlockSpec((1, PAGE_SIZE, H_IDX), qmap),
                  pl.BlockSpec((1, length, KV_B * DH_B), kmap),
                  pl.BlockSpec((1, nkb, KV_B * DH_B, PAGE_SIZE), lambda s, i: (s, 0, 0, 0)),
                  pl.BlockSpec((1, length, D_IDX), kmap),
                  pl.BlockSpec((H_B // HEADS_PER_PAIR, 2, PAGE_SIZE, pw), lambda s, i: (0, 0, 0, 0))],
        out_specs=pl.BlockSpec((1, PAGE_SIZE, H_B * DH_B), qmap),
        out_shape=jax.ShapeDtypeStruct((n_seq, length, H_B * DH_B), BF16),
        scratch_shapes=[pltpu.VMEM((nkb, PAGE_SIZE, PAGE_SIZE), I32),
                        pltpu.VMEM((nkb, PAGE_SIZE, PAGE_SIZE), F32),
                        pltpu.VMEM((nkb, PAGE_SIZE, pw), F32),
                        pltpu.VMEM((2, DH_B, G_B * PAGE_SIZE), F32),
                        pltpu.VMEM((H_B * DH_B, PAGE_SIZE), F32)],
        compiler_params=_cparams("arbitrary", "arbitrary"),
        name="dsa_prompt",
    )(qn, qi, wi, kb, vtb, kib, btt)


N_PAGES = PAST_LEN // PAGE_SIZE
N_KTILES = N_PAGES + 1
N_KTILES_PAD = -(-N_KTILES // PAGES_PER_STEP) * PAGES_PER_STEP


def _dsa_sample_sel_body(pt_ref, qi_ref, wi_ref, kin_ref, *rest, lv, topk):
    page_refs = rest[:PAGES_PER_STEP]
    am_ref, key_ref, qs_ref, ws_ref = rest[PAGES_PER_STEP:]
    c = pl.program_id(1)
    nsteps = pl.num_programs(1)
    r8 = SAMPLE_PAD

    @pl.when(c == 0)
    def _():
        qi = qi_ref[0]
        wsc = (wi_ref[0] * (H_IDX ** -0.5)) * (D_IDX ** -0.5)
        qs_ref[...] = jnp.concatenate([qi[:, h * D_IDX:(h + 1) * D_IDX] for h in range(H_IDX)],
                                      axis=0).astype(BF16)
        ws_ref[...] = jnp.concatenate([jnp.broadcast_to(wsc[:, h:h + 1], (r8, PAGE_SIZE))
                                       for h in range(H_IDX)], axis=0)

    def score(ki_t):
        s = jnp.maximum(_dot(qs_ref[...], ki_t), 0.0) * ws_ref[...]
        return _fold_rows(s, jnp.add)

    for r in range(PAGES_PER_STEP):
        key_ref[c * PAGES_PER_STEP + r] = _sort_key(score(page_refs[r][0].astype(BF16)))

    @pl.when(c == nsteps - 1)
    def _():
        rows = lax.broadcasted_iota(I32, (r8, PAGE_SIZE), 0)
        cols = lax.broadcasted_iota(I32, (r8, PAGE_SIZE), 1)
        new_valid = (cols <= rows) & (cols < lv)
        kin = jnp.concatenate([kin_ref[0], jnp.zeros((PAGE_SIZE - r8, D_IDX), F32)], axis=0)
        kin_t = jnp.concatenate([kin, jnp.zeros((PAGE_SIZE, PAGE_SIZE - D_IDX), F32)], axis=1).T[0:D_IDX]
        key_ref[N_PAGES] = jnp.where(new_valid, _sort_key(score(kin_t.astype(BF16))), INT_MIN)

        def tile_valid(j):
            return new_valid if j == N_PAGES else None

        def count_where(pred):
            parts = []
            for j in range(N_KTILES):
                hit = pred(key_ref[j])
                if tile_valid(j) is not None:
                    hit = hit & tile_valid(j)
                parts.append(jnp.where(hit, 1.0, 0.0))
            return jnp.sum(_tree(parts, jnp.add), axis=-1, keepdims=True)

        thr = _kth_largest(lambda cand: count_where(lambda key: key >= cand), (r8, 1), topk)
        n_tie = topk - count_where(lambda key: key > thr)
        n_eq = count_where(lambda key: key == thr)
        need_rank = jnp.max(n_eq - n_tie) > 0.0

        @pl.when(jnp.logical_not(need_rank))
        def _():
            for j in range(N_KTILES):
                sel = key_ref[j] >= thr
                if tile_valid(j) is not None:
                    sel = sel & tile_valid(j)
                am_ref[0, j] = jnp.where(sel, 0.0, NEG_BIG)

        @pl.when(need_rank)
        def _():
            upper = _strict_upper(PAGE_SIZE)

            def mask_step(j, run):
                key = key_ref[j]
                valid = (j < N_PAGES) | new_valid
                eq = valid & (key == thr)
                eqf = jnp.where(eq, 1.0, 0.0)
                rank = _dot(eqf.astype(BF16), upper) + run
                sel = (key > thr) | (eq & (rank < n_tie))
                am_ref[0, j] = jnp.where(sel, 0.0, NEG_BIG)
                return run + jnp.sum(eqf, axis=-1, keepdims=True)

            lax.fori_loop(0, N_KTILES, mask_step, jnp.zeros((r8, 1), F32))

        for j in range(N_KTILES, N_KTILES_PAD):
            am_ref[0, j] = jnp.full((r8, PAGE_SIZE), NEG_BIG, F32)


def dsa_sample_select(page_table, qi, wi, kin, kidx_pool_t, lv):
    n_seq = qi.shape[0]
    r8 = SAMPLE_PAD
    qmap = lambda s, c, pt: (s, 0, 0)
    page_specs = [pl.BlockSpec((1, D_IDX, PAGE_SIZE),
                               functools.partial(lambda s, c, pt, r: (pt[s, c * PAGES_PER_STEP + r], 0, 0), r=r))
                  for r in range(PAGES_PER_STEP)]
    grid_spec = pltpu.PrefetchScalarGridSpec(
        num_scalar_prefetch=1,
        grid=(n_seq, N_PAGES // PAGES_PER_STEP),
        in_specs=[pl.BlockSpec((1, r8, H_IDX * D_IDX), qmap),
                  pl.BlockSpec((1, r8, H_IDX), qmap),
                  pl.BlockSpec((1, r8, D_IDX), qmap)] + page_specs,
        out_specs=pl.BlockSpec((1, N_KTILES_PAD, r8, PAGE_SIZE), lambda s, c, pt: (s, 0, 0, 0)),
        scratch_shapes=[pltpu.VMEM((N_KTILES, r8, PAGE_SIZE), I32),
                        pltpu.VMEM((H_IDX * r8, D_IDX), BF16),
                        pltpu.VMEM((H_IDX * r8, PAGE_SIZE), F32)],
    )
    return pl.pallas_call(
        functools.partial(_dsa_sample_sel_body, lv=lv, topk=min(TOPK, (PAST_LEN + lv) // 4)),
        grid_spec=grid_spec,
        out_shape=jax.ShapeDtypeStruct((n_seq, N_KTILES_PAD, r8, PAGE_SIZE), F32),
        compiler_params=_cparams("arbitrary", "arbitrary"),
        name="dsa_sample_select",
    )(page_table, qi, wi, kin, *([kidx_pool_t] * PAGES_PER_STEP))


def _dsa_sample_attn_body(pt_ref, q_ref, kn_ref, vn_ref, am_ref, amn_ref, bt_ref, *rest):
    kp_refs = rest[:PAGES_PER_STEP]
    vp_refs = rest[PAGES_PER_STEP:2 * PAGES_PER_STEP]
    o_ref, m_ref, l_ref, acc_ref = rest[2 * PAGES_PER_STEP:]
    c = pl.program_id(1)
    nsteps = pl.num_programs(1)
    r8 = SAMPLE_PAD
    gr = G_B * r8

    @pl.when(c == 0)
    def _():
        m_ref[...] = jnp.full(m_ref.shape, NEG_BIG, F32)
        l_ref[...] = jnp.zeros(l_ref.shape, F32)
        acc_ref[...] = jnp.zeros(acc_ref.shape, F32)

    q = q_ref[0]
    qgs, b_subs, b_diags, b_fars = [], [], [], []
    for n in range(KV_B):
        heads = [n * G_B + hh for hh in range(G_B)]
        qg = jnp.concatenate([q[:, h * DH_B:(h + 1) * DH_B] for h in heads], axis=0)
        qgs.append((qg * (DH_B ** -0.5)).astype(BF16))
        b_diags.append(jnp.concatenate([bt_ref[h, :, 0:PAGE_SIZE] for h in heads], axis=0))
        b_subs.append(jnp.concatenate([bt_ref[h, :, PAGE_SIZE:2 * PAGE_SIZE] for h in heads], axis=0))
        b_fars.append(jnp.concatenate(
            [jnp.broadcast_to(bt_ref[h, 0:1, PAGE_SIZE:PAGE_SIZE + 1], (r8, PAGE_SIZE))
             for h in heads], axis=0))

    def update(n, s, pv):
        m_old = m_ref[n]
        m_new = jnp.maximum(m_old, jnp.max(s, axis=-1, keepdims=True))
        alpha = jnp.exp(m_old - m_new)
        p = jnp.exp(s - m_new[:, 0:1])
        l_ref[n] = alpha * l_ref[n] + jnp.sum(p, axis=-1, keepdims=True)
        acc_ref[n] = alpha[:, 0:DH_B] * acc_ref[n] + pv(p.astype(BF16))
        m_ref[n] = m_new

    def group_t(refs, n):
        return jnp.concatenate([ref[0, n] for ref in refs], axis=1).astype(BF16)

    amcat = jnp.concatenate([am_ref[0, r] for r in range(PAGES_PER_STEP)], axis=1)
    am4 = jnp.concatenate([amcat] * G_B, axis=0)
    is_last_step = c == nsteps - 1
    for n in range(KV_B):
        bias = jnp.concatenate([b_fars[n]] * (PAGES_PER_STEP - 1)
                               + [jnp.where(is_last_step, b_subs[n], b_fars[n])], axis=1)
        s = _dot(qgs[n], group_t(kp_refs, n)) + bias + am4
        v_t = group_t(vp_refs, n)
        update(n, s, lambda p: _dot_t(p, v_t))

    @pl.when(c == nsteps - 1)
    def _():
        pad = jnp.zeros((PAGE_SIZE - r8, KV_B * DH_B), F32)
        kn = jnp.concatenate([kn_ref[0], pad], axis=0).astype(BF16)
        vn = jnp.concatenate([vn_ref[0], pad], axis=0).astype(BF16)
        am4 = jnp.concatenate([amn_ref[0, 0]] * G_B, axis=0)
        for n in range(KV_B):
            ksl = slice(n * DH_B, (n + 1) * DH_B)
            s = _dot_t(qgs[n], kn[:, ksl]) + b_diags[n] + am4
            update(n, s, lambda p: _dot(p, vn[:, ksl]))
            o = acc_ref[n] / l_ref[n][:, 0:DH_B]
            for hh in range(G_B):
                h = n * G_B + hh
                o_ref[0, :, h * DH_B:(h + 1) * DH_B] = o[hh * r8:(hh + 1) * r8].astype(o_ref.dtype)


def dsa_sample_attend(page_table, qn, kn, vn, amask, bt, k_pool, v_pool):
    n_seq = qn.shape[0]
    r8 = SAMPLE_PAD
    qmap = lambda s, c, pt: (s, 0, 0)
    page_map = [functools.partial(lambda s, c, pt, r: (pt[s, c * PAGES_PER_STEP + r], 0, 0, 0), r=r)
                for r in range(PAGES_PER_STEP)]
    kv_w = KV_B * DH_B
    page_block = (1, KV_B, DH_B, PAGE_SIZE)
    grid_spec = pltpu.PrefetchScalarGridSpec(
        num_scalar_prefetch=1,
        grid=(n_seq, N_PAGES // PAGES_PER_STEP),
        in_specs=[pl.BlockSpec((1, r8, H_B * DH_B), qmap),
                  pl.BlockSpec((1, r8, kv_w), qmap),
                  pl.BlockSpec((1, r8, kv_w), qmap),
                  pl.BlockSpec((1, PAGES_PER_STEP, r8, PAGE_SIZE), lambda s, c, pt: (s, c, 0, 0)),
                  pl.BlockSpec((1, PAGES_PER_STEP, r8, PAGE_SIZE),
                               lambda s, c, pt: (s, N_PAGES // PAGES_PER_STEP, 0, 0)),
                  pl.BlockSpec((H_B, r8, 2 * PAGE_SIZE), lambda s, c, pt: (0, 0, 0))]
                 + [pl.BlockSpec(page_block, m) for m in page_map]
                 + [pl.BlockSpec(page_block, m) for m in page_map],
        out_specs=pl.BlockSpec((1, r8, H_B * DH_B), qmap),
        scratch_shapes=[pltpu.VMEM((KV_B, G_B * r8, PAGE_SIZE), F32),
                        pltpu.VMEM((KV_B, G_B * r8, PAGE_SIZE), F32),
                        pltpu.VMEM((KV_B, G_B * r8, DH_B), F32)],
    )
    return pl.pallas_call(
        _dsa_sample_attn_body,
        grid_spec=grid_spec,
        out_shape=jax.ShapeDtypeStruct((n_seq, r8, H_B * DH_B), BF16),
        compiler_params=_cparams("arbitrary", "arbitrary"),
        name="dsa_sample_attend",
    )(page_table, qn, kn, vn, amask, amask, bt,
      *([k_pool] * PAGES_PER_STEP), *([v_pool] * PAGES_PER_STEP))


HEADS_PER_GROUP = H_C // G_C
GROUP_W = D_IN_C // G_C


def _cumsum_rows(x):
    n = x.shape[0]
    idx = lax.broadcasted_iota(I32, x.shape, 0)
    d = 1
    while d < n:
        x = x + jnp.where(idx >= d, pltpu.roll(x, d, 0), 0.0)
        d *= 2
    return x


def _cumsum_lanes(x):
    n = x.shape[1]
    idx = lax.broadcasted_iota(I32, x.shape, 1)
    d = 1
    while d < n:
        x = x + jnp.where(idx >= d, pltpu.roll(x, d, 1), 0.0)
        d *= 2
    return x


def _softplus(x):
    return jnp.maximum(x, 0.0) + jnp.log1p(jnp.exp(-jnp.abs(x)))


def _split3(x):
    hi = x.astype(BF16)
    r = x - hi.astype(F32)
    mid = r.astype(BF16)
    lo = (r - mid.astype(F32)).astype(BF16)
    return hi, mid, lo


def _ssd_body(x_ref, xbc_ref, zlo_ref, zhi_ref, ng_ref, wdt_ref, wdtt_ref, dtb_ref, dtbt_ref,
              alog_ref, alogt_ref, dsk_ref, cn_ref, wo_ref, h0_ref, o_ref, hst_ref, *, qin, lv):
    c = pl.program_id(1)
    q = CHUNK_C
    rows = lax.broadcasted_iota(I32, (q, q), 0)
    cols = lax.broadcasted_iota(I32, (q, q), 1)
    tril = cols <= rows

    def pad_rows(a):
        if qin == q:
            return a
        return jnp.concatenate([a, jnp.zeros((q - qin, a.shape[1]), a.dtype)], axis=0)

    @pl.when(c == 0)
    def _():
        hst_ref[...] = h0_ref[...]

    x = x_ref[0]
    xn = _rms(pad_rows(x), ng_ref[...]).astype(BF16)
    dt_all = _softplus(_dot(xn, wdt_ref[...]) + dtb_ref[...])
    dtt_all = _softplus(_dot_t(wdtt_ref[...], xn) + dtbt_ref[...])
    if lv < q:
        dt_all = jnp.where(lax.broadcasted_iota(I32, dt_all.shape, 0) < lv, dt_all, 0.0)
        dtt_all = jnp.where(lax.broadcasted_iota(I32, dtt_all.shape, 1) < lv, dtt_all, 0.0)
    acs_all = _tree([_dot(tril.astype(BF16), p) for p in _split3(dt_all * (-jnp.exp(alog_ref[...])))], jnp.add)
    acst_all = _tree([_dot(p, (rows <= cols).astype(BF16))
                      for p in _split3(dtt_all * (-jnp.exp(alogt_ref[...])))], jnp.add)
    z_all = pad_rows(jnp.concatenate([zlo_ref[0], zhi_ref[0]], axis=1))
    y_groups = []
    for g in range(G_C):
        y_groups.append(_ssd_group(g, xbc_ref, hst_ref, dt_all, dtt_all, acs_all, acst_all, z_all,
                                   dsk_ref, cn_ref, tril, pad_rows))
    y = jnp.concatenate(y_groups, axis=1)
    o_ref[0] = x + _dot(y, wo_ref[...])[0:qin]


def _ssd_group(g, xbc_ref, hst_ref, dt_all, dtt_all, acs_all, acst_all, z_all, dsk_ref, cn_ref, tril, pad_rows):
    q = CHUNK_C
    hpg = HEADS_PER_GROUP
    gheads = slice(g * hpg, (g + 1) * hpg)
    gcols = slice(g * GROUP_W, (g + 1) * GROUP_W)
    dt = dt_all[:, gheads]
    dtt = dtt_all[gheads, :]
    acs = acs_all[:, gheads]
    acst = acst_all[gheads, :]
    xs = pad_rows(xbc_ref[0, :, gcols])
    bm = pad_rows(xbc_ref[0, :, D_IN_C + g * N_SSM:D_IN_C + (g + 1) * N_SSM])
    cm = pad_rows(xbc_ref[0, :, D_IN_C + (G_C + g) * N_SSM:D_IN_C + (G_C + g + 1) * N_SSM])
    xst = xs.T
    cmb = cm.astype(BF16)
    cb = _dot_t(cmb, bm.astype(BF16))

    def spread(v, width):
        src = lax.broadcasted_iota(I32, (hpg, hpg * width), 0)
        dst = lax.shift_right_logical(lax.broadcasted_iota(I32, (hpg, hpg * width), 1), int(math.log2(width)))
        sel = (src == dst).astype(BF16)
        return _tree([_dot(p, sel) for p in _split3(v)], jnp.add)

    acs_w = spread(acs, q)
    decay_w = jnp.exp(acs_w[q - 1:q, :] - acs_w)
    eacs_w = jnp.exp(acs_w)
    xdt = (xs * spread(dt, P_C)).astype(BF16)
    low_half = lax.broadcasted_iota(I32, (q, 2 * P_C), 1) < P_C
    h_all = hst_ref[0, gheads]
    ys, h_new = [], []
    for k in range(hpg // 2):
        xpair = xdt[:, k * 2 * P_C:(k + 1) * 2 * P_C]
        yd = []
        for j in (2 * k, 2 * k + 1):
            blk = slice(j * q, (j + 1) * q)
            acs_row = acst[j:j + 1, :]
            a_last = acs_row[:, q - 1:q]
            lmat = jnp.exp(jnp.where(tril, acs_w[:, blk] - acs_row, -jnp.inf))
            yd.append(_dot((cb * lmat).astype(BF16), xpair))
            xht = xst[j * P_C:(j + 1) * P_C, :] * dtt[j:j + 1, :]
            st = _dot(xht.astype(BF16), (bm * decay_w[:, blk]).astype(BF16))
            h_new.append(jnp.exp(a_last) * h_all[j] + st)
        h_pair = h_all[2 * k:2 * k + 2].reshape(2 * P_C, N_SSM).astype(BF16)
        e_pair = jnp.where(low_half, eacs_w[:, 2 * k * q:(2 * k + 1) * q], eacs_w[:, (2 * k + 1) * q:(2 * k + 2) * q])
        ys.append(jnp.where(low_half, yd[0], yd[1]) + _dot_t(cmb, h_pair) * e_pair)
    hst_ref[0, gheads] = jnp.stack(h_new, axis=0)
    y = jnp.concatenate(ys, axis=1) + xs * dsk_ref[:, gcols]
    y = y * _silu(z_all[:, gcols])
    return _rms(y, cn_ref[:, gcols]).astype(BF16)


def ssd_mix(x, xbc, proj, z_col0, h0, norm_g, w_dt, dt_bias, a_log, d_skip, c_norm, w_out, qin, lv):
    n_seq, length, d = x.shape
    nchunk = length // qin
    col = lambda a: a.reshape(-1, 1).astype(F32)
    row = lambda a: a.reshape(1, -1).astype(F32)
    full2 = lambda s, c: (0, 0)
    conv_c = xbc.shape[-1]
    zw = D_IN_C // 2
    assert z_col0 % zw == 0
    in_specs = [
        pl.BlockSpec((1, qin, d), lambda s, c: (s, c, 0)),
        pl.BlockSpec((1, qin, conv_c), lambda s, c: (s, c, 0)),
        pl.BlockSpec((1, qin, zw), lambda s, c: (s, c, z_col0 // zw)),
        pl.BlockSpec((1, qin, zw), lambda s, c: (s, c, z_col0 // zw + 1)),
        pl.BlockSpec((1, d), full2),
        pl.BlockSpec((d, H_C), full2),
        pl.BlockSpec((H_C, d), full2),
        pl.BlockSpec((1, H_C), full2),
        pl.BlockSpec((H_C, 1), full2),
        pl.BlockSpec((1, H_C), full2),
        pl.BlockSpec((H_C, 1), full2),
        pl.BlockSpec((1, D_IN_C), full2),
        pl.BlockSpec((1, D_IN_C), full2),
        pl.BlockSpec((D_IN_C, d), full2),
        pl.BlockSpec((1, H_C, P_C, N_SSM), lambda s, c: (s, 0, 0, 0)),
    ]
    out, hst = pl.pallas_call(
        functools.partial(_ssd_body, qin=qin, lv=lv),
        grid=(n_seq, nchunk),
        in_specs=in_specs,
        out_specs=[pl.BlockSpec((1, qin, d), lambda s, c: (s, c, 0)),
                   pl.BlockSpec((1, H_C, P_C, N_SSM), lambda s, c: (s, 0, 0, 0))],
        out_shape=[jax.ShapeDtypeStruct((n_seq, length, d), F32),
                   jax.ShapeDtypeStruct((n_seq, H_C, P_C, N_SSM), F32)],
        compiler_params=_cparams("arbitrary", "arbitrary"),
        name="ssd_mix",
    )(x, xbc, proj, proj, row(norm_g), w_dt.astype(BF16), w_dt.T.astype(BF16),
      row(dt_bias), col(dt_bias), row(a_log), col(a_log),
      row(jnp.repeat(d_skip, P_C)), row(c_norm), w_out, h0)
    return out, hst


def _flat(a):
    return a.reshape(-1, a.shape[-1])


def _unflat(a, like):
    return a.reshape(like.shape[0], like.shape[1], a.shape[-1])


def kernel(x_prompt, x_sample, mem_prompt, state_a_conv, cache_b_k, cache_b_v, cache_b_kidx, state_c_conv, state_c_ssm, state_d_conv, state_ffn_conv, cache_mem_k, cache_mem_v, page_table, rel_bias, norm_mix, norm_mem, norm_ffn, norm_memtok, a_w_in, a_b_in, a_w_conv, a_b_conv, a_ln_g, a_ln_b, a_w_out, b_w_in, b_w_out, b_q_norm, b_k_norm, b_kidx_norm, c_w_in, c_w_conv, c_b_conv, c_dt_bias, c_a_log, c_d_skip, c_norm, c_w_out, d_w_in, d_w_conv, d_w_out, m_w_q, m_w_kv, m_w_o, m_q_norm, m_k_norm, f_w_in, f_w_conv, f_b_conv, f_w_out):
    n_p, l_p, d = x_prompt.shape
    n_s, l_s, _ = x_sample.shape
    bf = lambda w: w.astype(BF16)
    xs_pad = jnp.pad(x_sample, ((0, 0), (0, SAMPLE_PAD - l_s), (0, 0)))
    groups = [dict(x=x_prompt, sb=1, tm=256, tmf=512, lv=256, qin=CHUNK_C, lvq=CHUNK_C, prompt=True),
              dict(x=xs_pad, sb=n_s, tm=SAMPLE_PAD, tmf=SAMPLE_PAD, lv=l_s, qin=SAMPLE_PAD, lvq=l_s, prompt=False)]
    bt, btt = t5_tiles(rel_bias)
    mem_flat = _flat(mem_prompt)
    outs = {k: [[], []] for k in ("a", "bk", "bv", "bki", "cc", "cs", "d", "f")}
    m_kp, m_vp = [], []

    for i in range(DEPTH):
        j = i // 4
        kind = i % 4
        kv = linear(mem_flat, bf(m_w_kv[i]), g=norm_memtok[i], name="mem_kv")
        mk, _ = head_norm(kv, 0, d, m_k_norm[i], DH_M, name="mem_k_norm")
        m_kp.append(mk.reshape(n_p, N_MEM, H_M, DH_M))
        m_vp.append(kv[:, d:].reshape(n_p, N_MEM, H_M, DH_M))
        mk = m_kp[-1].transpose(0, 2, 1, 3)
        mv = m_vp[-1].transpose(0, 2, 1, 3)

        for gi, grp in enumerate(groups):
            x = grp["x"]
            n_seq, length, _ = x.shape
            prompt = grp["prompt"]
            zeros_state = lambda w, c: jnp.zeros((n_seq, w - 1, c), F32)
            xf = _flat(x)
            if kind == 0:
                st = zeros_state(W_A, d) if prompt else state_a_conv[j]
                x, nst = conv_mixer(x, st, norm_mix[i], bf(a_w_in[j]), a_b_in[j], a_w_conv[j],
                                    [a_b_conv[j], a_ln_g[j], a_ln_b[j]], bf(a_w_out[j]), _a_pre, _a_post,
                                    grp["sb"], grp["tm"], grp["lv"], "a_mixer")
                outs["a"][gi].append(nst)
            elif kind == 1:
                w_in = b_w_in[j]
                n_main = H_B * DH_B + 2 * KV_B * DH_B + H_IDX * D_IDX
                w_tail = jnp.pad(w_in[:, n_main:], ((0, 0), (0, V7X_LANES - (w_in.shape[1] - n_main))))
                pm = linear(xf, bf(w_in[:, :n_main]), g=norm_mix[i], name="b_in")
                ptl = linear(xf, bf(w_tail), g=norm_mix[i], name="b_in_tail")
                kw = KV_B * DH_B
                qn, _ = head_norm(pm, 0, H_B * DH_B, b_q_norm[j], DH_B, name="b_q_norm")
                kn, knb = head_norm(pm, H_B * DH_B, kw, b_k_norm[j], DH_B, name="b_k_norm")
                kin, kinb = head_norm(ptl[:, :D_IDX], 0, D_IDX, b_kidx_norm[j], D_IDX, name="b_ki_norm")
                v = pm[:, H_B * DH_B + kw:H_B * DH_B + 2 * kw]
                qi = pm[:, H_B * DH_B + 2 * kw:]
                wi = ptl[:, D_IDX:D_IDX + H_IDX]
                r3 = lambda a: a.reshape(n_seq, length, a.shape[-1])
                if prompt:
                    vtb = bf(v).reshape(n_seq, length // PAGE_SIZE, PAGE_SIZE, kw).transpose(0, 1, 3, 2)
                    o = dsa_prompt(r3(qn), r3(qi), r3(wi), r3(knb), vtb, r3(kinb), btt)
                else:
                    am = dsa_sample_select(page_table, r3(qi), r3(wi), r3(kin),
                                           cache_b_kidx[j].transpose(0, 2, 1), grp["lv"])
                    o = dsa_sample_attend(page_table, r3(qn), r3(kn), r3(v), am, bt,
                                          cache_b_k[j].transpose(0, 2, 3, 1), cache_b_v[j].transpose(0, 2, 3, 1))
                outs["bk"][gi].append(r3(kn))
                outs["bv"][gi].append(r3(v))
                outs["bki"][gi].append(r3(kin))
                x = _unflat(linear(_flat(o), bf(b_w_out[j]), res=xf, name="b_out"), x)
            elif kind == 2:
                w_in = c_w_in[j]
                conv_c = D_IN_C + 2 * G_C * N_SSM
                w_main = jnp.concatenate([w_in[:, D_IN_C:D_IN_C + conv_c], w_in[:, :D_IN_C]], axis=1)
                proj = _unflat(linear(xf, bf(w_main), g=norm_mix[i], name="c_in"), x)
                st = zeros_state(W_C, conv_c) if prompt else state_c_conv[j]
                xbc, nst = seq_conv([(proj, conv_c, 0)], st, c_w_conv[j], [c_b_conv[j]], _c_pre, _c_post,
                                    conv_c, conv_c, F32, grp["tm"], grp["lv"], "c_conv")
                h0 = jnp.zeros((n_seq, H_C, P_C, N_SSM), F32) if prompt else state_c_ssm[j]
                x, hst = ssd_mix(x, xbc, proj, conv_c, h0, norm_mix[i], w_in[:, D_IN_C + conv_c:],
                                 c_dt_bias[j], c_a_log[j], c_d_skip[j], c_norm[j], bf(c_w_out[j]),
                                 grp["qin"], grp["lvq"])
                outs["cc"][gi].append(nst)
                outs["cs"][gi].append(hst)
            else:
                st = zeros_state(W_D, d) if prompt else state_d_conv[j]
                x, nst = conv_mixer(x, st, norm_mix[i], bf(d_w_in[j]), None, d_w_conv[j], [], bf(d_w_out[j]),
                                    _d_pre, _d_post, grp["sb"], grp["tm"], grp["lv"], "d_mixer")
                outs["d"][gi].append(nst)

            if prompt:
                x = mem_fused(x, mk, mv, norm_mem[i], bf(m_w_q[i]), m_q_norm[i], bf(m_w_o[i]), 512)
            else:
                xf = _flat(x)
                qm = _unflat(linear(xf, bf(m_w_q[i]), g=norm_mem[i], name="mem_q"), x)
                om = mem_attn_cached(qm, cache_mem_k.reshape(DEPTH * n_seq, N_MEM, H_M, DH_M),
                                     cache_mem_v.reshape(DEPTH * n_seq, N_MEM, H_M, DH_M),
                                     m_q_norm[i], MEM_SEQS_PER_STEP, i * n_seq)
                x = _unflat(linear(_flat(om), bf(m_w_o[i]), res=xf, name="mem_o"), x)

            st = zeros_state(W_F, D_FF) if prompt else state_ffn_conv[i]
            x, nst = ffn_fused(x, st, norm_ffn[i], bf(f_w_in[i]), f_w_conv[i], f_b_conv[i], bf(f_w_out[i]),
                               1 if prompt else n_seq, grp["tmf"], grp["tmf"] if prompt else grp["lv"])
            outs["f"][gi].append(nst)
            grp["x"] = x

    yp = groups[0]["x"]
    ys = groups[1]["x"][:, :l_s]
    st = lambda key, gi: jnp.stack(outs[key][gi])
    kvshape = lambda a, n, l: a.reshape(a.shape[0], n, -1, a.shape[-1])[:, :, :l]
    b_k_p = kvshape(st("bk", 0), n_p, l_p).reshape(-1, n_p, l_p, KV_B, DH_B)
    b_v_p = kvshape(st("bv", 0), n_p, l_p).reshape(-1, n_p, l_p, KV_B, DH_B)
    b_ki_p = kvshape(st("bki", 0), n_p, l_p)
    b_k_s = kvshape(st("bk", 1), n_s, l_s).reshape(-1, n_s, l_s, KV_B, DH_B)
    b_v_s = kvshape(st("bv", 1), n_s, l_s).reshape(-1, n_s, l_s, KV_B, DH_B)
    b_ki_s = kvshape(st("bki", 1), n_s, l_s)
    return (yp, ys, st("a", 0), st("a", 1), b_k_p, b_v_p, b_ki_p, b_k_s, b_v_s, b_ki_s,
            st("cc", 0), st("cs", 0), st("cc", 1), st("cs", 1), st("d", 0), st("d", 1),
            st("f", 0), st("f", 1), jnp.stack(m_kp), jnp.stack(m_vp))
```

```python
import functools
import math

import jax
import jax.numpy as jnp
from jax import lax
from jax.experimental import pallas as pl
from jax.experimental.pallas import tpu as pltpu

F32 = jnp.float32
BF16 = jnp.bfloat16
I32 = jnp.int32
I16 = jnp.int16

D_MODEL = 1024
DEPTH = 4
PAST_LEN = 8192
PAGE_SIZE = 128
W_A = 31
H_B = 16
DH_B = 64
KV_B = 4
G_B = H_B // KV_B
H_IDX = 8
D_IDX = 64
TOPK = 256
N_BUCKETS = 32
MAX_DIST = 128
D_IN_C = 2 * D_MODEL
P_C = 64
H_C = D_IN_C // P_C
G_C = 4
N_SSM = 128
W_C = 4
CHUNK_C = 128
W_D = 3
N_MEM = 256
H_M = 4
DH_M = D_MODEL // H_M
D_FF = 2816
W_F = 3
EPS = 1e-6

V7X_SUBLANES = 8
V7X_LANES = 128
V7X_VMEM_LIMIT_BYTES = 56 * 1024 * 1024

SAMPLE_PAD = V7X_SUBLANES
NEG_BIG = -1e30
INT_MIN = -(2 ** 31)
PAGES_PER_STEP = 16
MEM_SEQS_PER_STEP = 8


def _cparams(*sem):
    return pltpu.CompilerParams(dimension_semantics=sem, vmem_limit_bytes=V7X_VMEM_LIMIT_BYTES)


def _pick(n, cands):
    for c in cands:
        if n % c == 0:
            return c
    return n


def _rms(x, g):
    y = x * lax.rsqrt(jnp.mean(x * x, axis=-1, keepdims=True) + EPS)
    return y * g


def _dot(a, b):
    return jnp.dot(a, b, preferred_element_type=F32)


def _dot_t(a, b):
    return lax.dot_general(a, b, (((1,), (1,)), ((), ())), preferred_element_type=F32)


def _sigmoid(x):
    return 1.0 / (1.0 + jnp.exp(-x))


def _silu(x):
    return x * _sigmoid(x)


def _group_meansq(x, gsize):
    tm, c = x.shape
    x2 = x * x
    if gsize == c:
        return jnp.mean(x2, axis=-1, keepdims=True)
    if gsize % V7X_LANES == 0:
        parts = []
        for h in range(c // gsize):
            ms = jnp.mean(x2[:, h * gsize:(h + 1) * gsize], axis=-1, keepdims=True)
            parts.append(jnp.broadcast_to(ms, (tm, gsize)))
        return jnp.concatenate(parts, axis=-1)
    shift = int(math.log2(gsize))
    r = lax.shift_right_logical(lax.broadcasted_iota(I32, (c, c), 0), shift)
    q = lax.shift_right_logical(lax.broadcasted_iota(I32, (c, c), 1), shift)
    bd = (r == q).astype(BF16)
    hi = x2.astype(BF16)
    lo = (x2 - hi.astype(F32)).astype(BF16)
    return (_dot(hi, bd) + _dot(lo, bd)) * (1.0 / gsize)


def _group_rms(x, gain, gsize):
    return (x * lax.rsqrt(_group_meansq(x, gsize) + EPS)) * gain


def _linear_body(*refs, norm, bias, res):
    it = iter(refs)
    x_ref = next(it)
    g_ref = next(it) if norm else None
    w_ref = next(it)
    b_ref = next(it) if bias else None
    r_ref = next(it) if res else None
    o_ref = next(it)
    xn_ref = next(it)

    @pl.when(pl.program_id(1) == 0)
    def _():
        x = x_ref[...].astype(F32)
        if norm:
            x = _rms(x, g_ref[...])
        xn_ref[...] = x.astype(BF16)

    acc = _dot(xn_ref[...], w_ref[...])
    if bias:
        acc = acc + b_ref[...]
    if res:
        acc = acc + r_ref[...]
    o_ref[...] = acc.astype(o_ref.dtype)


def linear(x, w, *, g=None, b=None, res=None, out_dtype=F32, name="linear"):
    m, k = x.shape
    n = w.shape[1]
    tm = _pick(m, (1024, 512, 256, 128))
    tn = _pick(n, (1024, 1408, 512, 256, 128))
    in_specs = [pl.BlockSpec((tm, k), lambda i, j: (i, 0))]
    args = [x]
    if g is not None:
        in_specs.append(pl.BlockSpec((1, k), lambda i, j: (0, 0)))
        args.append(g.reshape(1, k).astype(F32))
    in_specs.append(pl.BlockSpec((k, tn), lambda i, j: (0, j)))
    args.append(w)
    if b is not None:
        in_specs.append(pl.BlockSpec((1, tn), lambda i, j: (0, j)))
        args.append(b.reshape(1, n).astype(F32))
    if res is not None:
        in_specs.append(pl.BlockSpec((tm, tn), lambda i, j: (i, j)))
        args.append(res)
    return pl.pallas_call(
        functools.partial(_linear_body, norm=g is not None, bias=b is not None, res=res is not None),
        grid=(m // tm, n // tn),
        in_specs=in_specs,
        out_specs=pl.BlockSpec((tm, tn), lambda i, j: (i, j)),
        out_shape=jax.ShapeDtypeStruct((m, n), out_dtype),
        scratch_shapes=[pltpu.VMEM((tm, k), BF16)],
        compiler_params=_cparams("arbitrary", "arbitrary"),
        name=name,
    )(*args)


def _head_norm_body(x_ref, g_ref, o_ref, ob_ref, *, gsize):
    y = _group_rms(x_ref[...], g_ref[...], gsize)
    o_ref[...] = y
    ob_ref[...] = y.astype(BF16)


def head_norm(x, col0, width, gain, gsize, name="head_norm"):
    m = x.shape[0]
    tm = _pick(m, (512, 256, 128))
    assert col0 % width == 0
    gt = jnp.tile(gain.astype(F32), width // gsize).reshape(1, width)
    return pl.pallas_call(
        functools.partial(_head_norm_body, gsize=gsize),
        grid=(m // tm,),
        in_specs=[pl.BlockSpec((tm, width), lambda i: (i, col0 // width)),
                  pl.BlockSpec((1, width), lambda i: (0, 0))],
        out_specs=[pl.BlockSpec((tm, width), lambda i: (i, 0)),
                   pl.BlockSpec((tm, width), lambda i: (i, 0))],
        out_shape=[jax.ShapeDtypeStruct((m, width), F32), jax.ShapeDtypeStruct((m, width), BF16)],
        compiler_params=_cparams("arbitrary"),
        name=name,
    )(x, gt)


B_QW = H_B * DH_B
B_KW = KV_B * DH_B


def _b_proj_body(x_ref, ng_ref, wm_ref, wt_ref, gq_ref, gk_ref, gki_ref,
                 qn_ref, kn_ref, knb_ref, v_ref, vb_ref, qi_ref, kin_ref, kinb_ref, wi_ref):
    xn = _rms(x_ref[...], ng_ref[...]).astype(BF16)
    pm = _dot(xn, wm_ref[...])
    pt = _dot(xn, wt_ref[...])
    qn_ref[...] = _group_rms(pm[:, 0:B_QW], gq_ref[...], DH_B)
    kn = _group_rms(pm[:, B_QW:B_QW + B_KW], gk_ref[...], DH_B)
    kn_ref[...] = kn
    knb_ref[...] = kn.astype(BF16)
    v = pm[:, B_QW + B_KW:B_QW + 2 * B_KW]
    v_ref[...] = v
    vb_ref[...] = v.astype(BF16)
    qi_ref[...] = pm[:, B_QW + 2 * B_KW:]
    kin = _group_rms(pt[:, 0:D_IDX], gki_ref[...], D_IDX)
    kin_ref[...] = kin
    kinb_ref[...] = kin.astype(BF16)
    wi_ref[...] = pt[:, D_IDX:D_IDX + H_IDX]


def b_proj(x, norm_g, w_in, q_gain, k_gain, ki_gain):
    m, d = x.shape
    tm = _pick(m, (512, 256))
    n_main = B_QW + 2 * B_KW + H_IDX * D_IDX
    w_tail = jnp.pad(w_in[:, n_main:], ((0, 0), (0, V7X_LANES - (w_in.shape[1] - n_main))))
    const = lambda i: (0, 0)
    rowmap = lambda i: (i, 0)
    widths = [(B_QW, F32), (B_KW, F32), (B_KW, BF16), (B_KW, F32), (B_KW, BF16), (H_IDX * D_IDX, F32),
              (D_IDX, F32), (D_IDX, BF16), (H_IDX, F32)]
    return pl.pallas_call(
        _b_proj_body,
        grid=(m // tm,),
        in_specs=[pl.BlockSpec((tm, d), rowmap),
                  pl.BlockSpec((1, d), const),
                  pl.BlockSpec((d, n_main), const),
                  pl.BlockSpec((d, V7X_LANES), const),
                  pl.BlockSpec((1, B_QW), const),
                  pl.BlockSpec((1, B_KW), const),
                  pl.BlockSpec((1, D_IDX), const)],
        out_specs=[pl.BlockSpec((tm, w), rowmap) for w, _ in widths],
        out_shape=[jax.ShapeDtypeStruct((m, w), dt) for w, dt in widths],
        compiler_params=_cparams("arbitrary"),
        name="b_proj",
    )(x, norm_g.reshape(1, d).astype(F32), w_in[:, :n_main].astype(BF16), w_tail.astype(BF16),
      jnp.tile(q_gain.astype(F32), H_B).reshape(1, B_QW), jnp.tile(k_gain.astype(F32), KV_B).reshape(1, B_KW),
      ki_gain.astype(F32).reshape(1, D_IDX))


def _seq_conv_body(*refs, n_in, n_par, pre, post, width, tm, lv, halo):
    in_refs = refs[:n_in]
    st_ref = refs[n_in]
    wc_ref = refs[n_in + 1]
    par_refs = refs[n_in + 2:n_in + 2 + n_par]
    o_ref, nst_ref, abuf, rbuf = refs[n_in + 2 + n_par:]
    t = pl.program_id(1)
    base = halo - (width - 1)

    @pl.when(t == 0)
    def _():
        abuf[base:halo, :] = st_ref[0]

    tiles = [r[0] for r in in_refs]
    pars = [r[...] for r in par_refs]
    abuf[halo:halo + tm, :] = pre(tiles)
    acc = None
    for r in range(V7X_SUBLANES):
        taps = [k for k in range(width) if (base + k) % V7X_SUBLANES == r]
        if not taps:
            continue
        qmax = max((base + k) // V7X_SUBLANES for k in taps)
        rows = V7X_SUBLANES * qmax + tm
        for k in taps:
            q0 = V7X_SUBLANES * ((base + k) // V7X_SUBLANES)
            if r == 0 or len(taps) == 1:
                shifted = abuf[r + q0:r + q0 + tm, :]
            else:
                if k == taps[0]:
                    rbuf[r, 0:rows, :] = abuf[r:r + rows, :]
                shifted = rbuf[r, q0:q0 + tm, :]
            term = wc_ref[k:k + 1, :] * shifted
            acc = term if acc is None else acc + term
    o_ref[0] = post(acc, tiles, pars).astype(o_ref.dtype)
    tail = abuf[base + lv:base + lv + width - 1, :]
    abuf[base:halo, :] = tail
    nst_ref[0] = tail


def seq_conv(inputs, state, wconv, params, pre, post, c_conv, c_out, out_dtype, tm, lv, name):
    n_seq, length, _ = inputs[0][0].shape
    width = wconv.shape[0]
    halo = -(-(width - 1) // V7X_SUBLANES) * V7X_SUBLANES
    in_specs = [pl.BlockSpec((1, tm, bw), functools.partial(lambda s, t, ci: (s, t, ci), ci=ci))
                for (_, bw, ci) in inputs]
    in_specs.append(pl.BlockSpec((1, width - 1, c_conv), lambda s, t: (s, 0, 0)))
    in_specs.append(pl.BlockSpec((width, c_conv), lambda s, t: (0, 0)))
    for p in params:
        in_specs.append(pl.BlockSpec((1, p.shape[-1]), lambda s, t: (0, 0)))
    out, nst = pl.pallas_call(
        functools.partial(_seq_conv_body, n_in=len(inputs), n_par=len(params), pre=pre, post=post,
                          width=width, tm=tm, lv=lv, halo=halo),
        grid=(n_seq, length // tm),
        in_specs=in_specs,
        out_specs=[pl.BlockSpec((1, tm, c_out), lambda s, t: (s, t, 0)),
                   pl.BlockSpec((1, width - 1, c_conv), lambda s, t: (s, 0, 0))],
        out_shape=[jax.ShapeDtypeStruct((n_seq, length, c_out), out_dtype),
                   jax.ShapeDtypeStruct((n_seq, width - 1, c_conv), F32)],
        scratch_shapes=[pltpu.VMEM((halo + tm, c_conv), F32),
                        pltpu.VMEM((V7X_SUBLANES, halo + tm, c_conv) if width > V7X_SUBLANES
                                   else (1, V7X_SUBLANES, V7X_LANES), F32)],
        compiler_params=_cparams("arbitrary", "arbitrary"),
        name=name,
    )(*[a for (a, _, _) in inputs], state, wconv.astype(F32),
      *[p.reshape(1, -1).astype(F32) for p in params])
    return out, nst


def _a_pre(tiles):
    a, g = tiles
    return a * _sigmoid(g)


def _a_post(y, tiles, pars):
    bc, lng, lnb = pars
    y = y + bc
    mu = jnp.mean(y, axis=-1, keepdims=True)
    yc = y - mu
    yn = yc * lax.rsqrt(jnp.mean(yc * yc, axis=-1, keepdims=True) + EPS)
    return _silu(yn * lng + lnb)


def _f_pre(tiles):
    return tiles[0]


def _f_post(y, tiles, pars):
    return _silu(y + pars[0]) * tiles[1]


def _d_pre(tiles):
    return tiles[1] * tiles[2]


def _d_post(y, tiles, pars):
    return tiles[0] * y


def _c_pre(tiles):
    return tiles[0]


def _c_post(y, tiles, pars):
    return _silu(y + pars[0])


def _conv_mixer_body(*refs, n_par, n_parts, pre, post, width, sb, tm, lv, halo, realign):
    x_ref, ng_ref, win_ref, bin_ref, st_ref, wc_ref = refs[:6]
    par_refs = refs[6:6 + n_par]
    wout_ref, o_ref, nst_ref, abuf, rbuf = refs[6 + n_par:]
    t = pl.program_id(1)
    base = halo - (width - 1)
    d = x_ref.shape[-1]
    c = abuf.shape[-1]

    @pl.when(t == 0)
    def _():
        abuf[:, base:halo, :] = st_ref[...]

    x = x_ref[...].reshape(sb * tm, d)
    proj = _dot(_rms(x, ng_ref[...]).astype(BF16), win_ref[...]) + bin_ref[...]
    parts = [proj[:, p * c:(p + 1) * c] for p in range(n_parts)]
    abuf[:, halo:halo + tm, :] = pre(parts).reshape(sb, tm, c)
    acc = None
    for r in range(V7X_SUBLANES):
        taps = [k for k in range(width) if (base + k) % V7X_SUBLANES == r]
        if not taps:
            continue
        rows = V7X_SUBLANES * max((base + k) // V7X_SUBLANES for k in taps) + tm
        for k in taps:
            q0 = V7X_SUBLANES * ((base + k) // V7X_SUBLANES)
            if r == 0 or len(taps) == 1 or not realign:
                shifted = abuf[:, r + q0:r + q0 + tm, :]
            else:
                if k == taps[0]:
                    rbuf[r, :, 0:rows, :] = abuf[:, r:r + rows, :]
                shifted = rbuf[r, :, q0:q0 + tm, :]
            term = wc_ref[k:k + 1, :] * shifted
            acc = term if acc is None else acc + term
    u = post(acc.reshape(sb * tm, c), parts, [p[...] for p in par_refs]).astype(BF16)
    o_ref[...] = (x + _dot(u, wout_ref[...])).reshape(sb, tm, d)
    tail = abuf[:, base + lv:base + lv + width - 1, :]
    abuf[:, base:halo, :] = tail
    nst_ref[...] = tail


def conv_mixer(x, state, norm_g, w_in, b_in, w_conv, params, w_out, pre, post, sb, tm, lv, name):
    n_seq, length, d = x.shape
    width, c = w_conv.shape
    n_parts = w_in.shape[1] // c
    halo = -(-(width - 1) // V7X_SUBLANES) * V7X_SUBLANES
    realign = width > V7X_SUBLANES and tm >= 8 * V7X_SUBLANES
    const = lambda s, t: (0, 0)
    b_in = jnp.zeros((n_parts * c,), F32) if b_in is None else b_in
    in_specs = [pl.BlockSpec((sb, tm, d), lambda s, t: (s, t, 0)),
                pl.BlockSpec((1, d), const),
                pl.BlockSpec((d, n_parts * c), const),
                pl.BlockSpec((1, n_parts * c), const),
                pl.BlockSpec((sb, width - 1, c), lambda s, t: (s, 0, 0)),
                pl.BlockSpec((width, c), const)]
    in_specs += [pl.BlockSpec((1, c), const) for _ in params]
    in_specs.append(pl.BlockSpec((c, d), const))
    return pl.pallas_call(
        functools.partial(_conv_mixer_body, n_par=len(params), n_parts=n_parts, pre=pre, post=post,
                          width=width, sb=sb, tm=tm, lv=lv, halo=halo, realign=realign),
        grid=(n_seq // sb, length // tm),
        in_specs=in_specs,
        out_specs=[pl.BlockSpec((sb, tm, d), lambda s, t: (s, t, 0)),
                   pl.BlockSpec((sb, width - 1, c), lambda s, t: (s, 0, 0))],
        out_shape=[jax.ShapeDtypeStruct((n_seq, length, d), F32),
                   jax.ShapeDtypeStruct((n_seq, width - 1, c), F32)],
        scratch_shapes=[pltpu.VMEM((sb, halo + tm, c), F32),
                        pltpu.VMEM((V7X_SUBLANES, sb, halo + tm, c) if realign
                                   else (1, 1, V7X_SUBLANES, V7X_LANES), F32)],
        compiler_params=_cparams("arbitrary", "arbitrary"),
        name=name,
    )(x, norm_g.reshape(1, d).astype(F32), w_in, b_in.reshape(1, -1).astype(F32), state, w_conv.astype(F32),
      *[p.reshape(1, c).astype(F32) for p in params], w_out)


FFN_CHUNK = 2 * V7X_LANES


def _ffn_body(x_ref, ng_ref, win_ref, st_ref, wc_ref, bc_ref, wout_ref, o_ref, nst_ref, abuf, *, sb, tm, lv, halo):
    t = pl.program_id(1)
    base = halo - (W_F - 1)
    d = x_ref.shape[-1]

    @pl.when(t == 0)
    def _():
        abuf[:, base:halo, :] = st_ref[...]

    x = x_ref[...].reshape(sb * tm, d)
    xn = _rms(x, ng_ref[...]).astype(BF16)
    out = x
    for c0 in range(0, D_FF, FFN_CHUNK):
        cs = slice(c0, c0 + FFN_CHUNK)
        abuf[:, halo:halo + tm, cs] = _dot(xn, win_ref[:, cs]).reshape(sb, tm, FFN_CHUNK)
        g = _dot(xn, win_ref[:, D_FF + c0:D_FF + c0 + FFN_CHUNK])
        y = None
        for k in range(W_F):
            term = wc_ref[k:k + 1, cs] * abuf[:, base + k:base + k + tm, cs]
            y = term if y is None else y + term
        y = y.reshape(sb * tm, FFN_CHUNK) + bc_ref[:, cs]
        out = out + _dot((_silu(y) * g).astype(BF16), wout_ref[cs, :])
    o_ref[...] = out.reshape(sb, tm, d)
    tail = abuf[:, base + lv:base + lv + W_F - 1, :]
    abuf[:, base:halo, :] = tail
    nst_ref[...] = tail


def ffn_fused(x, state, norm_g, w_in, w_conv, b_conv, w_out, sb, tm, lv):
    n_seq, length, d = x.shape
    halo = V7X_SUBLANES
    const = lambda s, t: (0, 0)
    resident = dict(pipeline_mode=pl.Buffered(1))
    return pl.pallas_call(
        functools.partial(_ffn_body, sb=sb, tm=tm, lv=lv, halo=halo),
        grid=(n_seq // sb, length // tm),
        in_specs=[pl.BlockSpec((sb, tm, d), lambda s, t: (s, t, 0)),
                  pl.BlockSpec((1, d), const),
                  pl.BlockSpec((d, 2 * D_FF), const, **resident),
                  pl.BlockSpec((sb, W_F - 1, D_FF), lambda s, t: (s, 0, 0)),
                  pl.BlockSpec((W_F, D_FF), const),
                  pl.BlockSpec((1, D_FF), const),
                  pl.BlockSpec((D_FF, d), const, **resident)],
        out_specs=[pl.BlockSpec((sb, tm, d), lambda s, t: (s, t, 0)),
                   pl.BlockSpec((sb, W_F - 1, D_FF), lambda s, t: (s, 0, 0))],
        out_shape=[jax.ShapeDtypeStruct((n_seq, length, d), F32),
                   jax.ShapeDtypeStruct((n_seq, W_F - 1, D_FF), F32)],
        scratch_shapes=[pltpu.VMEM((sb, halo + tm, D_FF), F32)],
        compiler_params=_cparams("arbitrary", "arbitrary"),
        name="ffn",
    )(x, norm_g.reshape(1, d).astype(F32), w_in, state, w_conv.astype(F32),
      b_conv.reshape(1, D_FF).astype(F32), w_out)


def _softmax_rows(s):
    p = jnp.exp(s - jnp.max(s, axis=-1, keepdims=True))
    return p / jnp.sum(p, axis=-1, keepdims=True)


def _mem_attend(q, k, v, gain, token_major):
    tm = q.shape[0]
    qh = [(_rms(q[:, h * DH_M:(h + 1) * DH_M], gain) * (DH_M ** -0.5)).astype(BF16) for h in range(H_M)]
    if not token_major:
        outs = []
        for h in range(H_M):
            p = _softmax_rows(_dot_t(qh[h], k[h].astype(BF16)))
            outs.append(_dot(p.astype(BF16), v[h].astype(BF16)))
    else:
        k_all = k.reshape(N_MEM * H_M, DH_M).astype(BF16)
        v_all = v.reshape(N_MEM * H_M, DH_M).astype(BF16)
        s = _dot_t(jnp.concatenate(qh, axis=0), k_all)
        col_head = lax.broadcasted_iota(I32, s.shape, 1) & (H_M - 1)
        row_head = lax.shift_right_logical(lax.broadcasted_iota(I32, s.shape, 0), int(math.log2(tm)))
        p = _softmax_rows(jnp.where(col_head == row_head, s, NEG_BIG))
        o = _dot(p.astype(BF16), v_all)
        outs = [o[h * tm:(h + 1) * tm] for h in range(H_M)]
    return jnp.concatenate(outs, axis=1).astype(BF16)


def _mem_attn_body(q_ref, k_ref, v_ref, g_ref, o_ref, *, sb):
    for s in range(sb):
        o_ref[s] = _mem_attend(q_ref[s], k_ref[s], v_ref[s], g_ref[...], True)


def mem_attn_cached(q, k, v, q_gain, sb, kv_seq0):
    n_seq, tm, d = q.shape
    kv_block = (sb,) + k.shape[1:]
    kv_map = lambda s: (kv_seq0 // sb + s, 0, 0, 0)
    return pl.pallas_call(
        functools.partial(_mem_attn_body, sb=sb),
        grid=(n_seq // sb,),
        in_specs=[pl.BlockSpec((sb, tm, d), lambda s: (s, 0, 0)),
                  pl.BlockSpec(kv_block, kv_map),
                  pl.BlockSpec(kv_block, kv_map),
                  pl.BlockSpec((1, DH_M), lambda s: (0, 0))],
        out_specs=pl.BlockSpec((sb, tm, d), lambda s: (s, 0, 0)),
        out_shape=jax.ShapeDtypeStruct((n_seq, tm, d), BF16),
        compiler_params=_cparams("arbitrary"),
        name="mem_attn",
    )(q, k, v, q_gain.reshape(1, DH_M).astype(F32))


def _mem_fused_body(x_ref, ng_ref, wq_ref, k_ref, v_ref, g_ref, wo_ref, o_ref):
    x = x_ref[0]
    q = _dot(_rms(x, ng_ref[...]).astype(BF16), wq_ref[...])
    o = _mem_attend(q, k_ref[0], v_ref[0], g_ref[...], False)
    o_ref[0] = x + _dot(o, wo_ref[...])


def mem_fused(x, k, v, norm_g, w_q, q_gain, w_o, tm):
    n_seq, length, d = x.shape
    const = lambda s, t: (0, 0)
    kv_block = (1,) + k.shape[1:]
    return pl.pallas_call(
        _mem_fused_body,
        grid=(n_seq, length // tm),
        in_specs=[pl.BlockSpec((1, tm, d), lambda s, t: (s, t, 0)),
                  pl.BlockSpec((1, d), const),
                  pl.BlockSpec((d, d), const),
                  pl.BlockSpec(kv_block, lambda s, t: (s, 0, 0, 0)),
                  pl.BlockSpec(kv_block, lambda s, t: (s, 0, 0, 0)),
                  pl.BlockSpec((1, DH_M), const),
                  pl.BlockSpec((d, d), const)],
        out_specs=pl.BlockSpec((1, tm, d), lambda s, t: (s, t, 0)),
        out_shape=jax.ShapeDtypeStruct((n_seq, length, d), F32),
        compiler_params=_cparams("arbitrary", "arbitrary"),
        name="mem_fused",
    )(x, norm_g.reshape(1, d).astype(F32), w_q, k, v, q_gain.reshape(1, DH_M).astype(F32), w_o)


HEADS_PER_PAIR = 2 * G_B


def _t5_lookup(tab_ref, h, dist):
    dist = jnp.maximum(dist, 0)
    max_exact = N_BUCKETS // 2
    df = jnp.maximum(dist, 1).astype(F32)
    large = max_exact + (jnp.log(df / max_exact) / math.log(MAX_DIST / max_exact)
                         * (N_BUCKETS - max_exact)).astype(I32)
    large = jnp.minimum(large, N_BUCKETS - 1)
    bucket = jnp.where(dist < max_exact, dist, large)
    acc = jnp.zeros(dist.shape, F32)
    for b in range(N_BUCKETS):
        acc = jnp.where(bucket == b, tab_ref[b, h], acc)
    return acc


def _t5_tiles_body(tab_ref, o_ref, ot_ref):
    h = pl.program_id(0)
    r = lax.broadcasted_iota(I32, (PAGE_SIZE, PAGE_SIZE), 0)
    c = lax.broadcasted_iota(I32, (PAGE_SIZE, PAGE_SIZE), 1)
    o_ref[0, :, 0:PAGE_SIZE] = _t5_lookup(tab_ref, h, r - c)
    o_ref[0, :, PAGE_SIZE:2 * PAGE_SIZE] = _t5_lookup(tab_ref, h, PAGE_SIZE + r - c)
    ot_ref[0, 0] = _t5_lookup(tab_ref, h, c - r)
    ot_ref[0, 1] = _t5_lookup(tab_ref, h, PAGE_SIZE + c - r)


def t5_tiles(table):
    return pl.pallas_call(
        _t5_tiles_body,
        grid=(H_B,),
        in_specs=[pl.BlockSpec(memory_space=pltpu.SMEM)],
        out_specs=[pl.BlockSpec((1, PAGE_SIZE, 2 * PAGE_SIZE), lambda h: (h, 0, 0)),
                   pl.BlockSpec((1, 2, PAGE_SIZE, PAGE_SIZE),
                                lambda h: (h // HEADS_PER_PAIR, 0, 0, h % HEADS_PER_PAIR))],
        out_shape=[jax.ShapeDtypeStruct((H_B, PAGE_SIZE, 2 * PAGE_SIZE), F32),
                   jax.ShapeDtypeStruct((H_B // HEADS_PER_PAIR, 2, PAGE_SIZE, HEADS_PER_PAIR * PAGE_SIZE), F32)],
        compiler_params=_cparams("arbitrary"),
        name="t5_tiles",
    )(table.astype(F32))


def _indexer_tile(qih, wcol, ki_tile):
    acc = jnp.zeros((qih[0].shape[0], ki_tile.shape[0]), F32)
    for h in range(H_IDX):
        acc = acc + jnp.maximum(_dot_t(qih[h], ki_tile), 0.0) * wcol[h]
    return acc


def _sort_key(score):
    score = jnp.where(score == 0.0, 0.0, score)
    u = lax.bitcast_convert_type(score, I32)
    return jnp.where(u < 0, u ^ 0x7FFFFFFF, u)


def _kth_largest(count_ge, shape, topk):
    def try_cand(cand, cur):
        return jnp.where(count_ge(cand) >= topk, cand, cur)

    t0 = try_cand(jnp.zeros(shape, I32), jnp.full(shape, INT_MIN, I32))

    def step(b, cur):
        cand = cur + lax.shift_left(jnp.int32(1), jnp.int32(30) - b)
        return try_cand(cand, cur)

    return lax.fori_loop(0, 31, step, t0)


def _strict_upper(n):
    r = lax.broadcasted_iota(I32, (n, n), 0)
    c = lax.broadcasted_iota(I32, (n, n), 1)
    return (r < c).astype(BF16)


def _strict_lower(n):
    r = lax.broadcasted_iota(I32, (n, n), 0)
    c = lax.broadcasted_iota(I32, (n, n), 1)
    return (c < r).astype(BF16)


def _tree(parts, op):
    parts = list(parts)
    while len(parts) > 1:
        nxt = [op(parts[k], parts[k + 1]) for k in range(0, len(parts) - 1, 2)]
        if len(parts) % 2:
            nxt.append(parts[-1])
        parts = nxt
    return parts[0]


def _fold_rows(x, op):
    return _tree([x[r:r + V7X_SUBLANES] for r in range(0, x.shape[0], V7X_SUBLANES)], op)


LOOP_UNROLL = 4


def _loop_tiles(n, body, carry, unroll=LOOP_UNROLL):
    shift = int(math.log2(unroll))
    n_main = lax.shift_right_logical(n, shift)
    carry = lax.fori_loop(0, n_main, lambda t, c: body(t * unroll, unroll, c), carry)
    return lax.fori_loop(n_main * unroll, n, lambda j, c: body(j, 1, c), carry)


def _dsa_prompt_body(q_ref, qi_ref, wi_ref, k_ref, vt_ref, ki_ref, bt_ref, o_ref,
                     key_ref, am_ref, lg_ref, acc_ref, ot_ref, hi_ref, lo_ref, *, topk):
    i = pl.program_id(1)
    nj = i + 1
    qb = PAGE_SIZE
    kpos = lax.broadcasted_iota(I32, (qb, qb), 0)
    qpos = lax.broadcasted_iota(I32, (qb, qb), 1)

    def kslice(j0, u=1):
        return pl.ds(pl.multiple_of(j0 * qb, qb), u * qb)

    qit = qi_ref[0].T
    qit = jnp.concatenate([qit[h * D_IDX:(h + 1) * D_IDX] for h in range(H_IDX)], axis=1).astype(BF16)
    wsc = (wi_ref[0] * (H_IDX ** -0.5)) * (D_IDX ** -0.5)
    wt = jnp.concatenate([wsc, jnp.zeros((qb, qb - H_IDX), F32)], axis=1).T
    wrow = jnp.concatenate([wt[h:h + 1] for h in range(H_IDX)], axis=1)

    def idx_step(j0, u, carry):
        s = jnp.maximum(_dot(ki_ref[0, kslice(j0, u), :], qit), 0.0) * wrow
        sc = s[:, 0:qb]
        for h in range(1, H_IDX):
            sc = sc + s[:, h * qb:(h + 1) * qb]
        key = _sort_key(sc)
        for t in range(u):
            valid = (j0 + t < i) | (kpos <= qpos)
            key_ref[j0 + t] = jnp.where(valid, key[t * qb:(t + 1) * qb], INT_MIN)
        return carry

    _loop_tiles(nj, idx_step, 0)

    def count_where(pred):
        def cstep(j0, u, c):
            for t in range(u):
                c = c + jnp.where(pred(key_ref[j0 + t]), 1.0, 0.0)
            return c
        c = _loop_tiles(nj, cstep, jnp.zeros((qb, qb), F32))
        return jnp.sum(_fold_rows(c, jnp.add), axis=0, keepdims=True)

    key_ref[nj] = jnp.full((qb, qb), INT_MIN, I32)
    npair = lax.shift_right_logical(nj + 1, 1)
    min16 = -(2 ** 15)

    def pack_step(t, carry):
        k2 = jnp.concatenate([key_ref[2 * t], key_ref[2 * t + 1]], axis=0)
        hi_ref[t] = lax.shift_right_arithmetic(k2, 16).astype(I16)
        lo_ref[t] = ((k2 & 0xFFFF) + min16).astype(I16)
        return carry

    lax.fori_loop(0, npair, pack_step, 0)

    def count16(ref, pred):
        def cstep(t0, u, c):
            for t in range(u):
                c = c + jnp.where(pred(ref[t0 + t]), jnp.int16(1), jnp.int16(0))
            return c
        c = _loop_tiles(npair, cstep, jnp.zeros((2 * qb, qb), I16)).astype(F32)
        return jnp.sum(_fold_rows(c, jnp.add), axis=0, keepdims=True)

    def search16(ref, base):
        def try_cand(cand, cur):
            c16 = cand.astype(I16)
            return jnp.where(base + count16(ref, lambda x: x >= c16) >= topk, cand, cur)

        t0 = try_cand(jnp.zeros((1, qb), I32), jnp.full((1, qb), min16, I32))
        return lax.fori_loop(0, 15, lambda b, cur: try_cand(cur + lax.shift_left(jnp.int32(1), jnp.int32(14) - b), cur), t0)

    thr_hi = search16(hi_ref, 0.0)
    thr_hi16 = thr_hi.astype(I16)
    n_hi_gt = count16(hi_ref, lambda x: x > thr_hi16)

    def keep_ties(t, carry):
        lo_ref[t] = jnp.where(hi_ref[t] == thr_hi16, lo_ref[t], jnp.int16(min16))
        return carry

    lax.fori_loop(0, npair, keep_ties, 0)
    thr = thr_hi * 65536 + (search16(lo_ref, n_hi_gt) - min16)
    n_tie = topk - count_where(lambda key: key > thr)
    n_eq = count_where(lambda key: (key == thr) & (key != INT_MIN))
    need_rank = jnp.max(n_eq - n_tie) > 0.0

    @pl.when(jnp.logical_not(need_rank))
    def _():
        def mask_step(j0, u, carry):
            for t in range(u):
                key = key_ref[j0 + t]
                am_ref[j0 + t] = jnp.where((key >= thr) & (key != INT_MIN), 0.0, NEG_BIG)
            return carry
        _loop_tiles(nj, mask_step, 0)

    @pl.when(need_rank)
    def _():
        lower = _strict_lower(qb)

        def mask_step(j, run):
            key = key_ref[j]
            eq = (key == thr) & (key != INT_MIN)
            eqf = jnp.where(eq, 1.0, 0.0)
            rank = _dot(lower, eqf.astype(BF16)) + run
            sel = (key > thr) | (eq & (rank < n_tie))
            am_ref[j] = jnp.where(sel, 0.0, NEG_BIG)
            return run + jnp.sum(_fold_rows(eqf, jnp.add), axis=0, keepdims=True)

        lax.fori_loop(0, nj, mask_step, jnp.zeros((1, qb), F32))

    qt = q_ref[0].T
    pw = HEADS_PER_PAIR * qb
    gw = G_B * qb
    for pp in range(KV_B // 2):
        def group_qt(n):
            return jnp.concatenate([qt[h * DH_B:(h + 1) * DH_B] for h in range(n * G_B, (n + 1) * G_B)],
                                   axis=1)
        zero = jnp.zeros((DH_B, gw), F32)
        rhs = jnp.concatenate([jnp.concatenate([group_qt(2 * pp), zero], axis=1),
                               jnp.concatenate([zero, group_qt(2 * pp + 1)], axis=1)], axis=0)
        rhs = (rhs * (DH_B ** -0.5)).astype(BF16)
        klanes = slice(pp * 2 * DH_B, (pp + 1) * 2 * DH_B)
        far = bt_ref[pp, 1, 0:1, :]

        def tile_logits(j0, u, bias, mx):
            s = _dot(k_ref[0, kslice(j0, u), klanes], rhs)
            for t in range(u):
                st = s[t * qb:(t + 1) * qb] + bias + jnp.concatenate([am_ref[j0 + t]] * HEADS_PER_PAIR, axis=1)
                lg_ref[j0 + t] = st
                mx = jnp.maximum(mx, _fold_rows(st, jnp.maximum))
            return mx

        neg = jnp.full((V7X_SUBLANES, pw), NEG_BIG, F32)
        mx = _loop_tiles(jnp.maximum(i - 1, 0), lambda j0, u, m: tile_logits(j0, u, far, m), neg)
        mx = lax.cond(i >= 1, lambda m: tile_logits(i - 1, 1, bt_ref[pp, 1], m), lambda m: m, mx)
        mx = tile_logits(i, 1, bt_ref[pp, 0], mx)
        m = jnp.max(mx, axis=0, keepdims=True)
        acc_ref[...] = jnp.zeros(acc_ref.shape, F32)

        def p2(j0, u, l):
            ps = []
            for t in range(u):
                p = jnp.exp(lg_ref[j0 + t] - m)
                l = l + _fold_rows(p, jnp.add)
                ps.append(p.astype(BF16))
            pb = jnp.concatenate(ps, axis=0)
            vt = jnp.concatenate([vt_ref[0, j0 + t] for t in range(u)], axis=1)
            for gg in range(2):
                n = 2 * pp + gg
                acc_ref[gg] += _dot(vt[n * DH_B:(n + 1) * DH_B, :], pb[:, gg * gw:(gg + 1) * gw])
            return l

        l = _loop_tiles(nj, p2, jnp.zeros((V7X_SUBLANES, pw), F32))
        lsum = jnp.sum(l, axis=0, keepdims=True)
        for gg in range(2):
            o_t = acc_ref[gg] / lsum[:, gg * gw:(gg + 1) * gw]
            for hh in range(G_B):
                h = (2 * pp + gg) * G_B + hh
                ot_ref[h * DH_B:(h + 1) * DH_B, :] = o_t[:, hh * qb:(hh + 1) * qb]
    o_ref[0] = ot_ref[...].T.astype(o_ref.dtype)


def dsa_prompt(qn, qi, wi, kb, vtb, kib, btt):
    n_seq, length, _ = qn.shape
    nkb = length // PAGE_SIZE
    qmap = lambda s, i: (s, i, 0)
    kmap = lambda s, i: (s, 0, 0)
    pw = HEADS_PER_PAIR * PAGE_SIZE
    return pl.pallas_call(
        functools.partial(_dsa_prompt_body, topk=min(TOPK, length // 4)),
        grid=(n_seq, nkb),
        in_specs=[pl.BlockSpec((1, PAGE_SIZE, H_B * DH_B), qmap),
                  pl.BlockSpec((1, PAGE_SIZE, H_IDX * D_IDX), qmap),
                  pl.BlockSpec((1, PAGE_SIZE, H_IDX), qmap),
                  pl.BlockSpec((1, length, KV_B * DH_B), kmap),
                  pl.BlockSpec((1, nkb, KV_B * DH_B, PAGE_SIZE), lambda s, i: (s, 0, 0, 0)),
                  pl.BlockSpec((1, length, D_IDX), kmap),
                  pl.BlockSpec((H_B // HEADS_PER_PAIR, 2, PAGE_SIZE, pw), lambda s, i: (0, 0, 0, 0))],
        out_specs=pl.BlockSpec((1, PAGE_SIZE, H_B * DH_B), qmap),
        out_shape=jax.ShapeDtypeStruct((n_seq, length, H_B * DH_B), BF16),
        scratch_shapes=[pltpu.VMEM((nkb + 1, PAGE_SIZE, PAGE_SIZE), I32),
                        pltpu.VMEM((nkb, PAGE_SIZE, PAGE_SIZE), F32),
                        pltpu.VMEM((nkb, PAGE_SIZE, pw), F32),
                        pltpu.VMEM((2, DH_B, G_B * PAGE_SIZE), F32),
                        pltpu.VMEM((H_B * DH_B, PAGE_SIZE), F32),
                        pltpu.VMEM((nkb // 2 + 1, 2 * PAGE_SIZE, PAGE_SIZE), I16),
                        pltpu.VMEM((nkb // 2 + 1, 2 * PAGE_SIZE, PAGE_SIZE), I16)],
        compiler_params=_cparams("arbitrary", "arbitrary"),
        name="dsa_prompt",
    )(qn, qi, wi, kb, vtb, kib, btt)


N_PAGES = PAST_LEN // PAGE_SIZE
N_KTILES = N_PAGES + 1
N_KTILES_PAD = -(-N_KTILES // PAGES_PER_STEP) * PAGES_PER_STEP


def _dsa_sample_sel_body(pt_ref, qi_ref, wi_ref, kin_ref, *rest, lv, topk):
    page_refs = rest[:PAGES_PER_STEP]
    am_ref, key_ref, qs_ref, ws_ref = rest[PAGES_PER_STEP:]
    c = pl.program_id(1)
    nsteps = pl.num_programs(1)
    r8 = SAMPLE_PAD

    @pl.when(c == 0)
    def _():
        qi = qi_ref[0]
        wsc = (wi_ref[0] * (H_IDX ** -0.5)) * (D_IDX ** -0.5)
        qs_ref[...] = jnp.concatenate([qi[:, h * D_IDX:(h + 1) * D_IDX] for h in range(H_IDX)],
                                      axis=0).astype(BF16)
        ws_ref[...] = jnp.concatenate([jnp.broadcast_to(wsc[:, h:h + 1], (r8, PAGE_SIZE))
                                       for h in range(H_IDX)], axis=0)

    def score(ki_t):
        s = jnp.maximum(_dot(qs_ref[...], ki_t), 0.0) * ws_ref[...]
        return _fold_rows(s, jnp.add)

    for r in range(PAGES_PER_STEP):
        key_ref[c * PAGES_PER_STEP + r] = _sort_key(score(page_refs[r][0].astype(BF16)))

    @pl.when(c == nsteps - 1)
    def _():
        rows = lax.broadcasted_iota(I32, (r8, PAGE_SIZE), 0)
        cols = lax.broadcasted_iota(I32, (r8, PAGE_SIZE), 1)
        new_valid = (cols <= rows) & (cols < lv)
        kin = jnp.concatenate([kin_ref[0], jnp.zeros((PAGE_SIZE - r8, D_IDX), F32)], axis=0)
        kin_t = jnp.concatenate([kin, jnp.zeros((PAGE_SIZE, PAGE_SIZE - D_IDX), F32)], axis=1).T[0:D_IDX]
        key_ref[N_PAGES] = jnp.where(new_valid, _sort_key(score(kin_t.astype(BF16))), INT_MIN)

        def tile_valid(j):
            return new_valid if j == N_PAGES else None

        def count_where(pred):
            parts = []
            for j in range(N_KTILES):
                hit = pred(key_ref[j])
                if tile_valid(j) is not None:
                    hit = hit & tile_valid(j)
                parts.append(jnp.where(hit, 1.0, 0.0))
            return jnp.sum(_tree(parts, jnp.add), axis=-1, keepdims=True)

        thr = _kth_largest(lambda cand: count_where(lambda key: key >= cand), (r8, 1), topk)
        n_tie = topk - count_where(lambda key: key > thr)
        n_eq = count_where(lambda key: key == thr)
        need_rank = jnp.max(n_eq - n_tie) > 0.0

        @pl.when(jnp.logical_not(need_rank))
        def _():
            for j in range(N_KTILES):
                sel = key_ref[j] >= thr
                if tile_valid(j) is not None:
                    sel = sel & tile_valid(j)
                am_ref[0, j] = jnp.where(sel, 0.0, NEG_BIG)

        @pl.when(need_rank)
        def _():
            upper = _strict_upper(PAGE_SIZE)

            def mask_step(j, run):
                key = key_ref[j]
                valid = (j < N_PAGES) | new_valid
                eq = valid & (key == thr)
                eqf = jnp.where(eq, 1.0, 0.0)
                rank = _dot(eqf.astype(BF16), upper) + run
                sel = (key > thr) | (eq & (rank < n_tie))
                am_ref[0, j] = jnp.where(sel, 0.0, NEG_BIG)
                return run + jnp.sum(eqf, axis=-1, keepdims=True)

            lax.fori_loop(0, N_KTILES, mask_step, jnp.zeros((r8, 1), F32))

        for j in range(N_KTILES, N_KTILES_PAD):
            am_ref[0, j] = jnp.full((r8, PAGE_SIZE), NEG_BIG, F32)


def dsa_sample_select(page_table, qi, wi, kin, kidx_pool_t, lv):
    n_seq = qi.shape[0]
    r8 = SAMPLE_PAD
    qmap = lambda s, c, pt: (s, 0, 0)
    page_specs = [pl.BlockSpec((1, D_IDX, PAGE_SIZE),
                               functools.partial(lambda s, c, pt, r: (pt[s, c * PAGES_PER_STEP + r], 0, 0), r=r))
                  for r in range(PAGES_PER_STEP)]
    grid_spec = pltpu.PrefetchScalarGridSpec(
        num_scalar_prefetch=1,
        grid=(n_seq, N_PAGES // PAGES_PER_STEP),
        in_specs=[pl.BlockSpec((1, r8, H_IDX * D_IDX), qmap),
                  pl.BlockSpec((1, r8, H_IDX), qmap),
                  pl.BlockSpec((1, r8, D_IDX), qmap)] + page_specs,
        out_specs=pl.BlockSpec((1, N_KTILES_PAD, r8, PAGE_SIZE), lambda s, c, pt: (s, 0, 0, 0)),
        scratch_shapes=[pltpu.VMEM((N_KTILES, r8, PAGE_SIZE), I32),
                        pltpu.VMEM((H_IDX * r8, D_IDX), BF16),
                        pltpu.VMEM((H_IDX * r8, PAGE_SIZE), F32)],
    )
    return pl.pallas_call(
        functools.partial(_dsa_sample_sel_body, lv=lv, topk=min(TOPK, (PAST_LEN + lv) // 4)),
        grid_spec=grid_spec,
        out_shape=jax.ShapeDtypeStruct((n_seq, N_KTILES_PAD, r8, PAGE_SIZE), F32),
        compiler_params=_cparams("arbitrary", "arbitrary"),
        name="dsa_sample_select",
    )(page_table, qi, wi, kin, *([kidx_pool_t] * PAGES_PER_STEP))


def _dsa_sample_attn_body(pt_ref, q_ref, kn_ref, vn_ref, am_ref, amn_ref, bt_ref, *rest):
    kp_refs = rest[:PAGES_PER_STEP]
    vp_refs = rest[PAGES_PER_STEP:2 * PAGES_PER_STEP]
    o_ref, m_ref, l_ref, acc_ref = rest[2 * PAGES_PER_STEP:]
    c = pl.program_id(1)
    nsteps = pl.num_programs(1)
    r8 = SAMPLE_PAD
    gr = G_B * r8

    @pl.when(c == 0)
    def _():
        m_ref[...] = jnp.full(m_ref.shape, NEG_BIG, F32)
        l_ref[...] = jnp.zeros(l_ref.shape, F32)
        acc_ref[...] = jnp.zeros(acc_ref.shape, F32)

    q = q_ref[0]
    qgs, b_subs, b_diags, b_fars = [], [], [], []
    for n in range(KV_B):
        heads = [n * G_B + hh for hh in range(G_B)]
        qg = jnp.concatenate([q[:, h * DH_B:(h + 1) * DH_B] for h in heads], axis=0)
        qgs.append((qg * (DH_B ** -0.5)).astype(BF16))
        b_diags.append(jnp.concatenate([bt_ref[h, :, 0:PAGE_SIZE] for h in heads], axis=0))
        b_subs.append(jnp.concatenate([bt_ref[h, :, PAGE_SIZE:2 * PAGE_SIZE] for h in heads], axis=0))
        b_fars.append(jnp.concatenate(
            [jnp.broadcast_to(bt_ref[h, 0:1, PAGE_SIZE:PAGE_SIZE + 1], (r8, PAGE_SIZE))
             for h in heads], axis=0))

    def update(n, s, pv):
        m_old = m_ref[n]
        m_new = jnp.maximum(m_old, jnp.max(s, axis=-1, keepdims=True))
        alpha = jnp.exp(m_old - m_new)
        p = jnp.exp(s - m_new[:, 0:1])
        l_ref[n] = alpha * l_ref[n] + jnp.sum(p, axis=-1, keepdims=True)
        acc_ref[n] = alpha[:, 0:DH_B] * acc_ref[n] + pv(p.astype(BF16))
        m_ref[n] = m_new

    def group_t(refs, n):
        return jnp.concatenate([ref[0, n] for ref in refs], axis=1).astype(BF16)

    amcat = jnp.concatenate([am_ref[0, r] for r in range(PAGES_PER_STEP)], axis=1)
    am4 = jnp.concatenate([amcat] * G_B, axis=0)
    is_last_step = c == nsteps - 1
    for n in range(KV_B):
        bias = jnp.concatenate([b_fars[n]] * (PAGES_PER_STEP - 1)
                               + [jnp.where(is_last_step, b_subs[n], b_fars[n])], axis=1)
        s = _dot(qgs[n], group_t(kp_refs, n)) + bias + am4
        v_t = group_t(vp_refs, n)
        update(n, s, lambda p: _dot_t(p, v_t))

    @pl.when(c == nsteps - 1)
    def _():
        pad = jnp.zeros((PAGE_SIZE - r8, KV_B * DH_B), F32)
        kn = jnp.concatenate([kn_ref[0], pad], axis=0).astype(BF16)
        vn = jnp.concatenate([vn_ref[0], pad], axis=0).astype(BF16)
        am4 = jnp.concatenate([amn_ref[0, 0]] * G_B, axis=0)
        for n in range(KV_B):
            ksl = slice(n * DH_B, (n + 1) * DH_B)
            s = _dot_t(qgs[n], kn[:, ksl]) + b_diags[n] + am4
            update(n, s, lambda p: _dot(p, vn[:, ksl]))
            o = acc_ref[n] / l_ref[n][:, 0:DH_B]
            for hh in range(G_B):
                h = n * G_B + hh
                o_ref[0, :, h * DH_B:(h + 1) * DH_B] = o[hh * r8:(hh + 1) * r8].astype(o_ref.dtype)


def dsa_sample_attend(page_table, qn, kn, vn, amask, bt, k_pool, v_pool):
    n_seq = qn.shape[0]
    r8 = SAMPLE_PAD
    qmap = lambda s, c, pt: (s, 0, 0)
    page_map = [functools.partial(lambda s, c, pt, r: (pt[s, c * PAGES_PER_STEP + r], 0, 0, 0), r=r)
                for r in range(PAGES_PER_STEP)]
    kv_w = KV_B * DH_B
    page_block = (1, KV_B, DH_B, PAGE_SIZE)
    grid_spec = pltpu.PrefetchScalarGridSpec(
        num_scalar_prefetch=1,
        grid=(n_seq, N_PAGES // PAGES_PER_STEP),
        in_specs=[pl.BlockSpec((1, r8, H_B * DH_B), qmap),
                  pl.BlockSpec((1, r8, kv_w), qmap),
                  pl.BlockSpec((1, r8, kv_w), qmap),
                  pl.BlockSpec((1, PAGES_PER_STEP, r8, PAGE_SIZE), lambda s, c, pt: (s, c, 0, 0)),
                  pl.BlockSpec((1, PAGES_PER_STEP, r8, PAGE_SIZE),
                               lambda s, c, pt: (s, N_PAGES // PAGES_PER_STEP, 0, 0)),
                  pl.BlockSpec((H_B, r8, 2 * PAGE_SIZE), lambda s, c, pt: (0, 0, 0))]
                 + [pl.BlockSpec(page_block, m) for m in page_map]
                 + [pl.BlockSpec(page_block, m) for m in page_map],
        out_specs=pl.BlockSpec((1, r8, H_B * DH_B), qmap),
        scratch_shapes=[pltpu.VMEM((KV_B, G_B * r8, PAGE_SIZE), F32),
                        pltpu.VMEM((KV_B, G_B * r8, PAGE_SIZE), F32),
                        pltpu.VMEM((KV_B, G_B * r8, DH_B), F32)],
    )
    return pl.pallas_call(
        _dsa_sample_attn_body,
        grid_spec=grid_spec,
        out_shape=jax.ShapeDtypeStruct((n_seq, r8, H_B * DH_B), BF16),
        compiler_params=_cparams("arbitrary", "arbitrary"),
        name="dsa_sample_attend",
    )(page_table, qn, kn, vn, amask, amask, bt,
      *([k_pool] * PAGES_PER_STEP), *([v_pool] * PAGES_PER_STEP))


HEADS_PER_GROUP = H_C // G_C
GROUP_W = D_IN_C // G_C


def _cumsum_rows(x):
    n = x.shape[0]
    idx = lax.broadcasted_iota(I32, x.shape, 0)
    d = 1
    while d < n:
        x = x + jnp.where(idx >= d, pltpu.roll(x, d, 0), 0.0)
        d *= 2
    return x


def _cumsum_lanes(x):
    n = x.shape[1]
    idx = lax.broadcasted_iota(I32, x.shape, 1)
    d = 1
    while d < n:
        x = x + jnp.where(idx >= d, pltpu.roll(x, d, 1), 0.0)
        d *= 2
    return x


def _softplus(x):
    return jnp.maximum(x, 0.0) + jnp.log1p(jnp.exp(-jnp.abs(x)))


def _split3(x):
    hi = x.astype(BF16)
    r = x - hi.astype(F32)
    mid = r.astype(BF16)
    lo = (r - mid.astype(F32)).astype(BF16)
    return hi, mid, lo


def _ssd_body(x_ref, xbc_ref, zlo_ref, zhi_ref, ng_ref, wdt_ref, wdtt_ref, dtb_ref, dtbt_ref,
              alog_ref, alogt_ref, dsk_ref, cn_ref, wo_ref, h0_ref, o_ref, hst_ref, *, qin, lv):
    c = pl.program_id(1)
    q = CHUNK_C
    rows = lax.broadcasted_iota(I32, (q, q), 0)
    cols = lax.broadcasted_iota(I32, (q, q), 1)
    tril = cols <= rows

    def pad_rows(a):
        if qin == q:
            return a
        return jnp.concatenate([a, jnp.zeros((q - qin, a.shape[1]), a.dtype)], axis=0)

    @pl.when(c == 0)
    def _():
        hst_ref[...] = h0_ref[...]

    x = x_ref[0]
    xn = _rms(pad_rows(x), ng_ref[...]).astype(BF16)
    dt_all = _softplus(_dot(xn, wdt_ref[...]) + dtb_ref[...])
    dtt_all = _softplus(_dot_t(wdtt_ref[...], xn) + dtbt_ref[...])
    if lv < q:
        dt_all = jnp.where(lax.broadcasted_iota(I32, dt_all.shape, 0) < lv, dt_all, 0.0)
        dtt_all = jnp.where(lax.broadcasted_iota(I32, dtt_all.shape, 1) < lv, dtt_all, 0.0)
    acs_all = _tree([_dot(tril.astype(BF16), p) for p in _split3(dt_all * (-jnp.exp(alog_ref[...])))], jnp.add)
    acst_all = _tree([_dot(p, (rows <= cols).astype(BF16))
                      for p in _split3(dtt_all * (-jnp.exp(alogt_ref[...])))], jnp.add)
    z_all = pad_rows(jnp.concatenate([zlo_ref[0], zhi_ref[0]], axis=1))
    y_groups = []
    for g in range(G_C):
        y_groups.append(_ssd_group(g, xbc_ref, hst_ref, dt_all, dtt_all, acs_all, acst_all, z_all,
                                   dsk_ref, cn_ref, tril, pad_rows))
    y = jnp.concatenate(y_groups, axis=1)
    o_ref[0] = x + _dot(y, wo_ref[...])[0:qin]


def _ssd_group(g, xbc_ref, hst_ref, dt_all, dtt_all, acs_all, acst_all, z_all, dsk_ref, cn_ref, tril, pad_rows):
    q = CHUNK_C
    hpg = HEADS_PER_GROUP
    gheads = slice(g * hpg, (g + 1) * hpg)
    gcols = slice(g * GROUP_W, (g + 1) * GROUP_W)
    dt = dt_all[:, gheads]
    dtt = dtt_all[gheads, :]
    acs = acs_all[:, gheads]
    acst = acst_all[gheads, :]
    xs = pad_rows(xbc_ref[0, :, gcols])
    bm = pad_rows(xbc_ref[0, :, D_IN_C + g * N_SSM:D_IN_C + (g + 1) * N_SSM])
    cm = pad_rows(xbc_ref[0, :, D_IN_C + (G_C + g) * N_SSM:D_IN_C + (G_C + g + 1) * N_SSM])
    xst = xs.T
    cmb = cm.astype(BF16)
    cb = _dot_t(cmb, bm.astype(BF16))

    def spread(v, width):
        src = lax.broadcasted_iota(I32, (hpg, hpg * width), 0)
        dst = lax.shift_right_logical(lax.broadcasted_iota(I32, (hpg, hpg * width), 1), int(math.log2(width)))
        sel = (src == dst).astype(BF16)
        return _tree([_dot(p, sel) for p in _split3(v)], jnp.add)

    acs_w = spread(acs, q)
    decay_w = jnp.exp(acs_w[q - 1:q, :] - acs_w)
    eacs_w = jnp.exp(acs_w)
    xdt = (xs * spread(dt, P_C)).astype(BF16)
    low_half = lax.broadcasted_iota(I32, (q, 2 * P_C), 1) < P_C
    h_all = hst_ref[0, gheads]
    ys, h_new = [], []
    for k in range(hpg // 2):
        xpair = xdt[:, k * 2 * P_C:(k + 1) * 2 * P_C]
        yd = []
        for j in (2 * k, 2 * k + 1):
            blk = slice(j * q, (j + 1) * q)
            acs_row = acst[j:j + 1, :]
            a_last = acs_row[:, q - 1:q]
            lmat = jnp.exp(jnp.where(tril, acs_w[:, blk] - acs_row, -jnp.inf))
            yd.append(_dot((cb * lmat).astype(BF16), xpair))
            xht = xst[j * P_C:(j + 1) * P_C, :] * dtt[j:j + 1, :]
            st = _dot(xht.astype(BF16), (bm * decay_w[:, blk]).astype(BF16))
            h_new.append(jnp.exp(a_last) * h_all[j] + st)
        h_pair = h_all[2 * k:2 * k + 2].reshape(2 * P_C, N_SSM).astype(BF16)
        e_pair = jnp.where(low_half, eacs_w[:, 2 * k * q:(2 * k + 1) * q], eacs_w[:, (2 * k + 1) * q:(2 * k + 2) * q])
        ys.append(jnp.where(low_half, yd[0], yd[1]) + _dot_t(cmb, h_pair) * e_pair)
    hst_ref[0, gheads] = jnp.stack(h_new, axis=0)
    y = jnp.concatenate(ys, axis=1) + xs * dsk_ref[:, gcols]
    y = y * _silu(z_all[:, gcols])
    return _rms(y, cn_ref[:, gcols]).astype(BF16)


def ssd_mix(x, xbc, proj, z_col0, h0, norm_g, w_dt, dt_bias, a_log, d_skip, c_norm, w_out, qin, lv):
    n_seq, length, d = x.shape
    nchunk = length // qin
    col = lambda a: a.reshape(-1, 1).astype(F32)
    row = lambda a: a.reshape(1, -1).astype(F32)
    full2 = lambda s, c: (0, 0)
    conv_c = xbc.shape[-1]
    zw = D_IN_C // 2
    assert z_col0 % zw == 0
    in_specs = [
        pl.BlockSpec((1, qin, d), lambda s, c: (s, c, 0)),
        pl.BlockSpec((1, qin, conv_c), lambda s, c: (s, c, 0)),
        pl.BlockSpec((1, qin, zw), lambda s, c: (s, c, z_col0 // zw)),
        pl.BlockSpec((1, qin, zw), lambda s, c: (s, c, z_col0 // zw + 1)),
        pl.BlockSpec((1, d), full2),
        pl.BlockSpec((d, H_C), full2),
        pl.BlockSpec((H_C, d), full2),
        pl.BlockSpec((1, H_C), full2),
        pl.BlockSpec((H_C, 1), full2),
        pl.BlockSpec((1, H_C), full2),
        pl.BlockSpec((H_C, 1), full2),
        pl.BlockSpec((1, D_IN_C), full2),
        pl.BlockSpec((1, D_IN_C), full2),
        pl.BlockSpec((D_IN_C, d), full2),
        pl.BlockSpec((1, H_C, P_C, N_SSM), lambda s, c: (s, 0, 0, 0)),
    ]
    out, hst = pl.pallas_call(
        functools.partial(_ssd_body, qin=qin, lv=lv),
        grid=(n_seq, nchunk),
        in_specs=in_specs,
        out_specs=[pl.BlockSpec((1, qin, d), lambda s, c: (s, c, 0)),
                   pl.BlockSpec((1, H_C, P_C, N_SSM), lambda s, c: (s, 0, 0, 0))],
        out_shape=[jax.ShapeDtypeStruct((n_seq, length, d), F32),
                   jax.ShapeDtypeStruct((n_seq, H_C, P_C, N_SSM), F32)],
        compiler_params=_cparams("arbitrary", "arbitrary"),
        name="ssd_mix",
    )(x, xbc, proj, proj, row(norm_g), w_dt.astype(BF16), w_dt.T.astype(BF16),
      row(dt_bias), col(dt_bias), row(a_log), col(a_log),
      row(jnp.repeat(d_skip, P_C)), row(c_norm), w_out, h0)
    return out, hst


def _flat(a):
    return a.reshape(-1, a.shape[-1])


def _unflat(a, like):
    return a.reshape(like.shape[0], like.shape[1], a.shape[-1])


def kernel(x_prompt, x_sample, mem_prompt, state_a_conv, cache_b_k, cache_b_v, cache_b_kidx, state_c_conv, state_c_ssm, state_d_conv, state_ffn_conv, cache_mem_k, cache_mem_v, page_table, rel_bias, norm_mix, norm_mem, norm_ffn, norm_memtok, a_w_in, a_b_in, a_w_conv, a_b_conv, a_ln_g, a_ln_b, a_w_out, b_w_in, b_w_out, b_q_norm, b_k_norm, b_kidx_norm, c_w_in, c_w_conv, c_b_conv, c_dt_bias, c_a_log, c_d_skip, c_norm, c_w_out, d_w_in, d_w_conv, d_w_out, m_w_q, m_w_kv, m_w_o, m_q_norm, m_k_norm, f_w_in, f_w_conv, f_b_conv, f_w_out):
    n_p, l_p, d = x_prompt.shape
    n_s, l_s, _ = x_sample.shape
    bf = lambda w: w.astype(BF16)
    xs_pad = jnp.pad(x_sample, ((0, 0), (0, SAMPLE_PAD - l_s), (0, 0)))
    groups = [dict(x=x_prompt, sb=1, tm=256, tmf=512, lv=256, qin=CHUNK_C, lvq=CHUNK_C, prompt=True),
              dict(x=xs_pad, sb=n_s, tm=SAMPLE_PAD, tmf=SAMPLE_PAD, lv=l_s, qin=SAMPLE_PAD, lvq=l_s, prompt=False)]
    bt, btt = t5_tiles(rel_bias)
    mem_flat = _flat(mem_prompt)
    outs = {k: [[], []] for k in ("a", "bk", "bv", "bki", "cc", "cs", "d", "f")}
    m_kp, m_vp = [], []

    for i in range(DEPTH):
        j = i // 4
        kind = i % 4
        kv = linear(mem_flat, bf(m_w_kv[i]), g=norm_memtok[i], name="mem_kv")
        mk, _ = head_norm(kv, 0, d, m_k_norm[i], DH_M, name="mem_k_norm")
        m_kp.append(mk.reshape(n_p, N_MEM, H_M, DH_M))
        m_vp.append(kv[:, d:].reshape(n_p, N_MEM, H_M, DH_M))
        mk = m_kp[-1].transpose(0, 2, 1, 3)
        mv = m_vp[-1].transpose(0, 2, 1, 3)

        for gi, grp in enumerate(groups):
            x = grp["x"]
            n_seq, length, _ = x.shape
            prompt = grp["prompt"]
            zeros_state = lambda w, c: jnp.zeros((n_seq, w - 1, c), F32)
            xf = _flat(x)
            if kind == 0:
                st = zeros_state(W_A, d) if prompt else state_a_conv[j]
                x, nst = conv_mixer(x, st, norm_mix[i], bf(a_w_in[j]), a_b_in[j], a_w_conv[j],
                                    [a_b_conv[j], a_ln_g[j], a_ln_b[j]], bf(a_w_out[j]), _a_pre, _a_post,
                                    grp["sb"], grp["tm"], grp["lv"], "a_mixer")
                outs["a"][gi].append(nst)
            elif kind == 1:
                qn, kn, knb, v, vb, qi, kin, kinb, wi = b_proj(xf, norm_mix[i], b_w_in[j], b_q_norm[j],
                                                               b_k_norm[j], b_kidx_norm[j])
                kw = KV_B * DH_B
                r3 = lambda a: a.reshape(n_seq, length, a.shape[-1])
                if prompt:
                    vtb = vb.reshape(n_seq, length // PAGE_SIZE, PAGE_SIZE, kw).transpose(0, 1, 3, 2)
                    o = dsa_prompt(r3(qn), r3(qi), r3(wi), r3(knb), vtb, r3(kinb), btt)
                else:
                    am = dsa_sample_select(page_table, r3(qi), r3(wi), r3(kin),
                                           cache_b_kidx[j].transpose(0, 2, 1), grp["lv"])
                    o = dsa_sample_attend(page_table, r3(qn), r3(kn), r3(v), am, bt,
                                          cache_b_k[j].transpose(0, 2, 3, 1), cache_b_v[j].transpose(0, 2, 3, 1))
                outs["bk"][gi].append(r3(kn))
                outs["bv"][gi].append(r3(v))
                outs["bki"][gi].append(r3(kin))
                x = _unflat(linear(_flat(o), bf(b_w_out[j]), res=xf, name="b_out"), x)
            elif kind == 2:
                w_in = c_w_in[j]
                conv_c = D_IN_C + 2 * G_C * N_SSM
                w_main = jnp.concatenate([w_in[:, D_IN_C:D_IN_C + conv_c], w_in[:, :D_IN_C]], axis=1)
                proj = _unflat(linear(xf, bf(w_main), g=norm_mix[i], name="c_in"), x)
                st = zeros_state(W_C, conv_c) if prompt else state_c_conv[j]
                xbc, nst = seq_conv([(proj, conv_c, 0)], st, c_w_conv[j], [c_b_conv[j]], _c_pre, _c_post,
                                    conv_c, conv_c, F32, grp["tm"], grp["lv"], "c_conv")
                h0 = jnp.zeros((n_seq, H_C, P_C, N_SSM), F32) if prompt else state_c_ssm[j]
                x, hst = ssd_mix(x, xbc, proj, conv_c, h0, norm_mix[i], w_in[:, D_IN_C + conv_c:],
                                 c_dt_bias[j], c_a_log[j], c_d_skip[j], c_norm[j], bf(c_w_out[j]),
                                 grp["qin"], grp["lvq"])
                outs["cc"][gi].append(nst)
                outs["cs"][gi].append(hst)
            else:
                st = zeros_state(W_D, d) if prompt else state_d_conv[j]
                x, nst = conv_mixer(x, st, norm_mix[i], bf(d_w_in[j]), None, d_w_conv[j], [], bf(d_w_out[j]),
                                    _d_pre, _d_post, grp["sb"], grp["tm"], grp["lv"], "d_mixer")
                outs["d"][gi].append(nst)

            if prompt:
                x = mem_fused(x, mk, mv, norm_mem[i], bf(m_w_q[i]), m_q_norm[i], bf(m_w_o[i]), 512)
            else:
                xf = _flat(x)
                qm = _unflat(linear(xf, bf(m_w_q[i]), g=norm_mem[i], name="mem_q"), x)
                om = mem_attn_cached(qm, cache_mem_k.reshape(DEPTH * n_seq, N_MEM, H_M, DH_M),
                                     cache_mem_v.reshape(DEPTH * n_seq, N_MEM, H_M, DH_M),
                                     m_q_norm[i], MEM_SEQS_PER_STEP, i * n_seq)
                x = _unflat(linear(_flat(om), bf(m_w_o[i]), res=xf, name="mem_o"), x)

            st = zeros_state(W_F, D_FF) if prompt else state_ffn_conv[i]
            x, nst = ffn_fused(x, st, norm_ffn[i], bf(f_w_in[i]), f_w_conv[i], f_b_conv[i], bf(f_w_out[i]),
                               1 if prompt else n_seq, grp["tmf"], grp["tmf"] if prompt else grp["lv"])
            outs["f"][gi].append(nst)
            grp["x"] = x

    yp = groups[0]["x"]
    ys = groups[1]["x"][:, :l_s]
    st = lambda key, gi: jnp.stack(outs[key][gi])
    kvshape = lambda a, n, l: a.reshape(a.shape[0], n, -1, a.shape[-1])[:, :, :l]
    b_k_p = kvshape(st("bk", 0), n_p, l_p).reshape(-1, n_p, l_p, KV_B, DH_B)
    b_v_p = kvshape(st("bv", 0), n_p, l_p).reshape(-1, n_p, l_p, KV_B, DH_B)
    b_ki_p = kvshape(st("bki", 0), n_p, l_p)
    b_k_s = kvshape(st("bk", 1), n_s, l_s).reshape(-1, n_s, l_s, KV_B, DH_B)
    b_v_s = kvshape(st("bv", 1), n_s, l_s).reshape(-1, n_s, l_s, KV_B, DH_B)
    b_ki_s = kvshape(st("bki", 1), n_s, l_s)
    return (yp, ys, st("a", 0), st("a", 1), b_k_p, b_v_p, b_ki_p, b_k_s, b_v_s, b_ki_s,
            st("cc", 0), st("cs", 0), st("cc", 1), st("cs", 1), st("d", 0), st("d", 1),
            st("f", 0), st("f", 1), jnp.stack(m_kp), jnp.stack(m_vp))
```

```python
import functools
import math

import jax
import jax.numpy as jnp
from jax import lax
from jax.experimental import pallas as pl
from jax.experimental.pallas import tpu as pltpu

F32 = jnp.float32
BF16 = jnp.bfloat16
I32 = jnp.int32

D_MODEL = 1024
DEPTH = 4
PAST_LEN = 8192
PAGE_SIZE = 128
W_A = 31
H_B = 16
DH_B = 64
KV_B = 4
G_B = H_B // KV_B
H_IDX = 8
D_IDX = 64
TOPK = 256
N_BUCKETS = 32
MAX_DIST = 128
D_IN_C = 2 * D_MODEL
P_C = 64
H_C = D_IN_C // P_C
G_C = 4
N_SSM = 128
W_C = 4
CHUNK_C = 128
W_D = 3
N_MEM = 256
H_M = 4
DH_M = D_MODEL // H_M
D_FF = 2816
W_F = 3
EPS = 1e-6

V7X_SUBLANES = 8
V7X_LANES = 128
V7X_VMEM_LIMIT_BYTES = 56 * 1024 * 1024

SAMPLE_PAD = V7X_SUBLANES
NEG_BIG = -1e30
INT_MIN = -(2 ** 31)
PAGES_PER_STEP = 16
MEM_SEQS_PER_STEP = 8


def _cparams(*sem):
    return pltpu.CompilerParams(dimension_semantics=sem, vmem_limit_bytes=V7X_VMEM_LIMIT_BYTES)


def _pick(n, cands):
    for c in cands:
        if n % c == 0:
            return c
    return n


def _rms(x, g):
    y = x * lax.rsqrt(jnp.mean(x * x, axis=-1, keepdims=True) + EPS)
    return y * g


def _dot(a, b):
    return jnp.dot(a, b, preferred_element_type=F32)


def _dot_t(a, b):
    return lax.dot_general(a, b, (((1,), (1,)), ((), ())), preferred_element_type=F32)


def _sigmoid(x):
    return 1.0 / (1.0 + jnp.exp(-x))


def _silu(x):
    return x * _sigmoid(x)


def _group_meansq(x, gsize):
    tm, c = x.shape
    x2 = x * x
    if gsize == c:
        return jnp.mean(x2, axis=-1, keepdims=True)
    if gsize % V7X_LANES == 0:
        parts = []
        for h in range(c // gsize):
            ms = jnp.mean(x2[:, h * gsize:(h + 1) * gsize], axis=-1, keepdims=True)
            parts.append(jnp.broadcast_to(ms, (tm, gsize)))
        return jnp.concatenate(parts, axis=-1)
    shift = int(math.log2(gsize))
    r = lax.shift_right_logical(lax.broadcasted_iota(I32, (c, c), 0), shift)
    q = lax.shift_right_logical(lax.broadcasted_iota(I32, (c, c), 1), shift)
    bd = (r == q).astype(BF16)
    hi = x2.astype(BF16)
    lo = (x2 - hi.astype(F32)).astype(BF16)
    return (_dot(hi, bd) + _dot(lo, bd)) * (1.0 / gsize)


def _group_rms(x, gain, gsize):
    return (x * lax.rsqrt(_group_meansq(x, gsize) + EPS)) * gain


def _linear_body(*refs, norm, bias, res):
    it = iter(refs)
    x_ref = next(it)
    g_ref = next(it) if norm else None
    w_ref = next(it)
    b_ref = next(it) if bias else None
    r_ref = next(it) if res else None
    o_ref = next(it)
    xn_ref = next(it)

    @pl.when(pl.program_id(1) == 0)
    def _():
        x = x_ref[...].astype(F32)
        if norm:
            x = _rms(x, g_ref[...])
        xn_ref[...] = x.astype(BF16)

    acc = _dot(xn_ref[...], w_ref[...])
    if bias:
        acc = acc + b_ref[...]
    if res:
        acc = acc + r_ref[...]
    o_ref[...] = acc.astype(o_ref.dtype)


def linear(x, w, *, g=None, b=None, res=None, out_dtype=F32, name="linear"):
    m, k = x.shape
    n = w.shape[1]
    tm = _pick(m, (1024, 512, 256, 128))
    tn = _pick(n, (1024, 1408, 512, 256, 128))
    in_specs = [pl.BlockSpec((tm, k), lambda i, j: (i, 0))]
    args = [x]
    if g is not None:
        in_specs.append(pl.BlockSpec((1, k), lambda i, j: (0, 0)))
        args.append(g.reshape(1, k).astype(F32))
    in_specs.append(pl.BlockSpec((k, tn), lambda i, j: (0, j)))
    args.append(w)
    if b is not None:
        in_specs.append(pl.BlockSpec((1, tn), lambda i, j: (0, j)))
        args.append(b.reshape(1, n).astype(F32))
    if res is not None:
        in_specs.append(pl.BlockSpec((tm, tn), lambda i, j: (i, j)))
        args.append(res)
    return pl.pallas_call(
        functools.partial(_linear_body, norm=g is not None, bias=b is not None, res=res is not None),
        grid=(m // tm, n // tn),
        in_specs=in_specs,
        out_specs=pl.BlockSpec((tm, tn), lambda i, j: (i, j)),
        out_shape=jax.ShapeDtypeStruct((m, n), out_dtype),
        scratch_shapes=[pltpu.VMEM((tm, k), BF16)],
        compiler_params=_cparams("arbitrary", "arbitrary"),
        name=name,
    )(*args)


def _head_norm_body(x_ref, g_ref, o_ref, ob_ref, *, gsize):
    y = _group_rms(x_ref[...], g_ref[...], gsize)
    o_ref[...] = y
    ob_ref[...] = y.astype(BF16)


def head_norm(x, col0, width, gain, gsize, name="head_norm"):
    m = x.shape[0]
    tm = _pick(m, (512, 256, 128))
    assert col0 % width == 0
    gt = jnp.tile(gain.astype(F32), width // gsize).reshape(1, width)
    return pl.pallas_call(
        functools.partial(_head_norm_body, gsize=gsize),
        grid=(m // tm,),
        in_specs=[pl.BlockSpec((tm, width), lambda i: (i, col0 // width)),
                  pl.BlockSpec((1, width), lambda i: (0, 0))],
        out_specs=[pl.BlockSpec((tm, width), lambda i: (i, 0)),
                   pl.BlockSpec((tm, width), lambda i: (i, 0))],
        out_shape=[jax.ShapeDtypeStruct((m, width), F32), jax.ShapeDtypeStruct((m, width), BF16)],
        compiler_params=_cparams("arbitrary"),
        name=name,
    )(x, gt)


B_QW = H_B * DH_B
B_KW = KV_B * DH_B


def _b_proj_body(x_ref, ng_ref, wm_ref, wt_ref, gq_ref, gk_ref, gki_ref,
                 qn_ref, kn_ref, knb_ref, v_ref, vb_ref, qi_ref, kin_ref, kinb_ref, wi_ref):
    xn = _rms(x_ref[...], ng_ref[...]).astype(BF16)
    pm = _dot(xn, wm_ref[...])
    pt = _dot(xn, wt_ref[...])
    qn_ref[...] = _group_rms(pm[:, 0:B_QW], gq_ref[...], DH_B)
    kn = _group_rms(pm[:, B_QW:B_QW + B_KW], gk_ref[...], DH_B)
    kn_ref[...] = kn
    knb_ref[...] = kn.astype(BF16)
    v = pm[:, B_QW + B_KW:B_QW + 2 * B_KW]
    v_ref[...] = v
    vb_ref[...] = v.astype(BF16)
    qi_ref[...] = pm[:, B_QW + 2 * B_KW:]
    kin = _group_rms(pt[:, 0:D_IDX], gki_ref[...], D_IDX)
    kin_ref[...] = kin
    kinb_ref[...] = kin.astype(BF16)
    wi_ref[...] = pt[:, D_IDX:D_IDX + H_IDX]


def b_proj(x, norm_g, w_in, q_gain, k_gain, ki_gain):
    m, d = x.shape
    tm = _pick(m, (512, 256))
    n_main = B_QW + 2 * B_KW + H_IDX * D_IDX
    w_tail = jnp.pad(w_in[:, n_main:], ((0, 0), (0, V7X_LANES - (w_in.shape[1] - n_main))))
    const = lambda i: (0, 0)
    rowmap = lambda i: (i, 0)
    widths = [(B_QW, F32), (B_KW, F32), (B_KW, BF16), (B_KW, F32), (B_KW, BF16), (H_IDX * D_IDX, F32),
              (D_IDX, F32), (D_IDX, BF16), (H_IDX, F32)]
    return pl.pallas_call(
        _b_proj_body,
        grid=(m // tm,),
        in_specs=[pl.BlockSpec((tm, d), rowmap),
                  pl.BlockSpec((1, d), const),
                  pl.BlockSpec((d, n_main), const),
                  pl.BlockSpec((d, V7X_LANES), const),
                  pl.BlockSpec((1, B_QW), const),
                  pl.BlockSpec((1, B_KW), const),
                  pl.BlockSpec((1, D_IDX), const)],
        out_specs=[pl.BlockSpec((tm, w), rowmap) for w, _ in widths],
        out_shape=[jax.ShapeDtypeStruct((m, w), dt) for w, dt in widths],
        compiler_params=_cparams("arbitrary"),
        name="b_proj",
    )(x, norm_g.reshape(1, d).astype(F32), w_in[:, :n_main].astype(BF16), w_tail.astype(BF16),
      jnp.tile(q_gain.astype(F32), H_B).reshape(1, B_QW), jnp.tile(k_gain.astype(F32), KV_B).reshape(1, B_KW),
      ki_gain.astype(F32).reshape(1, D_IDX))


def _a_pre(tiles):
    a, g = tiles
    return a * _sigmoid(g)


def _a_post(y, tiles, pars):
    bc, lng, lnb = pars
    y = y + bc
    mu = jnp.mean(y, axis=-1, keepdims=True)
    yc = y - mu
    yn = yc * lax.rsqrt(jnp.mean(yc * yc, axis=-1, keepdims=True) + EPS)
    return _silu(yn * lng + lnb)


def _d_pre(tiles):
    return tiles[1] * tiles[2]


def _d_post(y, tiles, pars):
    return tiles[0] * y


def _conv_mixer_body(*refs, n_par, n_parts, pre, post, width, sb, tm, lv, halo, realign):
    x_ref, ng_ref, win_ref, bin_ref, st_ref, wc_ref = refs[:6]
    par_refs = refs[6:6 + n_par]
    wout_ref, o_ref, nst_ref, abuf, rbuf = refs[6 + n_par:]
    t = pl.program_id(1)
    base = halo - (width - 1)
    d = x_ref.shape[-1]
    c = abuf.shape[-1]

    @pl.when(t == 0)
    def _():
        abuf[:, base:halo, :] = st_ref[...]

    x = x_ref[...].reshape(sb * tm, d)
    proj = _dot(_rms(x, ng_ref[...]).astype(BF16), win_ref[...]) + bin_ref[...]
    parts = [proj[:, p * c:(p + 1) * c] for p in range(n_parts)]
    abuf[:, halo:halo + tm, :] = pre(parts).reshape(sb, tm, c)
    acc = None
    for r in range(V7X_SUBLANES):
        taps = [k for k in range(width) if (base + k) % V7X_SUBLANES == r]
        if not taps:
            continue
        rows = V7X_SUBLANES * max((base + k) // V7X_SUBLANES for k in taps) + tm
        for k in taps:
            q0 = V7X_SUBLANES * ((base + k) // V7X_SUBLANES)
            if r == 0 or len(taps) == 1 or not realign:
                shifted = abuf[:, r + q0:r + q0 + tm, :]
            else:
                if k == taps[0]:
                    rbuf[r, :, 0:rows, :] = abuf[:, r:r + rows, :]
                shifted = rbuf[r, :, q0:q0 + tm, :]
            term = wc_ref[k:k + 1, :] * shifted
            acc = term if acc is None else acc + term
    u = post(acc.reshape(sb * tm, c), parts, [p[...] for p in par_refs]).astype(BF16)
    o_ref[...] = (x + _dot(u, wout_ref[...])).reshape(sb, tm, d)
    tail = abuf[:, base + lv:base + lv + width - 1, :]
    abuf[:, base:halo, :] = tail
    nst_ref[...] = tail


def conv_mixer(x, state, norm_g, w_in, b_in, w_conv, params, w_out, pre, post, sb, tm, lv, name):
    n_seq, length, d = x.shape
    width, c = w_conv.shape
    n_parts = w_in.shape[1] // c
    halo = -(-(width - 1) // V7X_SUBLANES) * V7X_SUBLANES
    realign = width > V7X_SUBLANES and tm >= 8 * V7X_SUBLANES
    const = lambda s, t: (0, 0)
    b_in = jnp.zeros((n_parts * c,), F32) if b_in is None else b_in
    in_specs = [pl.BlockSpec((sb, tm, d), lambda s, t: (s, t, 0)),
                pl.BlockSpec((1, d), const),
                pl.BlockSpec((d, n_parts * c), const),
                pl.BlockSpec((1, n_parts * c), const),
                pl.BlockSpec((sb, width - 1, c), lambda s, t: (s, 0, 0)),
                pl.BlockSpec((width, c), const)]
    in_specs += [pl.BlockSpec((1, c), const) for _ in params]
    in_specs.append(pl.BlockSpec((c, d), const))
    return pl.pallas_call(
        functools.partial(_conv_mixer_body, n_par=len(params), n_parts=n_parts, pre=pre, post=post,
                          width=width, sb=sb, tm=tm, lv=lv, halo=halo, realign=realign),
        grid=(n_seq // sb, length // tm),
        in_specs=in_specs,
        out_specs=[pl.BlockSpec((sb, tm, d), lambda s, t: (s, t, 0)),
                   pl.BlockSpec((sb, width - 1, c), lambda s, t: (s, 0, 0))],
        out_shape=[jax.ShapeDtypeStruct((n_seq, length, d), F32),
                   jax.ShapeDtypeStruct((n_seq, width - 1, c), F32)],
        scratch_shapes=[pltpu.VMEM((sb, halo + tm, c), F32),
                        pltpu.VMEM((V7X_SUBLANES, sb, halo + tm, c) if realign
                                   else (1, 1, V7X_SUBLANES, V7X_LANES), F32)],
        compiler_params=_cparams("arbitrary", "arbitrary"),
        name=name,
    )(x, norm_g.reshape(1, d).astype(F32), w_in, b_in.reshape(1, -1).astype(F32), state, w_conv.astype(F32),
      *[p.reshape(1, c).astype(F32) for p in params], w_out)


C_CONV_CHUNK = 1024


def _c_in_body(x_ref, ng_ref, wx_ref, wz_ref, st_ref, wc_ref, bc_ref, xbc_ref, z_ref, nst_ref, abuf, *, sb, tm, lv, halo):
    t = pl.program_id(1)
    base = halo - (W_C - 1)
    d = x_ref.shape[-1]
    conv_c = abuf.shape[-1]

    @pl.when(t == 0)
    def _():
        abuf[:, base:halo, :] = st_ref[...]

    xn = _rms(x_ref[...].reshape(sb * tm, d), ng_ref[...]).astype(BF16)
    for c0 in range(0, conv_c, C_CONV_CHUNK):
        cs = slice(c0, c0 + C_CONV_CHUNK)
        abuf[:, halo:halo + tm, cs] = _dot(xn, wx_ref[:, cs]).reshape(sb, tm, C_CONV_CHUNK)
        y = None
        for k in range(W_C):
            term = wc_ref[k:k + 1, cs] * abuf[:, base + k:base + k + tm, cs]
            y = term if y is None else y + term
        xbc_ref[:, :, cs] = _silu(y + bc_ref[:, cs])
    z_ref[...] = _dot(xn, wz_ref[...]).reshape(sb, tm, z_ref.shape[-1])
    tail = abuf[:, base + lv:base + lv + W_C - 1, :]
    abuf[:, base:halo, :] = tail
    nst_ref[...] = tail


def c_in_conv(x, state, norm_g, w_xbc, w_z, w_conv, b_conv, sb, tm, lv):
    n_seq, length, d = x.shape
    conv_c = w_xbc.shape[1]
    halo = V7X_SUBLANES
    const = lambda s, t: (0, 0)
    tile = lambda w: pl.BlockSpec((sb, tm, w), lambda s, t: (s, t, 0))
    stt = pl.BlockSpec((sb, W_C - 1, conv_c), lambda s, t: (s, 0, 0))
    return pl.pallas_call(
        functools.partial(_c_in_body, sb=sb, tm=tm, lv=lv, halo=halo),
        grid=(n_seq // sb, length // tm),
        in_specs=[tile(d), pl.BlockSpec((1, d), const), pl.BlockSpec((d, conv_c), const),
                  pl.BlockSpec((d, D_IN_C), const), stt, pl.BlockSpec((W_C, conv_c), const),
                  pl.BlockSpec((1, conv_c), const)],
        out_specs=[tile(conv_c), tile(D_IN_C), stt],
        out_shape=[jax.ShapeDtypeStruct((n_seq, length, conv_c), F32),
                   jax.ShapeDtypeStruct((n_seq, length, D_IN_C), F32),
                   jax.ShapeDtypeStruct((n_seq, W_C - 1, conv_c), F32)],
        scratch_shapes=[pltpu.VMEM((sb, halo + tm, conv_c), F32)],
        compiler_params=_cparams("arbitrary", "arbitrary"),
        name="c_in_conv",
    )(x, norm_g.reshape(1, d).astype(F32), w_xbc, w_z, state, w_conv.astype(F32),
      b_conv.reshape(1, conv_c).astype(F32))


FFN_CHUNK = 2 * V7X_LANES


def _ffn_body(x_ref, ng_ref, win_ref, st_ref, wc_ref, bc_ref, wout_ref, o_ref, nst_ref, abuf, *, sb, tm, lv, halo):
    t = pl.program_id(1)
    base = halo - (W_F - 1)
    d = x_ref.shape[-1]

    @pl.when(t == 0)
    def _():
        abuf[:, base:halo, :] = st_ref[...]

    x = x_ref[...].reshape(sb * tm, d)
    xn = _rms(x, ng_ref[...]).astype(BF16)
    out = x
    for c0 in range(0, D_FF, FFN_CHUNK):
        cs = slice(c0, c0 + FFN_CHUNK)
        abuf[:, halo:halo + tm, cs] = _dot(xn, win_ref[:, cs]).reshape(sb, tm, FFN_CHUNK)
        g = _dot(xn, win_ref[:, D_FF + c0:D_FF + c0 + FFN_CHUNK])
        y = None
        for k in range(W_F):
            term = wc_ref[k:k + 1, cs] * abuf[:, base + k:base + k + tm, cs]
            y = term if y is None else y + term
        y = y.reshape(sb * tm, FFN_CHUNK) + bc_ref[:, cs]
        out = out + _dot((_silu(y) * g).astype(BF16), wout_ref[cs, :])
    o_ref[...] = out.reshape(sb, tm, d)
    tail = abuf[:, base + lv:base + lv + W_F - 1, :]
    abuf[:, base:halo, :] = tail
    nst_ref[...] = tail


def ffn_fused(x, state, norm_g, w_in, w_conv, b_conv, w_out, sb, tm, lv):
    n_seq, length, d = x.shape
    halo = V7X_SUBLANES
    const = lambda s, t: (0, 0)
    resident = dict(pipeline_mode=pl.Buffered(1))
    return pl.pallas_call(
        functools.partial(_ffn_body, sb=sb, tm=tm, lv=lv, halo=halo),
        grid=(n_seq // sb, length // tm),
        in_specs=[pl.BlockSpec((sb, tm, d), lambda s, t: (s, t, 0)),
                  pl.BlockSpec((1, d), const),
                  pl.BlockSpec((d, 2 * D_FF), const, **resident),
                  pl.BlockSpec((sb, W_F - 1, D_FF), lambda s, t: (s, 0, 0)),
                  pl.BlockSpec((W_F, D_FF), const),
                  pl.BlockSpec((1, D_FF), const),
                  pl.BlockSpec((D_FF, d), const, **resident)],
        out_specs=[pl.BlockSpec((sb, tm, d), lambda s, t: (s, t, 0)),
                   pl.BlockSpec((sb, W_F - 1, D_FF), lambda s, t: (s, 0, 0))],
        out_shape=[jax.ShapeDtypeStruct((n_seq, length, d), F32),
                   jax.ShapeDtypeStruct((n_seq, W_F - 1, D_FF), F32)],
        scratch_shapes=[pltpu.VMEM((sb, halo + tm, D_FF), F32)],
        compiler_params=_cparams("arbitrary", "arbitrary"),
        name="ffn",
    )(x, norm_g.reshape(1, d).astype(F32), w_in, state, w_conv.astype(F32),
      b_conv.reshape(1, D_FF).astype(F32), w_out)


def _softmax_rows(s):
    p = jnp.exp(s - jnp.max(s, axis=-1, keepdims=True))
    return p / jnp.sum(p, axis=-1, keepdims=True)


def _mem_attend(q, k, v, gain, token_major):
    tm = q.shape[0]
    qh = [(_rms(q[:, h * DH_M:(h + 1) * DH_M], gain) * (DH_M ** -0.5)).astype(BF16) for h in range(H_M)]
    if not token_major:
        outs = []
        for h in range(H_M):
            p = _softmax_rows(_dot_t(qh[h], k[h].astype(BF16)))
            outs.append(_dot(p.astype(BF16), v[h].astype(BF16)))
    else:
        k_all = k.reshape(N_MEM * H_M, DH_M).astype(BF16)
        v_all = v.reshape(N_MEM * H_M, DH_M).astype(BF16)
        s = _dot_t(jnp.concatenate(qh, axis=0), k_all)
        col_head = lax.broadcasted_iota(I32, s.shape, 1) & (H_M - 1)
        row_head = lax.shift_right_logical(lax.broadcasted_iota(I32, s.shape, 0), int(math.log2(tm)))
        p = _softmax_rows(jnp.where(col_head == row_head, s, NEG_BIG))
        o = _dot(p.astype(BF16), v_all)
        outs = [o[h * tm:(h + 1) * tm] for h in range(H_M)]
    return jnp.concatenate(outs, axis=1).astype(BF16)


def _mem_attn_body(q_ref, k_ref, v_ref, g_ref, o_ref, *, sb):
    for s in range(sb):
        o_ref[s] = _mem_attend(q_ref[s], k_ref[s], v_ref[s], g_ref[...], True)


def mem_attn_cached(q, k, v, q_gain, sb, kv_seq0):
    n_seq, tm, d = q.shape
    kv_block = (sb,) + k.shape[1:]
    kv_map = lambda s: (kv_seq0 // sb + s, 0, 0, 0)
    return pl.pallas_call(
        functools.partial(_mem_attn_body, sb=sb),
        grid=(n_seq // sb,),
        in_specs=[pl.BlockSpec((sb, tm, d), lambda s: (s, 0, 0)),
                  pl.BlockSpec(kv_block, kv_map),
                  pl.BlockSpec(kv_block, kv_map),
                  pl.BlockSpec((1, DH_M), lambda s: (0, 0))],
        out_specs=pl.BlockSpec((sb, tm, d), lambda s: (s, 0, 0)),
        out_shape=jax.ShapeDtypeStruct((n_seq, tm, d), BF16),
        compiler_params=_cparams("arbitrary"),
        name="mem_attn",
    )(q, k, v, q_gain.reshape(1, DH_M).astype(F32))


def _mem_fused_body(x_ref, ng_ref, wq_ref, k_ref, v_ref, g_ref, wo_ref, o_ref):
    x = x_ref[0]
    q = _dot(_rms(x, ng_ref[...]).astype(BF16), wq_ref[...])
    o = _mem_attend(q, k_ref[0], v_ref[0], g_ref[...], False)
    o_ref[0] = x + _dot(o, wo_ref[...])


def mem_fused(x, k, v, norm_g, w_q, q_gain, w_o, tm):
    n_seq, length, d = x.shape
    const = lambda s, t: (0, 0)
    kv_block = (1,) + k.shape[1:]
    return pl.pallas_call(
        _mem_fused_body,
        grid=(n_seq, length // tm),
        in_specs=[pl.BlockSpec((1, tm, d), lambda s, t: (s, t, 0)),
                  pl.BlockSpec((1, d), const),
                  pl.BlockSpec((d, d), const),
                  pl.BlockSpec(kv_block, lambda s, t: (s, 0, 0, 0)),
                  pl.BlockSpec(kv_block, lambda s, t: (s, 0, 0, 0)),
                  pl.BlockSpec((1, DH_M), const),
                  pl.BlockSpec((d, d), const)],
        out_specs=pl.BlockSpec((1, tm, d), lambda s, t: (s, t, 0)),
        out_shape=jax.ShapeDtypeStruct((n_seq, length, d), F32),
        compiler_params=_cparams("arbitrary", "arbitrary"),
        name="mem_fused",
    )(x, norm_g.reshape(1, d).astype(F32), w_q, k, v, q_gain.reshape(1, DH_M).astype(F32), w_o)


HEADS_PER_PAIR = 2 * G_B


def _t5_lookup(tab_ref, h, dist):
    dist = jnp.maximum(dist, 0)
    max_exact = N_BUCKETS // 2
    df = jnp.maximum(dist, 1).astype(F32)
    large = max_exact + (jnp.log(df / max_exact) / math.log(MAX_DIST / max_exact)
                         * (N_BUCKETS - max_exact)).astype(I32)
    large = jnp.minimum(large, N_BUCKETS - 1)
    bucket = jnp.where(dist < max_exact, dist, large)
    acc = jnp.zeros(dist.shape, F32)
    for b in range(N_BUCKETS):
        acc = jnp.where(bucket == b, tab_ref[b, h], acc)
    return acc


def _t5_tiles_body(tab_ref, o_ref, ot_ref):
    h = pl.program_id(0)
    r = lax.broadcasted_iota(I32, (PAGE_SIZE, PAGE_SIZE), 0)
    c = lax.broadcasted_iota(I32, (PAGE_SIZE, PAGE_SIZE), 1)
    o_ref[0, :, 0:PAGE_SIZE] = _t5_lookup(tab_ref, h, r - c)
    o_ref[0, :, PAGE_SIZE:2 * PAGE_SIZE] = _t5_lookup(tab_ref, h, PAGE_SIZE + r - c)
    ot_ref[0, 0] = _t5_lookup(tab_ref, h, c - r)
    ot_ref[0, 1] = _t5_lookup(tab_ref, h, PAGE_SIZE + c - r)


def t5_tiles(table):
    return pl.pallas_call(
        _t5_tiles_body,
        grid=(H_B,),
        in_specs=[pl.BlockSpec(memory_space=pltpu.SMEM)],
        out_specs=[pl.BlockSpec((1, PAGE_SIZE, 2 * PAGE_SIZE), lambda h: (h, 0, 0)),
                   pl.BlockSpec((1, 2, PAGE_SIZE, PAGE_SIZE),
                                lambda h: (h // HEADS_PER_PAIR, 0, 0, h % HEADS_PER_PAIR))],
        out_shape=[jax.ShapeDtypeStruct((H_B, PAGE_SIZE, 2 * PAGE_SIZE), F32),
                   jax.ShapeDtypeStruct((H_B // HEADS_PER_PAIR, 2, PAGE_SIZE, HEADS_PER_PAIR * PAGE_SIZE), F32)],
        compiler_params=_cparams("arbitrary"),
        name="t5_tiles",
    )(table.astype(F32))


def _sort_key(score):
    score = jnp.where(score == 0.0, 0.0, score)
    u = lax.bitcast_convert_type(score, I32)
    return jnp.where(u < 0, u ^ 0x7FFFFFFF, u)


def _kth_largest(count_ge, shape, topk):
    def try_cand(cand, cur):
        return jnp.where(count_ge(cand) >= topk, cand, cur)

    t0 = try_cand(jnp.zeros(shape, I32), jnp.full(shape, INT_MIN, I32))

    def step(b, cur):
        cand = cur + lax.shift_left(jnp.int32(1), jnp.int32(30) - b)
        return try_cand(cand, cur)

    return lax.fori_loop(0, 31, step, t0)


def _strict_upper(n):
    r = lax.broadcasted_iota(I32, (n, n), 0)
    c = lax.broadcasted_iota(I32, (n, n), 1)
    return (r < c).astype(BF16)


def _strict_lower(n):
    r = lax.broadcasted_iota(I32, (n, n), 0)
    c = lax.broadcasted_iota(I32, (n, n), 1)
    return (c < r).astype(BF16)


def _tree(parts, op):
    parts = list(parts)
    while len(parts) > 1:
        nxt = [op(parts[k], parts[k + 1]) for k in range(0, len(parts) - 1, 2)]
        if len(parts) % 2:
            nxt.append(parts[-1])
        parts = nxt
    return parts[0]


def _fold_rows(x, op):
    return _tree([x[r:r + V7X_SUBLANES] for r in range(0, x.shape[0], V7X_SUBLANES)], op)


LOOP_UNROLL = 4


def _loop_tiles(n, body, carry, unroll=LOOP_UNROLL):
    shift = int(math.log2(unroll))
    n_main = lax.shift_right_logical(n, shift)
    carry = lax.fori_loop(0, n_main, lambda t, c: body(t * unroll, unroll, c), carry)
    return lax.fori_loop(n_main * unroll, n, lambda j, c: body(j, 1, c), carry)


def _dsa_prompt_body(q_ref, qi_ref, wi_ref, k_ref, vt_ref, ki_ref, bt_ref, o_ref,
                     key_ref, am_ref, lg_ref, acc_ref, ot_ref, *, topk):
    i = pl.program_id(1)
    nj = i + 1
    qb = PAGE_SIZE
    kpos = lax.broadcasted_iota(I32, (qb, qb), 0)
    qpos = lax.broadcasted_iota(I32, (qb, qb), 1)

    def kslice(j0, u=1):
        return pl.ds(pl.multiple_of(j0 * qb, qb), u * qb)

    qit = qi_ref[0].T
    qit = jnp.concatenate([qit[h * D_IDX:(h + 1) * D_IDX] for h in range(H_IDX)], axis=1).astype(BF16)
    wsc = (wi_ref[0] * (H_IDX ** -0.5)) * (D_IDX ** -0.5)
    wt = jnp.concatenate([wsc, jnp.zeros((qb, qb - H_IDX), F32)], axis=1).T
    wrow = jnp.concatenate([wt[h:h + 1] for h in range(H_IDX)], axis=1)

    def idx_step(j0, u, carry):
        s = jnp.maximum(_dot(ki_ref[0, kslice(j0, u), :], qit), 0.0) * wrow
        sc = s[:, 0:qb]
        for h in range(1, H_IDX):
            sc = sc + s[:, h * qb:(h + 1) * qb]
        key = _sort_key(sc)
        for t in range(u):
            valid = (j0 + t < i) | (kpos <= qpos)
            key_ref[j0 + t] = jnp.where(valid, key[t * qb:(t + 1) * qb], INT_MIN)
        return carry

    _loop_tiles(nj, idx_step, 0)

    def count_where(pred):
        def cstep(j0, u, c):
            for t in range(u):
                c = c + jnp.where(pred(key_ref[j0 + t]), 1.0, 0.0)
            return c
        c = _loop_tiles(nj, cstep, jnp.zeros((qb, qb), F32))
        return jnp.sum(_fold_rows(c, jnp.add), axis=0, keepdims=True)

    thr = _kth_largest(lambda cand: count_where(lambda key: key >= cand), (1, qb), topk)
    n_tie = topk - count_where(lambda key: key > thr)
    n_eq = count_where(lambda key: (key == thr) & (key != INT_MIN))
    need_rank = jnp.max(n_eq - n_tie) > 0.0

    @pl.when(jnp.logical_not(need_rank))
    def _():
        def mask_step(j0, u, carry):
            for t in range(u):
                key = key_ref[j0 + t]
                am_ref[j0 + t] = jnp.where((key >= thr) & (key != INT_MIN), 0.0, NEG_BIG)
            return carry
        _loop_tiles(nj, mask_step, 0)

    @pl.when(need_rank)
    def _():
        lower = _strict_lower(qb)

        def mask_step(j, run):
            key = key_ref[j]
            eq = (key == thr) & (key != INT_MIN)
            eqf = jnp.where(eq, 1.0, 0.0)
            rank = _dot(lower, eqf.astype(BF16)) + run
            sel = (key > thr) | (eq & (rank < n_tie))
            am_ref[j] = jnp.where(sel, 0.0, NEG_BIG)
            return run + jnp.sum(_fold_rows(eqf, jnp.add), axis=0, keepdims=True)

        lax.fori_loop(0, nj, mask_step, jnp.zeros((1, qb), F32))

    qt = q_ref[0].T
    pw = HEADS_PER_PAIR * qb
    gw = G_B * qb
    for pp in range(KV_B // 2):
        def group_qt(n):
            return jnp.concatenate([qt[h * DH_B:(h + 1) * DH_B] for h in range(n * G_B, (n + 1) * G_B)],
                                   axis=1)
        zero = jnp.zeros((DH_B, gw), F32)
        rhs = jnp.concatenate([jnp.concatenate([group_qt(2 * pp), zero], axis=1),
                               jnp.concatenate([zero, group_qt(2 * pp + 1)], axis=1)], axis=0)
        rhs = (rhs * (DH_B ** -0.5)).astype(BF16)
        klanes = slice(pp * 2 * DH_B, (pp + 1) * 2 * DH_B)
        far = bt_ref[pp, 1, 0:1, :]

        def tile_logits(j0, u, bias, mx):
            s = _dot(k_ref[0, kslice(j0, u), klanes], rhs)
            for t in range(u):
                st = s[t * qb:(t + 1) * qb] + bias + jnp.concatenate([am_ref[j0 + t]] * HEADS_PER_PAIR, axis=1)
                lg_ref[j0 + t] = st
                mx = jnp.maximum(mx, _fold_rows(st, jnp.maximum))
            return mx

        neg = jnp.full((V7X_SUBLANES, pw), NEG_BIG, F32)
        mx = _loop_tiles(jnp.maximum(i - 1, 0), lambda j0, u, m: tile_logits(j0, u, far, m), neg)
        mx = lax.cond(i >= 1, lambda m: tile_logits(i - 1, 1, bt_ref[pp, 1], m), lambda m: m, mx)
        mx = tile_logits(i, 1, bt_ref[pp, 0], mx)
        m = jnp.max(mx, axis=0, keepdims=True)
        acc_ref[...] = jnp.zeros(acc_ref.shape, F32)

        def p2(j0, u, l):
            ps = []
            for t in range(u):
                p = jnp.exp(lg_ref[j0 + t] - m)
                l = l + _fold_rows(p, jnp.add)
                ps.append(p.astype(BF16))
            pb = jnp.concatenate(ps, axis=0)
            vt = jnp.concatenate([vt_ref[0, j0 + t] for t in range(u)], axis=1)
            for gg in range(2):
                n = 2 * pp + gg
                acc_ref[gg] += _dot(vt[n * DH_B:(n + 1) * DH_B, :], pb[:, gg * gw:(gg + 1) * gw])
            return l

        l = _loop_tiles(nj, p2, jnp.zeros((V7X_SUBLANES, pw), F32))
        lsum = jnp.sum(l, axis=0, keepdims=True)
        for gg in range(2):
            o_t = acc_ref[gg] / lsum[:, gg * gw:(gg + 1) * gw]
            for hh in range(G_B):
                h = (2 * pp + gg) * G_B + hh
                ot_ref[h * DH_B:(h + 1) * DH_B, :] = o_t[:, hh * qb:(hh + 1) * qb]
    o_ref[0] = ot_ref[...].T.astype(o_ref.dtype)


def dsa_prompt(qn, qi, wi, kb, vtb, kib, btt):
    n_seq, length, _ = qn.shape
    nkb = length // PAGE_SIZE
    qmap = lambda s, i: (s, i, 0)
    kmap = lambda s, i: (s, 0, 0)
    pw = HEADS_PER_PAIR * PAGE_SIZE
    return pl.pallas_call(
        functools.partial(_dsa_prompt_body, topk=min(TOPK, length // 4)),
        grid=(n_seq, nkb),
        in_specs=[pl.BlockSpec((1, PAGE_SIZE, H_B * DH_B), qmap),
                  pl.BlockSpec((1, PAGE_SIZE, H_IDX * D_IDX), qmap),
                  pl.BlockSpec((1, PAGE_SIZE, H_IDX), qmap),
                  pl.BlockSpec((1, length, KV_B * DH_B), kmap),
                  pl.BlockSpec((1, nkb, KV_B * DH_B, PAGE_SIZE), lambda s, i: (s, 0, 0, 0)),
                  pl.BlockSpec((1, length, D_IDX), kmap),
                  pl.BlockSpec((H_B // HEADS_PER_PAIR, 2, PAGE_SIZE, pw), lambda s, i: (0, 0, 0, 0))],
        out_specs=pl.BlockSpec((1, PAGE_SIZE, H_B * DH_B), qmap),
        out_shape=jax.ShapeDtypeStruct((n_seq, length, H_B * DH_B), BF16),
        scratch_shapes=[pltpu.VMEM((nkb, PAGE_SIZE, PAGE_SIZE), I32),
                        pltpu.VMEM((nkb, PAGE_SIZE, PAGE_SIZE), F32),
                        pltpu.VMEM((nkb, PAGE_SIZE, pw), F32),
                        pltpu.VMEM((2, DH_B, G_B * PAGE_SIZE), F32),
                        pltpu.VMEM((H_B * DH_B, PAGE_SIZE), F32)],
        compiler_params=_cparams("arbitrary", "arbitrary"),
        name="dsa_prompt",
    )(qn, qi, wi, kb, vtb, kib, btt)


N_PAGES = PAST_LEN // PAGE_SIZE
N_KTILES = N_PAGES + 1
N_KTILES_PAD = -(-N_KTILES // PAGES_PER_STEP) * PAGES_PER_STEP


def _dsa_sample_sel_body(pt_ref, qi_ref, wi_ref, kin_ref, *rest, lv, topk):
    page_refs = rest[:PAGES_PER_STEP]
    am_ref, key_ref, qs_ref, ws_ref = rest[PAGES_PER_STEP:]
    c = pl.program_id(1)
    nsteps = pl.num_programs(1)
    r8 = SAMPLE_PAD

    @pl.when(c == 0)
    def _():
        qi = qi_ref[0]
        wsc = (wi_ref[0] * (H_IDX ** -0.5)) * (D_IDX ** -0.5)
        qs_ref[...] = jnp.concatenate([qi[:, h * D_IDX:(h + 1) * D_IDX] for h in range(H_IDX)],
                                      axis=0).astype(BF16)
        ws_ref[...] = jnp.concatenate([jnp.broadcast_to(wsc[:, h:h + 1], (r8, PAGE_SIZE))
                                       for h in range(H_IDX)], axis=0)

    def score(ki_t):
        s = jnp.maximum(_dot(qs_ref[...], ki_t), 0.0) * ws_ref[...]
        return _fold_rows(s, jnp.add)

    for r in range(PAGES_PER_STEP):
        key_ref[c * PAGES_PER_STEP + r] = _sort_key(score(page_refs[r][0].astype(BF16)))

    @pl.when(c == nsteps - 1)
    def _():
        rows = lax.broadcasted_iota(I32, (r8, PAGE_SIZE), 0)
        cols = lax.broadcasted_iota(I32, (r8, PAGE_SIZE), 1)
        new_valid = (cols <= rows) & (cols < lv)
        kin = jnp.concatenate([kin_ref[0], jnp.zeros((PAGE_SIZE - r8, D_IDX), F32)], axis=0)
        kin_t = jnp.concatenate([kin, jnp.zeros((PAGE_SIZE, PAGE_SIZE - D_IDX), F32)], axis=1).T[0:D_IDX]
        key_ref[N_PAGES] = jnp.where(new_valid, _sort_key(score(kin_t.astype(BF16))), INT_MIN)

        def tile_valid(j):
            return new_valid if j == N_PAGES else None

        def count_where(pred):
            parts = []
            for j in range(N_KTILES):
                hit = pred(key_ref[j])
                if tile_valid(j) is not None:
                    hit = hit & tile_valid(j)
                parts.append(jnp.where(hit, 1.0, 0.0))
            return jnp.sum(_tree(parts, jnp.add), axis=-1, keepdims=True)

        thr = _kth_largest(lambda cand: count_where(lambda key: key >= cand), (r8, 1), topk)
        n_tie = topk - count_where(lambda key: key > thr)
        n_eq = count_where(lambda key: key == thr)
        need_rank = jnp.max(n_eq - n_tie) > 0.0

        @pl.when(jnp.logical_not(need_rank))
        def _():
            for j in range(N_KTILES):
                sel = key_ref[j] >= thr
                if tile_valid(j) is not None:
                    sel = sel & tile_valid(j)
                am_ref[0, j] = jnp.where(sel, 0.0, NEG_BIG)

        @pl.when(need_rank)
        def _():
            upper = _strict_upper(PAGE_SIZE)

            def mask_step(j, run):
                key = key_ref[j]
                valid = (j < N_PAGES) | new_valid
                eq = valid & (key == thr)
                eqf = jnp.where(eq, 1.0, 0.0)
                rank = _dot(eqf.astype(BF16), upper) + run
                sel = (key > thr) | (eq & (rank < n_tie))
                am_ref[0, j] = jnp.where(sel, 0.0, NEG_BIG)
                return run + jnp.sum(eqf, axis=-1, keepdims=True)

            lax.fori_loop(0, N_KTILES, mask_step, jnp.zeros((r8, 1), F32))

        for j in range(N_KTILES, N_KTILES_PAD):
            am_ref[0, j] = jnp.full((r8, PAGE_SIZE), NEG_BIG, F32)


def dsa_sample_select(page_table, qi, wi, kin, kidx_pool_t, lv):
    n_seq = qi.shape[0]
    r8 = SAMPLE_PAD
    qmap = lambda s, c, pt: (s, 0, 0)
    page_specs = [pl.BlockSpec((1, D_IDX, PAGE_SIZE),
                               functools.partial(lambda s, c, pt, r: (pt[s, c * PAGES_PER_STEP + r], 0, 0), r=r))
                  for r in range(PAGES_PER_STEP)]
    grid_spec = pltpu.PrefetchScalarGridSpec(
        num_scalar_prefetch=1,
        grid=(n_seq, N_PAGES // PAGES_PER_STEP),
        in_specs=[pl.BlockSpec((1, r8, H_IDX * D_IDX), qmap),
                  pl.BlockSpec((1, r8, H_IDX), qmap),
                  pl.BlockSpec((1, r8, D_IDX), qmap)] + page_specs,
        out_specs=pl.BlockSpec((1, N_KTILES_PAD, r8, PAGE_SIZE), lambda s, c, pt: (s, 0, 0, 0)),
        scratch_shapes=[pltpu.VMEM((N_KTILES, r8, PAGE_SIZE), I32),
                        pltpu.VMEM((H_IDX * r8, D_IDX), BF16),
                        pltpu.VMEM((H_IDX * r8, PAGE_SIZE), F32)],
    )
    return pl.pallas_call(
        functools.partial(_dsa_sample_sel_body, lv=lv, topk=min(TOPK, (PAST_LEN + lv) // 4)),
        grid_spec=grid_spec,
        out_shape=jax.ShapeDtypeStruct((n_seq, N_KTILES_PAD, r8, PAGE_SIZE), F32),
        compiler_params=_cparams("arbitrary", "arbitrary"),
        name="dsa_sample_select",
    )(page_table, qi, wi, kin, *([kidx_pool_t] * PAGES_PER_STEP))


def _dsa_sample_attn_body(pt_ref, q_ref, kn_ref, vn_ref, am_ref, amn_ref, bt_ref, *rest):
    kp_refs = rest[:PAGES_PER_STEP]
    vp_refs = rest[PAGES_PER_STEP:2 * PAGES_PER_STEP]
    o_ref, m_ref, l_ref, acc_ref = rest[2 * PAGES_PER_STEP:]
    c = pl.program_id(1)
    nsteps = pl.num_programs(1)
    r8 = SAMPLE_PAD
    gr = G_B * r8

    @pl.when(c == 0)
    def _():
        m_ref[...] = jnp.full(m_ref.shape, NEG_BIG, F32)
        l_ref[...] = jnp.zeros(l_ref.shape, F32)
        acc_ref[...] = jnp.zeros(acc_ref.shape, F32)

    q = q_ref[0]
    qgs, b_subs, b_diags, b_fars = [], [], [], []
    for n in range(KV_B):
        heads = [n * G_B + hh for hh in range(G_B)]
        qg = jnp.concatenate([q[:, h * DH_B:(h + 1) * DH_B] for h in heads], axis=0)
        qgs.append((qg * (DH_B ** -0.5)).astype(BF16))
        b_diags.append(jnp.concatenate([bt_ref[h, :, 0:PAGE_SIZE] for h in heads], axis=0))
        b_subs.append(jnp.concatenate([bt_ref[h, :, PAGE_SIZE:2 * PAGE_SIZE] for h in heads], axis=0))
        b_fars.append(jnp.concatenate(
            [jnp.broadcast_to(bt_ref[h, 0:1, PAGE_SIZE:PAGE_SIZE + 1], (r8, PAGE_SIZE))
             for h in heads], axis=0))

    def update(n, s, pv):
        m_old = m_ref[n]
        m_new = jnp.maximum(m_old, jnp.max(s, axis=-1, keepdims=True))
        alpha = jnp.exp(m_old - m_new)
        p = jnp.exp(s - m_new[:, 0:1])
        l_ref[n] = alpha * l_ref[n] + jnp.sum(p, axis=-1, keepdims=True)
        acc_ref[n] = alpha[:, 0:DH_B] * acc_ref[n] + pv(p.astype(BF16))
        m_ref[n] = m_new

    def group_t(refs, n):
        return jnp.concatenate([ref[0, n] for ref in refs], axis=1).astype(BF16)

    amcat = jnp.concatenate([am_ref[0, r] for r in range(PAGES_PER_STEP)], axis=1)
    am4 = jnp.concatenate([amcat] * G_B, axis=0)
    is_last_step = c == nsteps - 1
    for n in range(KV_B):
        bias = jnp.concatenate([b_fars[n]] * (PAGES_PER_STEP - 1)
                               + [jnp.where(is_last_step, b_subs[n], b_fars[n])], axis=1)
        s = _dot(qgs[n], group_t(kp_refs, n)) + bias + am4
        v_t = group_t(vp_refs, n)
        update(n, s, lambda p: _dot_t(p, v_t))

    @pl.when(c == nsteps - 1)
    def _():
        pad = jnp.zeros((PAGE_SIZE - r8, KV_B * DH_B), F32)
        kn = jnp.concatenate([kn_ref[0], pad], axis=0).astype(BF16)
        vn = jnp.concatenate([vn_ref[0], pad], axis=0).astype(BF16)
        am4 = jnp.concatenate([amn_ref[0, 0]] * G_B, axis=0)
        for n in range(KV_B):
            ksl = slice(n * DH_B, (n + 1) * DH_B)
            s = _dot_t(qgs[n], kn[:, ksl]) + b_diags[n] + am4
            update(n, s, lambda p: _dot(p, vn[:, ksl]))
            o = acc_ref[n] / l_ref[n][:, 0:DH_B]
            for hh in range(G_B):
                h = n * G_B + hh
                o_ref[0, :, h * DH_B:(h + 1) * DH_B] = o[hh * r8:(hh + 1) * r8].astype(o_ref.dtype)


def dsa_sample_attend(page_table, qn, kn, vn, amask, bt, k_pool, v_pool):
    n_seq = qn.shape[0]
    r8 = SAMPLE_PAD
    qmap = lambda s, c, pt: (s, 0, 0)
    page_map = [functools.partial(lambda s, c, pt, r: (pt[s, c * PAGES_PER_STEP + r], 0, 0, 0), r=r)
                for r in range(PAGES_PER_STEP)]
    kv_w = KV_B * DH_B
    page_block = (1, KV_B, DH_B, PAGE_SIZE)
    grid_spec = pltpu.PrefetchScalarGridSpec(
        num_scalar_prefetch=1,
        grid=(n_seq, N_PAGES // PAGES_PER_STEP),
        in_specs=[pl.BlockSpec((1, r8, H_B * DH_B), qmap),
                  pl.BlockSpec((1, r8, kv_w), qmap),
                  pl.BlockSpec((1, r8, kv_w), qmap),
                  pl.BlockSpec((1, PAGES_PER_STEP, r8, PAGE_SIZE), lambda s, c, pt: (s, c, 0, 0)),
                  pl.BlockSpec((1, PAGES_PER_STEP, r8, PAGE_SIZE),
                               lambda s, c, pt: (s, N_PAGES // PAGES_PER_STEP, 0, 0)),
                  pl.BlockSpec((H_B, r8, 2 * PAGE_SIZE), lambda s, c, pt: (0, 0, 0))]
                 + [pl.BlockSpec(page_block, m) for m in page_map]
                 + [pl.BlockSpec(page_block, m) for m in page_map],
        out_specs=pl.BlockSpec((1, r8, H_B * DH_B), qmap),
        scratch_shapes=[pltpu.VMEM((KV_B, G_B * r8, PAGE_SIZE), F32),
                        pltpu.VMEM((KV_B, G_B * r8, PAGE_SIZE), F32),
                        pltpu.VMEM((KV_B, G_B * r8, DH_B), F32)],
    )
    return pl.pallas_call(
        _dsa_sample_attn_body,
        grid_spec=grid_spec,
        out_shape=jax.ShapeDtypeStruct((n_seq, r8, H_B * DH_B), BF16),
        compiler_params=_cparams("arbitrary", "arbitrary"),
        name="dsa_sample_attend",
    )(page_table, qn, kn, vn, amask, amask, bt,
      *([k_pool] * PAGES_PER_STEP), *([v_pool] * PAGES_PER_STEP))


HEADS_PER_GROUP = H_C // G_C
GROUP_W = D_IN_C // G_C


def _softplus(x):
    return jnp.maximum(x, 0.0) + jnp.log1p(jnp.exp(-jnp.abs(x)))


def _split3(x):
    hi = x.astype(BF16)
    r = x - hi.astype(F32)
    mid = r.astype(BF16)
    lo = (r - mid.astype(F32)).astype(BF16)
    return hi, mid, lo


def _ssd_body(x_ref, xbc_ref, zlo_ref, zhi_ref, ng_ref, wdt_ref, wdtt_ref, dtb_ref, dtbt_ref,
              alog_ref, alogt_ref, dsk_ref, cn_ref, wo_ref, h0_ref, o_ref, hst_ref, *, qin, lv):
    c = pl.program_id(1)
    q = CHUNK_C
    rows = lax.broadcasted_iota(I32, (q, q), 0)
    cols = lax.broadcasted_iota(I32, (q, q), 1)
    tril = cols <= rows

    def pad_rows(a):
        if qin == q:
            return a
        return jnp.concatenate([a, jnp.zeros((q - qin, a.shape[1]), a.dtype)], axis=0)

    @pl.when(c == 0)
    def _():
        hst_ref[...] = h0_ref[...]

    x = x_ref[0]
    xn = _rms(pad_rows(x), ng_ref[...]).astype(BF16)
    dt_all = _softplus(_dot(xn, wdt_ref[...]) + dtb_ref[...])
    dtt_all = _softplus(_dot_t(wdtt_ref[...], xn) + dtbt_ref[...])
    if lv < q:
        dt_all = jnp.where(lax.broadcasted_iota(I32, dt_all.shape, 0) < lv, dt_all, 0.0)
        dtt_all = jnp.where(lax.broadcasted_iota(I32, dtt_all.shape, 1) < lv, dtt_all, 0.0)
    acs_all = _tree([_dot(tril.astype(BF16), p) for p in _split3(dt_all * (-jnp.exp(alog_ref[...])))], jnp.add)
    acst_all = _tree([_dot(p, (rows <= cols).astype(BF16))
                      for p in _split3(dtt_all * (-jnp.exp(alogt_ref[...])))], jnp.add)
    z_all = pad_rows(jnp.concatenate([zlo_ref[0], zhi_ref[0]], axis=1))
    y_groups = []
    for g in range(G_C):
        y_groups.append(_ssd_group(g, xbc_ref, hst_ref, dt_all, dtt_all, acs_all, acst_all, z_all,
                                   dsk_ref, cn_ref, tril, pad_rows))
    y = jnp.concatenate(y_groups, axis=1)
    o_ref[0] = x + _dot(y, wo_ref[...])[0:qin]


def _ssd_group(g, xbc_ref, hst_ref, dt_all, dtt_all, acs_all, acst_all, z_all, dsk_ref, cn_ref, tril, pad_rows):
    q = CHUNK_C
    hpg = HEADS_PER_GROUP
    gheads = slice(g * hpg, (g + 1) * hpg)
    gcols = slice(g * GROUP_W, (g + 1) * GROUP_W)
    dt = dt_all[:, gheads]
    dtt = dtt_all[gheads, :]
    acs = acs_all[:, gheads]
    acst = acst_all[gheads, :]
    xs = pad_rows(xbc_ref[0, :, gcols])
    bm = pad_rows(xbc_ref[0, :, D_IN_C + g * N_SSM:D_IN_C + (g + 1) * N_SSM])
    cm = pad_rows(xbc_ref[0, :, D_IN_C + (G_C + g) * N_SSM:D_IN_C + (G_C + g + 1) * N_SSM])
    xst = xs.T
    cmb = cm.astype(BF16)
    cb = _dot_t(cmb, bm.astype(BF16))

    def spread(v, width):
        src = lax.broadcasted_iota(I32, (hpg, hpg * width), 0)
        dst = lax.shift_right_logical(lax.broadcasted_iota(I32, (hpg, hpg * width), 1), int(math.log2(width)))
        sel = (src == dst).astype(BF16)
        return _tree([_dot(p, sel) for p in _split3(v)], jnp.add)

    acs_w = spread(acs, q)
    decay_w = jnp.exp(acs_w[q - 1:q, :] - acs_w)
    eacs_w = jnp.exp(acs_w)
    xdt = (xs * spread(dt, P_C)).astype(BF16)
    low_half = lax.broadcasted_iota(I32, (q, 2 * P_C), 1) < P_C
    h_all = hst_ref[0, gheads]
    ys, h_new = [], []
    for k in range(hpg // 2):
        xpair = xdt[:, k * 2 * P_C:(k + 1) * 2 * P_C]
        yd = []
        for j in (2 * k, 2 * k + 1):
            blk = slice(j * q, (j + 1) * q)
            acs_row = acst[j:j + 1, :]
            a_last = acs_row[:, q - 1:q]
            lmat = jnp.exp(jnp.where(tril, acs_w[:, blk] - acs_row, -jnp.inf))
            yd.append(_dot((cb * lmat).astype(BF16), xpair))
            xht = xst[j * P_C:(j + 1) * P_C, :] * dtt[j:j + 1, :]
            st = _dot(xht.astype(BF16), (bm * decay_w[:, blk]).astype(BF16))
            h_new.append(jnp.exp(a_last) * h_all[j] + st)
        h_pair = h_all[2 * k:2 * k + 2].reshape(2 * P_C, N_SSM).astype(BF16)
        e_pair = jnp.where(low_half, eacs_w[:, 2 * k * q:(2 * k + 1) * q], eacs_w[:, (2 * k + 1) * q:(2 * k + 2) * q])
        ys.append(jnp.where(low_half, yd[0], yd[1]) + _dot_t(cmb, h_pair) * e_pair)
    hst_ref[0, gheads] = jnp.stack(h_new, axis=0)
    y = jnp.concatenate(ys, axis=1) + xs * dsk_ref[:, gcols]
    y = y * _silu(z_all[:, gcols])
    return _rms(y, cn_ref[:, gcols]).astype(BF16)


def ssd_mix(x, xbc, proj, z_col0, h0, norm_g, w_dt, dt_bias, a_log, d_skip, c_norm, w_out, qin, lv):
    n_seq, length, d = x.shape
    nchunk = length // qin
    col = lambda a: a.reshape(-1, 1).astype(F32)
    row = lambda a: a.reshape(1, -1).astype(F32)
    full2 = lambda s, c: (0, 0)
    conv_c = xbc.shape[-1]
    zw = D_IN_C // 2
    assert z_col0 % zw == 0
    in_specs = [
        pl.BlockSpec((1, qin, d), lambda s, c: (s, c, 0)),
        pl.BlockSpec((1, qin, conv_c), lambda s, c: (s, c, 0)),
        pl.BlockSpec((1, qin, zw), lambda s, c: (s, c, z_col0 // zw)),
        pl.BlockSpec((1, qin, zw), lambda s, c: (s, c, z_col0 // zw + 1)),
        pl.BlockSpec((1, d), full2),
        pl.BlockSpec((d, H_C), full2),
        pl.BlockSpec((H_C, d), full2),
        pl.BlockSpec((1, H_C), full2),
        pl.BlockSpec((H_C, 1), full2),
        pl.BlockSpec((1, H_C), full2),
        pl.BlockSpec((H_C, 1), full2),
        pl.BlockSpec((1, D_IN_C), full2),
        pl.BlockSpec((1, D_IN_C), full2),
        pl.BlockSpec((D_IN_C, d), full2),
        pl.BlockSpec((1, H_C, P_C, N_SSM), lambda s, c: (s, 0, 0, 0)),
    ]
    out, hst = pl.pallas_call(
        functools.partial(_ssd_body, qin=qin, lv=lv),
        grid=(n_seq, nchunk),
        in_specs=in_specs,
        out_specs=[pl.BlockSpec((1, qin, d), lambda s, c: (s, c, 0)),
                   pl.BlockSpec((1, H_C, P_C, N_SSM), lambda s, c: (s, 0, 0, 0))],
        out_shape=[jax.ShapeDtypeStruct((n_seq, length, d), F32),
                   jax.ShapeDtypeStruct((n_seq, H_C, P_C, N_SSM), F32)],
        compiler_params=_cparams("arbitrary", "arbitrary"),
        name="ssd_mix",
    )(x, xbc, proj, proj, row(norm_g), w_dt.astype(BF16), w_dt.T.astype(BF16),
      row(dt_bias), col(dt_bias), row(a_log), col(a_log),
      row(jnp.repeat(d_skip, P_C)), row(c_norm), w_out, h0)
    return out, hst


def _flat(a):
    return a.reshape(-1, a.shape[-1])


def _unflat(a, like):
    return a.reshape(like.shape[0], like.shape[1], a.shape[-1])


def kernel(x_prompt, x_sample, mem_prompt, state_a_conv, cache_b_k, cache_b_v, cache_b_kidx, state_c_conv, state_c_ssm, state_d_conv, state_ffn_conv, cache_mem_k, cache_mem_v, page_table, rel_bias, norm_mix, norm_mem, norm_ffn, norm_memtok, a_w_in, a_b_in, a_w_conv, a_b_conv, a_ln_g, a_ln_b, a_w_out, b_w_in, b_w_out, b_q_norm, b_k_norm, b_kidx_norm, c_w_in, c_w_conv, c_b_conv, c_dt_bias, c_a_log, c_d_skip, c_norm, c_w_out, d_w_in, d_w_conv, d_w_out, m_w_q, m_w_kv, m_w_o, m_q_norm, m_k_norm, f_w_in, f_w_conv, f_b_conv, f_w_out):
    n_p, l_p, d = x_prompt.shape
    n_s, l_s, _ = x_sample.shape
    bf = lambda w: w.astype(BF16)
    xs_pad = jnp.pad(x_sample, ((0, 0), (0, SAMPLE_PAD - l_s), (0, 0)))
    groups = [dict(x=x_prompt, sb=1, tm=256, tmf=512, lv=256, qin=CHUNK_C, lvq=CHUNK_C, prompt=True),
              dict(x=xs_pad, sb=n_s, tm=SAMPLE_PAD, tmf=SAMPLE_PAD, lv=l_s, qin=SAMPLE_PAD, lvq=l_s, prompt=False)]
    bt, btt = t5_tiles(rel_bias)
    mem_flat = _flat(mem_prompt)
    outs = {k: [[], []] for k in ("a", "bk", "bv", "bki", "cc", "cs", "d", "f")}
    m_kp, m_vp = [], []

    for i in range(DEPTH):
        j = i // 4
        kind = i % 4
        kv = linear(mem_flat, bf(m_w_kv[i]), g=norm_memtok[i], name="mem_kv")
        mk, _ = head_norm(kv, 0, d, m_k_norm[i], DH_M, name="mem_k_norm")
        m_kp.append(mk.reshape(n_p, N_MEM, H_M, DH_M))
        m_vp.append(kv[:, d:].reshape(n_p, N_MEM, H_M, DH_M))
        mk = m_kp[-1].transpose(0, 2, 1, 3)
        mv = m_vp[-1].transpose(0, 2, 1, 3)

        for gi, grp in enumerate(groups):
            x = grp["x"]
            n_seq, length, _ = x.shape
            prompt = grp["prompt"]
            zeros_state = lambda w, c: jnp.zeros((n_seq, w - 1, c), F32)
            xf = _flat(x)
            if kind == 0:
                st = zeros_state(W_A, d) if prompt else state_a_conv[j]
                x, nst = conv_mixer(x, st, norm_mix[i], bf(a_w_in[j]), a_b_in[j], a_w_conv[j],
                                    [a_b_conv[j], a_ln_g[j], a_ln_b[j]], bf(a_w_out[j]), _a_pre, _a_post,
                                    grp["sb"], grp["tm"], grp["lv"], "a_mixer")
                outs["a"][gi].append(nst)
            elif kind == 1:
                qn, kn, knb, v, vb, qi, kin, kinb, wi = b_proj(xf, norm_mix[i], b_w_in[j], b_q_norm[j],
                                                               b_k_norm[j], b_kidx_norm[j])
                kw = KV_B * DH_B
                r3 = lambda a: a.reshape(n_seq, length, a.shape[-1])
                if prompt:
                    vtb = vb.reshape(n_seq, length // PAGE_SIZE, PAGE_SIZE, kw).transpose(0, 1, 3, 2)
                    o = dsa_prompt(r3(qn), r3(qi), r3(wi), r3(knb), vtb, r3(kinb), btt)
                else:
                    am = dsa_sample_select(page_table, r3(qi), r3(wi), r3(kin),
                                           cache_b_kidx[j].transpose(0, 2, 1), grp["lv"])
                    o = dsa_sample_attend(page_table, r3(qn), r3(kn), r3(v), am, bt,
                                          cache_b_k[j].transpose(0, 2, 3, 1), cache_b_v[j].transpose(0, 2, 3, 1))
                outs["bk"][gi].append(r3(kn))
                outs["bv"][gi].append(r3(v))
                outs["bki"][gi].append(r3(kin))
                x = _unflat(linear(_flat(o), bf(b_w_out[j]), res=xf, name="b_out"), x)
            elif kind == 2:
                w_in = c_w_in[j]
                conv_c = D_IN_C + 2 * G_C * N_SSM
                st = zeros_state(W_C, conv_c) if prompt else state_c_conv[j]
                xbc, z, nst = c_in_conv(x, st, norm_mix[i], bf(w_in[:, D_IN_C:D_IN_C + conv_c]), bf(w_in[:, :D_IN_C]),
                                        c_w_conv[j], c_b_conv[j], grp["sb"], grp["tm"], grp["lv"])
                h0 = jnp.zeros((n_seq, H_C, P_C, N_SSM), F32) if prompt else state_c_ssm[j]
                x, hst = ssd_mix(x, xbc, z, 0, h0, norm_mix[i], w_in[:, D_IN_C + conv_c:],
                                 c_dt_bias[j], c_a_log[j], c_d_skip[j], c_norm[j], bf(c_w_out[j]),
                                 grp["qin"], grp["lvq"])
                outs["cc"][gi].append(nst)
                outs["cs"][gi].append(hst)
            else:
                st = zeros_state(W_D, d) if prompt else state_d_conv[j]
                x, nst = conv_mixer(x, st, norm_mix[i], bf(d_w_in[j]), None, d_w_conv[j], [], bf(d_w_out[j]),
                                    _d_pre, _d_post, grp["sb"], grp["tm"], grp["lv"], "d_mixer")
                outs["d"][gi].append(nst)

            if prompt:
                x = mem_fused(x, mk, mv, norm_mem[i], bf(m_w_q[i]), m_q_norm[i], bf(m_w_o[i]), 512)
            else:
                xf = _flat(x)
                qm = _unflat(linear(xf, bf(m_w_q[i]), g=norm_mem[i], name="mem_q"), x)
                om = mem_attn_cached(qm, cache_mem_k.reshape(DEPTH * n_seq, N_MEM, H_M, DH_M),
                                     cache_mem_v.reshape(DEPTH * n_seq, N_MEM, H_M, DH_M),
                                     m_q_norm[i], MEM_SEQS_PER_STEP, i * n_seq)
                x = _unflat(linear(_flat(om), bf(m_w_o[i]), res=xf, name="mem_o"), x)

            st = zeros_state(W_F, D_FF) if prompt else state_ffn_conv[i]
            x, nst = ffn_fused(x, st, norm_ffn[i], bf(f_w_in[i]), f_w_conv[i], f_b_conv[i], bf(f_w_out[i]),
                               1 if prompt else n_seq, grp["tmf"], grp["tmf"] if prompt else grp["lv"])
            outs["f"][gi].append(nst)
            grp["x"] = x

    yp = groups[0]["x"]
    ys = groups[1]["x"][:, :l_s]
    st = lambda key, gi: jnp.stack(outs[key][gi])
    kvshape = lambda a, n, l: a.reshape(a.shape[0], n, -1, a.shape[-1])[:, :, :l]
    b_k_p = kvshape(st("bk", 0), n_p, l_p).reshape(-1, n_p, l_p, KV_B, DH_B)
    b_v_p = kvshape(st("bv", 0), n_p, l_p).reshape(-1, n_p, l_p, KV_B, DH_B)
    b_ki_p = kvshape(st("bki", 0), n_p, l_p)
    b_k_s = kvshape(st("bk", 1), n_s, l_s).reshape(-1, n_s, l_s, KV_B, DH_B)
    b_v_s = kvshape(st("bv", 1), n_s, l_s).reshape(-1, n_s, l_s, KV_B, DH_B)
    b_ki_s = kvshape(st("bki", 1), n_s, l_s)
    return (yp, ys, st("a", 0), st("a", 1), b_k_p, b_v_p, b_ki_p, b_k_s, b_v_s, b_ki_s,
            st("cc", 0), st("cs", 0), st("cc", 1), st("cs", 1), st("d", 0), st("d", 1),
            st("f", 0), st("f", 1), jnp.stack(m_kp), jnp.stack(m_vp))
```

```python
import functools
import math

import jax
import jax.numpy as jnp
from jax import lax
from jax.experimental import pallas as pl
from jax.experimental.pallas import tpu as pltpu

F32 = jnp.float32
BF16 = jnp.bfloat16
I32 = jnp.int32

D_MODEL = 1024
DEPTH = 4
PAST_LEN = 8192
PAGE_SIZE = 128
W_A = 31
H_B = 16
DH_B = 64
KV_B = 4
G_B = H_B // KV_B
H_IDX = 8
D_IDX = 64
TOPK = 256
N_BUCKETS = 32
MAX_DIST = 128
D_IN_C = 2 * D_MODEL
P_C = 64
H_C = D_IN_C // P_C
G_C = 4
N_SSM = 128
W_C = 4
CHUNK_C = 128
W_D = 3
N_MEM = 256
H_M = 4
DH_M = D_MODEL // H_M
D_FF = 2816
W_F = 3
EPS = 1e-6

V7X_SUBLANES = 8
V7X_LANES = 128
V7X_VMEM_LIMIT_BYTES = 56 * 1024 * 1024

SAMPLE_PAD = V7X_SUBLANES
NEG_BIG = -1e30
INT_MIN = -(2 ** 31)
PAGES_PER_STEP = 16
MEM_SEQS_PER_STEP = 8


def _cparams(*sem):
    return pltpu.CompilerParams(dimension_semantics=sem, vmem_limit_bytes=V7X_VMEM_LIMIT_BYTES)


def _pick(n, cands):
    for c in cands:
        if n % c == 0:
            return c
    return n


def _rms(x, g):
    y = x * lax.rsqrt(jnp.mean(x * x, axis=-1, keepdims=True) + EPS)
    return y * g


def _dot(a, b):
    return jnp.dot(a, b, preferred_element_type=F32)


def _dot_t(a, b):
    return lax.dot_general(a, b, (((1,), (1,)), ((), ())), preferred_element_type=F32)


def _sigmoid(x):
    return 1.0 / (1.0 + jnp.exp(-x))


def _silu(x):
    return x * _sigmoid(x)


def _group_meansq(x, gsize):
    tm, c = x.shape
    x2 = x * x
    if gsize == c:
        return jnp.mean(x2, axis=-1, keepdims=True)
    if gsize % V7X_LANES == 0:
        parts = []
        for h in range(c // gsize):
            ms = jnp.mean(x2[:, h * gsize:(h + 1) * gsize], axis=-1, keepdims=True)
            parts.append(jnp.broadcast_to(ms, (tm, gsize)))
        return jnp.concatenate(parts, axis=-1)
    shift = int(math.log2(gsize))
    r = lax.shift_right_logical(lax.broadcasted_iota(I32, (c, c), 0), shift)
    q = lax.shift_right_logical(lax.broadcasted_iota(I32, (c, c), 1), shift)
    bd = (r == q).astype(BF16)
    hi = x2.astype(BF16)
    lo = (x2 - hi.astype(F32)).astype(BF16)
    return (_dot(hi, bd) + _dot(lo, bd)) * (1.0 / gsize)


def _group_rms(x, gain, gsize):
    return (x * lax.rsqrt(_group_meansq(x, gsize) + EPS)) * gain


def _linear_body(*refs, norm, bias, res):
    it = iter(refs)
    x_ref = next(it)
    g_ref = next(it) if norm else None
    w_ref = next(it)
    b_ref = next(it) if bias else None
    r_ref = next(it) if res else None
    o_ref = next(it)
    xn_ref = next(it)

    @pl.when(pl.program_id(1) == 0)
    def _():
        x = x_ref[...].astype(F32)
        if norm:
            x = _rms(x, g_ref[...])
        xn_ref[...] = x.astype(BF16)

    acc = _dot(xn_ref[...], w_ref[...])
    if bias:
        acc = acc + b_ref[...]
    if res:
        acc = acc + r_ref[...]
    o_ref[...] = acc.astype(o_ref.dtype)


def linear(x, w, *, g=None, b=None, res=None, out_dtype=F32, name="linear"):
    m, k = x.shape
    n = w.shape[1]
    tm = _pick(m, (1024, 512, 256, 128))
    tn = _pick(n, (1024, 1408, 512, 256, 128))
    in_specs = [pl.BlockSpec((tm, k), lambda i, j: (i, 0))]
    args = [x]
    if g is not None:
        in_specs.append(pl.BlockSpec((1, k), lambda i, j: (0, 0)))
        args.append(g.reshape(1, k).astype(F32))
    in_specs.append(pl.BlockSpec((k, tn), lambda i, j: (0, j)))
    args.append(w)
    if b is not None:
        in_specs.append(pl.BlockSpec((1, tn), lambda i, j: (0, j)))
        args.append(b.reshape(1, n).astype(F32))
    if res is not None:
        in_specs.append(pl.BlockSpec((tm, tn), lambda i, j: (i, j)))
        args.append(res)
    return pl.pallas_call(
        functools.partial(_linear_body, norm=g is not None, bias=b is not None, res=res is not None),
        grid=(m // tm, n // tn),
        in_specs=in_specs,
        out_specs=pl.BlockSpec((tm, tn), lambda i, j: (i, j)),
        out_shape=jax.ShapeDtypeStruct((m, n), out_dtype),
        scratch_shapes=[pltpu.VMEM((tm, k), BF16)],
        compiler_params=_cparams("arbitrary", "arbitrary"),
        name=name,
    )(*args)


def _head_norm_body(x_ref, g_ref, o_ref, ob_ref, *, gsize):
    y = _group_rms(x_ref[...], g_ref[...], gsize)
    o_ref[...] = y
    ob_ref[...] = y.astype(BF16)


def head_norm(x, col0, width, gain, gsize, name="head_norm"):
    m = x.shape[0]
    tm = _pick(m, (512, 256, 128))
    assert col0 % width == 0
    gt = jnp.tile(gain.astype(F32), width // gsize).reshape(1, width)
    return pl.pallas_call(
        functools.partial(_head_norm_body, gsize=gsize),
        grid=(m // tm,),
        in_specs=[pl.BlockSpec((tm, width), lambda i: (i, col0 // width)),
                  pl.BlockSpec((1, width), lambda i: (0, 0))],
        out_specs=[pl.BlockSpec((tm, width), lambda i: (i, 0)),
                   pl.BlockSpec((tm, width), lambda i: (i, 0))],
        out_shape=[jax.ShapeDtypeStruct((m, width), F32), jax.ShapeDtypeStruct((m, width), BF16)],
        compiler_params=_cparams("arbitrary"),
        name=name,
    )(x, gt)


B_QW = H_B * DH_B
B_KW = KV_B * DH_B


def _b_proj_body(x_ref, ng_ref, wm_ref, wt_ref, gq_ref, gk_ref, gki_ref,
                 qn_ref, kn_ref, knb_ref, v_ref, vb_ref, qi_ref, kin_ref, kinb_ref, wi_ref):
    xn = _rms(x_ref[...], ng_ref[...]).astype(BF16)
    pm = _dot(xn, wm_ref[...])
    pt = _dot(xn, wt_ref[...])
    qn_ref[...] = _group_rms(pm[:, 0:B_QW], gq_ref[...], DH_B)
    kn = _group_rms(pm[:, B_QW:B_QW + B_KW], gk_ref[...], DH_B)
    kn_ref[...] = kn
    knb_ref[...] = kn.astype(BF16)
    v = pm[:, B_QW + B_KW:B_QW + 2 * B_KW]
    v_ref[...] = v
    vb_ref[...] = v.astype(BF16)
    qi_ref[...] = pm[:, B_QW + 2 * B_KW:]
    kin = _group_rms(pt[:, 0:D_IDX], gki_ref[...], D_IDX)
    kin_ref[...] = kin
    kinb_ref[...] = kin.astype(BF16)
    wi_ref[...] = pt[:, D_IDX:D_IDX + H_IDX]


def b_proj(x, norm_g, w_in, q_gain, k_gain, ki_gain):
    m, d = x.shape
    tm = _pick(m, (512, 256))
    n_main = B_QW + 2 * B_KW + H_IDX * D_IDX
    w_tail = jnp.pad(w_in[:, n_main:], ((0, 0), (0, V7X_LANES - (w_in.shape[1] - n_main))))
    const = lambda i: (0, 0)
    rowmap = lambda i: (i, 0)
    widths = [(B_QW, F32), (B_KW, F32), (B_KW, BF16), (B_KW, F32), (B_KW, BF16), (H_IDX * D_IDX, F32),
              (D_IDX, F32), (D_IDX, BF16), (H_IDX, F32)]
    return pl.pallas_call(
        _b_proj_body,
        grid=(m // tm,),
        in_specs=[pl.BlockSpec((tm, d), rowmap),
                  pl.BlockSpec((1, d), const),
                  pl.BlockSpec((d, n_main), const),
                  pl.BlockSpec((d, V7X_LANES), const),
                  pl.BlockSpec((1, B_QW), const),
                  pl.BlockSpec((1, B_KW), const),
                  pl.BlockSpec((1, D_IDX), const)],
        out_specs=[pl.BlockSpec((tm, w), rowmap) for w, _ in widths],
        out_shape=[jax.ShapeDtypeStruct((m, w), dt) for w, dt in widths],
        compiler_params=_cparams("arbitrary"),
        name="b_proj",
    )(x, norm_g.reshape(1, d).astype(F32), w_in[:, :n_main].astype(BF16), w_tail.astype(BF16),
      jnp.tile(q_gain.astype(F32), H_B).reshape(1, B_QW), jnp.tile(k_gain.astype(F32), KV_B).reshape(1, B_KW),
      ki_gain.astype(F32).reshape(1, D_IDX))


def _a_pre(tiles):
    a, g = tiles
    return a * _sigmoid(g)


def _a_post(y, tiles, pars):
    bc, lng, lnb = pars
    y = y + bc
    mu = jnp.mean(y, axis=-1, keepdims=True)
    yc = y - mu
    yn = yc * lax.rsqrt(jnp.mean(yc * yc, axis=-1, keepdims=True) + EPS)
    return _silu(yn * lng + lnb)


def _d_pre(tiles):
    return tiles[1] * tiles[2]


def _d_post(y, tiles, pars):
    return tiles[0] * y


def _conv_mixer_body(*refs, n_par, n_parts, pre, post, width, sb, tm, lv, halo, realign):
    x_ref, ng_ref, win_ref, bin_ref, st_ref, wc_ref = refs[:6]
    par_refs = refs[6:6 + n_par]
    wout_ref, o_ref, nst_ref, abuf, rbuf = refs[6 + n_par:]
    t = pl.program_id(1)
    base = halo - (width - 1)
    d = x_ref.shape[-1]
    c = abuf.shape[-1]

    @pl.when(t == 0)
    def _():
        abuf[:, base:halo, :] = st_ref[...]

    x = x_ref[...].reshape(sb * tm, d)
    proj = _dot(_rms(x, ng_ref[...]).astype(BF16), win_ref[...]) + bin_ref[...]
    parts = [proj[:, p * c:(p + 1) * c] for p in range(n_parts)]
    abuf[:, halo:halo + tm, :] = pre(parts).reshape(sb, tm, c)
    acc = None
    for r in range(V7X_SUBLANES):
        taps = [k for k in range(width) if (base + k) % V7X_SUBLANES == r]
        if not taps:
            continue
        rows = V7X_SUBLANES * max((base + k) // V7X_SUBLANES for k in taps) + tm
        for k in taps:
            q0 = V7X_SUBLANES * ((base + k) // V7X_SUBLANES)
            if r == 0 or len(taps) == 1 or not realign:
                shifted = abuf[:, r + q0:r + q0 + tm, :]
            else:
                if k == taps[0]:
                    rbuf[r, :, 0:rows, :] = abuf[:, r:r + rows, :]
                shifted = rbuf[r, :, q0:q0 + tm, :]
            term = wc_ref[k:k + 1, :] * shifted
            acc = term if acc is None else acc + term
    u = post(acc.reshape(sb * tm, c), parts, [p[...] for p in par_refs]).astype(BF16)
    o_ref[...] = (x + _dot(u, wout_ref[...])).reshape(sb, tm, d)
    tail = abuf[:, base + lv:base + lv + width - 1, :]
    abuf[:, base:halo, :] = tail
    nst_ref[...] = tail


def conv_mixer(x, state, norm_g, w_in, b_in, w_conv, params, w_out, pre, post, sb, tm, lv, name):
    n_seq, length, d = x.shape
    width, c = w_conv.shape
    n_parts = w_in.shape[1] // c
    halo = -(-(width - 1) // V7X_SUBLANES) * V7X_SUBLANES
    realign = width > V7X_SUBLANES and tm >= 8 * V7X_SUBLANES
    const = lambda s, t: (0, 0)
    b_in = jnp.zeros((n_parts * c,), F32) if b_in is None else b_in
    in_specs = [pl.BlockSpec((sb, tm, d), lambda s, t: (s, t, 0)),
                pl.BlockSpec((1, d), const),
                pl.BlockSpec((d, n_parts * c), const),
                pl.BlockSpec((1, n_parts * c), const),
                pl.BlockSpec((sb, width - 1, c), lambda s, t: (s, 0, 0)),
                pl.BlockSpec((width, c), const)]
    in_specs += [pl.BlockSpec((1, c), const) for _ in params]
    in_specs.append(pl.BlockSpec((c, d), const))
    return pl.pallas_call(
        functools.partial(_conv_mixer_body, n_par=len(params), n_parts=n_parts, pre=pre, post=post,
                          width=width, sb=sb, tm=tm, lv=lv, halo=halo, realign=realign),
        grid=(n_seq // sb, length // tm),
        in_specs=in_specs,
        out_specs=[pl.BlockSpec((sb, tm, d), lambda s, t: (s, t, 0)),
                   pl.BlockSpec((sb, width - 1, c), lambda s, t: (s, 0, 0))],
        out_shape=[jax.ShapeDtypeStruct((n_seq, length, d), F32),
                   jax.ShapeDtypeStruct((n_seq, width - 1, c), F32)],
        scratch_shapes=[pltpu.VMEM((sb, halo + tm, c), F32),
                        pltpu.VMEM((V7X_SUBLANES, sb, halo + tm, c) if realign
                                   else (1, 1, V7X_SUBLANES, V7X_LANES), F32)],
        compiler_params=_cparams("arbitrary", "arbitrary"),
        name=name,
    )(x, norm_g.reshape(1, d).astype(F32), w_in, b_in.reshape(1, -1).astype(F32), state, w_conv.astype(F32),
      *[p.reshape(1, c).astype(F32) for p in params], w_out)


C_CONV_CHUNK = 1024


def _c_in_body(x_ref, ng_ref, wx_ref, wz_ref, st_ref, wc_ref, bc_ref, xbc_ref, z_ref, nst_ref, abuf, *, sb, tm, lv, halo):
    t = pl.program_id(1)
    base = halo - (W_C - 1)
    d = x_ref.shape[-1]
    conv_c = abuf.shape[-1]

    @pl.when(t == 0)
    def _():
        abuf[:, base:halo, :] = st_ref[...]

    xn = _rms(x_ref[...].reshape(sb * tm, d), ng_ref[...]).astype(BF16)
    for c0 in range(0, conv_c, C_CONV_CHUNK):
        cs = slice(c0, c0 + C_CONV_CHUNK)
        abuf[:, halo:halo + tm, cs] = _dot(xn, wx_ref[:, cs]).reshape(sb, tm, C_CONV_CHUNK)
        y = None
        for k in range(W_C):
            term = wc_ref[k:k + 1, cs] * abuf[:, base + k:base + k + tm, cs]
            y = term if y is None else y + term
        xbc_ref[:, :, cs] = _silu(y + bc_ref[:, cs])
    z_ref[...] = _dot(xn, wz_ref[...]).reshape(sb, tm, z_ref.shape[-1])
    tail = abuf[:, base + lv:base + lv + W_C - 1, :]
    abuf[:, base:halo, :] = tail
    nst_ref[...] = tail


def c_in_conv(x, state, norm_g, w_xbc, w_z, w_conv, b_conv, sb, tm, lv):
    n_seq, length, d = x.shape
    conv_c = w_xbc.shape[1]
    halo = V7X_SUBLANES
    const = lambda s, t: (0, 0)
    tile = lambda w: pl.BlockSpec((sb, tm, w), lambda s, t: (s, t, 0))
    stt = pl.BlockSpec((sb, W_C - 1, conv_c), lambda s, t: (s, 0, 0))
    return pl.pallas_call(
        functools.partial(_c_in_body, sb=sb, tm=tm, lv=lv, halo=halo),
        grid=(n_seq // sb, length // tm),
        in_specs=[tile(d), pl.BlockSpec((1, d), const), pl.BlockSpec((d, conv_c), const),
                  pl.BlockSpec((d, D_IN_C), const), stt, pl.BlockSpec((W_C, conv_c), const),
                  pl.BlockSpec((1, conv_c), const)],
        out_specs=[tile(conv_c), tile(D_IN_C), stt],
        out_shape=[jax.ShapeDtypeStruct((n_seq, length, conv_c), F32),
                   jax.ShapeDtypeStruct((n_seq, length, D_IN_C), F32),
                   jax.ShapeDtypeStruct((n_seq, W_C - 1, conv_c), F32)],
        scratch_shapes=[pltpu.VMEM((sb, halo + tm, conv_c), F32)],
        compiler_params=_cparams("arbitrary", "arbitrary"),
        name="c_in_conv",
    )(x, norm_g.reshape(1, d).astype(F32), w_xbc, w_z, state, w_conv.astype(F32),
      b_conv.reshape(1, conv_c).astype(F32))


FFN_CHUNK = 11 * V7X_LANES


def _ffn_body(x_ref, ng_ref, win_ref, st_ref, wc_ref, bc_ref, wout_ref, o_ref, nst_ref, abuf, *, sb, tm, lv, halo):
    t = pl.program_id(1)
    base = halo - (W_F - 1)
    d = x_ref.shape[-1]

    @pl.when(t == 0)
    def _():
        abuf[:, base:halo, :] = st_ref[...]

    x = x_ref[...].reshape(sb * tm, d)
    xn = _rms(x, ng_ref[...]).astype(BF16)
    out = x
    for c0 in range(0, D_FF, FFN_CHUNK):
        cs = slice(c0, c0 + FFN_CHUNK)
        abuf[:, halo:halo + tm, cs] = _dot(xn, win_ref[:, cs]).reshape(sb, tm, FFN_CHUNK)
        g = _dot(xn, win_ref[:, D_FF + c0:D_FF + c0 + FFN_CHUNK])
        y = None
        for k in range(W_F):
            term = wc_ref[k:k + 1, cs] * abuf[:, base + k:base + k + tm, cs]
            y = term if y is None else y + term
        y = y.reshape(sb * tm, FFN_CHUNK) + bc_ref[:, cs]
        out = out + _dot((_silu(y) * g).astype(BF16), wout_ref[cs, :])
    o_ref[...] = out.reshape(sb, tm, d)
    tail = abuf[:, base + lv:base + lv + W_F - 1, :]
    abuf[:, base:halo, :] = tail
    nst_ref[...] = tail


def ffn_fused(x, state, norm_g, w_in, w_conv, b_conv, w_out, sb, tm, lv):
    n_seq, length, d = x.shape
    halo = V7X_SUBLANES
    const = lambda s, t: (0, 0)
    resident = dict(pipeline_mode=pl.Buffered(1))
    return pl.pallas_call(
        functools.partial(_ffn_body, sb=sb, tm=tm, lv=lv, halo=halo),
        grid=(n_seq // sb, length // tm),
        in_specs=[pl.BlockSpec((sb, tm, d), lambda s, t: (s, t, 0)),
                  pl.BlockSpec((1, d), const),
                  pl.BlockSpec((d, 2 * D_FF), const, **resident),
                  pl.BlockSpec((sb, W_F - 1, D_FF), lambda s, t: (s, 0, 0)),
                  pl.BlockSpec((W_F, D_FF), const),
                  pl.BlockSpec((1, D_FF), const),
                  pl.BlockSpec((D_FF, d), const, **resident)],
        out_specs=[pl.BlockSpec((sb, tm, d), lambda s, t: (s, t, 0)),
                   pl.BlockSpec((sb, W_F - 1, D_FF), lambda s, t: (s, 0, 0))],
        out_shape=[jax.ShapeDtypeStruct((n_seq, length, d), F32),
                   jax.ShapeDtypeStruct((n_seq, W_F - 1, D_FF), F32)],
        scratch_shapes=[pltpu.VMEM((sb, halo + tm, D_FF), F32)],
        compiler_params=_cparams("arbitrary", "arbitrary"),
        name="ffn",
    )(x, norm_g.reshape(1, d).astype(F32), w_in, state, w_conv.astype(F32),
      b_conv.reshape(1, D_FF).astype(F32), w_out)


def _softmax_rows(s):
    p = jnp.exp(s - jnp.max(s, axis=-1, keepdims=True))
    return p / jnp.sum(p, axis=-1, keepdims=True)


def _mem_attend(q, k, v, gain, token_major):
    tm = q.shape[0]
    qh = [(_rms(q[:, h * DH_M:(h + 1) * DH_M], gain) * (DH_M ** -0.5)).astype(BF16) for h in range(H_M)]
    if not token_major:
        outs = []
        for h in range(H_M):
            p = _softmax_rows(_dot_t(qh[h], k[h].astype(BF16)))
            outs.append(_dot(p.astype(BF16), v[h].astype(BF16)))
    else:
        k_all = k.reshape(N_MEM * H_M, DH_M).astype(BF16)
        v_all = v.reshape(N_MEM * H_M, DH_M).astype(BF16)
        s = _dot_t(jnp.concatenate(qh, axis=0), k_all)
        col_head = lax.broadcasted_iota(I32, s.shape, 1) & (H_M - 1)
        row_head = lax.shift_right_logical(lax.broadcasted_iota(I32, s.shape, 0), int(math.log2(tm)))
        p = _softmax_rows(jnp.where(col_head == row_head, s, NEG_BIG))
        o = _dot(p.astype(BF16), v_all)
        outs = [o[h * tm:(h + 1) * tm] for h in range(H_M)]
    return jnp.concatenate(outs, axis=1).astype(BF16)


def _mem_attn_body(q_ref, k_ref, v_ref, g_ref, o_ref, *, sb):
    for s in range(sb):
        o_ref[s] = _mem_attend(q_ref[s], k_ref[s], v_ref[s], g_ref[...], True)


def mem_attn_cached(q, k, v, q_gain, sb, kv_seq0):
    n_seq, tm, d = q.shape
    kv_block = (sb,) + k.shape[1:]
    kv_map = lambda s: (kv_seq0 // sb + s, 0, 0, 0)
    return pl.pallas_call(
        functools.partial(_mem_attn_body, sb=sb),
        grid=(n_seq // sb,),
        in_specs=[pl.BlockSpec((sb, tm, d), lambda s: (s, 0, 0)),
                  pl.BlockSpec(kv_block, kv_map),
                  pl.BlockSpec(kv_block, kv_map),
                  pl.BlockSpec((1, DH_M), lambda s: (0, 0))],
        out_specs=pl.BlockSpec((sb, tm, d), lambda s: (s, 0, 0)),
        out_shape=jax.ShapeDtypeStruct((n_seq, tm, d), BF16),
        compiler_params=_cparams("arbitrary"),
        name="mem_attn",
    )(q, k, v, q_gain.reshape(1, DH_M).astype(F32))


def _mem_fused_body(x_ref, ng_ref, wq_ref, k_ref, v_ref, g_ref, wo_ref, o_ref):
    x = x_ref[0]
    q = _dot(_rms(x, ng_ref[...]).astype(BF16), wq_ref[...])
    o = _mem_attend(q, k_ref[0], v_ref[0], g_ref[...], False)
    o_ref[0] = x + _dot(o, wo_ref[...])


def mem_fused(x, k, v, norm_g, w_q, q_gain, w_o, tm):
    n_seq, length, d = x.shape
    const = lambda s, t: (0, 0)
    kv_block = (1,) + k.shape[1:]
    return pl.pallas_call(
        _mem_fused_body,
        grid=(n_seq, length // tm),
        in_specs=[pl.BlockSpec((1, tm, d), lambda s, t: (s, t, 0)),
                  pl.BlockSpec((1, d), const),
                  pl.BlockSpec((d, d), const),
                  pl.BlockSpec(kv_block, lambda s, t: (s, 0, 0, 0)),
                  pl.BlockSpec(kv_block, lambda s, t: (s, 0, 0, 0)),
                  pl.BlockSpec((1, DH_M), const),
                  pl.BlockSpec((d, d), const)],
        out_specs=pl.BlockSpec((1, tm, d), lambda s, t: (s, t, 0)),
        out_shape=jax.ShapeDtypeStruct((n_seq, length, d), F32),
        compiler_params=_cparams("arbitrary", "arbitrary"),
        name="mem_fused",
    )(x, norm_g.reshape(1, d).astype(F32), w_q, k, v, q_gain.reshape(1, DH_M).astype(F32), w_o)


HEADS_PER_PAIR = 2 * G_B


def _t5_lookup(tab_ref, h, dist):
    dist = jnp.maximum(dist, 0)
    max_exact = N_BUCKETS // 2
    df = jnp.maximum(dist, 1).astype(F32)
    large = max_exact + (jnp.log(df / max_exact) / math.log(MAX_DIST / max_exact)
                         * (N_BUCKETS - max_exact)).astype(I32)
    large = jnp.minimum(large, N_BUCKETS - 1)
    bucket = jnp.where(dist < max_exact, dist, large)
    acc = jnp.zeros(dist.shape, F32)
    for b in range(N_BUCKETS):
        acc = jnp.where(bucket == b, tab_ref[b, h], acc)
    return acc


def _t5_tiles_body(tab_ref, o_ref, ot_ref):
    h = pl.program_id(0)
    r = lax.broadcasted_iota(I32, (PAGE_SIZE, PAGE_SIZE), 0)
    c = lax.broadcasted_iota(I32, (PAGE_SIZE, PAGE_SIZE), 1)
    o_ref[0, :, 0:PAGE_SIZE] = _t5_lookup(tab_ref, h, r - c)
    o_ref[0, :, PAGE_SIZE:2 * PAGE_SIZE] = _t5_lookup(tab_ref, h, PAGE_SIZE + r - c)
    ot_ref[0, 0] = _t5_lookup(tab_ref, h, c - r)
    ot_ref[0, 1] = _t5_lookup(tab_ref, h, PAGE_SIZE + c - r)


def t5_tiles(table):
    return pl.pallas_call(
        _t5_tiles_body,
        grid=(H_B,),
        in_specs=[pl.BlockSpec(memory_space=pltpu.SMEM)],
        out_specs=[pl.BlockSpec((1, PAGE_SIZE, 2 * PAGE_SIZE), lambda h: (h, 0, 0)),
                   pl.BlockSpec((1, 2, PAGE_SIZE, PAGE_SIZE),
                                lambda h: (h // HEADS_PER_PAIR, 0, 0, h % HEADS_PER_PAIR))],
        out_shape=[jax.ShapeDtypeStruct((H_B, PAGE_SIZE, 2 * PAGE_SIZE), F32),
                   jax.ShapeDtypeStruct((H_B // HEADS_PER_PAIR, 2, PAGE_SIZE, HEADS_PER_PAIR * PAGE_SIZE), F32)],
        compiler_params=_cparams("arbitrary"),
        name="t5_tiles",
    )(table.astype(F32))


def _sort_key(score):
    score = jnp.where(score == 0.0, 0.0, score)
    u = lax.bitcast_convert_type(score, I32)
    return jnp.where(u < 0, u ^ 0x7FFFFFFF, u)


def _kth_largest(count_ge, shape, topk):
    def try_cand(cand, cur):
        return jnp.where(count_ge(cand) >= topk, cand, cur)

    t0 = try_cand(jnp.zeros(shape, I32), jnp.full(shape, INT_MIN, I32))

    def step(b, cur):
        cand = cur + lax.shift_left(jnp.int32(1), jnp.int32(30) - b)
        return try_cand(cand, cur)

    return lax.fori_loop(0, 31, step, t0)


def _strict_upper(n):
    r = lax.broadcasted_iota(I32, (n, n), 0)
    c = lax.broadcasted_iota(I32, (n, n), 1)
    return (r < c).astype(BF16)


def _strict_lower(n):
    r = lax.broadcasted_iota(I32, (n, n), 0)
    c = lax.broadcasted_iota(I32, (n, n), 1)
    return (c < r).astype(BF16)


def _tree(parts, op):
    parts = list(parts)
    while len(parts) > 1:
        nxt = [op(parts[k], parts[k + 1]) for k in range(0, len(parts) - 1, 2)]
        if len(parts) % 2:
            nxt.append(parts[-1])
        parts = nxt
    return parts[0]


def _fold_rows(x, op):
    return _tree([x[r:r + V7X_SUBLANES] for r in range(0, x.shape[0], V7X_SUBLANES)], op)


LOOP_UNROLL = 4


def _loop_tiles(n, body, carry, unroll=LOOP_UNROLL):
    shift = int(math.log2(unroll))
    n_main = lax.shift_right_logical(n, shift)
    carry = lax.fori_loop(0, n_main, lambda t, c: body(t * unroll, unroll, c), carry)
    return lax.fori_loop(n_main * unroll, n, lambda j, c: body(j, 1, c), carry)


def _dsa_prompt_body(q_ref, qi_ref, wi_ref, k_ref, vt_ref, ki_ref, bt_ref, o_ref,
                     key_ref, am_ref, lg_ref, acc_ref, ot_ref, *, topk):
    i = pl.program_id(1)
    nj = i + 1
    qb = PAGE_SIZE
    kpos = lax.broadcasted_iota(I32, (qb, qb), 0)
    qpos = lax.broadcasted_iota(I32, (qb, qb), 1)

    def kslice(j0, u=1):
        return pl.ds(pl.multiple_of(j0 * qb, qb), u * qb)

    qit = qi_ref[0].T
    qit = jnp.concatenate([qit[h * D_IDX:(h + 1) * D_IDX] for h in range(H_IDX)], axis=1).astype(BF16)
    wsc = (wi_ref[0] * (H_IDX ** -0.5)) * (D_IDX ** -0.5)
    wt = jnp.concatenate([wsc, jnp.zeros((qb, qb - H_IDX), F32)], axis=1).T
    wrow = jnp.concatenate([wt[h:h + 1] for h in range(H_IDX)], axis=1)

    def idx_step(j0, u, carry):
        s = jnp.maximum(_dot(ki_ref[0, kslice(j0, u), :], qit), 0.0) * wrow
        sc = s[:, 0:qb]
        for h in range(1, H_IDX):
            sc = sc + s[:, h * qb:(h + 1) * qb]
        key = _sort_key(sc)
        for t in range(u):
            valid = (j0 + t < i) | (kpos <= qpos)
            key_ref[j0 + t] = jnp.where(valid, key[t * qb:(t + 1) * qb], INT_MIN)
        return carry

    _loop_tiles(nj, idx_step, 0)

    def count_where(pred):
        def cstep(j0, u, c):
            for t in range(u):
                c = jnp.where(pred(key_ref[j0 + t]), c + 1.0, c)
            return c
        c = _loop_tiles(nj, cstep, jnp.zeros((qb, qb), F32))
        return jnp.sum(_fold_rows(c, jnp.add), axis=0, keepdims=True)

    thr = _kth_largest(lambda cand: count_where(lambda key: key >= cand), (1, qb), topk)
    n_tie = topk - count_where(lambda key: key > thr)
    n_eq = count_where(lambda key: (key == thr) & (key != INT_MIN))
    need_rank = jnp.max(n_eq - n_tie) > 0.0

    @pl.when(jnp.logical_not(need_rank))
    def _():
        def mask_step(j0, u, carry):
            for t in range(u):
                key = key_ref[j0 + t]
                am_ref[j0 + t] = jnp.where((key >= thr) & (key != INT_MIN), 0.0, NEG_BIG)
            return carry
        _loop_tiles(nj, mask_step, 0)

    @pl.when(need_rank)
    def _():
        lower = _strict_lower(qb)

        def mask_step(j, run):
            key = key_ref[j]
            eq = (key == thr) & (key != INT_MIN)
            eqf = jnp.where(eq, 1.0, 0.0)
            rank = _dot(lower, eqf.astype(BF16)) + run
            sel = (key > thr) | (eq & (rank < n_tie))
            am_ref[j] = jnp.where(sel, 0.0, NEG_BIG)
            return run + jnp.sum(_fold_rows(eqf, jnp.add), axis=0, keepdims=True)

        lax.fori_loop(0, nj, mask_step, jnp.zeros((1, qb), F32))

    qt = q_ref[0].T
    pw = HEADS_PER_PAIR * qb
    gw = G_B * qb
    for pp in range(KV_B // 2):
        def group_qt(n):
            return jnp.concatenate([qt[h * DH_B:(h + 1) * DH_B] for h in range(n * G_B, (n + 1) * G_B)],
                                   axis=1)
        zero = jnp.zeros((DH_B, gw), F32)
        rhs = jnp.concatenate([jnp.concatenate([group_qt(2 * pp), zero], axis=1),
                               jnp.concatenate([zero, group_qt(2 * pp + 1)], axis=1)], axis=0)
        rhs = (rhs * (DH_B ** -0.5)).astype(BF16)
        klanes = slice(pp * 2 * DH_B, (pp + 1) * 2 * DH_B)
        far = bt_ref[pp, 1, 0:1, :]

        def tile_logits(j0, u, bias, mx):
            s = _dot(k_ref[0, kslice(j0, u), klanes], rhs)
            for t in range(u):
                st = s[t * qb:(t + 1) * qb] + bias + jnp.concatenate([am_ref[j0 + t]] * HEADS_PER_PAIR, axis=1)
                lg_ref[j0 + t] = st
                mx = jnp.maximum(mx, _fold_rows(st, jnp.maximum))
            return mx

        neg = jnp.full((V7X_SUBLANES, pw), NEG_BIG, F32)
        mx = _loop_tiles(jnp.maximum(i - 1, 0), lambda j0, u, m: tile_logits(j0, u, far, m), neg)
        mx = lax.cond(i >= 1, lambda m: tile_logits(i - 1, 1, bt_ref[pp, 1], m), lambda m: m, mx)
        mx = tile_logits(i, 1, bt_ref[pp, 0], mx)
        m = jnp.max(mx, axis=0, keepdims=True)
        acc_ref[...] = jnp.zeros(acc_ref.shape, F32)

        def p2(j0, u, l):
            ps = []
            for t in range(u):
                p = jnp.exp(lg_ref[j0 + t] - m)
                l = l + _fold_rows(p, jnp.add)
                ps.append(p.astype(BF16))
            pb = jnp.concatenate(ps, axis=0)
            vt = jnp.concatenate([vt_ref[0, j0 + t] for t in range(u)], axis=1)
            for gg in range(2):
                n = 2 * pp + gg
                acc_ref[gg] += _dot(vt[n * DH_B:(n + 1) * DH_B, :], pb[:, gg * gw:(gg + 1) * gw])
            return l

        l = _loop_tiles(nj, p2, jnp.zeros((V7X_SUBLANES, pw), F32))
        lsum = jnp.sum(l, axis=0, keepdims=True)
        for gg in range(2):
            o_t = acc_ref[gg] / lsum[:, gg * gw:(gg + 1) * gw]
            for hh in range(G_B):
                h = (2 * pp + gg) * G_B + hh
                ot_ref[h * DH_B:(h + 1) * DH_B, :] = o_t[:, hh * qb:(hh + 1) * qb]
    o_ref[0] = ot_ref[...].T.astype(o_ref.dtype)


def dsa_prompt(qn, qi, wi, kb, vtb, kib, btt):
    n_seq, length, _ = qn.shape
    nkb = length // PAGE_SIZE
    qmap = lambda s, i: (s, i, 0)
    kmap = lambda s, i: (s, 0, 0)
    pw = HEADS_PER_PAIR * PAGE_SIZE
    return pl.pallas_call(
        functools.partial(_dsa_prompt_body, topk=min(TOPK, length // 4)),
        grid=(n_seq, nkb),
        in_specs=[pl.BlockSpec((1, PAGE_SIZE, H_B * DH_B), qmap),
                  pl.BlockSpec((1, PAGE_SIZE, H_IDX * D_IDX), qmap),
                  pl.BlockSpec((1, PAGE_SIZE, H_IDX), qmap),
                  pl.BlockSpec((1, length, KV_B * DH_B), kmap),
                  pl.BlockSpec((1, nkb, KV_B * DH_B, PAGE_SIZE), lambda s, i: (s, 0, 0, 0)),
                  pl.BlockSpec((1, length, D_IDX), kmap),
                  pl.BlockSpec((H_B // HEADS_PER_PAIR, 2, PAGE_SIZE, pw), lambda s, i: (0, 0, 0, 0))],
        out_specs=pl.BlockSpec((1, PAGE_SIZE, H_B * DH_B), qmap),
        out_shape=jax.ShapeDtypeStruct((n_seq, length, H_B * DH_B), BF16),
        scratch_shapes=[pltpu.VMEM((nkb, PAGE_SIZE, PAGE_SIZE), I32),
                        pltpu.VMEM((nkb, PAGE_SIZE, PAGE_SIZE), F32),
                        pltpu.VMEM((nkb, PAGE_SIZE, pw), F32),
                        pltpu.VMEM((2, DH_B, G_B * PAGE_SIZE), F32),
                        pltpu.VMEM((H_B * DH_B, PAGE_SIZE), F32)],
        compiler_params=_cparams("arbitrary", "arbitrary"),
        name="dsa_prompt",
    )(qn, qi, wi, kb, vtb, kib, btt)


N_PAGES = PAST_LEN // PAGE_SIZE
N_KTILES = N_PAGES + 1
N_KTILES_PAD = -(-N_KTILES // PAGES_PER_STEP) * PAGES_PER_STEP


def _dsa_sample_sel_body(pt_ref, qi_ref, wi_ref, kin_ref, *rest, lv, topk):
    page_refs = rest[:PAGES_PER_STEP]
    am_ref, key_ref, qs_ref, ws_ref = rest[PAGES_PER_STEP:]
    c = pl.program_id(1)
    nsteps = pl.num_programs(1)
    r8 = SAMPLE_PAD

    @pl.when(c == 0)
    def _():
        qi = qi_ref[0]
        wsc = (wi_ref[0] * (H_IDX ** -0.5)) * (D_IDX ** -0.5)
        qs_ref[...] = jnp.concatenate([qi[:, h * D_IDX:(h + 1) * D_IDX] for h in range(H_IDX)],
                                      axis=0).astype(BF16)
        ws_ref[...] = jnp.concatenate([jnp.broadcast_to(wsc[:, h:h + 1], (r8, PAGE_SIZE))
                                       for h in range(H_IDX)], axis=0)

    def score(ki_t):
        s = jnp.maximum(_dot(qs_ref[...], ki_t), 0.0) * ws_ref[...]
        return _fold_rows(s, jnp.add)

    for r in range(PAGES_PER_STEP):
        key_ref[c * PAGES_PER_STEP + r] = _sort_key(score(page_refs[r][0].astype(BF16)))

    @pl.when(c == nsteps - 1)
    def _():
        rows = lax.broadcasted_iota(I32, (r8, PAGE_SIZE), 0)
        cols = lax.broadcasted_iota(I32, (r8, PAGE_SIZE), 1)
        new_valid = (cols <= rows) & (cols < lv)
        kin = jnp.concatenate([kin_ref[0], jnp.zeros((PAGE_SIZE - r8, D_IDX), F32)], axis=0)
        kin_t = jnp.concatenate([kin, jnp.zeros((PAGE_SIZE, PAGE_SIZE - D_IDX), F32)], axis=1).T[0:D_IDX]
        key_ref[N_PAGES] = jnp.where(new_valid, _sort_key(score(kin_t.astype(BF16))), INT_MIN)

        def tile_valid(j):
            return new_valid if j == N_PAGES else None

        def count_where(pred):
            parts = []
            for j in range(N_KTILES):
                hit = pred(key_ref[j])
                if tile_valid(j) is not None:
                    hit = hit & tile_valid(j)
                parts.append(jnp.where(hit, 1.0, 0.0))
            return jnp.sum(_tree(parts, jnp.add), axis=-1, keepdims=True)

        thr = _kth_largest(lambda cand: count_where(lambda key: key >= cand), (r8, 1), topk)
        n_tie = topk - count_where(lambda key: key > thr)
        n_eq = count_where(lambda key: key == thr)
        need_rank = jnp.max(n_eq - n_tie) > 0.0

        @pl.when(jnp.logical_not(need_rank))
        def _():
            for j in range(N_KTILES):
                sel = key_ref[j] >= thr
                if tile_valid(j) is not None:
                    sel = sel & tile_valid(j)
                am_ref[0, j] = jnp.where(sel, 0.0, NEG_BIG)

        @pl.when(need_rank)
        def _():
            upper = _strict_upper(PAGE_SIZE)

            def mask_step(j, run):
                key = key_ref[j]
                valid = (j < N_PAGES) | new_valid
                eq = valid & (key == thr)
                eqf = jnp.where(eq, 1.0, 0.0)
                rank = _dot(eqf.astype(BF16), upper) + run
                sel = (key > thr) | (eq & (rank < n_tie))
                am_ref[0, j] = jnp.where(sel, 0.0, NEG_BIG)
                return run + jnp.sum(eqf, axis=-1, keepdims=True)

            lax.fori_loop(0, N_KTILES, mask_step, jnp.zeros((r8, 1), F32))

        for j in range(N_KTILES, N_KTILES_PAD):
            am_ref[0, j] = jnp.full((r8, PAGE_SIZE), NEG_BIG, F32)


def dsa_sample_select(page_table, qi, wi, kin, kidx_pool_t, lv):
    n_seq = qi.shape[0]
    r8 = SAMPLE_PAD
    qmap = lambda s, c, pt: (s, 0, 0)
    page_specs = [pl.BlockSpec((1, D_IDX, PAGE_SIZE),
                               functools.partial(lambda s, c, pt, r: (pt[s, c * PAGES_PER_STEP + r], 0, 0), r=r))
                  for r in range(PAGES_PER_STEP)]
    grid_spec = pltpu.PrefetchScalarGridSpec(
        num_scalar_prefetch=1,
        grid=(n_seq, N_PAGES // PAGES_PER_STEP),
        in_specs=[pl.BlockSpec((1, r8, H_IDX * D_IDX), qmap),
                  pl.BlockSpec((1, r8, H_IDX), qmap),
                  pl.BlockSpec((1, r8, D_IDX), qmap)] + page_specs,
        out_specs=pl.BlockSpec((1, N_KTILES_PAD, r8, PAGE_SIZE), lambda s, c, pt: (s, 0, 0, 0)),
        scratch_shapes=[pltpu.VMEM((N_KTILES, r8, PAGE_SIZE), I32),
                        pltpu.VMEM((H_IDX * r8, D_IDX), BF16),
                        pltpu.VMEM((H_IDX * r8, PAGE_SIZE), F32)],
    )
    return pl.pallas_call(
        functools.partial(_dsa_sample_sel_body, lv=lv, topk=min(TOPK, (PAST_LEN + lv) // 4)),
        grid_spec=grid_spec,
        out_shape=jax.ShapeDtypeStruct((n_seq, N_KTILES_PAD, r8, PAGE_SIZE), F32),
        compiler_params=_cparams("arbitrary", "arbitrary"),
        name="dsa_sample_select",
    )(page_table, qi, wi, kin, *([kidx_pool_t] * PAGES_PER_STEP))


def _dsa_sample_attn_body(pt_ref, q_ref, kn_ref, vn_ref, am_ref, amn_ref, bt_ref, *rest):
    kp_refs = rest[:PAGES_PER_STEP]
    vp_refs = rest[PAGES_PER_STEP:2 * PAGES_PER_STEP]
    o_ref, m_ref, l_ref, acc_ref = rest[2 * PAGES_PER_STEP:]
    c = pl.program_id(1)
    nsteps = pl.num_programs(1)
    r8 = SAMPLE_PAD
    gr = G_B * r8

    @pl.when(c == 0)
    def _():
        m_ref[...] = jnp.full(m_ref.shape, NEG_BIG, F32)
        l_ref[...] = jnp.zeros(l_ref.shape, F32)
        acc_ref[...] = jnp.zeros(acc_ref.shape, F32)

    q = q_ref[0]
    qgs, b_subs, b_diags, b_fars = [], [], [], []
    for n in range(KV_B):
        heads = [n * G_B + hh for hh in range(G_B)]
        qg = jnp.concatenate([q[:, h * DH_B:(h + 1) * DH_B] for h in heads], axis=0)
        qgs.append((qg * (DH_B ** -0.5)).astype(BF16))
        b_diags.append(jnp.concatenate([bt_ref[h, :, 0:PAGE_SIZE] for h in heads], axis=0))
        b_subs.append(jnp.concatenate([bt_ref[h, :, PAGE_SIZE:2 * PAGE_SIZE] for h in heads], axis=0))
        b_fars.append(jnp.concatenate(
            [jnp.broadcast_to(bt_ref[h, 0:1, PAGE_SIZE:PAGE_SIZE + 1], (r8, PAGE_SIZE))
             for h in heads], axis=0))

    def update(n, s, pv):
        m_old = m_ref[n]
        m_new = jnp.maximum(m_old, jnp.max(s, axis=-1, keepdims=True))
        alpha = jnp.exp(m_old - m_new)
        p = jnp.exp(s - m_new[:, 0:1])
        l_ref[n] = alpha * l_ref[n] + jnp.sum(p, axis=-1, keepdims=True)
        acc_ref[n] = alpha[:, 0:DH_B] * acc_ref[n] + pv(p.astype(BF16))
        m_ref[n] = m_new

    def group_t(refs, n):
        return jnp.concatenate([ref[0, n] for ref in refs], axis=1).astype(BF16)

    amcat = jnp.concatenate([am_ref[0, r] for r in range(PAGES_PER_STEP)], axis=1)
    am4 = jnp.concatenate([amcat] * G_B, axis=0)
    is_last_step = c == nsteps - 1
    for n in range(KV_B):
        bias = jnp.concatenate([b_fars[n]] * (PAGES_PER_STEP - 1)
                               + [jnp.where(is_last_step, b_subs[n], b_fars[n])], axis=1)
        s = _dot(qgs[n], group_t(kp_refs, n)) + bias + am4
        v_t = group_t(vp_refs, n)
        update(n, s, lambda p: _dot_t(p, v_t))

    @pl.when(c == nsteps - 1)
    def _():
        pad = jnp.zeros((PAGE_SIZE - r8, KV_B * DH_B), F32)
        kn = jnp.concatenate([kn_ref[0], pad], axis=0).astype(BF16)
        vn = jnp.concatenate([vn_ref[0], pad], axis=0).astype(BF16)
        am4 = jnp.concatenate([amn_ref[0, 0]] * G_B, axis=0)
        for n in range(KV_B):
            ksl = slice(n * DH_B, (n + 1) * DH_B)
            s = _dot_t(qgs[n], kn[:, ksl]) + b_diags[n] + am4
            update(n, s, lambda p: _dot(p, vn[:, ksl]))
            o = acc_ref[n] / l_ref[n][:, 0:DH_B]
            for hh in range(G_B):
                h = n * G_B + hh
                o_ref[0, :, h * DH_B:(h + 1) * DH_B] = o[hh * r8:(hh + 1) * r8].astype(o_ref.dtype)


def dsa_sample_attend(page_table, qn, kn, vn, amask, bt, k_pool, v_pool):
    n_seq = qn.shape[0]
    r8 = SAMPLE_PAD
    qmap = lambda s, c, pt: (s, 0, 0)
    page_map = [functools.partial(lambda s, c, pt, r: (pt[s, c * PAGES_PER_STEP + r], 0, 0, 0), r=r)
                for r in range(PAGES_PER_STEP)]
    kv_w = KV_B * DH_B
    page_block = (1, KV_B, DH_B, PAGE_SIZE)
    grid_spec = pltpu.PrefetchScalarGridSpec(
        num_scalar_prefetch=1,
        grid=(n_seq, N_PAGES // PAGES_PER_STEP),
        in_specs=[pl.BlockSpec((1, r8, H_B * DH_B), qmap),
                  pl.BlockSpec((1, r8, kv_w), qmap),
                  pl.BlockSpec((1, r8, kv_w), qmap),
                  pl.BlockSpec((1, PAGES_PER_STEP, r8, PAGE_SIZE), lambda s, c, pt: (s, c, 0, 0)),
                  pl.BlockSpec((1, PAGES_PER_STEP, r8, PAGE_SIZE),
                               lambda s, c, pt: (s, N_PAGES // PAGES_PER_STEP, 0, 0)),
                  pl.BlockSpec((H_B, r8, 2 * PAGE_SIZE), lambda s, c, pt: (0, 0, 0))]
                 + [pl.BlockSpec(page_block, m) for m in page_map]
                 + [pl.BlockSpec(page_block, m) for m in page_map],
        out_specs=pl.BlockSpec((1, r8, H_B * DH_B), qmap),
        scratch_shapes=[pltpu.VMEM((KV_B, G_B * r8, PAGE_SIZE), F32),
                        pltpu.VMEM((KV_B, G_B * r8, PAGE_SIZE), F32),
                        pltpu.VMEM((KV_B, G_B * r8, DH_B), F32)],
    )
    return pl.pallas_call(
        _dsa_sample_attn_body,
        grid_spec=grid_spec,
        out_shape=jax.ShapeDtypeStruct((n_seq, r8, H_B * DH_B), BF16),
        compiler_params=_cparams("arbitrary", "arbitrary"),
        name="dsa_sample_attend",
    )(page_table, qn, kn, vn, amask, amask, bt,
      *([k_pool] * PAGES_PER_STEP), *([v_pool] * PAGES_PER_STEP))


HEADS_PER_GROUP = H_C // G_C
GROUP_W = D_IN_C // G_C


def _softplus(x):
    return jnp.maximum(x, 0.0) + jnp.log1p(jnp.exp(-jnp.abs(x)))


def _split3(x):
    hi = x.astype(BF16)
    r = x - hi.astype(F32)
    mid = r.astype(BF16)
    lo = (r - mid.astype(F32)).astype(BF16)
    return hi, mid, lo


def _ssd_body(x_ref, xbc_ref, zlo_ref, zhi_ref, ng_ref, wdt_ref, wdtt_ref, dtb_ref, dtbt_ref,
              alog_ref, alogt_ref, dsk_ref, cn_ref, wo_ref, h0_ref, o_ref, hst_ref, *, qin, lv):
    c = pl.program_id(1)
    q = CHUNK_C
    rows = lax.broadcasted_iota(I32, (q, q), 0)
    cols = lax.broadcasted_iota(I32, (q, q), 1)
    tril = cols <= rows

    def pad_rows(a):
        if qin == q:
            return a
        return jnp.concatenate([a, jnp.zeros((q - qin, a.shape[1]), a.dtype)], axis=0)

    @pl.when(c == 0)
    def _():
        hst_ref[...] = h0_ref[...]

    x = x_ref[0]
    xn = _rms(pad_rows(x), ng_ref[...]).astype(BF16)
    dt_all = _softplus(_dot(xn, wdt_ref[...]) + dtb_ref[...])
    dtt_all = _softplus(_dot_t(wdtt_ref[...], xn) + dtbt_ref[...])
    if lv < q:
        dt_all = jnp.where(lax.broadcasted_iota(I32, dt_all.shape, 0) < lv, dt_all, 0.0)
        dtt_all = jnp.where(lax.broadcasted_iota(I32, dtt_all.shape, 1) < lv, dtt_all, 0.0)
    acs_all = _tree([_dot(tril.astype(BF16), p) for p in _split3(dt_all * (-jnp.exp(alog_ref[...])))], jnp.add)
    acst_all = _tree([_dot(p, (rows <= cols).astype(BF16))
                      for p in _split3(dtt_all * (-jnp.exp(alogt_ref[...])))], jnp.add)
    z_all = pad_rows(jnp.concatenate([zlo_ref[0], zhi_ref[0]], axis=1))
    y_groups = []
    for g in range(G_C):
        y_groups.append(_ssd_group(g, xbc_ref, hst_ref, dt_all, dtt_all, acs_all, acst_all, z_all,
                                   dsk_ref, cn_ref, tril, pad_rows))
    y = jnp.concatenate(y_groups, axis=1)
    o_ref[0] = x + _dot(y, wo_ref[...])[0:qin]


def _ssd_group(g, xbc_ref, hst_ref, dt_all, dtt_all, acs_all, acst_all, z_all, dsk_ref, cn_ref, tril, pad_rows):
    q = CHUNK_C
    hpg = HEADS_PER_GROUP
    gheads = slice(g * hpg, (g + 1) * hpg)
    gcols = slice(g * GROUP_W, (g + 1) * GROUP_W)
    dt = dt_all[:, gheads]
    dtt = dtt_all[gheads, :]
    acs = acs_all[:, gheads]
    acst = acst_all[gheads, :]
    xs = pad_rows(xbc_ref[0, :, gcols])
    bm = pad_rows(xbc_ref[0, :, D_IN_C + g * N_SSM:D_IN_C + (g + 1) * N_SSM])
    cm = pad_rows(xbc_ref[0, :, D_IN_C + (G_C + g) * N_SSM:D_IN_C + (G_C + g + 1) * N_SSM])
    xst = xs.T
    cmb = cm.astype(BF16)
    cb = _dot_t(cmb, bm.astype(BF16))

    def spread(v, width):
        src = lax.broadcasted_iota(I32, (hpg, hpg * width), 0)
        dst = lax.shift_right_logical(lax.broadcasted_iota(I32, (hpg, hpg * width), 1), int(math.log2(width)))
        sel = (src == dst).astype(BF16)
        return _tree([_dot(p, sel) for p in _split3(v)], jnp.add)

    acs_w = spread(acs, q)
    decay_w = jnp.exp(acs_w[q - 1:q, :] - acs_w)
    eacs_w = jnp.exp(acs_w)
    xdt = (xs * spread(dt, P_C)).astype(BF16)
    low_half = lax.broadcasted_iota(I32, (q, 2 * P_C), 1) < P_C
    h_all = hst_ref[0, gheads]
    ys, h_new = [], []
    for k in range(hpg // 2):
        xpair = xdt[:, k * 2 * P_C:(k + 1) * 2 * P_C]
        yd = []
        for j in (2 * k, 2 * k + 1):
            blk = slice(j * q, (j + 1) * q)
            acs_row = acst[j:j + 1, :]
            a_last = acs_row[:, q - 1:q]
            lmat = jnp.exp(jnp.where(tril, acs_w[:, blk] - acs_row, -jnp.inf))
            yd.append(_dot((cb * lmat).astype(BF16), xpair))
            xht = xst[j * P_C:(j + 1) * P_C, :] * dtt[j:j + 1, :]
            st = _dot(xht.astype(BF16), (bm * decay_w[:, blk]).astype(BF16))
            h_new.append(jnp.exp(a_last) * h_all[j] + st)
        h_pair = h_all[2 * k:2 * k + 2].reshape(2 * P_C, N_SSM).astype(BF16)
        e_pair = jnp.where(low_half, eacs_w[:, 2 * k * q:(2 * k + 1) * q], eacs_w[:, (2 * k + 1) * q:(2 * k + 2) * q])
        ys.append(jnp.where(low_half, yd[0], yd[1]) + _dot_t(cmb, h_pair) * e_pair)
    hst_ref[0, gheads] = jnp.stack(h_new, axis=0)
    y = jnp.concatenate(ys, axis=1) + xs * dsk_ref[:, gcols]
    y = y * _silu(z_all[:, gcols])
    return _rms(y, cn_ref[:, gcols]).astype(BF16)


def ssd_mix(x, xbc, proj, z_col0, h0, norm_g, w_dt, dt_bias, a_log, d_skip, c_norm, w_out, qin, lv):
    n_seq, length, d = x.shape
    nchunk = length // qin
    col = lambda a: a.reshape(-1, 1).astype(F32)
    row = lambda a: a.reshape(1, -1).astype(F32)
    full2 = lambda s, c: (0, 0)
    conv_c = xbc.shape[-1]
    zw = D_IN_C // 2
    assert z_col0 % zw == 0
    in_specs = [
        pl.BlockSpec((1, qin, d), lambda s, c: (s, c, 0)),
        pl.BlockSpec((1, qin, conv_c), lambda s, c: (s, c, 0)),
        pl.BlockSpec((1, qin, zw), lambda s, c: (s, c, z_col0 // zw)),
        pl.BlockSpec((1, qin, zw), lambda s, c: (s, c, z_col0 // zw + 1)),
        pl.BlockSpec((1, d), full2),
        pl.BlockSpec((d, H_C), full2),
        pl.BlockSpec((H_C, d), full2),
        pl.BlockSpec((1, H_C), full2),
        pl.BlockSpec((H_C, 1), full2),
        pl.BlockSpec((1, H_C), full2),
        pl.BlockSpec((H_C, 1), full2),
        pl.BlockSpec((1, D_IN_C), full2),
        pl.BlockSpec((1, D_IN_C), full2),
        pl.BlockSpec((D_IN_C, d), full2),
        pl.BlockSpec((1, H_C, P_C, N_SSM), lambda s, c: (s, 0, 0, 0)),
    ]
    out, hst = pl.pallas_call(
        functools.partial(_ssd_body, qin=qin, lv=lv),
        grid=(n_seq, nchunk),
        in_specs=in_specs,
        out_specs=[pl.BlockSpec((1, qin, d), lambda s, c: (s, c, 0)),
                   pl.BlockSpec((1, H_C, P_C, N_SSM), lambda s, c: (s, 0, 0, 0))],
        out_shape=[jax.ShapeDtypeStruct((n_seq, length, d), F32),
                   jax.ShapeDtypeStruct((n_seq, H_C, P_C, N_SSM), F32)],
        compiler_params=_cparams("arbitrary", "arbitrary"),
        name="ssd_mix",
    )(x, xbc, proj, proj, row(norm_g), w_dt.astype(BF16), w_dt.T.astype(BF16),
      row(dt_bias), col(dt_bias), row(a_log), col(a_log),
      row(jnp.repeat(d_skip, P_C)), row(c_norm), w_out, h0)
    return out, hst


def _flat(a):
    return a.reshape(-1, a.shape[-1])


def _unflat(a, like):
    return a.reshape(like.shape[0], like.shape[1], a.shape[-1])


def kernel(x_prompt, x_sample, mem_prompt, state_a_conv, cache_b_k, cache_b_v, cache_b_kidx, state_c_conv, state_c_ssm, state_d_conv, state_ffn_conv, cache_mem_k, cache_mem_v, page_table, rel_bias, norm_mix, norm_mem, norm_ffn, norm_memtok, a_w_in, a_b_in, a_w_conv, a_b_conv, a_ln_g, a_ln_b, a_w_out, b_w_in, b_w_out, b_q_norm, b_k_norm, b_kidx_norm, c_w_in, c_w_conv, c_b_conv, c_dt_bias, c_a_log, c_d_skip, c_norm, c_w_out, d_w_in, d_w_conv, d_w_out, m_w_q, m_w_kv, m_w_o, m_q_norm, m_k_norm, f_w_in, f_w_conv, f_b_conv, f_w_out):
    n_p, l_p, d = x_prompt.shape
    n_s, l_s, _ = x_sample.shape
    bf = lambda w: w.astype(BF16)
    xs_pad = jnp.pad(x_sample, ((0, 0), (0, SAMPLE_PAD - l_s), (0, 0)))
    groups = [dict(x=x_prompt, sb=1, tm=256, tmf=512, lv=256, qin=CHUNK_C, lvq=CHUNK_C, prompt=True),
              dict(x=xs_pad, sb=n_s, tm=SAMPLE_PAD, tmf=SAMPLE_PAD, lv=l_s, qin=SAMPLE_PAD, lvq=l_s, prompt=False)]
    bt, btt = t5_tiles(rel_bias)
    mem_flat = _flat(mem_prompt)
    outs = {k: [[], []] for k in ("a", "bk", "bv", "bki", "cc", "cs", "d", "f")}
    m_kp, m_vp = [], []

    for i in range(DEPTH):
        j = i // 4
        kind = i % 4
        kv = linear(mem_flat, bf(m_w_kv[i]), g=norm_memtok[i], name="mem_kv")
        mk, _ = head_norm(kv, 0, d, m_k_norm[i], DH_M, name="mem_k_norm")
        m_kp.append(mk.reshape(n_p, N_MEM, H_M, DH_M))
        m_vp.append(kv[:, d:].reshape(n_p, N_MEM, H_M, DH_M))
        mk = m_kp[-1].transpose(0, 2, 1, 3)
        mv = m_vp[-1].transpose(0, 2, 1, 3)

        for gi, grp in enumerate(groups):
            x = grp["x"]
            n_seq, length, _ = x.shape
            prompt = grp["prompt"]
            zeros_state = lambda w, c: jnp.zeros((n_seq, w - 1, c), F32)
            xf = _flat(x)
            if kind == 0:
                st = zeros_state(W_A, d) if prompt else state_a_conv[j]
                x, nst = conv_mixer(x, st, norm_mix[i], bf(a_w_in[j]), a_b_in[j], a_w_conv[j],
                                    [a_b_conv[j], a_ln_g[j], a_ln_b[j]], bf(a_w_out[j]), _a_pre, _a_post,
                                    grp["sb"], grp["tm"], grp["lv"], "a_mixer")
                outs["a"][gi].append(nst)
            elif kind == 1:
                qn, kn, knb, v, vb, qi, kin, kinb, wi = b_proj(xf, norm_mix[i], b_w_in[j], b_q_norm[j],
                                                               b_k_norm[j], b_kidx_norm[j])
                kw = KV_B * DH_B
                r3 = lambda a: a.reshape(n_seq, length, a.shape[-1])
                if prompt:
                    vtb = vb.reshape(n_seq, length // PAGE_SIZE, PAGE_SIZE, kw).transpose(0, 1, 3, 2)
                    o = dsa_prompt(r3(qn), r3(qi), r3(wi), r3(knb), vtb, r3(kinb), btt)
                else:
                    am = dsa_sample_select(page_table, r3(qi), r3(wi), r3(kin),
                                           cache_b_kidx[j].transpose(0, 2, 1), grp["lv"])
                    o = dsa_sample_attend(page_table, r3(qn), r3(kn), r3(v), am, bt,
                                          cache_b_k[j].transpose(0, 2, 3, 1), cache_b_v[j].transpose(0, 2, 3, 1))
                outs["bk"][gi].append(r3(kn))
                outs["bv"][gi].append(r3(v))
                outs["bki"][gi].append(r3(kin))
                x = _unflat(linear(_flat(o), bf(b_w_out[j]), res=xf, name="b_out"), x)
            elif kind == 2:
                w_in = c_w_in[j]
                conv_c = D_IN_C + 2 * G_C * N_SSM
                st = zeros_state(W_C, conv_c) if prompt else state_c_conv[j]
                xbc, z, nst = c_in_conv(x, st, norm_mix[i], bf(w_in[:, D_IN_C:D_IN_C + conv_c]), bf(w_in[:, :D_IN_C]),
                                        c_w_conv[j], c_b_conv[j], grp["sb"], grp["tm"], grp["lv"])
                h0 = jnp.zeros((n_seq, H_C, P_C, N_SSM), F32) if prompt else state_c_ssm[j]
                x, hst = ssd_mix(x, xbc, z, 0, h0, norm_mix[i], w_in[:, D_IN_C + conv_c:],
                                 c_dt_bias[j], c_a_log[j], c_d_skip[j], c_norm[j], bf(c_w_out[j]),
                                 grp["qin"], grp["lvq"])
                outs["cc"][gi].append(nst)
                outs["cs"][gi].append(hst)
            else:
                st = zeros_state(W_D, d) if prompt else state_d_conv[j]
                x, nst = conv_mixer(x, st, norm_mix[i], bf(d_w_in[j]), None, d_w_conv[j], [], bf(d_w_out[j]),
                                    _d_pre, _d_post, grp["sb"], grp["tm"], grp["lv"], "d_mixer")
                outs["d"][gi].append(nst)

            if prompt:
                x = mem_fused(x, mk, mv, norm_mem[i], bf(m_w_q[i]), m_q_norm[i], bf(m_w_o[i]), 512)
            else:
                xf = _flat(x)
                qm = _unflat(linear(xf, bf(m_w_q[i]), g=norm_mem[i], name="mem_q"), x)
                om = mem_attn_cached(qm, cache_mem_k.reshape(DEPTH * n_seq, N_MEM, H_M, DH_M),
                                     cache_mem_v.reshape(DEPTH * n_seq, N_MEM, H_M, DH_M),
                                     m_q_norm[i], MEM_SEQS_PER_STEP, i * n_seq)
                x = _unflat(linear(_flat(om), bf(m_w_o[i]), res=xf, name="mem_o"), x)

            st = zeros_state(W_F, D_FF) if prompt else state_ffn_conv[i]
            x, nst = ffn_fused(x, st, norm_ffn[i], bf(f_w_in[i]), f_w_conv[i], f_b_conv[i], bf(f_w_out[i]),
                               1 if prompt else n_seq, grp["tmf"], grp["tmf"] if prompt else grp["lv"])
            outs["f"][gi].append(nst)
            grp["x"] = x

    yp = groups[0]["x"]
    ys = groups[1]["x"][:, :l_s]
    st = lambda key, gi: jnp.stack(outs[key][gi])
    kvshape = lambda a, n, l: a.reshape(a.shape[0], n, -1, a.shape[-1])[:, :, :l]
    b_k_p = kvshape(st("bk", 0), n_p, l_p).reshape(-1, n_p, l_p, KV_B, DH_B)
    b_v_p = kvshape(st("bv", 0), n_p, l_p).reshape(-1, n_p, l_p, KV_B, DH_B)
    b_ki_p = kvshape(st("bki", 0), n_p, l_p)
    b_k_s = kvshape(st("bk", 1), n_s, l_s).reshape(-1, n_s, l_s, KV_B, DH_B)
    b_v_s = kvshape(st("bv", 1), n_s, l_s).reshape(-1, n_s, l_s, KV_B, DH_B)
    b_ki_s = kvshape(st("bki", 1), n_s, l_s)
    return (yp, ys, st("a", 0), st("a", 1), b_k_p, b_v_p, b_ki_p, b_k_s, b_v_s, b_ki_s,
            st("cc", 0), st("cs", 0), st("cc", 1), st("cs", 1), st("d", 0), st("d", 1),
            st("f", 0), st("f", 1), jnp.stack(m_kp), jnp.stack(m_vp))
```

```python
import functools
import math

import jax
import jax.numpy as jnp
from jax import lax
from jax.experimental import pallas as pl
from jax.experimental.pallas import tpu as pltpu

F32 = jnp.float32
BF16 = jnp.bfloat16
I32 = jnp.int32

D_MODEL = 1024
DEPTH = 4
PAST_LEN = 8192
PAGE_SIZE = 128
W_A = 31
H_B = 16
DH_B = 64
KV_B = 4
G_B = H_B // KV_B
H_IDX = 8
D_IDX = 64
TOPK = 256
N_BUCKETS = 32
MAX_DIST = 128
D_IN_C = 2 * D_MODEL
P_C = 64
H_C = D_IN_C // P_C
G_C = 4
N_SSM = 128
W_C = 4
CHUNK_C = 128
W_D = 3
N_MEM = 256
H_M = 4
DH_M = D_MODEL // H_M
D_FF = 2816
W_F = 3
EPS = 1e-6

V7X_SUBLANES = 8
V7X_LANES = 128
V7X_VMEM_LIMIT_BYTES = 56 * 1024 * 1024

SAMPLE_PAD = V7X_SUBLANES
NEG_BIG = -1e30
INT_MIN = -(2 ** 31)
PAGES_PER_STEP = 32
MEM_SEQS_PER_STEP = 8


def _cparams(*sem):
    return pltpu.CompilerParams(dimension_semantics=sem, vmem_limit_bytes=V7X_VMEM_LIMIT_BYTES)


def _pick(n, cands):
    for c in cands:
        if n % c == 0:
            return c
    return n


def _rms(x, g):
    y = x * lax.rsqrt(jnp.mean(x * x, axis=-1, keepdims=True) + EPS)
    return y * g


def _dot(a, b):
    return jnp.dot(a, b, preferred_element_type=F32)


def _dot_t(a, b):
    return lax.dot_general(a, b, (((1,), (1,)), ((), ())), preferred_element_type=F32)


def _sigmoid(x):
    return 1.0 / (1.0 + jnp.exp(-x))


def _silu(x):
    return x * _sigmoid(x)


def _group_meansq(x, gsize):
    tm, c = x.shape
    x2 = x * x
    if gsize == c:
        return jnp.mean(x2, axis=-1, keepdims=True)
    if gsize % V7X_LANES == 0:
        parts = []
        for h in range(c // gsize):
            ms = jnp.mean(x2[:, h * gsize:(h + 1) * gsize], axis=-1, keepdims=True)
            parts.append(jnp.broadcast_to(ms, (tm, gsize)))
        return jnp.concatenate(parts, axis=-1)
    shift = int(math.log2(gsize))
    r = lax.shift_right_logical(lax.broadcasted_iota(I32, (c, c), 0), shift)
    q = lax.shift_right_logical(lax.broadcasted_iota(I32, (c, c), 1), shift)
    bd = (r == q).astype(BF16)
    hi = x2.astype(BF16)
    lo = (x2 - hi.astype(F32)).astype(BF16)
    return (_dot(hi, bd) + _dot(lo, bd)) * (1.0 / gsize)


def _group_rms(x, gain, gsize):
    return (x * lax.rsqrt(_group_meansq(x, gsize) + EPS)) * gain


def _linear_body(*refs, norm, bias, res):
    it = iter(refs)
    x_ref = next(it)
    g_ref = next(it) if norm else None
    w_ref = next(it)
    b_ref = next(it) if bias else None
    r_ref = next(it) if res else None
    o_ref = next(it)
    xn_ref = next(it)

    @pl.when(pl.program_id(1) == 0)
    def _():
        x = x_ref[...].astype(F32)
        if norm:
            x = _rms(x, g_ref[...])
        xn_ref[...] = x.astype(BF16)

    acc = _dot(xn_ref[...], w_ref[...])
    if bias:
        acc = acc + b_ref[...]
    if res:
        acc = acc + r_ref[...]
    o_ref[...] = acc.astype(o_ref.dtype)


def linear(x, w, *, g=None, b=None, res=None, out_dtype=F32, name="linear"):
    m, k = x.shape
    n = w.shape[1]
    tm = _pick(m, (1024, 512, 256, 128))
    tn = _pick(n, (1024, 1408, 512, 256, 128))
    in_specs = [pl.BlockSpec((tm, k), lambda i, j: (i, 0))]
    args = [x]
    if g is not None:
        in_specs.append(pl.BlockSpec((1, k), lambda i, j: (0, 0)))
        args.append(g.reshape(1, k).astype(F32))
    in_specs.append(pl.BlockSpec((k, tn), lambda i, j: (0, j)))
    args.append(w)
    if b is not None:
        in_specs.append(pl.BlockSpec((1, tn), lambda i, j: (0, j)))
        args.append(b.reshape(1, n).astype(F32))
    if res is not None:
        in_specs.append(pl.BlockSpec((tm, tn), lambda i, j: (i, j)))
        args.append(res)
    return pl.pallas_call(
        functools.partial(_linear_body, norm=g is not None, bias=b is not None, res=res is not None),
        grid=(m // tm, n // tn),
        in_specs=in_specs,
        out_specs=pl.BlockSpec((tm, tn), lambda i, j: (i, j)),
        out_shape=jax.ShapeDtypeStruct((m, n), out_dtype),
        scratch_shapes=[pltpu.VMEM((tm, k), BF16)],
        compiler_params=_cparams("arbitrary", "arbitrary"),
        name=name,
    )(*args)


def _head_norm_body(x_ref, g_ref, o_ref, ob_ref, *, gsize):
    y = _group_rms(x_ref[...], g_ref[...], gsize)
    o_ref[...] = y
    ob_ref[...] = y.astype(BF16)


def head_norm(x, col0, width, gain, gsize, name="head_norm"):
    m = x.shape[0]
    tm = _pick(m, (512, 256, 128))
    assert col0 % width == 0
    gt = jnp.tile(gain.astype(F32), width // gsize).reshape(1, width)
    return pl.pallas_call(
        functools.partial(_head_norm_body, gsize=gsize),
        grid=(m // tm,),
        in_specs=[pl.BlockSpec((tm, width), lambda i: (i, col0 // width)),
                  pl.BlockSpec((1, width), lambda i: (0, 0))],
        out_specs=[pl.BlockSpec((tm, width), lambda i: (i, 0)),
                   pl.BlockSpec((tm, width), lambda i: (i, 0))],
        out_shape=[jax.ShapeDtypeStruct((m, width), F32), jax.ShapeDtypeStruct((m, width), BF16)],
        compiler_params=_cparams("arbitrary"),
        name=name,
    )(x, gt)


B_QW = H_B * DH_B
B_KW = KV_B * DH_B


def _b_proj_body(x_ref, ng_ref, wm_ref, wt_ref, gq_ref, gk_ref, gki_ref,
                 qn_ref, kn_ref, knb_ref, v_ref, vb_ref, qi_ref, kin_ref, kinb_ref, wi_ref):
    xn = _rms(x_ref[...], ng_ref[...]).astype(BF16)
    pm = _dot(xn, wm_ref[...])
    pt = _dot(xn, wt_ref[...])
    qn_ref[...] = _group_rms(pm[:, 0:B_QW], gq_ref[...], DH_B)
    kn = _group_rms(pm[:, B_QW:B_QW + B_KW], gk_ref[...], DH_B)
    kn_ref[...] = kn
    knb_ref[...] = kn.astype(BF16)
    v = pm[:, B_QW + B_KW:B_QW + 2 * B_KW]
    v_ref[...] = v
    vb_ref[...] = v.astype(BF16)
    qi_ref[...] = pm[:, B_QW + 2 * B_KW:]
    kin = _group_rms(pt[:, 0:D_IDX], gki_ref[...], D_IDX)
    kin_ref[...] = kin
    kinb_ref[...] = kin.astype(BF16)
    wi_ref[...] = pt[:, D_IDX:D_IDX + H_IDX]


def b_proj(x, norm_g, w_in, q_gain, k_gain, ki_gain):
    m, d = x.shape
    tm = _pick(m, (512, 256))
    n_main = B_QW + 2 * B_KW + H_IDX * D_IDX
    w_tail = jnp.pad(w_in[:, n_main:], ((0, 0), (0, V7X_LANES - (w_in.shape[1] - n_main))))
    const = lambda i: (0, 0)
    rowmap = lambda i: (i, 0)
    widths = [(B_QW, F32), (B_KW, F32), (B_KW, BF16), (B_KW, F32), (B_KW, BF16), (H_IDX * D_IDX, F32),
              (D_IDX, F32), (D_IDX, BF16), (H_IDX, F32)]
    return pl.pallas_call(
        _b_proj_body,
        grid=(m // tm,),
        in_specs=[pl.BlockSpec((tm, d), rowmap),
                  pl.BlockSpec((1, d), const),
                  pl.BlockSpec((d, n_main), const),
                  pl.BlockSpec((d, V7X_LANES), const),
                  pl.BlockSpec((1, B_QW), const),
                  pl.BlockSpec((1, B_KW), const),
                  pl.BlockSpec((1, D_IDX), const)],
        out_specs=[pl.BlockSpec((tm, w), rowmap) for w, _ in widths],
        out_shape=[jax.ShapeDtypeStruct((m, w), dt) for w, dt in widths],
        compiler_params=_cparams("arbitrary"),
        name="b_proj",
    )(x, norm_g.reshape(1, d).astype(F32), w_in[:, :n_main].astype(BF16), w_tail.astype(BF16),
      jnp.tile(q_gain.astype(F32), H_B).reshape(1, B_QW), jnp.tile(k_gain.astype(F32), KV_B).reshape(1, B_KW),
      ki_gain.astype(F32).reshape(1, D_IDX))


def _a_pre(tiles):
    a, g = tiles
    return a * _sigmoid(g)


def _a_post(y, tiles, pars):
    bc, lng, lnb = pars
    y = y + bc
    mu = jnp.mean(y, axis=-1, keepdims=True)
    yc = y - mu
    yn = yc * lax.rsqrt(jnp.mean(yc * yc, axis=-1, keepdims=True) + EPS)
    return _silu(yn * lng + lnb)


def _d_pre(tiles):
    return tiles[1] * tiles[2]


def _d_post(y, tiles, pars):
    return tiles[0] * y


def _conv_mixer_body(*refs, n_par, n_parts, pre, post, width, sb, tm, lv, halo, realign):
    x_ref, ng_ref, win_ref, bin_ref, st_ref, wc_ref = refs[:6]
    par_refs = refs[6:6 + n_par]
    wout_ref, o_ref, nst_ref, abuf, rbuf = refs[6 + n_par:]
    t = pl.program_id(1)
    base = halo - (width - 1)
    d = x_ref.shape[-1]
    c = abuf.shape[-1]

    @pl.when(t == 0)
    def _():
        abuf[:, base:halo, :] = st_ref[...]

    x = x_ref[...].reshape(sb * tm, d)
    proj = _dot(_rms(x, ng_ref[...]).astype(BF16), win_ref[...]) + bin_ref[...]
    parts = [proj[:, p * c:(p + 1) * c] for p in range(n_parts)]
    abuf[:, halo:halo + tm, :] = pre(parts).reshape(sb, tm, c)
    acc = None
    for r in range(V7X_SUBLANES):
        taps = [k for k in range(width) if (base + k) % V7X_SUBLANES == r]
        if not taps:
            continue
        rows = V7X_SUBLANES * max((base + k) // V7X_SUBLANES for k in taps) + tm
        for k in taps:
            q0 = V7X_SUBLANES * ((base + k) // V7X_SUBLANES)
            if r == 0 or len(taps) == 1 or not realign:
                shifted = abuf[:, r + q0:r + q0 + tm, :]
            else:
                if k == taps[0]:
                    rbuf[r, :, 0:rows, :] = abuf[:, r:r + rows, :]
                shifted = rbuf[r, :, q0:q0 + tm, :]
            term = wc_ref[k:k + 1, :] * shifted
            acc = term if acc is None else acc + term
    u = post(acc.reshape(sb * tm, c), parts, [p[...] for p in par_refs]).astype(BF16)
    o_ref[...] = (x + _dot(u, wout_ref[...])).reshape(sb, tm, d)
    tail = abuf[:, base + lv:base + lv + width - 1, :]
    abuf[:, base:halo, :] = tail
    nst_ref[...] = tail


def conv_mixer(x, state, norm_g, w_in, b_in, w_conv, params, w_out, pre, post, sb, tm, lv, name):
    n_seq, length, d = x.shape
    width, c = w_conv.shape
    n_parts = w_in.shape[1] // c
    halo = -(-(width - 1) // V7X_SUBLANES) * V7X_SUBLANES
    realign = width > V7X_SUBLANES and tm >= 8 * V7X_SUBLANES
    const = lambda s, t: (0, 0)
    b_in = jnp.zeros((n_parts * c,), F32) if b_in is None else b_in
    in_specs = [pl.BlockSpec((sb, tm, d), lambda s, t: (s, t, 0)),
                pl.BlockSpec((1, d), const),
                pl.BlockSpec((d, n_parts * c), const),
                pl.BlockSpec((1, n_parts * c), const),
                pl.BlockSpec((sb, width - 1, c), lambda s, t: (s, 0, 0)),
                pl.BlockSpec((width, c), const)]
    in_specs += [pl.BlockSpec((1, c), const) for _ in params]
    in_specs.append(pl.BlockSpec((c, d), const))
    return pl.pallas_call(
        functools.partial(_conv_mixer_body, n_par=len(params), n_parts=n_parts, pre=pre, post=post,
                          width=width, sb=sb, tm=tm, lv=lv, halo=halo, realign=realign),
        grid=(n_seq // sb, length // tm),
        in_specs=in_specs,
        out_specs=[pl.BlockSpec((sb, tm, d), lambda s, t: (s, t, 0)),
                   pl.BlockSpec((sb, width - 1, c), lambda s, t: (s, 0, 0))],
        out_shape=[jax.ShapeDtypeStruct((n_seq, length, d), F32),
                   jax.ShapeDtypeStruct((n_seq, width - 1, c), F32)],
        scratch_shapes=[pltpu.VMEM((sb, halo + tm, c), F32),
                        pltpu.VMEM((V7X_SUBLANES, sb, halo + tm, c) if realign
                                   else (1, 1, V7X_SUBLANES, V7X_LANES), F32)],
        compiler_params=_cparams("arbitrary", "arbitrary"),
        name=name,
    )(x, norm_g.reshape(1, d).astype(F32), w_in, b_in.reshape(1, -1).astype(F32), state, w_conv.astype(F32),
      *[p.reshape(1, c).astype(F32) for p in params], w_out)


C_CONV_CHUNK = 1024


def _c_in_body(x_ref, ng_ref, wx_ref, wz_ref, st_ref, wc_ref, bc_ref, xbc_ref, z_ref, nst_ref, abuf, *, sb, tm, lv, halo):
    t = pl.program_id(1)
    base = halo - (W_C - 1)
    d = x_ref.shape[-1]
    conv_c = abuf.shape[-1]

    @pl.when(t == 0)
    def _():
        abuf[:, base:halo, :] = st_ref[...]

    xn = _rms(x_ref[...].reshape(sb * tm, d), ng_ref[...]).astype(BF16)
    for c0 in range(0, conv_c, C_CONV_CHUNK):
        cs = slice(c0, c0 + C_CONV_CHUNK)
        abuf[:, halo:halo + tm, cs] = _dot(xn, wx_ref[:, cs]).reshape(sb, tm, C_CONV_CHUNK)
        y = None
        for k in range(W_C):
            term = wc_ref[k:k + 1, cs] * abuf[:, base + k:base + k + tm, cs]
            y = term if y is None else y + term
        xbc_ref[:, :, cs] = _silu(y + bc_ref[:, cs])
    z_ref[...] = _dot(xn, wz_ref[...]).reshape(sb, tm, z_ref.shape[-1])
    tail = abuf[:, base + lv:base + lv + W_C - 1, :]
    abuf[:, base:halo, :] = tail
    nst_ref[...] = tail


def c_in_conv(x, state, norm_g, w_xbc, w_z, w_conv, b_conv, sb, tm, lv):
    n_seq, length, d = x.shape
    conv_c = w_xbc.shape[1]
    halo = V7X_SUBLANES
    const = lambda s, t: (0, 0)
    tile = lambda w: pl.BlockSpec((sb, tm, w), lambda s, t: (s, t, 0))
    stt = pl.BlockSpec((sb, W_C - 1, conv_c), lambda s, t: (s, 0, 0))
    return pl.pallas_call(
        functools.partial(_c_in_body, sb=sb, tm=tm, lv=lv, halo=halo),
        grid=(n_seq // sb, length // tm),
        in_specs=[tile(d), pl.BlockSpec((1, d), const), pl.BlockSpec((d, conv_c), const),
                  pl.BlockSpec((d, D_IN_C), const), stt, pl.BlockSpec((W_C, conv_c), const),
                  pl.BlockSpec((1, conv_c), const)],
        out_specs=[tile(conv_c), tile(D_IN_C), stt],
        out_shape=[jax.ShapeDtypeStruct((n_seq, length, conv_c), F32),
                   jax.ShapeDtypeStruct((n_seq, length, D_IN_C), F32),
                   jax.ShapeDtypeStruct((n_seq, W_C - 1, conv_c), F32)],
        scratch_shapes=[pltpu.VMEM((sb, halo + tm, conv_c), F32)],
        compiler_params=_cparams("arbitrary", "arbitrary"),
        name="c_in_conv",
    )(x, norm_g.reshape(1, d).astype(F32), w_xbc, w_z, state, w_conv.astype(F32),
      b_conv.reshape(1, conv_c).astype(F32))


FFN_CHUNK = 11 * V7X_LANES


def _ffn_body(x_ref, ng_ref, win_ref, st_ref, wc_ref, bc_ref, wout_ref, o_ref, nst_ref, abuf, *, sb, tm, lv, halo):
    t = pl.program_id(1)
    base = halo - (W_F - 1)
    d = x_ref.shape[-1]

    @pl.when(t == 0)
    def _():
        abuf[:, base:halo, :] = st_ref[...]

    x = x_ref[...].reshape(sb * tm, d)
    xn = _rms(x, ng_ref[...]).astype(BF16)
    out = x
    for c0 in range(0, D_FF, FFN_CHUNK):
        cs = slice(c0, c0 + FFN_CHUNK)
        abuf[:, halo:halo + tm, cs] = _dot(xn, win_ref[:, cs]).reshape(sb, tm, FFN_CHUNK)
        g = _dot(xn, win_ref[:, D_FF + c0:D_FF + c0 + FFN_CHUNK])
        y = None
        for k in range(W_F):
            term = wc_ref[k:k + 1, cs] * abuf[:, base + k:base + k + tm, cs]
            y = term if y is None else y + term
        y = y.reshape(sb * tm, FFN_CHUNK) + bc_ref[:, cs]
        out = out + _dot((_silu(y) * g).astype(BF16), wout_ref[cs, :])
    o_ref[...] = out.reshape(sb, tm, d)
    tail = abuf[:, base + lv:base + lv + W_F - 1, :]
    abuf[:, base:halo, :] = tail
    nst_ref[...] = tail


def ffn_fused(x, state, norm_g, w_in, w_conv, b_conv, w_out, sb, tm, lv):
    n_seq, length, d = x.shape
    halo = V7X_SUBLANES
    const = lambda s, t: (0, 0)
    resident = dict(pipeline_mode=pl.Buffered(1))
    return pl.pallas_call(
        functools.partial(_ffn_body, sb=sb, tm=tm, lv=lv, halo=halo),
        grid=(n_seq // sb, length // tm),
        in_specs=[pl.BlockSpec((sb, tm, d), lambda s, t: (s, t, 0)),
                  pl.BlockSpec((1, d), const),
                  pl.BlockSpec((d, 2 * D_FF), const, **resident),
                  pl.BlockSpec((sb, W_F - 1, D_FF), lambda s, t: (s, 0, 0)),
                  pl.BlockSpec((W_F, D_FF), const),
                  pl.BlockSpec((1, D_FF), const),
                  pl.BlockSpec((D_FF, d), const, **resident)],
        out_specs=[pl.BlockSpec((sb, tm, d), lambda s, t: (s, t, 0)),
                   pl.BlockSpec((sb, W_F - 1, D_FF), lambda s, t: (s, 0, 0))],
        out_shape=[jax.ShapeDtypeStruct((n_seq, length, d), F32),
                   jax.ShapeDtypeStruct((n_seq, W_F - 1, D_FF), F32)],
        scratch_shapes=[pltpu.VMEM((sb, halo + tm, D_FF), F32)],
        compiler_params=_cparams("arbitrary", "arbitrary"),
        name="ffn",
    )(x, norm_g.reshape(1, d).astype(F32), w_in, state, w_conv.astype(F32),
      b_conv.reshape(1, D_FF).astype(F32), w_out)


def _softmax_rows(s):
    p = jnp.exp(s - jnp.max(s, axis=-1, keepdims=True))
    return p / jnp.sum(p, axis=-1, keepdims=True)


def _mem_attend(q, k, v, gain, token_major):
    tm = q.shape[0]
    qh = [(_rms(q[:, h * DH_M:(h + 1) * DH_M], gain) * (DH_M ** -0.5)).astype(BF16) for h in range(H_M)]
    if not token_major:
        outs = []
        for h in range(H_M):
            p = _softmax_rows(_dot_t(qh[h], k[h].astype(BF16)))
            outs.append(_dot(p.astype(BF16), v[h].astype(BF16)))
    else:
        k_all = k.reshape(N_MEM * H_M, DH_M).astype(BF16)
        v_all = v.reshape(N_MEM * H_M, DH_M).astype(BF16)
        s = _dot_t(jnp.concatenate(qh, axis=0), k_all)
        col_head = lax.broadcasted_iota(I32, s.shape, 1) & (H_M - 1)
        row_head = lax.shift_right_logical(lax.broadcasted_iota(I32, s.shape, 0), int(math.log2(tm)))
        p = _softmax_rows(jnp.where(col_head == row_head, s, NEG_BIG))
        o = _dot(p.astype(BF16), v_all)
        outs = [o[h * tm:(h + 1) * tm] for h in range(H_M)]
    return jnp.concatenate(outs, axis=1).astype(BF16)


def _mem_attn_body(q_ref, k_ref, v_ref, g_ref, o_ref, *, sb):
    for s in range(sb):
        o_ref[s] = _mem_attend(q_ref[s], k_ref[s], v_ref[s], g_ref[...], True)


def mem_attn_cached(q, k, v, q_gain, sb, kv_seq0):
    n_seq, tm, d = q.shape
    kv_block = (sb,) + k.shape[1:]
    kv_map = lambda s: (kv_seq0 // sb + s, 0, 0, 0)
    return pl.pallas_call(
        functools.partial(_mem_attn_body, sb=sb),
        grid=(n_seq // sb,),
        in_specs=[pl.BlockSpec((sb, tm, d), lambda s: (s, 0, 0)),
                  pl.BlockSpec(kv_block, kv_map),
                  pl.BlockSpec(kv_block, kv_map),
                  pl.BlockSpec((1, DH_M), lambda s: (0, 0))],
        out_specs=pl.BlockSpec((sb, tm, d), lambda s: (s, 0, 0)),
        out_shape=jax.ShapeDtypeStruct((n_seq, tm, d), BF16),
        compiler_params=_cparams("arbitrary"),
        name="mem_attn",
    )(q, k, v, q_gain.reshape(1, DH_M).astype(F32))


def _mem_fused_body(x_ref, ng_ref, wq_ref, k_ref, v_ref, g_ref, wo_ref, o_ref):
    x = x_ref[0]
    q = _dot(_rms(x, ng_ref[...]).astype(BF16), wq_ref[...])
    o = _mem_attend(q, k_ref[0], v_ref[0], g_ref[...], False)
    o_ref[0] = x + _dot(o, wo_ref[...])


def mem_fused(x, k, v, norm_g, w_q, q_gain, w_o, tm):
    n_seq, length, d = x.shape
    const = lambda s, t: (0, 0)
    kv_block = (1,) + k.shape[1:]
    return pl.pallas_call(
        _mem_fused_body,
        grid=(n_seq, length // tm),
        in_specs=[pl.BlockSpec((1, tm, d), lambda s, t: (s, t, 0)),
                  pl.BlockSpec((1, d), const),
                  pl.BlockSpec((d, d), const),
                  pl.BlockSpec(kv_block, lambda s, t: (s, 0, 0, 0)),
                  pl.BlockSpec(kv_block, lambda s, t: (s, 0, 0, 0)),
                  pl.BlockSpec((1, DH_M), const),
                  pl.BlockSpec((d, d), const)],
        out_specs=pl.BlockSpec((1, tm, d), lambda s, t: (s, t, 0)),
        out_shape=jax.ShapeDtypeStruct((n_seq, length, d), F32),
        compiler_params=_cparams("arbitrary", "arbitrary"),
        name="mem_fused",
    )(x, norm_g.reshape(1, d).astype(F32), w_q, k, v, q_gain.reshape(1, DH_M).astype(F32), w_o)


HEADS_PER_PAIR = 2 * G_B


def _t5_lookup(tab_ref, h, dist):
    dist = jnp.maximum(dist, 0)
    max_exact = N_BUCKETS // 2
    df = jnp.maximum(dist, 1).astype(F32)
    large = max_exact + (jnp.log(df / max_exact) / math.log(MAX_DIST / max_exact)
                         * (N_BUCKETS - max_exact)).astype(I32)
    large = jnp.minimum(large, N_BUCKETS - 1)
    bucket = jnp.where(dist < max_exact, dist, large)
    acc = jnp.zeros(dist.shape, F32)
    for b in range(N_BUCKETS):
        acc = jnp.where(bucket == b, tab_ref[b, h], acc)
    return acc


def _t5_tiles_body(tab_ref, o_ref, ot_ref):
    h = pl.program_id(0)
    r = lax.broadcasted_iota(I32, (PAGE_SIZE, PAGE_SIZE), 0)
    c = lax.broadcasted_iota(I32, (PAGE_SIZE, PAGE_SIZE), 1)
    o_ref[0, :, 0:PAGE_SIZE] = _t5_lookup(tab_ref, h, r - c)
    o_ref[0, :, PAGE_SIZE:2 * PAGE_SIZE] = _t5_lookup(tab_ref, h, PAGE_SIZE + r - c)
    ot_ref[0, 0] = _t5_lookup(tab_ref, h, c - r)
    ot_ref[0, 1] = _t5_lookup(tab_ref, h, PAGE_SIZE + c - r)


def t5_tiles(table):
    return pl.pallas_call(
        _t5_tiles_body,
        grid=(H_B,),
        in_specs=[pl.BlockSpec(memory_space=pltpu.SMEM)],
        out_specs=[pl.BlockSpec((1, PAGE_SIZE, 2 * PAGE_SIZE), lambda h: (h, 0, 0)),
                   pl.BlockSpec((1, 2, PAGE_SIZE, PAGE_SIZE),
                                lambda h: (h // HEADS_PER_PAIR, 0, 0, h % HEADS_PER_PAIR))],
        out_shape=[jax.ShapeDtypeStruct((H_B, PAGE_SIZE, 2 * PAGE_SIZE), F32),
                   jax.ShapeDtypeStruct((H_B // HEADS_PER_PAIR, 2, PAGE_SIZE, HEADS_PER_PAIR * PAGE_SIZE), F32)],
        compiler_params=_cparams("arbitrary"),
        name="t5_tiles",
    )(table.astype(F32))


def _sort_key(score):
    score = jnp.where(score == 0.0, 0.0, score)
    u = lax.bitcast_convert_type(score, I32)
    return jnp.where(u < 0, u ^ 0x7FFFFFFF, u)


def _kth_largest(count_ge, shape, topk):
    def try_cand(cand, cur):
        return jnp.where(count_ge(cand) >= topk, cand, cur)

    t0 = try_cand(jnp.zeros(shape, I32), jnp.full(shape, INT_MIN, I32))

    def step(b, cur):
        cand = cur + lax.shift_left(jnp.int32(1), jnp.int32(30) - b)
        return try_cand(cand, cur)

    return lax.fori_loop(0, 31, step, t0)


def _strict_upper(n):
    r = lax.broadcasted_iota(I32, (n, n), 0)
    c = lax.broadcasted_iota(I32, (n, n), 1)
    return (r < c).astype(BF16)


def _strict_lower(n):
    r = lax.broadcasted_iota(I32, (n, n), 0)
    c = lax.broadcasted_iota(I32, (n, n), 1)
    return (c < r).astype(BF16)


def _tree(parts, op):
    parts = list(parts)
    while len(parts) > 1:
        nxt = [op(parts[k], parts[k + 1]) for k in range(0, len(parts) - 1, 2)]
        if len(parts) % 2:
            nxt.append(parts[-1])
        parts = nxt
    return parts[0]


def _fold_rows(x, op):
    return _tree([x[r:r + V7X_SUBLANES] for r in range(0, x.shape[0], V7X_SUBLANES)], op)


LOOP_UNROLL = 4


def _loop_tiles(n, body, carry, unroll=LOOP_UNROLL):
    shift = int(math.log2(unroll))
    n_main = lax.shift_right_logical(n, shift)
    carry = lax.fori_loop(0, n_main, lambda t, c: body(t * unroll, unroll, c), carry)
    return lax.fori_loop(n_main * unroll, n, lambda j, c: body(j, 1, c), carry)


def _dsa_prompt_body(q_ref, qi_ref, wi_ref, k_ref, vt_ref, ki_ref, bt_ref, o_ref,
                     key_ref, am_ref, lg_ref, acc_ref, ot_ref, *, topk):
    i = pl.program_id(1)
    nj = i + 1
    qb = PAGE_SIZE
    kpos = lax.broadcasted_iota(I32, (qb, qb), 0)
    qpos = lax.broadcasted_iota(I32, (qb, qb), 1)

    def kslice(j0, u=1):
        return pl.ds(pl.multiple_of(j0 * qb, qb), u * qb)

    qit = qi_ref[0].T
    qit = jnp.concatenate([qit[h * D_IDX:(h + 1) * D_IDX] for h in range(H_IDX)], axis=1).astype(BF16)
    wsc = (wi_ref[0] * (H_IDX ** -0.5)) * (D_IDX ** -0.5)
    wt = jnp.concatenate([wsc, jnp.zeros((qb, qb - H_IDX), F32)], axis=1).T
    wrow = jnp.concatenate([wt[h:h + 1] for h in range(H_IDX)], axis=1)

    def idx_step(j0, u, carry):
        s = jnp.maximum(_dot(ki_ref[0, kslice(j0, u), :], qit), 0.0) * wrow
        sc = s[:, 0:qb]
        for h in range(1, H_IDX):
            sc = sc + s[:, h * qb:(h + 1) * qb]
        key = _sort_key(sc)
        for t in range(u):
            valid = (j0 + t < i) | (kpos <= qpos)
            key_ref[j0 + t] = jnp.where(valid, key[t * qb:(t + 1) * qb], INT_MIN)
        return carry

    _loop_tiles(nj, idx_step, 0)

    def count_where(pred):
        def cstep(j0, u, c):
            for t in range(u):
                c = jnp.where(pred(key_ref[j0 + t]), c + 1.0, c)
            return c
        c = _loop_tiles(nj, cstep, jnp.zeros((qb, qb), F32))
        return jnp.sum(_fold_rows(c, jnp.add), axis=0, keepdims=True)

    thr = _kth_largest(lambda cand: count_where(lambda key: key >= cand), (1, qb), topk)
    n_tie = topk - count_where(lambda key: key > thr)
    n_eq = count_where(lambda key: (key == thr) & (key != INT_MIN))
    need_rank = jnp.max(n_eq - n_tie) > 0.0

    @pl.when(jnp.logical_not(need_rank))
    def _():
        def mask_step(j0, u, carry):
            for t in range(u):
                key = key_ref[j0 + t]
                am_ref[j0 + t] = jnp.where((key >= thr) & (key != INT_MIN), 0.0, NEG_BIG)
            return carry
        _loop_tiles(nj, mask_step, 0)

    @pl.when(need_rank)
    def _():
        lower = _strict_lower(qb)

        def mask_step(j, run):
            key = key_ref[j]
            eq = (key == thr) & (key != INT_MIN)
            eqf = jnp.where(eq, 1.0, 0.0)
            rank = _dot(lower, eqf.astype(BF16)) + run
            sel = (key > thr) | (eq & (rank < n_tie))
            am_ref[j] = jnp.where(sel, 0.0, NEG_BIG)
            return run + jnp.sum(_fold_rows(eqf, jnp.add), axis=0, keepdims=True)

        lax.fori_loop(0, nj, mask_step, jnp.zeros((1, qb), F32))

    qt = q_ref[0].T
    pw = HEADS_PER_PAIR * qb
    gw = G_B * qb
    for pp in range(KV_B // 2):
        def group_qt(n):
            return jnp.concatenate([qt[h * DH_B:(h + 1) * DH_B] for h in range(n * G_B, (n + 1) * G_B)],
                                   axis=1)
        zero = jnp.zeros((DH_B, gw), F32)
        rhs = jnp.concatenate([jnp.concatenate([group_qt(2 * pp), zero], axis=1),
                               jnp.concatenate([zero, group_qt(2 * pp + 1)], axis=1)], axis=0)
        rhs = (rhs * (DH_B ** -0.5)).astype(BF16)
        klanes = slice(pp * 2 * DH_B, (pp + 1) * 2 * DH_B)
        far = bt_ref[pp, 1, 0:1, :]

        def tile_logits(j0, u, bias, mx):
            s = _dot(k_ref[0, kslice(j0, u), klanes], rhs)
            for t in range(u):
                st = s[t * qb:(t + 1) * qb] + bias + jnp.concatenate([am_ref[j0 + t]] * HEADS_PER_PAIR, axis=1)
                lg_ref[j0 + t] = st
                mx = jnp.maximum(mx, _fold_rows(st, jnp.maximum))
            return mx

        neg = jnp.full((V7X_SUBLANES, pw), NEG_BIG, F32)
        mx = _loop_tiles(jnp.maximum(i - 1, 0), lambda j0, u, m: tile_logits(j0, u, far, m), neg)
        mx = lax.cond(i >= 1, lambda m: tile_logits(i - 1, 1, bt_ref[pp, 1], m), lambda m: m, mx)
        mx = tile_logits(i, 1, bt_ref[pp, 0], mx)
        m = jnp.max(mx, axis=0, keepdims=True)
        acc_ref[...] = jnp.zeros(acc_ref.shape, F32)

        def p2(j0, u, l):
            ps = []
            for t in range(u):
                p = jnp.exp(lg_ref[j0 + t] - m)
                l = l + _fold_rows(p, jnp.add)
                ps.append(p.astype(BF16))
            pb = jnp.concatenate(ps, axis=0)
            vt = jnp.concatenate([vt_ref[0, j0 + t] for t in range(u)], axis=1)
            for gg in range(2):
                n = 2 * pp + gg
                acc_ref[gg] += _dot(vt[n * DH_B:(n + 1) * DH_B, :], pb[:, gg * gw:(gg + 1) * gw])
            return l

        l = _loop_tiles(nj, p2, jnp.zeros((V7X_SUBLANES, pw), F32))
        lsum = jnp.sum(l, axis=0, keepdims=True)
        for gg in range(2):
            o_t = acc_ref[gg] / lsum[:, gg * gw:(gg + 1) * gw]
            for hh in range(G_B):
                h = (2 * pp + gg) * G_B + hh
                ot_ref[h * DH_B:(h + 1) * DH_B, :] = o_t[:, hh * qb:(hh + 1) * qb]
    o_ref[0] = ot_ref[...].T.astype(o_ref.dtype)


def dsa_prompt(qn, qi, wi, kb, vtb, kib, btt):
    n_seq, length, _ = qn.shape
    nkb = length // PAGE_SIZE
    qmap = lambda s, i: (s, i, 0)
    kmap = lambda s, i: (s, 0, 0)
    pw = HEADS_PER_PAIR * PAGE_SIZE
    return pl.pallas_call(
        functools.partial(_dsa_prompt_body, topk=min(TOPK, length // 4)),
        grid=(n_seq, nkb),
        in_specs=[pl.BlockSpec((1, PAGE_SIZE, H_B * DH_B), qmap),
                  pl.BlockSpec((1, PAGE_SIZE, H_IDX * D_IDX), qmap),
                  pl.BlockSpec((1, PAGE_SIZE, H_IDX), qmap),
                  pl.BlockSpec((1, length, KV_B * DH_B), kmap),
                  pl.BlockSpec((1, nkb, KV_B * DH_B, PAGE_SIZE), lambda s, i: (s, 0, 0, 0)),
                  pl.BlockSpec((1, length, D_IDX), kmap),
                  pl.BlockSpec((H_B // HEADS_PER_PAIR, 2, PAGE_SIZE, pw), lambda s, i: (0, 0, 0, 0))],
        out_specs=pl.BlockSpec((1, PAGE_SIZE, H_B * DH_B), qmap),
        out_shape=jax.ShapeDtypeStruct((n_seq, length, H_B * DH_B), BF16),
        scratch_shapes=[pltpu.VMEM((nkb, PAGE_SIZE, PAGE_SIZE), I32),
                        pltpu.VMEM((nkb, PAGE_SIZE, PAGE_SIZE), F32),
                        pltpu.VMEM((nkb, PAGE_SIZE, pw), F32),
                        pltpu.VMEM((2, DH_B, G_B * PAGE_SIZE), F32),
                        pltpu.VMEM((H_B * DH_B, PAGE_SIZE), F32)],
        compiler_params=_cparams("arbitrary", "arbitrary"),
        name="dsa_prompt",
    )(qn, qi, wi, kb, vtb, kib, btt)


N_PAGES = PAST_LEN // PAGE_SIZE
N_KTILES = N_PAGES + 1
N_KTILES_PAD = -(-N_KTILES // PAGES_PER_STEP) * PAGES_PER_STEP


def _dsa_sample_sel_body(pt_ref, qi_ref, wi_ref, kin_ref, *rest, lv, topk):
    page_refs = rest[:PAGES_PER_STEP]
    am_ref, key_ref, qs_ref, ws_ref = rest[PAGES_PER_STEP:]
    c = pl.program_id(1)
    nsteps = pl.num_programs(1)
    r8 = SAMPLE_PAD

    @pl.when(c == 0)
    def _():
        qi = qi_ref[0]
        wsc = (wi_ref[0] * (H_IDX ** -0.5)) * (D_IDX ** -0.5)
        qs_ref[...] = jnp.concatenate([qi[:, h * D_IDX:(h + 1) * D_IDX] for h in range(H_IDX)],
                                      axis=0).astype(BF16)
        ws_ref[...] = jnp.concatenate([jnp.broadcast_to(wsc[:, h:h + 1], (r8, PAGE_SIZE))
                                       for h in range(H_IDX)], axis=0)

    def score(ki_t):
        s = jnp.maximum(_dot(qs_ref[...], ki_t), 0.0) * ws_ref[...]
        return _fold_rows(s, jnp.add)

    for r in range(PAGES_PER_STEP):
        key_ref[c * PAGES_PER_STEP + r] = _sort_key(score(page_refs[r][0].astype(BF16)))

    @pl.when(c == nsteps - 1)
    def _():
        rows = lax.broadcasted_iota(I32, (r8, PAGE_SIZE), 0)
        cols = lax.broadcasted_iota(I32, (r8, PAGE_SIZE), 1)
        new_valid = (cols <= rows) & (cols < lv)
        kin = jnp.concatenate([kin_ref[0], jnp.zeros((PAGE_SIZE - r8, D_IDX), F32)], axis=0)
        kin_t = jnp.concatenate([kin, jnp.zeros((PAGE_SIZE, PAGE_SIZE - D_IDX), F32)], axis=1).T[0:D_IDX]
        key_ref[N_PAGES] = jnp.where(new_valid, _sort_key(score(kin_t.astype(BF16))), INT_MIN)

        def tile_valid(j):
            return new_valid if j == N_PAGES else None

        def count_where(pred):
            parts = []
            for j in range(N_KTILES):
                hit = pred(key_ref[j])
                if tile_valid(j) is not None:
                    hit = hit & tile_valid(j)
                parts.append(jnp.where(hit, 1.0, 0.0))
            return jnp.sum(_tree(parts, jnp.add), axis=-1, keepdims=True)

        thr = _kth_largest(lambda cand: count_where(lambda key: key >= cand), (r8, 1), topk)
        n_tie = topk - count_where(lambda key: key > thr)
        n_eq = count_where(lambda key: key == thr)
        need_rank = jnp.max(n_eq - n_tie) > 0.0

        @pl.when(jnp.logical_not(need_rank))
        def _():
            for j in range(N_KTILES):
                sel = key_ref[j] >= thr
                if tile_valid(j) is not None:
                    sel = sel & tile_valid(j)
                am_ref[0, j] = jnp.where(sel, 0.0, NEG_BIG)

        @pl.when(need_rank)
        def _():
            upper = _strict_upper(PAGE_SIZE)

            def mask_step(j, run):
                key = key_ref[j]
                valid = (j < N_PAGES) | new_valid
                eq = valid & (key == thr)
                eqf = jnp.where(eq, 1.0, 0.0)
                rank = _dot(eqf.astype(BF16), upper) + run
                sel = (key > thr) | (eq & (rank < n_tie))
                am_ref[0, j] = jnp.where(sel, 0.0, NEG_BIG)
                return run + jnp.sum(eqf, axis=-1, keepdims=True)

            lax.fori_loop(0, N_KTILES, mask_step, jnp.zeros((r8, 1), F32))

        for j in range(N_KTILES, N_KTILES_PAD):
            am_ref[0, j] = jnp.full((r8, PAGE_SIZE), NEG_BIG, F32)


def dsa_sample_select(page_table, qi, wi, kin, kidx_pool_t, lv):
    n_seq = qi.shape[0]
    r8 = SAMPLE_PAD
    qmap = lambda s, c, pt: (s, 0, 0)
    page_specs = [pl.BlockSpec((1, D_IDX, PAGE_SIZE),
                               functools.partial(lambda s, c, pt, r: (pt[s, c * PAGES_PER_STEP + r], 0, 0), r=r))
                  for r in range(PAGES_PER_STEP)]
    grid_spec = pltpu.PrefetchScalarGridSpec(
        num_scalar_prefetch=1,
        grid=(n_seq, N_PAGES // PAGES_PER_STEP),
        in_specs=[pl.BlockSpec((1, r8, H_IDX * D_IDX), qmap),
                  pl.BlockSpec((1, r8, H_IDX), qmap),
                  pl.BlockSpec((1, r8, D_IDX), qmap)] + page_specs,
        out_specs=pl.BlockSpec((1, N_KTILES_PAD, r8, PAGE_SIZE), lambda s, c, pt: (s, 0, 0, 0)),
        scratch_shapes=[pltpu.VMEM((N_KTILES, r8, PAGE_SIZE), I32),
                        pltpu.VMEM((H_IDX * r8, D_IDX), BF16),
                        pltpu.VMEM((H_IDX * r8, PAGE_SIZE), F32)],
    )
    return pl.pallas_call(
        functools.partial(_dsa_sample_sel_body, lv=lv, topk=min(TOPK, (PAST_LEN + lv) // 4)),
        grid_spec=grid_spec,
        out_shape=jax.ShapeDtypeStruct((n_seq, N_KTILES_PAD, r8, PAGE_SIZE), F32),
        compiler_params=_cparams("arbitrary", "arbitrary"),
        name="dsa_sample_select",
    )(page_table, qi, wi, kin, *([kidx_pool_t] * PAGES_PER_STEP))


def _dsa_sample_attn_body(pt_ref, q_ref, kn_ref, vn_ref, am_ref, amn_ref, bt_ref, *rest):
    kp_refs = rest[:PAGES_PER_STEP]
    vp_refs = rest[PAGES_PER_STEP:2 * PAGES_PER_STEP]
    o_ref, m_ref, l_ref, acc_ref = rest[2 * PAGES_PER_STEP:]
    c = pl.program_id(1)
    nsteps = pl.num_programs(1)
    r8 = SAMPLE_PAD
    gr = G_B * r8

    @pl.when(c == 0)
    def _():
        m_ref[...] = jnp.full(m_ref.shape, NEG_BIG, F32)
        l_ref[...] = jnp.zeros(l_ref.shape, F32)
        acc_ref[...] = jnp.zeros(acc_ref.shape, F32)

    q = q_ref[0]
    qgs, b_subs, b_diags, b_fars = [], [], [], []
    for n in range(KV_B):
        heads = [n * G_B + hh for hh in range(G_B)]
        qg = jnp.concatenate([q[:, h * DH_B:(h + 1) * DH_B] for h in heads], axis=0)
        qgs.append((qg * (DH_B ** -0.5)).astype(BF16))
        b_diags.append(jnp.concatenate([bt_ref[h, :, 0:PAGE_SIZE] for h in heads], axis=0))
        b_subs.append(jnp.concatenate([bt_ref[h, :, PAGE_SIZE:2 * PAGE_SIZE] for h in heads], axis=0))
        b_fars.append(jnp.concatenate(
            [jnp.broadcast_to(bt_ref[h, 0:1, PAGE_SIZE:PAGE_SIZE + 1], (r8, PAGE_SIZE))
             for h in heads], axis=0))

    def update(n, s, pv):
        m_old = m_ref[n]
        m_new = jnp.maximum(m_old, jnp.max(s, axis=-1, keepdims=True))
        alpha = jnp.exp(m_old - m_new)
        p = jnp.exp(s - m_new[:, 0:1])
        l_ref[n] = alpha * l_ref[n] + jnp.sum(p, axis=-1, keepdims=True)
        acc_ref[n] = alpha[:, 0:DH_B] * acc_ref[n] + pv(p.astype(BF16))
        m_ref[n] = m_new

    def group_t(refs, n):
        return jnp.concatenate([ref[0, n] for ref in refs], axis=1).astype(BF16)

    amcat = jnp.concatenate([am_ref[0, r] for r in range(PAGES_PER_STEP)], axis=1)
    am4 = jnp.concatenate([amcat] * G_B, axis=0)
    is_last_step = c == nsteps - 1
    for n in range(KV_B):
        bias = jnp.concatenate([b_fars[n]] * (PAGES_PER_STEP - 1)
                               + [jnp.where(is_last_step, b_subs[n], b_fars[n])], axis=1)
        s = _dot(qgs[n], group_t(kp_refs, n)) + bias + am4
        v_t = group_t(vp_refs, n)
        update(n, s, lambda p: _dot_t(p, v_t))

    @pl.when(c == nsteps - 1)
    def _():
        pad = jnp.zeros((PAGE_SIZE - r8, KV_B * DH_B), F32)
        kn = jnp.concatenate([kn_ref[0], pad], axis=0).astype(BF16)
        vn = jnp.concatenate([vn_ref[0], pad], axis=0).astype(BF16)
        am4 = jnp.concatenate([amn_ref[0, 0]] * G_B, axis=0)
        for n in range(KV_B):
            ksl = slice(n * DH_B, (n + 1) * DH_B)
            s = _dot_t(qgs[n], kn[:, ksl]) + b_diags[n] + am4
            update(n, s, lambda p: _dot(p, vn[:, ksl]))
            o = acc_ref[n] / l_ref[n][:, 0:DH_B]
            for hh in range(G_B):
                h = n * G_B + hh
                o_ref[0, :, h * DH_B:(h + 1) * DH_B] = o[hh * r8:(hh + 1) * r8].astype(o_ref.dtype)


def dsa_sample_attend(page_table, qn, kn, vn, amask, bt, k_pool, v_pool):
    n_seq = qn.shape[0]
    r8 = SAMPLE_PAD
    qmap = lambda s, c, pt: (s, 0, 0)
    page_map = [functools.partial(lambda s, c, pt, r: (pt[s, c * PAGES_PER_STEP + r], 0, 0, 0), r=r)
                for r in range(PAGES_PER_STEP)]
    kv_w = KV_B * DH_B
    page_block = (1, KV_B, DH_B, PAGE_SIZE)
    grid_spec = pltpu.PrefetchScalarGridSpec(
        num_scalar_prefetch=1,
        grid=(n_seq, N_PAGES // PAGES_PER_STEP),
        in_specs=[pl.BlockSpec((1, r8, H_B * DH_B), qmap),
                  pl.BlockSpec((1, r8, kv_w), qmap),
                  pl.BlockSpec((1, r8, kv_w), qmap),
                  pl.BlockSpec((1, PAGES_PER_STEP, r8, PAGE_SIZE), lambda s, c, pt: (s, c, 0, 0)),
                  pl.BlockSpec((1, PAGES_PER_STEP, r8, PAGE_SIZE),
                               lambda s, c, pt: (s, N_PAGES // PAGES_PER_STEP, 0, 0)),
                  pl.BlockSpec((H_B, r8, 2 * PAGE_SIZE), lambda s, c, pt: (0, 0, 0))]
                 + [pl.BlockSpec(page_block, m) for m in page_map]
                 + [pl.BlockSpec(page_block, m) for m in page_map],
        out_specs=pl.BlockSpec((1, r8, H_B * DH_B), qmap),
        scratch_shapes=[pltpu.VMEM((KV_B, G_B * r8, PAGE_SIZE), F32),
                        pltpu.VMEM((KV_B, G_B * r8, PAGE_SIZE), F32),
                        pltpu.VMEM((KV_B, G_B * r8, DH_B), F32)],
    )
    return pl.pallas_call(
        _dsa_sample_attn_body,
        grid_spec=grid_spec,
        out_shape=jax.ShapeDtypeStruct((n_seq, r8, H_B * DH_B), BF16),
        compiler_params=_cparams("arbitrary", "arbitrary"),
        name="dsa_sample_attend",
    )(page_table, qn, kn, vn, amask, amask, bt,
      *([k_pool] * PAGES_PER_STEP), *([v_pool] * PAGES_PER_STEP))


HEADS_PER_GROUP = H_C // G_C
GROUP_W = D_IN_C // G_C


def _softplus(x):
    return jnp.maximum(x, 0.0) + jnp.log1p(jnp.exp(-jnp.abs(x)))


def _split3(x):
    hi = x.astype(BF16)
    r = x - hi.astype(F32)
    mid = r.astype(BF16)
    lo = (r - mid.astype(F32)).astype(BF16)
    return hi, mid, lo


def _ssd_body(x_ref, xbc_ref, zlo_ref, zhi_ref, ng_ref, wdt_ref, wdtt_ref, dtb_ref, dtbt_ref,
              alog_ref, alogt_ref, dsk_ref, cn_ref, wo_ref, h0_ref, o_ref, hst_ref, *, qin, lv):
    c = pl.program_id(1)
    q = CHUNK_C
    rows = lax.broadcasted_iota(I32, (q, q), 0)
    cols = lax.broadcasted_iota(I32, (q, q), 1)
    tril = cols <= rows

    def pad_rows(a):
        if qin == q:
            return a
        return jnp.concatenate([a, jnp.zeros((q - qin, a.shape[1]), a.dtype)], axis=0)

    @pl.when(c == 0)
    def _():
        hst_ref[...] = h0_ref[...]

    x = x_ref[0]
    xn = _rms(pad_rows(x), ng_ref[...]).astype(BF16)
    dt_all = _softplus(_dot(xn, wdt_ref[...]) + dtb_ref[...])
    dtt_all = _softplus(_dot_t(wdtt_ref[...], xn) + dtbt_ref[...])
    if lv < q:
        dt_all = jnp.where(lax.broadcasted_iota(I32, dt_all.shape, 0) < lv, dt_all, 0.0)
        dtt_all = jnp.where(lax.broadcasted_iota(I32, dtt_all.shape, 1) < lv, dtt_all, 0.0)
    acs_all = _tree([_dot(tril.astype(BF16), p) for p in _split3(dt_all * (-jnp.exp(alog_ref[...])))], jnp.add)
    acst_all = _tree([_dot(p, (rows <= cols).astype(BF16))
                      for p in _split3(dtt_all * (-jnp.exp(alogt_ref[...])))], jnp.add)
    z_all = pad_rows(jnp.concatenate([zlo_ref[0], zhi_ref[0]], axis=1))
    y_groups = []
    for g in range(G_C):
        y_groups.append(_ssd_group(g, xbc_ref, hst_ref, dt_all, dtt_all, acs_all, acst_all, z_all,
                                   dsk_ref, cn_ref, tril, pad_rows))
    y = jnp.concatenate(y_groups, axis=1)
    o_ref[0] = x + _dot(y, wo_ref[...])[0:qin]


def _ssd_group(g, xbc_ref, hst_ref, dt_all, dtt_all, acs_all, acst_all, z_all, dsk_ref, cn_ref, tril, pad_rows):
    q = CHUNK_C
    hpg = HEADS_PER_GROUP
    gheads = slice(g * hpg, (g + 1) * hpg)
    gcols = slice(g * GROUP_W, (g + 1) * GROUP_W)
    dt = dt_all[:, gheads]
    dtt = dtt_all[gheads, :]
    acs = acs_all[:, gheads]
    acst = acst_all[gheads, :]
    xs = pad_rows(xbc_ref[0, :, gcols])
    bm = pad_rows(xbc_ref[0, :, D_IN_C + g * N_SSM:D_IN_C + (g + 1) * N_SSM])
    cm = pad_rows(xbc_ref[0, :, D_IN_C + (G_C + g) * N_SSM:D_IN_C + (G_C + g + 1) * N_SSM])
    xst = xs.T
    cmb = cm.astype(BF16)
    cb = _dot_t(cmb, bm.astype(BF16))

    def spread(v, width):
        src = lax.broadcasted_iota(I32, (hpg, hpg * width), 0)
        dst = lax.shift_right_logical(lax.broadcasted_iota(I32, (hpg, hpg * width), 1), int(math.log2(width)))
        sel = (src == dst).astype(BF16)
        return _tree([_dot(p, sel) for p in _split3(v)], jnp.add)

    acs_w = spread(acs, q)
    decay_w = jnp.exp(acs_w[q - 1:q, :] - acs_w)
    eacs_w = jnp.exp(acs_w)
    xdt = (xs * spread(dt, P_C)).astype(BF16)
    low_half = lax.broadcasted_iota(I32, (q, 2 * P_C), 1) < P_C
    h_all = hst_ref[0, gheads]
    ys, h_new = [], []
    for k in range(hpg // 2):
        xpair = xdt[:, k * 2 * P_C:(k + 1) * 2 * P_C]
        yd = []
        for j in (2 * k, 2 * k + 1):
            blk = slice(j * q, (j + 1) * q)
            acs_row = acst[j:j + 1, :]
            a_last = acs_row[:, q - 1:q]
            lmat = jnp.exp(jnp.where(tril, acs_w[:, blk] - acs_row, -jnp.inf))
            yd.append(_dot((cb * lmat).astype(BF16), xpair))
            xht = xst[j * P_C:(j + 1) * P_C, :] * dtt[j:j + 1, :]
            st = _dot(xht.astype(BF16), (bm * decay_w[:, blk]).astype(BF16))
            h_new.append(jnp.exp(a_last) * h_all[j] + st)
        h_pair = h_all[2 * k:2 * k + 2].reshape(2 * P_C, N_SSM).astype(BF16)
        e_pair = jnp.where(low_half, eacs_w[:, 2 * k * q:(2 * k + 1) * q], eacs_w[:, (2 * k + 1) * q:(2 * k + 2) * q])
        ys.append(jnp.where(low_half, yd[0], yd[1]) + _dot_t(cmb, h_pair) * e_pair)
    hst_ref[0, gheads] = jnp.stack(h_new, axis=0)
    y = jnp.concatenate(ys, axis=1) + xs * dsk_ref[:, gcols]
    y = y * _silu(z_all[:, gcols])
    return _rms(y, cn_ref[:, gcols]).astype(BF16)


def ssd_mix(x, xbc, proj, z_col0, h0, norm_g, w_dt, dt_bias, a_log, d_skip, c_norm, w_out, qin, lv):
    n_seq, length, d = x.shape
    nchunk = length // qin
    col = lambda a: a.reshape(-1, 1).astype(F32)
    row = lambda a: a.reshape(1, -1).astype(F32)
    full2 = lambda s, c: (0, 0)
    conv_c = xbc.shape[-1]
    zw = D_IN_C // 2
    assert z_col0 % zw == 0
    in_specs = [
        pl.BlockSpec((1, qin, d), lambda s, c: (s, c, 0)),
        pl.BlockSpec((1, qin, conv_c), lambda s, c: (s, c, 0)),
        pl.BlockSpec((1, qin, zw), lambda s, c: (s, c, z_col0 // zw)),
        pl.BlockSpec((1, qin, zw), lambda s, c: (s, c, z_col0 // zw + 1)),
        pl.BlockSpec((1, d), full2),
        pl.BlockSpec((d, H_C), full2),
        pl.BlockSpec((H_C, d), full2),
        pl.BlockSpec((1, H_C), full2),
        pl.BlockSpec((H_C, 1), full2),
        pl.BlockSpec((1, H_C), full2),
        pl.BlockSpec((H_C, 1), full2),
        pl.BlockSpec((1, D_IN_C), full2),
        pl.BlockSpec((1, D_IN_C), full2),
        pl.BlockSpec((D_IN_C, d), full2),
        pl.BlockSpec((1, H_C, P_C, N_SSM), lambda s, c: (s, 0, 0, 0)),
    ]
    out, hst = pl.pallas_call(
        functools.partial(_ssd_body, qin=qin, lv=lv),
        grid=(n_seq, nchunk),
        in_specs=in_specs,
        out_specs=[pl.BlockSpec((1, qin, d), lambda s, c: (s, c, 0)),
                   pl.BlockSpec((1, H_C, P_C, N_SSM), lambda s, c: (s, 0, 0, 0))],
        out_shape=[jax.ShapeDtypeStruct((n_seq, length, d), F32),
                   jax.ShapeDtypeStruct((n_seq, H_C, P_C, N_SSM), F32)],
        compiler_params=_cparams("arbitrary", "arbitrary"),
        name="ssd_mix",
    )(x, xbc, proj, proj, row(norm_g), w_dt.astype(BF16), w_dt.T.astype(BF16),
      row(dt_bias), col(dt_bias), row(a_log), col(a_log),
      row(jnp.repeat(d_skip, P_C)), row(c_norm), w_out, h0)
    return out, hst


def _flat(a):
    return a.reshape(-1, a.shape[-1])


def _unflat(a, like):
    return a.reshape(like.shape[0], like.shape[1], a.shape[-1])


def kernel(x_prompt, x_sample, mem_prompt, state_a_conv, cache_b_k, cache_b_v, cache_b_kidx, state_c_conv, state_c_ssm, state_d_conv, state_ffn_conv, cache_mem_k, cache_mem_v, page_table, rel_bias, norm_mix, norm_mem, norm_ffn, norm_memtok, a_w_in, a_b_in, a_w_conv, a_b_conv, a_ln_g, a_ln_b, a_w_out, b_w_in, b_w_out, b_q_norm, b_k_norm, b_kidx_norm, c_w_in, c_w_conv, c_b_conv, c_dt_bias, c_a_log, c_d_skip, c_norm, c_w_out, d_w_in, d_w_conv, d_w_out, m_w_q, m_w_kv, m_w_o, m_q_norm, m_k_norm, f_w_in, f_w_conv, f_b_conv, f_w_out):
    n_p, l_p, d = x_prompt.shape
    n_s, l_s, _ = x_sample.shape
    bf = lambda w: w.astype(BF16)
    xs_pad = jnp.pad(x_sample, ((0, 0), (0, SAMPLE_PAD - l_s), (0, 0)))
    groups = [dict(x=x_prompt, sb=1, tm=256, tmf=512, lv=256, qin=CHUNK_C, lvq=CHUNK_C, prompt=True),
              dict(x=xs_pad, sb=n_s, tm=SAMPLE_PAD, tmf=SAMPLE_PAD, lv=l_s, qin=SAMPLE_PAD, lvq=l_s, prompt=False)]
    bt, btt = t5_tiles(rel_bias)
    mem_flat = _flat(mem_prompt)
    outs = {k: [[], []] for k in ("a", "bk", "bv", "bki", "cc", "cs", "d", "f")}
    m_kp, m_vp = [], []

    for i in range(DEPTH):
        j = i // 4
        kind = i % 4
        kv = linear(mem_flat, bf(m_w_kv[i]), g=norm_memtok[i], name="mem_kv")
        mk, _ = head_norm(kv, 0, d, m_k_norm[i], DH_M, name="mem_k_norm")
        m_kp.append(mk.reshape(n_p, N_MEM, H_M, DH_M))
        m_vp.append(kv[:, d:].reshape(n_p, N_MEM, H_M, DH_M))
        mk = m_kp[-1].transpose(0, 2, 1, 3)
        mv = m_vp[-1].transpose(0, 2, 1, 3)

        for gi, grp in enumerate(groups):
            x = grp["x"]
            n_seq, length, _ = x.shape
            prompt = grp["prompt"]
            zeros_state = lambda w, c: jnp.zeros((n_seq, w - 1, c), F32)
            xf = _flat(x)
            if kind == 0:
                st = zeros_state(W_A, d) if prompt else state_a_conv[j]
                x, nst = conv_mixer(x, st, norm_mix[i], bf(a_w_in[j]), a_b_in[j], a_w_conv[j],
                                    [a_b_conv[j], a_ln_g[j], a_ln_b[j]], bf(a_w_out[j]), _a_pre, _a_post,
                                    grp["sb"], grp["tm"], grp["lv"], "a_mixer")
                outs["a"][gi].append(nst)
            elif kind == 1:
                qn, kn, knb, v, vb, qi, kin, kinb, wi = b_proj(xf, norm_mix[i], b_w_in[j], b_q_norm[j],
                                                               b_k_norm[j], b_kidx_norm[j])
                kw = KV_B * DH_B
                r3 = lambda a: a.reshape(n_seq, length, a.shape[-1])
                if prompt:
                    vtb = vb.reshape(n_seq, length // PAGE_SIZE, PAGE_SIZE, kw).transpose(0, 1, 3, 2)
                    o = dsa_prompt(r3(qn), r3(qi), r3(wi), r3(knb), vtb, r3(kinb), btt)
                else:
                    am = dsa_sample_select(page_table, r3(qi), r3(wi), r3(kin),
                                           cache_b_kidx[j].transpose(0, 2, 1), grp["lv"])
                    o = dsa_sample_attend(page_table, r3(qn), r3(kn), r3(v), am, bt,
                                          cache_b_k[j].transpose(0, 2, 3, 1), cache_b_v[j].transpose(0, 2, 3, 1))
                outs["bk"][gi].append(r3(kn))
                outs["bv"][gi].append(r3(v))
                outs["bki"][gi].append(r3(kin))
                x = _unflat(linear(_flat(o), bf(b_w_out[j]), res=xf, name="b_out"), x)
            elif kind == 2:
                w_in = c_w_in[j]
                conv_c = D_IN_C + 2 * G_C * N_SSM
                st = zeros_state(W_C, conv_c) if prompt else state_c_conv[j]
                xbc, z, nst = c_in_conv(x, st, norm_mix[i], bf(w_in[:, D_IN_C:D_IN_C + conv_c]), bf(w_in[:, :D_IN_C]),
                                        c_w_conv[j], c_b_conv[j], grp["sb"], grp["tm"], grp["lv"])
                h0 = jnp.zeros((n_seq, H_C, P_C, N_SSM), F32) if prompt else state_c_ssm[j]
                x, hst = ssd_mix(x, xbc, z, 0, h0, norm_mix[i], w_in[:, D_IN_C + conv_c:],
                                 c_dt_bias[j], c_a_log[j], c_d_skip[j], c_norm[j], bf(c_w_out[j]),
                                 grp["qin"], grp["lvq"])
                outs["cc"][gi].append(nst)
                outs["cs"][gi].append(hst)
            else:
                st = zeros_state(W_D, d) if prompt else state_d_conv[j]
                x, nst = conv_mixer(x, st, norm_mix[i], bf(d_w_in[j]), None, d_w_conv[j], [], bf(d_w_out[j]),
                                    _d_pre, _d_post, grp["sb"], grp["tm"], grp["lv"], "d_mixer")
                outs["d"][gi].append(nst)

            if prompt:
                x = mem_fused(x, mk, mv, norm_mem[i], bf(m_w_q[i]), m_q_norm[i], bf(m_w_o[i]), 512)
            else:
                xf = _flat(x)
                qm = _unflat(linear(xf, bf(m_w_q[i]), g=norm_mem[i], name="mem_q"), x)
                om = mem_attn_cached(qm, cache_mem_k.reshape(DEPTH * n_seq, N_MEM, H_M, DH_M),
                                     cache_mem_v.reshape(DEPTH * n_seq, N_MEM, H_M, DH_M),
                                     m_q_norm[i], MEM_SEQS_PER_STEP, i * n_seq)
                x = _unflat(linear(_flat(om), bf(m_w_o[i]), res=xf, name="mem_o"), x)

            st = zeros_state(W_F, D_FF) if prompt else state_ffn_conv[i]
            x, nst = ffn_fused(x, st, norm_ffn[i], bf(f_w_in[i]), f_w_conv[i], f_b_conv[i], bf(f_w_out[i]),
                               1 if prompt else n_seq, grp["tmf"], grp["tmf"] if prompt else grp["lv"])
            outs["f"][gi].append(nst)
            grp["x"] = x

    yp = groups[0]["x"]
    ys = groups[1]["x"][:, :l_s]
    st = lambda key, gi: jnp.stack(outs[key][gi])
    kvshape = lambda a, n, l: a.reshape(a.shape[0], n, -1, a.shape[-1])[:, :, :l]
    b_k_p = kvshape(st("bk", 0), n_p, l_p).reshape(-1, n_p, l_p, KV_B, DH_B)
    b_v_p = kvshape(st("bv", 0), n_p, l_p).reshape(-1, n_p, l_p, KV_B, DH_B)
    b_ki_p = kvshape(st("bki", 0), n_p, l_p)
    b_k_s = kvshape(st("bk", 1), n_s, l_s).reshape(-1, n_s, l_s, KV_B, DH_B)
    b_v_s = kvshape(st("bv", 1), n_s, l_s).reshape(-1, n_s, l_s, KV_B, DH_B)
    b_ki_s = kvshape(st("bki", 1), n_s, l_s)
    return (yp, ys, st("a", 0), st("a", 1), b_k_p, b_v_p, b_ki_p, b_k_s, b_v_s, b_ki_s,
            st("cc", 0), st("cs", 0), st("cc", 1), st("cs", 1), st("d", 0), st("d", 1),
            st("f", 0), st("f", 1), jnp.stack(m_kp), jnp.stack(m_vp))
```

```python
import functools
import math

import jax
import jax.numpy as jnp
from jax import lax
from jax.experimental import pallas as pl
from jax.experimental.pallas import tpu as pltpu

F32 = jnp.float32
BF16 = jnp.bfloat16
I32 = jnp.int32

D_MODEL = 1024
DEPTH = 4
PAST_LEN = 8192
PAGE_SIZE = 128
W_A = 31
H_B = 16
DH_B = 64
KV_B = 4
G_B = H_B // KV_B
H_IDX = 8
D_IDX = 64
TOPK = 256
N_BUCKETS = 32
MAX_DIST = 128
D_IN_C = 2 * D_MODEL
P_C = 64
H_C = D_IN_C // P_C
G_C = 4
N_SSM = 128
W_C = 4
CHUNK_C = 128
W_D = 3
N_MEM = 256
H_M = 4
DH_M = D_MODEL // H_M
D_FF = 2816
W_F = 3
EPS = 1e-6

V7X_SUBLANES = 8
V7X_LANES = 128
V7X_VMEM_LIMIT_BYTES = 56 * 1024 * 1024

SAMPLE_PAD = V7X_SUBLANES
NEG_BIG = -1e30
INT_MIN = -(2 ** 31)
PAGES_PER_STEP = 32
MEM_SEQS_PER_STEP = 8


def _cparams(*sem):
    return pltpu.CompilerParams(dimension_semantics=sem, vmem_limit_bytes=V7X_VMEM_LIMIT_BYTES)


def _pick(n, cands):
    for c in cands:
        if n % c == 0:
            return c
    return n


def _rms(x, g):
    y = x * lax.rsqrt(jnp.mean(x * x, axis=-1, keepdims=True) + EPS)
    return y * g


def _dot(a, b):
    return jnp.dot(a, b, preferred_element_type=F32)


def _dot_t(a, b):
    return lax.dot_general(a, b, (((1,), (1,)), ((), ())), preferred_element_type=F32)


def _sigmoid(x):
    return 1.0 / (1.0 + jnp.exp(-x))


def _silu(x):
    return x * _sigmoid(x)


def _group_meansq(x, gsize):
    tm, c = x.shape
    x2 = x * x
    if gsize == c:
        return jnp.mean(x2, axis=-1, keepdims=True)
    if gsize % V7X_LANES == 0:
        parts = []
        for h in range(c // gsize):
            ms = jnp.mean(x2[:, h * gsize:(h + 1) * gsize], axis=-1, keepdims=True)
            parts.append(jnp.broadcast_to(ms, (tm, gsize)))
        return jnp.concatenate(parts, axis=-1)
    shift = int(math.log2(gsize))
    r = lax.shift_right_logical(lax.broadcasted_iota(I32, (c, c), 0), shift)
    q = lax.shift_right_logical(lax.broadcasted_iota(I32, (c, c), 1), shift)
    bd = (r == q).astype(BF16)
    hi = x2.astype(BF16)
    lo = (x2 - hi.astype(F32)).astype(BF16)
    return (_dot(hi, bd) + _dot(lo, bd)) * (1.0 / gsize)


def _group_rms(x, gain, gsize):
    return (x * lax.rsqrt(_group_meansq(x, gsize) + EPS)) * gain


def _linear_body(*refs, norm, bias, res):
    it = iter(refs)
    x_ref = next(it)
    g_ref = next(it) if norm else None
    w_ref = next(it)
    b_ref = next(it) if bias else None
    r_ref = next(it) if res else None
    o_ref = next(it)
    xn_ref = next(it)

    @pl.when(pl.program_id(1) == 0)
    def _():
        x = x_ref[...].astype(F32)
        if norm:
            x = _rms(x, g_ref[...])
        xn_ref[...] = x.astype(BF16)

    acc = _dot(xn_ref[...], w_ref[...])
    if bias:
        acc = acc + b_ref[...]
    if res:
        acc = acc + r_ref[...]
    o_ref[...] = acc.astype(o_ref.dtype)


def linear(x, w, *, g=None, b=None, res=None, out_dtype=F32, name="linear"):
    m, k = x.shape
    n = w.shape[1]
    tm = _pick(m, (1024, 512, 256, 128))
    tn = _pick(n, (1024, 1408, 512, 256, 128))
    in_specs = [pl.BlockSpec((tm, k), lambda i, j: (i, 0))]
    args = [x]
    if g is not None:
        in_specs.append(pl.BlockSpec((1, k), lambda i, j: (0, 0)))
        args.append(g.reshape(1, k).astype(F32))
    in_specs.append(pl.BlockSpec((k, tn), lambda i, j: (0, j)))
    args.append(w)
    if b is not None:
        in_specs.append(pl.BlockSpec((1, tn), lambda i, j: (0, j)))
        args.append(b.reshape(1, n).astype(F32))
    if res is not None:
        in_specs.append(pl.BlockSpec((tm, tn), lambda i, j: (i, j)))
        args.append(res)
    return pl.pallas_call(
        functools.partial(_linear_body, norm=g is not None, bias=b is not None, res=res is not None),
        grid=(m // tm, n // tn),
        in_specs=in_specs,
        out_specs=pl.BlockSpec((tm, tn), lambda i, j: (i, j)),
        out_shape=jax.ShapeDtypeStruct((m, n), out_dtype),
        scratch_shapes=[pltpu.VMEM((tm, k), BF16)],
        compiler_params=_cparams("arbitrary", "arbitrary"),
        name=name,
    )(*args)


def _head_norm_body(x_ref, g_ref, o_ref, ob_ref, *, gsize):
    y = _group_rms(x_ref[...], g_ref[...], gsize)
    o_ref[...] = y
    ob_ref[...] = y.astype(BF16)


def head_norm(x, col0, width, gain, gsize, name="head_norm"):
    m = x.shape[0]
    tm = _pick(m, (512, 256, 128))
    assert col0 % width == 0
    gt = jnp.tile(gain.astype(F32), width // gsize).reshape(1, width)
    return pl.pallas_call(
        functools.partial(_head_norm_body, gsize=gsize),
        grid=(m // tm,),
        in_specs=[pl.BlockSpec((tm, width), lambda i: (i, col0 // width)),
                  pl.BlockSpec((1, width), lambda i: (0, 0))],
        out_specs=[pl.BlockSpec((tm, width), lambda i: (i, 0)),
                   pl.BlockSpec((tm, width), lambda i: (i, 0))],
        out_shape=[jax.ShapeDtypeStruct((m, width), F32), jax.ShapeDtypeStruct((m, width), BF16)],
        compiler_params=_cparams("arbitrary"),
        name=name,
    )(x, gt)


B_QW = H_B * DH_B
B_KW = KV_B * DH_B


def _b_proj_body(x_ref, ng_ref, wm_ref, wt_ref, gq_ref, gk_ref, gki_ref,
                 qn_ref, kn_ref, knb_ref, v_ref, vb_ref, qi_ref, kin_ref, kinb_ref, wi_ref):
    xn = _rms(x_ref[...], ng_ref[...]).astype(BF16)
    pm = _dot(xn, wm_ref[...])
    pt = _dot(xn, wt_ref[...])
    qn_ref[...] = _group_rms(pm[:, 0:B_QW], gq_ref[...], DH_B)
    kn = _group_rms(pm[:, B_QW:B_QW + B_KW], gk_ref[...], DH_B)
    kn_ref[...] = kn
    knb_ref[...] = kn.astype(BF16)
    v = pm[:, B_QW + B_KW:B_QW + 2 * B_KW]
    v_ref[...] = v
    vb_ref[...] = v.astype(BF16)
    qi_ref[...] = pm[:, B_QW + 2 * B_KW:]
    kin = _group_rms(pt[:, 0:D_IDX], gki_ref[...], D_IDX)
    kin_ref[...] = kin
    kinb_ref[...] = kin.astype(BF16)
    wi_ref[...] = pt[:, D_IDX:D_IDX + H_IDX]


def b_proj(x, norm_g, w_in, q_gain, k_gain, ki_gain):
    m, d = x.shape
    tm = _pick(m, (512, 256))
    n_main = B_QW + 2 * B_KW + H_IDX * D_IDX
    w_tail = jnp.pad(w_in[:, n_main:], ((0, 0), (0, V7X_LANES - (w_in.shape[1] - n_main))))
    const = lambda i: (0, 0)
    rowmap = lambda i: (i, 0)
    widths = [(B_QW, F32), (B_KW, F32), (B_KW, BF16), (B_KW, F32), (B_KW, BF16), (H_IDX * D_IDX, F32),
              (D_IDX, F32), (D_IDX, BF16), (H_IDX, F32)]
    return pl.pallas_call(
        _b_proj_body,
        grid=(m // tm,),
        in_specs=[pl.BlockSpec((tm, d), rowmap),
                  pl.BlockSpec((1, d), const),
                  pl.BlockSpec((d, n_main), const),
                  pl.BlockSpec((d, V7X_LANES), const),
                  pl.BlockSpec((1, B_QW), const),
                  pl.BlockSpec((1, B_KW), const),
                  pl.BlockSpec((1, D_IDX), const)],
        out_specs=[pl.BlockSpec((tm, w), rowmap) for w, _ in widths],
        out_shape=[jax.ShapeDtypeStruct((m, w), dt) for w, dt in widths],
        compiler_params=_cparams("arbitrary"),
        name="b_proj",
    )(x, norm_g.reshape(1, d).astype(F32), w_in[:, :n_main].astype(BF16), w_tail.astype(BF16),
      jnp.tile(q_gain.astype(F32), H_B).reshape(1, B_QW), jnp.tile(k_gain.astype(F32), KV_B).reshape(1, B_KW),
      ki_gain.astype(F32).reshape(1, D_IDX))


def _a_pre(tiles):
    a, g = tiles
    return a * _sigmoid(g)


def _a_post(y, tiles, pars):
    bc, lng, lnb = pars
    y = y + bc
    mu = jnp.mean(y, axis=-1, keepdims=True)
    yc = y - mu
    yn = yc * lax.rsqrt(jnp.mean(yc * yc, axis=-1, keepdims=True) + EPS)
    return _silu(yn * lng + lnb)


def _d_pre(tiles):
    return tiles[1] * tiles[2]


def _d_post(y, tiles, pars):
    return tiles[0] * y


def _conv_mixer_body(*refs, n_par, n_parts, pre, post, width, sb, tm, lv, halo, realign):
    x_ref, ng_ref, win_ref, bin_ref, st_ref, wc_ref = refs[:6]
    par_refs = refs[6:6 + n_par]
    wout_ref, o_ref, nst_ref, abuf, rbuf = refs[6 + n_par:]
    t = pl.program_id(1)
    base = halo - (width - 1)
    d = x_ref.shape[-1]
    c = abuf.shape[-1]

    @pl.when(t == 0)
    def _():
        abuf[:, base:halo, :] = st_ref[...]

    x = x_ref[...].reshape(sb * tm, d)
    proj = _dot(_rms(x, ng_ref[...]).astype(BF16), win_ref[...]) + bin_ref[...]
    parts = [proj[:, p * c:(p + 1) * c] for p in range(n_parts)]
    abuf[:, halo:halo + tm, :] = pre(parts).reshape(sb, tm, c)
    acc = None
    for r in range(V7X_SUBLANES):
        taps = [k for k in range(width) if (base + k) % V7X_SUBLANES == r]
        if not taps:
            continue
        rows = V7X_SUBLANES * max((base + k) // V7X_SUBLANES for k in taps) + tm
        for k in taps:
            q0 = V7X_SUBLANES * ((base + k) // V7X_SUBLANES)
            if r == 0 or len(taps) == 1 or not realign:
                shifted = abuf[:, r + q0:r + q0 + tm, :]
            else:
                if k == taps[0]:
                    rbuf[r, :, 0:rows, :] = abuf[:, r:r + rows, :]
                shifted = rbuf[r, :, q0:q0 + tm, :]
            term = wc_ref[k:k + 1, :] * shifted
            acc = term if acc is None else acc + term
    u = post(acc.reshape(sb * tm, c), parts, [p[...] for p in par_refs]).astype(BF16)
    o_ref[...] = (x + _dot(u, wout_ref[...])).reshape(sb, tm, d)
    tail = abuf[:, base + lv:base + lv + width - 1, :]
    abuf[:, base:halo, :] = tail
    nst_ref[...] = tail


def conv_mixer(x, state, norm_g, w_in, b_in, w_conv, params, w_out, pre, post, sb, tm, lv, name):
    n_seq, length, d = x.shape
    width, c = w_conv.shape
    n_parts = w_in.shape[1] // c
    halo = -(-(width - 1) // V7X_SUBLANES) * V7X_SUBLANES
    realign = width > V7X_SUBLANES and tm >= 8 * V7X_SUBLANES
    const = lambda s, t: (0, 0)
    b_in = jnp.zeros((n_parts * c,), F32) if b_in is None else b_in
    in_specs = [pl.BlockSpec((sb, tm, d), lambda s, t: (s, t, 0)),
                pl.BlockSpec((1, d), const),
                pl.BlockSpec((d, n_parts * c), const),
                pl.BlockSpec((1, n_parts * c), const),
                pl.BlockSpec((sb, width - 1, c), lambda s, t: (s, 0, 0)),
                pl.BlockSpec((width, c), const)]
    in_specs += [pl.BlockSpec((1, c), const) for _ in params]
    in_specs.append(pl.BlockSpec((c, d), const))
    return pl.pallas_call(
        functools.partial(_conv_mixer_body, n_par=len(params), n_parts=n_parts, pre=pre, post=post,
                          width=width, sb=sb, tm=tm, lv=lv, halo=halo, realign=realign),
        grid=(n_seq // sb, length // tm),
        in_specs=in_specs,
        out_specs=[pl.BlockSpec((sb, tm, d), lambda s, t: (s, t, 0)),
                   pl.BlockSpec((sb, width - 1, c), lambda s, t: (s, 0, 0))],
        out_shape=[jax.ShapeDtypeStruct((n_seq, length, d), F32),
                   jax.ShapeDtypeStruct((n_seq, width - 1, c), F32)],
        scratch_shapes=[pltpu.VMEM((sb, halo + tm, c), F32),
                        pltpu.VMEM((V7X_SUBLANES, sb, halo + tm, c) if realign
                                   else (1, 1, V7X_SUBLANES, V7X_LANES), F32)],
        compiler_params=_cparams("arbitrary", "arbitrary"),
        name=name,
    )(x, norm_g.reshape(1, d).astype(F32), w_in, b_in.reshape(1, -1).astype(F32), state, w_conv.astype(F32),
      *[p.reshape(1, c).astype(F32) for p in params], w_out)


C_CONV_CHUNK = 1024


def _c_in_body(x_ref, ng_ref, wx_ref, wz_ref, st_ref, wc_ref, bc_ref, xbc_ref, z_ref, nst_ref, abuf, *, sb, tm, lv, halo):
    t = pl.program_id(1)
    base = halo - (W_C - 1)
    d = x_ref.shape[-1]
    conv_c = abuf.shape[-1]

    @pl.when(t == 0)
    def _():
        abuf[:, base:halo, :] = st_ref[...]

    xn = _rms(x_ref[...].reshape(sb * tm, d), ng_ref[...]).astype(BF16)
    for c0 in range(0, conv_c, C_CONV_CHUNK):
        cs = slice(c0, c0 + C_CONV_CHUNK)
        abuf[:, halo:halo + tm, cs] = _dot(xn, wx_ref[:, cs]).reshape(sb, tm, C_CONV_CHUNK)
        y = None
        for k in range(W_C):
            term = wc_ref[k:k + 1, cs] * abuf[:, base + k:base + k + tm, cs]
            y = term if y is None else y + term
        xbc_ref[:, :, cs] = _silu(y + bc_ref[:, cs])
    z_ref[...] = _dot(xn, wz_ref[...]).reshape(sb, tm, z_ref.shape[-1])
    tail = abuf[:, base + lv:base + lv + W_C - 1, :]
    abuf[:, base:halo, :] = tail
    nst_ref[...] = tail


def c_in_conv(x, state, norm_g, w_xbc, w_z, w_conv, b_conv, sb, tm, lv):
    n_seq, length, d = x.shape
    conv_c = w_xbc.shape[1]
    halo = V7X_SUBLANES
    const = lambda s, t: (0, 0)
    tile = lambda w: pl.BlockSpec((sb, tm, w), lambda s, t: (s, t, 0))
    stt = pl.BlockSpec((sb, W_C - 1, conv_c), lambda s, t: (s, 0, 0))
    return pl.pallas_call(
        functools.partial(_c_in_body, sb=sb, tm=tm, lv=lv, halo=halo),
        grid=(n_seq // sb, length // tm),
        in_specs=[tile(d), pl.BlockSpec((1, d), const), pl.BlockSpec((d, conv_c), const),
                  pl.BlockSpec((d, D_IN_C), const), stt, pl.BlockSpec((W_C, conv_c), const),
                  pl.BlockSpec((1, conv_c), const)],
        out_specs=[tile(conv_c), tile(D_IN_C), stt],
        out_shape=[jax.ShapeDtypeStruct((n_seq, length, conv_c), F32),
                   jax.ShapeDtypeStruct((n_seq, length, D_IN_C), F32),
                   jax.ShapeDtypeStruct((n_seq, W_C - 1, conv_c), F32)],
        scratch_shapes=[pltpu.VMEM((sb, halo + tm, conv_c), F32)],
        compiler_params=_cparams("arbitrary", "arbitrary"),
        name="c_in_conv",
    )(x, norm_g.reshape(1, d).astype(F32), w_xbc, w_z, state, w_conv.astype(F32),
      b_conv.reshape(1, conv_c).astype(F32))


FFN_CHUNK = 11 * V7X_LANES


def _ffn_body(x_ref, ng_ref, win_ref, st_ref, wc_ref, bc_ref, wout_ref, o_ref, nst_ref, abuf, *, sb, tm, lv, halo):
    t = pl.program_id(1)
    base = halo - (W_F - 1)
    d = x_ref.shape[-1]

    @pl.when(t == 0)
    def _():
        abuf[:, base:halo, :] = st_ref[...]

    x = x_ref[...].reshape(sb * tm, d)
    xn = _rms(x, ng_ref[...]).astype(BF16)
    out = x
    for c0 in range(0, D_FF, FFN_CHUNK):
        cs = slice(c0, c0 + FFN_CHUNK)
        abuf[:, halo:halo + tm, cs] = _dot(xn, win_ref[:, cs]).reshape(sb, tm, FFN_CHUNK)
        g = _dot(xn, win_ref[:, D_FF + c0:D_FF + c0 + FFN_CHUNK])
        y = None
        for k in range(W_F):
            term = wc_ref[k:k + 1, cs] * abuf[:, base + k:base + k + tm, cs]
            y = term if y is None else y + term
        y = y.reshape(sb * tm, FFN_CHUNK) + bc_ref[:, cs]
        out = out + _dot((_silu(y) * g).astype(BF16), wout_ref[cs, :])
    o_ref[...] = out.reshape(sb, tm, d)
    tail = abuf[:, base + lv:base + lv + W_F - 1, :]
    abuf[:, base:halo, :] = tail
    nst_ref[...] = tail


def ffn_fused(x, state, norm_g, w_in, w_conv, b_conv, w_out, sb, tm, lv):
    n_seq, length, d = x.shape
    halo = V7X_SUBLANES
    const = lambda s, t: (0, 0)
    resident = dict(pipeline_mode=pl.Buffered(1))
    return pl.pallas_call(
        functools.partial(_ffn_body, sb=sb, tm=tm, lv=lv, halo=halo),
        grid=(n_seq // sb, length // tm),
        in_specs=[pl.BlockSpec((sb, tm, d), lambda s, t: (s, t, 0)),
                  pl.BlockSpec((1, d), const),
                  pl.BlockSpec((d, 2 * D_FF), const, **resident),
                  pl.BlockSpec((sb, W_F - 1, D_FF), lambda s, t: (s, 0, 0)),
                  pl.BlockSpec((W_F, D_FF), const),
                  pl.BlockSpec((1, D_FF), const),
                  pl.BlockSpec((D_FF, d), const, **resident)],
        out_specs=[pl.BlockSpec((sb, tm, d), lambda s, t: (s, t, 0)),
                   pl.BlockSpec((sb, W_F - 1, D_FF), lambda s, t: (s, 0, 0))],
        out_shape=[jax.ShapeDtypeStruct((n_seq, length, d), F32),
                   jax.ShapeDtypeStruct((n_seq, W_F - 1, D_FF), F32)],
        scratch_shapes=[pltpu.VMEM((sb, halo + tm, D_FF), F32)],
        compiler_params=_cparams("arbitrary", "arbitrary"),
        name="ffn",
    )(x, norm_g.reshape(1, d).astype(F32), w_in, state, w_conv.astype(F32),
      b_conv.reshape(1, D_FF).astype(F32), w_out)


def _softmax_rows(s):
    p = jnp.exp(s - jnp.max(s, axis=-1, keepdims=True))
    return p / jnp.sum(p, axis=-1, keepdims=True)


def _mem_attend(q, k, v, gain, token_major):
    tm = q.shape[0]
    qh = [(_rms(q[:, h * DH_M:(h + 1) * DH_M], gain) * (DH_M ** -0.5)).astype(BF16) for h in range(H_M)]
    if not token_major:
        outs = []
        for h in range(H_M):
            p = _softmax_rows(_dot_t(qh[h], k[h].astype(BF16)))
            outs.append(_dot(p.astype(BF16), v[h].astype(BF16)))
    else:
        k_all = k.reshape(N_MEM * H_M, DH_M).astype(BF16)
        v_all = v.reshape(N_MEM * H_M, DH_M).astype(BF16)
        s = _dot_t(jnp.concatenate(qh, axis=0), k_all)
        col_head = lax.broadcasted_iota(I32, s.shape, 1) & (H_M - 1)
        row_head = lax.shift_right_logical(lax.broadcasted_iota(I32, s.shape, 0), int(math.log2(tm)))
        p = _softmax_rows(jnp.where(col_head == row_head, s, NEG_BIG))
        o = _dot(p.astype(BF16), v_all)
        outs = [o[h * tm:(h + 1) * tm] for h in range(H_M)]
    return jnp.concatenate(outs, axis=1).astype(BF16)


def _mem_attn_body(q_ref, k_ref, v_ref, g_ref, o_ref, *, sb):
    for s in range(sb):
        o_ref[s] = _mem_attend(q_ref[s], k_ref[s], v_ref[s], g_ref[...], True)


def mem_attn_cached(q, k, v, q_gain, sb, kv_seq0):
    n_seq, tm, d = q.shape
    kv_block = (sb,) + k.shape[1:]
    kv_map = lambda s: (kv_seq0 // sb + s, 0, 0, 0)
    return pl.pallas_call(
        functools.partial(_mem_attn_body, sb=sb),
        grid=(n_seq // sb,),
        in_specs=[pl.BlockSpec((sb, tm, d), lambda s: (s, 0, 0)),
                  pl.BlockSpec(kv_block, kv_map),
                  pl.BlockSpec(kv_block, kv_map),
                  pl.BlockSpec((1, DH_M), lambda s: (0, 0))],
        out_specs=pl.BlockSpec((sb, tm, d), lambda s: (s, 0, 0)),
        out_shape=jax.ShapeDtypeStruct((n_seq, tm, d), BF16),
        compiler_params=_cparams("arbitrary"),
        name="mem_attn",
    )(q, k, v, q_gain.reshape(1, DH_M).astype(F32))


def _mem_fused_body(x_ref, ng_ref, wq_ref, k_ref, v_ref, g_ref, wo_ref, o_ref):
    x = x_ref[0]
    q = _dot(_rms(x, ng_ref[...]).astype(BF16), wq_ref[...])
    o = _mem_attend(q, k_ref[0], v_ref[0], g_ref[...], False)
    o_ref[0] = x + _dot(o, wo_ref[...])


def mem_fused(x, k, v, norm_g, w_q, q_gain, w_o, tm):
    n_seq, length, d = x.shape
    const = lambda s, t: (0, 0)
    kv_block = (1,) + k.shape[1:]
    return pl.pallas_call(
        _mem_fused_body,
        grid=(n_seq, length // tm),
        in_specs=[pl.BlockSpec((1, tm, d), lambda s, t: (s, t, 0)),
                  pl.BlockSpec((1, d), const),
                  pl.BlockSpec((d, d), const),
                  pl.BlockSpec(kv_block, lambda s, t: (s, 0, 0, 0)),
                  pl.BlockSpec(kv_block, lambda s, t: (s, 0, 0, 0)),
                  pl.BlockSpec((1, DH_M), const),
                  pl.BlockSpec((d, d), const)],
        out_specs=pl.BlockSpec((1, tm, d), lambda s, t: (s, t, 0)),
        out_shape=jax.ShapeDtypeStruct((n_seq, length, d), F32),
        compiler_params=_cparams("arbitrary", "arbitrary"),
        name="mem_fused",
    )(x, norm_g.reshape(1, d).astype(F32), w_q, k, v, q_gain.reshape(1, DH_M).astype(F32), w_o)


HEADS_PER_PAIR = 2 * G_B


def _t5_lookup(tab_ref, h, dist):
    dist = jnp.maximum(dist, 0)
    max_exact = N_BUCKETS // 2
    df = jnp.maximum(dist, 1).astype(F32)
    large = max_exact + (jnp.log(df / max_exact) / math.log(MAX_DIST / max_exact)
                         * (N_BUCKETS - max_exact)).astype(I32)
    large = jnp.minimum(large, N_BUCKETS - 1)
    bucket = jnp.where(dist < max_exact, dist, large)
    acc = jnp.zeros(dist.shape, F32)
    for b in range(N_BUCKETS):
        acc = jnp.where(bucket == b, tab_ref[b, h], acc)
    return acc


def _t5_tiles_body(tab_ref, o_ref, ot_ref):
    h = pl.program_id(0)
    r = lax.broadcasted_iota(I32, (PAGE_SIZE, PAGE_SIZE), 0)
    c = lax.broadcasted_iota(I32, (PAGE_SIZE, PAGE_SIZE), 1)
    o_ref[0, :, 0:PAGE_SIZE] = _t5_lookup(tab_ref, h, r - c)
    o_ref[0, :, PAGE_SIZE:2 * PAGE_SIZE] = _t5_lookup(tab_ref, h, PAGE_SIZE + r - c)
    ot_ref[0, 0] = _t5_lookup(tab_ref, h, c - r)
    ot_ref[0, 1] = _t5_lookup(tab_ref, h, PAGE_SIZE + c - r)


def t5_tiles(table):
    return pl.pallas_call(
        _t5_tiles_body,
        grid=(H_B,),
        in_specs=[pl.BlockSpec(memory_space=pltpu.SMEM)],
        out_specs=[pl.BlockSpec((1, PAGE_SIZE, 2 * PAGE_SIZE), lambda h: (h, 0, 0)),
                   pl.BlockSpec((1, 2, PAGE_SIZE, PAGE_SIZE),
                                lambda h: (h // HEADS_PER_PAIR, 0, 0, h % HEADS_PER_PAIR))],
        out_shape=[jax.ShapeDtypeStruct((H_B, PAGE_SIZE, 2 * PAGE_SIZE), F32),
                   jax.ShapeDtypeStruct((H_B // HEADS_PER_PAIR, 2, PAGE_SIZE, HEADS_PER_PAIR * PAGE_SIZE), F32)],
        compiler_params=_cparams("arbitrary"),
        name="t5_tiles",
    )(table.astype(F32))


def _sort_key(score):
    score = jnp.where(score == 0.0, 0.0, score)
    u = lax.bitcast_convert_type(score, I32)
    return jnp.where(u < 0, u ^ 0x7FFFFFFF, u)


def _kth_largest(count_ge, shape, topk):
    def try_cand(cand, cur):
        return jnp.where(count_ge(cand) >= topk, cand, cur)

    t0 = try_cand(jnp.zeros(shape, I32), jnp.full(shape, INT_MIN, I32))

    def step(b, cur):
        cand = cur + lax.shift_left(jnp.int32(1), jnp.int32(30) - b)
        return try_cand(cand, cur)

    return lax.fori_loop(0, 31, step, t0)


def _strict_upper(n):
    r = lax.broadcasted_iota(I32, (n, n), 0)
    c = lax.broadcasted_iota(I32, (n, n), 1)
    return (r < c).astype(BF16)


def _strict_lower(n):
    r = lax.broadcasted_iota(I32, (n, n), 0)
    c = lax.broadcasted_iota(I32, (n, n), 1)
    return (c < r).astype(BF16)


def _tree(parts, op):
    parts = list(parts)
    while len(parts) > 1:
        nxt = [op(parts[k], parts[k + 1]) for k in range(0, len(parts) - 1, 2)]
        if len(parts) % 2:
            nxt.append(parts[-1])
        parts = nxt
    return parts[0]


def _fold_rows(x, op):
    return _tree([x[r:r + V7X_SUBLANES] for r in range(0, x.shape[0], V7X_SUBLANES)], op)


LOOP_UNROLL = 4


def _loop_tiles(n, body, carry, unroll=LOOP_UNROLL):
    shift = int(math.log2(unroll))
    n_main = lax.shift_right_logical(n, shift)
    carry = lax.fori_loop(0, n_main, lambda t, c: body(t * unroll, unroll, c), carry)
    return lax.fori_loop(n_main * unroll, n, lambda j, c: body(j, 1, c), carry)


def _dsa_prompt_body(q_ref, qi_ref, wi_ref, k_ref, vt_ref, ki_ref, bt_ref, o_ref,
                     key_ref, am_ref, lg_ref, acc_ref, ot_ref, *, topk):
    i = pl.program_id(1)
    nj = i + 1
    qb = PAGE_SIZE
    kpos = lax.broadcasted_iota(I32, (qb, qb), 0)
    qpos = lax.broadcasted_iota(I32, (qb, qb), 1)

    def kslice(j0, u=1):
        return pl.ds(pl.multiple_of(j0 * qb, qb), u * qb)

    qit = qi_ref[0].T
    qit = jnp.concatenate([qit[h * D_IDX:(h + 1) * D_IDX] for h in range(H_IDX)], axis=1).astype(BF16)
    wsc = (wi_ref[0] * (H_IDX ** -0.5)) * (D_IDX ** -0.5)
    wt = jnp.concatenate([wsc, jnp.zeros((qb, qb - H_IDX), F32)], axis=1).T
    wrow = jnp.concatenate([wt[h:h + 1] for h in range(H_IDX)], axis=1)

    def idx_step(j0, u, carry):
        s = jnp.maximum(_dot(ki_ref[0, kslice(j0, u), :], qit), 0.0) * wrow
        sc = s[:, 0:qb]
        for h in range(1, H_IDX):
            sc = sc + s[:, h * qb:(h + 1) * qb]
        key = _sort_key(sc)
        for t in range(u):
            valid = (j0 + t < i) | (kpos <= qpos)
            key_ref[j0 + t] = jnp.where(valid, key[t * qb:(t + 1) * qb], INT_MIN)
        return carry

    _loop_tiles(nj, idx_step, 0)

    def count_where(pred):
        def cstep(j0, u, c):
            for t in range(u):
                c = jnp.where(pred(key_ref[j0 + t]), c + 1.0, c)
            return c
        c = _loop_tiles(nj, cstep, jnp.zeros((qb, qb), F32))
        return jnp.sum(_fold_rows(c, jnp.add), axis=0, keepdims=True)

    thr = _kth_largest(lambda cand: count_where(lambda key: key >= cand), (1, qb), topk)
    n_tie = topk - count_where(lambda key: key > thr)
    n_eq = count_where(lambda key: (key == thr) & (key != INT_MIN))
    need_rank = jnp.max(n_eq - n_tie) > 0.0

    @pl.when(jnp.logical_not(need_rank))
    def _():
        def mask_step(j0, u, carry):
            for t in range(u):
                key = key_ref[j0 + t]
                am_ref[j0 + t] = jnp.where((key >= thr) & (key != INT_MIN), 0.0, NEG_BIG)
            return carry
        _loop_tiles(nj, mask_step, 0)

    @pl.when(need_rank)
    def _():
        lower = _strict_lower(qb)

        def mask_step(j, run):
            key = key_ref[j]
            eq = (key == thr) & (key != INT_MIN)
            eqf = jnp.where(eq, 1.0, 0.0)
            rank = _dot(lower, eqf.astype(BF16)) + run
            sel = (key > thr) | (eq & (rank < n_tie))
            am_ref[j] = jnp.where(sel, 0.0, NEG_BIG)
            return run + jnp.sum(_fold_rows(eqf, jnp.add), axis=0, keepdims=True)

        lax.fori_loop(0, nj, mask_step, jnp.zeros((1, qb), F32))

    qt = q_ref[0].T
    pw = HEADS_PER_PAIR * qb
    gw = G_B * qb
    for pp in range(KV_B // 2):
        def group_qt(n):
            return jnp.concatenate([qt[h * DH_B:(h + 1) * DH_B] for h in range(n * G_B, (n + 1) * G_B)],
                                   axis=1)
        zero = jnp.zeros((DH_B, gw), F32)
        rhs = jnp.concatenate([jnp.concatenate([group_qt(2 * pp), zero], axis=1),
                               jnp.concatenate([zero, group_qt(2 * pp + 1)], axis=1)], axis=0)
        rhs = (rhs * (DH_B ** -0.5)).astype(BF16)
        klanes = slice(pp * 2 * DH_B, (pp + 1) * 2 * DH_B)
        far = bt_ref[pp, 1, 0:1, :]

        def tile_logits(j0, u, bias, mx):
            s = _dot(k_ref[0, kslice(j0, u), klanes], rhs)
            for t in range(u):
                st = s[t * qb:(t + 1) * qb] + bias + jnp.concatenate([am_ref[j0 + t]] * HEADS_PER_PAIR, axis=1)
                lg_ref[j0 + t] = st
                mx = jnp.maximum(mx, _fold_rows(st, jnp.maximum))
            return mx

        neg = jnp.full((V7X_SUBLANES, pw), NEG_BIG, F32)
        mx = _loop_tiles(jnp.maximum(i - 1, 0), lambda j0, u, m: tile_logits(j0, u, far, m), neg)
        mx = lax.cond(i >= 1, lambda m: tile_logits(i - 1, 1, bt_ref[pp, 1], m), lambda m: m, mx)
        mx = tile_logits(i, 1, bt_ref[pp, 0], mx)
        m = jnp.max(mx, axis=0, keepdims=True)
        acc_ref[...] = jnp.zeros(acc_ref.shape, F32)

        def p2(j0, u, l):
            ps = []
            for t in range(u):
                p = jnp.exp(lg_ref[j0 + t] - m)
                l = l + _fold_rows(p, jnp.add)
                ps.append(p.astype(BF16))
            pb = jnp.concatenate(ps, axis=0)
            vt = jnp.concatenate([vt_ref[0, j0 + t] for t in range(u)], axis=1)
            for gg in range(2):
                n = 2 * pp + gg
                acc_ref[gg] += _dot(vt[n * DH_B:(n + 1) * DH_B, :], pb[:, gg * gw:(gg + 1) * gw])
            return l

        l = _loop_tiles(nj, p2, jnp.zeros((V7X_SUBLANES, pw), F32))
        lsum = jnp.sum(l, axis=0, keepdims=True)
        for gg in range(2):
            o_t = acc_ref[gg] / lsum[:, gg * gw:(gg + 1) * gw]
            for hh in range(G_B):
                h = (2 * pp + gg) * G_B + hh
                ot_ref[h * DH_B:(h + 1) * DH_B, :] = o_t[:, hh * qb:(hh + 1) * qb]
    o_ref[0] = ot_ref[...].T.astype(o_ref.dtype)


def dsa_prompt(qn, qi, wi, kb, vtb, kib, btt):
    n_seq, length, _ = qn.shape
    nkb = length // PAGE_SIZE
    qmap = lambda s, i: (s, i, 0)
    kmap = lambda s, i: (s, 0, 0)
    pw = HEADS_PER_PAIR * PAGE_SIZE
    return pl.pallas_call(
        functools.partial(_dsa_prompt_body, topk=min(TOPK, length // 4)),
        grid=(n_seq, nkb),
        in_specs=[pl.BlockSpec((1, PAGE_SIZE, H_B * DH_B), qmap),
                  pl.BlockSpec((1, PAGE_SIZE, H_IDX * D_IDX), qmap),
                  pl.BlockSpec((1, PAGE_SIZE, H_IDX), qmap),
                  pl.BlockSpec((1, length, KV_B * DH_B), kmap),
                  pl.BlockSpec((1, nkb, KV_B * DH_B, PAGE_SIZE), lambda s, i: (s, 0, 0, 0)),
                  pl.BlockSpec((1, length, D_IDX), kmap),
                  pl.BlockSpec((H_B // HEADS_PER_PAIR, 2, PAGE_SIZE, pw), lambda s, i: (0, 0, 0, 0))],
        out_specs=pl.BlockSpec((1, PAGE_SIZE, H_B * DH_B), qmap),
        out_shape=jax.ShapeDtypeStruct((n_seq, length, H_B * DH_B), BF16),
        scratch_shapes=[pltpu.VMEM((nkb, PAGE_SIZE, PAGE_SIZE), I32),
                        pltpu.VMEM((nkb, PAGE_SIZE, PAGE_SIZE), F32),
                        pltpu.VMEM((nkb, PAGE_SIZE, pw), F32),
                        pltpu.VMEM((2, DH_B, G_B * PAGE_SIZE), F32),
                        pltpu.VMEM((H_B * DH_B, PAGE_SIZE), F32)],
        compiler_params=_cparams("arbitrary", "arbitrary"),
        name="dsa_prompt",
    )(qn, qi, wi, kb, vtb, kib, btt)


N_PAGES = PAST_LEN // PAGE_SIZE
N_KTILES = N_PAGES + 1
N_KTILES_PAD = -(-N_KTILES // PAGES_PER_STEP) * PAGES_PER_STEP


def _dsa_sample_sel_body(pt_ref, qi_ref, wi_ref, kin_ref, *rest, lv, topk):
    page_refs = rest[:PAGES_PER_STEP]
    am_ref, key_ref, qs_ref, ws_ref = rest[PAGES_PER_STEP:]
    c = pl.program_id(1)
    nsteps = pl.num_programs(1)
    r8 = SAMPLE_PAD

    @pl.when(c == 0)
    def _():
        qi = qi_ref[0]
        wsc = (wi_ref[0] * (H_IDX ** -0.5)) * (D_IDX ** -0.5)
        qs_ref[...] = jnp.concatenate([qi[:, h * D_IDX:(h + 1) * D_IDX] for h in range(H_IDX)],
                                      axis=0).astype(BF16)
        ws_ref[...] = jnp.concatenate([jnp.broadcast_to(wsc[:, h:h + 1], (r8, PAGE_SIZE))
                                       for h in range(H_IDX)], axis=0)

    def score(ki_t):
        s = jnp.maximum(_dot(qs_ref[...], ki_t), 0.0) * ws_ref[...]
        return _fold_rows(s, jnp.add)

    for r in range(PAGES_PER_STEP):
        key_ref[c * PAGES_PER_STEP + r] = _sort_key(score(page_refs[r][0].astype(BF16)))

    @pl.when(c == nsteps - 1)
    def _():
        rows = lax.broadcasted_iota(I32, (r8, PAGE_SIZE), 0)
        cols = lax.broadcasted_iota(I32, (r8, PAGE_SIZE), 1)
        new_valid = (cols <= rows) & (cols < lv)
        kin = jnp.concatenate([kin_ref[0], jnp.zeros((PAGE_SIZE - r8, D_IDX), F32)], axis=0)
        kin_t = jnp.concatenate([kin, jnp.zeros((PAGE_SIZE, PAGE_SIZE - D_IDX), F32)], axis=1).T[0:D_IDX]
        key_ref[N_PAGES] = jnp.where(new_valid, _sort_key(score(kin_t.astype(BF16))), INT_MIN)

        def tile_valid(j):
            return new_valid if j == N_PAGES else None

        def count_where(pred):
            parts = []
            for j in range(N_KTILES):
                hit = pred(key_ref[j])
                if tile_valid(j) is not None:
                    hit = hit & tile_valid(j)
                parts.append(jnp.where(hit, 1.0, 0.0))
            return jnp.sum(_tree(parts, jnp.add), axis=-1, keepdims=True)

        thr = _kth_largest(lambda cand: count_where(lambda key: key >= cand), (r8, 1), topk)
        n_tie = topk - count_where(lambda key: key > thr)
        n_eq = count_where(lambda key: key == thr)
        need_rank = jnp.max(n_eq - n_tie) > 0.0

        @pl.when(jnp.logical_not(need_rank))
        def _():
            for j in range(N_KTILES):
                sel = key_ref[j] >= thr
                if tile_valid(j) is not None:
                    sel = sel & tile_valid(j)
                am_ref[0, j] = jnp.where(sel, 0.0, NEG_BIG)

        @pl.when(need_rank)
        def _():
            upper = _strict_upper(PAGE_SIZE)

            def mask_step(j, run):
                key = key_ref[j]
                valid = (j < N_PAGES) | new_valid
                eq = valid & (key == thr)
                eqf = jnp.where(eq, 1.0, 0.0)
                rank = _dot(eqf.astype(BF16), upper) + run
                sel = (key > thr) | (eq & (rank < n_tie))
                am_ref[0, j] = jnp.where(sel, 0.0, NEG_BIG)
                return run + jnp.sum(eqf, axis=-1, keepdims=True)

            lax.fori_loop(0, N_KTILES, mask_step, jnp.zeros((r8, 1), F32))

        for j in range(N_KTILES, N_KTILES_PAD):
            am_ref[0, j] = jnp.full((r8, PAGE_SIZE), NEG_BIG, F32)


def dsa_sample_select(page_table, qi, wi, kin, kidx_pool_t, lv):
    n_seq = qi.shape[0]
    r8 = SAMPLE_PAD
    qmap = lambda s, c, pt: (s, 0, 0)
    page_specs = [pl.BlockSpec((1, D_IDX, PAGE_SIZE),
                               functools.partial(lambda s, c, pt, r: (pt[s, c * PAGES_PER_STEP + r], 0, 0), r=r))
                  for r in range(PAGES_PER_STEP)]
    grid_spec = pltpu.PrefetchScalarGridSpec(
        num_scalar_prefetch=1,
        grid=(n_seq, N_PAGES // PAGES_PER_STEP),
        in_specs=[pl.BlockSpec((1, r8, H_IDX * D_IDX), qmap),
                  pl.BlockSpec((1, r8, H_IDX), qmap),
                  pl.BlockSpec((1, r8, D_IDX), qmap)] + page_specs,
        out_specs=pl.BlockSpec((1, N_KTILES_PAD, r8, PAGE_SIZE), lambda s, c, pt: (s, 0, 0, 0)),
        scratch_shapes=[pltpu.VMEM((N_KTILES, r8, PAGE_SIZE), I32),
                        pltpu.VMEM((H_IDX * r8, D_IDX), BF16),
                        pltpu.VMEM((H_IDX * r8, PAGE_SIZE), F32)],
    )
    return pl.pallas_call(
        functools.partial(_dsa_sample_sel_body, lv=lv, topk=min(TOPK, (PAST_LEN + lv) // 4)),
        grid_spec=grid_spec,
        out_shape=jax.ShapeDtypeStruct((n_seq, N_KTILES_PAD, r8, PAGE_SIZE), F32),
        compiler_params=_cparams("arbitrary", "arbitrary"),
        name="dsa_sample_select",
    )(page_table, qi, wi, kin, *([kidx_pool_t] * PAGES_PER_STEP))


def _dsa_sample_attn_body(pt_ref, q_ref, kn_ref, vn_ref, am_ref, amn_ref, bt_ref, *rest):
    kp_refs = rest[:PAGES_PER_STEP]
    vp_refs = rest[PAGES_PER_STEP:2 * PAGES_PER_STEP]
    o_ref, m_ref, l_ref, acc_ref = rest[2 * PAGES_PER_STEP:]
    c = pl.program_id(1)
    nsteps = pl.num_programs(1)
    r8 = SAMPLE_PAD
    gr = G_B * r8

    @pl.when(c == 0)
    def _():
        m_ref[...] = jnp.full(m_ref.shape, NEG_BIG, F32)
        l_ref[...] = jnp.zeros(l_ref.shape, F32)
        acc_ref[...] = jnp.zeros(acc_ref.shape, F32)

    q = q_ref[0]
    qgs, b_subs, b_diags, b_fars = [], [], [], []
    for n in range(KV_B):
        heads = [n * G_B + hh for hh in range(G_B)]
        qg = jnp.concatenate([q[:, h * DH_B:(h + 1) * DH_B] for h in heads], axis=0)
        qgs.append((qg * (DH_B ** -0.5)).astype(BF16))
        b_diags.append(jnp.concatenate([bt_ref[h, :, 0:PAGE_SIZE] for h in heads], axis=0))
        b_subs.append(jnp.concatenate([bt_ref[h, :, PAGE_SIZE:2 * PAGE_SIZE] for h in heads], axis=0))
        b_fars.append(jnp.concatenate(
            [jnp.broadcast_to(bt_ref[h, 0:1, PAGE_SIZE:PAGE_SIZE + 1], (r8, PAGE_SIZE))
             for h in heads], axis=0))

    def update(n, s, pv):
        m_old = m_ref[n]
        m_new = jnp.maximum(m_old, jnp.max(s, axis=-1, keepdims=True))
        alpha = jnp.exp(m_old - m_new)
        p = jnp.exp(s - m_new[:, 0:1])
        l_ref[n] = alpha * l_ref[n] + jnp.sum(p, axis=-1, keepdims=True)
        acc_ref[n] = alpha[:, 0:DH_B] * acc_ref[n] + pv(p.astype(BF16))
        m_ref[n] = m_new

    def group_t(refs, n):
        return jnp.concatenate([ref[0, n] for ref in refs], axis=1).astype(BF16)

    amcat = jnp.concatenate([am_ref[0, r] for r in range(PAGES_PER_STEP)], axis=1)
    am4 = jnp.concatenate([amcat] * G_B, axis=0)
    is_last_step = c == nsteps - 1
    for n in range(KV_B):
        bias = jnp.concatenate([b_fars[n]] * (PAGES_PER_STEP - 1)
                               + [jnp.where(is_last_step, b_subs[n], b_fars[n])], axis=1)
        s = _dot(qgs[n], group_t(kp_refs, n)) + bias + am4
        v_t = group_t(vp_refs, n)
        update(n, s, lambda p: _dot_t(p, v_t))

    @pl.when(c == nsteps - 1)
    def _():
        pad = jnp.zeros((PAGE_SIZE - r8, KV_B * DH_B), F32)
        kn = jnp.concatenate([kn_ref[0], pad], axis=0).astype(BF16)
        vn = jnp.concatenate([vn_ref[0], pad], axis=0).astype(BF16)
        am4 = jnp.concatenate([amn_ref[0, 0]] * G_B, axis=0)
        for n in range(KV_B):
            ksl = slice(n * DH_B, (n + 1) * DH_B)
            s = _dot_t(qgs[n], kn[:, ksl]) + b_diags[n] + am4
            update(n, s, lambda p: _dot(p, vn[:, ksl]))
            o = acc_ref[n] / l_ref[n][:, 0:DH_B]
            for hh in range(G_B):
                h = n * G_B + hh
                o_ref[0, :, h * DH_B:(h + 1) * DH_B] = o[hh * r8:(hh + 1) * r8].astype(o_ref.dtype)


def dsa_sample_attend(page_table, qn, kn, vn, amask, bt, k_pool, v_pool):
    n_seq = qn.shape[0]
    r8 = SAMPLE_PAD
    qmap = lambda s, c, pt: (s, 0, 0)
    page_map = [functools.partial(lambda s, c, pt, r: (pt[s, c * PAGES_PER_STEP + r], 0, 0, 0), r=r)
                for r in range(PAGES_PER_STEP)]
    kv_w = KV_B * DH_B
    page_block = (1, KV_B, DH_B, PAGE_SIZE)
    grid_spec = pltpu.PrefetchScalarGridSpec(
        num_scalar_prefetch=1,
        grid=(n_seq, N_PAGES // PAGES_PER_STEP),
        in_specs=[pl.BlockSpec((1, r8, H_B * DH_B), qmap),
                  pl.BlockSpec((1, r8, kv_w), qmap),
                  pl.BlockSpec((1, r8, kv_w), qmap),
                  pl.BlockSpec((1, PAGES_PER_STEP, r8, PAGE_SIZE), lambda s, c, pt: (s, c, 0, 0)),
                  pl.BlockSpec((1, PAGES_PER_STEP, r8, PAGE_SIZE),
                               lambda s, c, pt: (s, N_PAGES // PAGES_PER_STEP, 0, 0)),
                  pl.BlockSpec((H_B, r8, 2 * PAGE_SIZE), lambda s, c, pt: (0, 0, 0))]
                 + [pl.BlockSpec(page_block, m) for m in page_map]
                 + [pl.BlockSpec(page_block, m) for m in page_map],
        out_specs=pl.BlockSpec((1, r8, H_B * DH_B), qmap),
        scratch_shapes=[pltpu.VMEM((KV_B, G_B * r8, PAGE_SIZE), F32),
                        pltpu.VMEM((KV_B, G_B * r8, PAGE_SIZE), F32),
                        pltpu.VMEM((KV_B, G_B * r8, DH_B), F32)],
    )
    return pl.pallas_call(
        _dsa_sample_attn_body,
        grid_spec=grid_spec,
        out_shape=jax.ShapeDtypeStruct((n_seq, r8, H_B * DH_B), BF16),
        compiler_params=_cparams("arbitrary", "arbitrary"),
        name="dsa_sample_attend",
    )(page_table, qn, kn, vn, amask, amask, bt,
      *([k_pool] * PAGES_PER_STEP), *([v_pool] * PAGES_PER_STEP))


HEADS_PER_GROUP = H_C // G_C
GROUP_W = D_IN_C // G_C


def _softplus(x):
    return jnp.maximum(x, 0.0) + jnp.log1p(jnp.exp(-jnp.abs(x)))


def _split3(x):
    hi = x.astype(BF16)
    r = x - hi.astype(F32)
    mid = r.astype(BF16)
    lo = (r - mid.astype(F32)).astype(BF16)
    return hi, mid, lo


def _ssd_body(x_ref, xbc_ref, zlo_ref, zhi_ref, ng_ref, wdt_ref, wdtt_ref, dtb_ref, dtbt_ref,
              alog_ref, alogt_ref, dsk_ref, cn_ref, wo_ref, h0_ref, o_ref, hst_ref, *, qin, lv):
    c = pl.program_id(1)
    q = CHUNK_C
    rows = lax.broadcasted_iota(I32, (q, q), 0)
    cols = lax.broadcasted_iota(I32, (q, q), 1)
    tril = cols <= rows

    def pad_rows(a):
        if qin == q:
            return a
        return jnp.concatenate([a, jnp.zeros((q - qin, a.shape[1]), a.dtype)], axis=0)

    @pl.when(c == 0)
    def _():
        hst_ref[...] = h0_ref[...]

    x = x_ref[0]
    xn = _rms(pad_rows(x), ng_ref[...]).astype(BF16)
    dt_all = _softplus(_dot(xn, wdt_ref[...]) + dtb_ref[...])
    dtt_all = _softplus(_dot_t(wdtt_ref[...], xn) + dtbt_ref[...])
    if lv < q:
        dt_all = jnp.where(lax.broadcasted_iota(I32, dt_all.shape, 0) < lv, dt_all, 0.0)
        dtt_all = jnp.where(lax.broadcasted_iota(I32, dtt_all.shape, 1) < lv, dtt_all, 0.0)
    acs_all = _tree([_dot(tril.astype(BF16), p) for p in _split3(dt_all * (-jnp.exp(alog_ref[...])))], jnp.add)
    acst_all = _tree([_dot(p, (rows <= cols).astype(BF16))
                      for p in _split3(dtt_all * (-jnp.exp(alogt_ref[...])))], jnp.add)
    z_all = pad_rows(jnp.concatenate([zlo_ref[0], zhi_ref[0]], axis=1))
    y_groups = []
    for g in range(G_C):
        y_groups.append(_ssd_group(g, xbc_ref, hst_ref, dt_all, dtt_all, acs_all, acst_all, z_all,
                                   dsk_ref, cn_ref, tril, pad_rows))
    y = jnp.concatenate(y_groups, axis=1)
    o_ref[0] = x + _dot(y, wo_ref[...])[0:qin]


def _ssd_group(g, xbc_ref, hst_ref, dt_all, dtt_all, acs_all, acst_all, z_all, dsk_ref, cn_ref, tril, pad_rows):
    q = CHUNK_C
    hpg = HEADS_PER_GROUP
    gheads = slice(g * hpg, (g + 1) * hpg)
    gcols = slice(g * GROUP_W, (g + 1) * GROUP_W)
    dt = dt_all[:, gheads]
    dtt = dtt_all[gheads, :]
    acs = acs_all[:, gheads]
    acst = acst_all[gheads, :]
    xs = pad_rows(xbc_ref[0, :, gcols])
    bm = pad_rows(xbc_ref[0, :, D_IN_C + g * N_SSM:D_IN_C + (g + 1) * N_SSM])
    cm = pad_rows(xbc_ref[0, :, D_IN_C + (G_C + g) * N_SSM:D_IN_C + (G_C + g + 1) * N_SSM])
    xst = xs.T
    cmb = cm.astype(BF16)
    cb = _dot_t(cmb, bm.astype(BF16))

    def spread(v, width):
        src = lax.broadcasted_iota(I32, (hpg, hpg * width), 0)
        dst = lax.shift_right_logical(lax.broadcasted_iota(I32, (hpg, hpg * width), 1), int(math.log2(width)))
        sel = (src == dst).astype(BF16)
        return _tree([_dot(p, sel) for p in _split3(v)], jnp.add)

    acs_w = spread(acs, q)
    decay_w = jnp.exp(acs_w[q - 1:q, :] - acs_w)
    eacs_w = jnp.exp(acs_w)
    xdt = (xs * spread(dt, P_C)).astype(BF16)
    low_half = lax.broadcasted_iota(I32, (q, 2 * P_C), 1) < P_C
    h_all = hst_ref[0, gheads]
    ys, h_new = [], []
    for k in range(hpg // 2):
        xpair = xdt[:, k * 2 * P_C:(k + 1) * 2 * P_C]
        yd = []
        for j in (2 * k, 2 * k + 1):
            blk = slice(j * q, (j + 1) * q)
            acs_row = acst[j:j + 1, :]
            a_last = acs_row[:, q - 1:q]
            lmat = jnp.exp(jnp.where(tril, acs_w[:, blk] - acs_row, -jnp.inf))
            yd.append(_dot((cb * lmat).astype(BF16), xpair))
            xht = xst[j * P_C:(j + 1) * P_C, :] * dtt[j:j + 1, :]
            st = _dot(xht.astype(BF16), (bm * decay_w[:, blk]).astype(BF16))
            h_new.append(jnp.exp(a_last) * h_all[j] + st)
        h_pair = h_all[2 * k:2 * k + 2].reshape(2 * P_C, N_SSM).astype(BF16)
        e_pair = jnp.where(low_half, eacs_w[:, 2 * k * q:(2 * k + 1) * q], eacs_w[:, (2 * k + 1) * q:(2 * k + 2) * q])
        ys.append(jnp.where(low_half, yd[0], yd[1]) + _dot_t(cmb, h_pair) * e_pair)
    hst_ref[0, gheads] = jnp.stack(h_new, axis=0)
    y = jnp.concatenate(ys, axis=1) + xs * dsk_ref[:, gcols]
    y = y * _silu(z_all[:, gcols])
    return _rms(y, cn_ref[:, gcols]).astype(BF16)


def ssd_mix(x, xbc, proj, z_col0, h0, norm_g, w_dt, dt_bias, a_log, d_skip, c_norm, w_out, qin, lv):
    n_seq, length, d = x.shape
    nchunk = length // qin
    col = lambda a: a.reshape(-1, 1).astype(F32)
    row = lambda a: a.reshape(1, -1).astype(F32)
    full2 = lambda s, c: (0, 0)
    conv_c = xbc.shape[-1]
    zw = D_IN_C // 2
    assert z_col0 % zw == 0
    in_specs = [
        pl.BlockSpec((1, qin, d), lambda s, c: (s, c, 0)),
        pl.BlockSpec((1, qin, conv_c), lambda s, c: (s, c, 0)),
        pl.BlockSpec((1, qin, zw), lambda s, c: (s, c, z_col0 // zw)),
        pl.BlockSpec((1, qin, zw), lambda s, c: (s, c, z_col0 // zw + 1)),
        pl.BlockSpec((1, d), full2),
        pl.BlockSpec((d, H_C), full2),
        pl.BlockSpec((H_C, d), full2),
        pl.BlockSpec((1, H_C), full2),
        pl.BlockSpec((H_C, 1), full2),
        pl.BlockSpec((1, H_C), full2),
        pl.BlockSpec((H_C, 1), full2),
        pl.BlockSpec((1, D_IN_C), full2),
        pl.BlockSpec((1, D_IN_C), full2),
        pl.BlockSpec((D_IN_C, d), full2),
        pl.BlockSpec((1, H_C, P_C, N_SSM), lambda s, c: (s, 0, 0, 0)),
    ]
    out, hst = pl.pallas_call(
        functools.partial(_ssd_body, qin=qin, lv=lv),
        grid=(n_seq, nchunk),
        in_specs=in_specs,
        out_specs=[pl.BlockSpec((1, qin, d), lambda s, c: (s, c, 0)),
                   pl.BlockSpec((1, H_C, P_C, N_SSM), lambda s, c: (s, 0, 0, 0))],
        out_shape=[jax.ShapeDtypeStruct((n_seq, length, d), F32),
                   jax.ShapeDtypeStruct((n_seq, H_C, P_C, N_SSM), F32)],
        compiler_params=_cparams("arbitrary", "arbitrary"),
        name="ssd_mix",
    )(x, xbc, proj, proj, row(norm_g), w_dt.astype(BF16), w_dt.T.astype(BF16),
      row(dt_bias), col(dt_bias), row(a_log), col(a_log),
      row(jnp.repeat(d_skip, P_C)), row(c_norm), w_out, h0)
    return out, hst


def _flat(a):
    return a.reshape(-1, a.shape[-1])


def _unflat(a, like):
    return a.reshape(like.shape[0], like.shape[1], a.shape[-1])


def kernel(x_prompt, x_sample, mem_prompt, state_a_conv, cache_b_k, cache_b_v, cache_b_kidx, state_c_conv, state_c_ssm, state_d_conv, state_ffn_conv, cache_mem_k, cache_mem_v, page_table, rel_bias, norm_mix, norm_mem, norm_ffn, norm_memtok, a_w_in, a_b_in, a_w_conv, a_b_conv, a_ln_g, a_ln_b, a_w_out, b_w_in, b_w_out, b_q_norm, b_k_norm, b_kidx_norm, c_w_in, c_w_conv, c_b_conv, c_dt_bias, c_a_log, c_d_skip, c_norm, c_w_out, d_w_in, d_w_conv, d_w_out, m_w_q, m_w_kv, m_w_o, m_q_norm, m_k_norm, f_w_in, f_w_conv, f_b_conv, f_w_out):
    n_p, l_p, d = x_prompt.shape
    n_s, l_s, _ = x_sample.shape
    bf = lambda w: w.astype(BF16)
    xs_pad = jnp.pad(x_sample, ((0, 0), (0, SAMPLE_PAD - l_s), (0, 0)))
    groups = [dict(x=x_prompt, sb=1, tm=512, tmf=512, lv=512, qin=CHUNK_C, lvq=CHUNK_C, prompt=True),
              dict(x=xs_pad, sb=n_s, tm=SAMPLE_PAD, tmf=SAMPLE_PAD, lv=l_s, qin=SAMPLE_PAD, lvq=l_s, prompt=False)]
    bt, btt = t5_tiles(rel_bias)
    mem_flat = _flat(mem_prompt)
    outs = {k: [[], []] for k in ("a", "bk", "bv", "bki", "cc", "cs", "d", "f")}
    m_kp, m_vp = [], []

    for i in range(DEPTH):
        j = i // 4
        kind = i % 4
        kv = linear(mem_flat, bf(m_w_kv[i]), g=norm_memtok[i], name="mem_kv")
        mk, _ = head_norm(kv, 0, d, m_k_norm[i], DH_M, name="mem_k_norm")
        m_kp.append(mk.reshape(n_p, N_MEM, H_M, DH_M))
        m_vp.append(kv[:, d:].reshape(n_p, N_MEM, H_M, DH_M))
        mk = m_kp[-1].transpose(0, 2, 1, 3)
        mv = m_vp[-1].transpose(0, 2, 1, 3)

        for gi, grp in enumerate(groups):
            x = grp["x"]
            n_seq, length, _ = x.shape
            prompt = grp["prompt"]
            zeros_state = lambda w, c: jnp.zeros((n_seq, w - 1, c), F32)
            xf = _flat(x)
            if kind == 0:
                st = zeros_state(W_A, d) if prompt else state_a_conv[j]
                x, nst = conv_mixer(x, st, norm_mix[i], bf(a_w_in[j]), a_b_in[j], a_w_conv[j],
                                    [a_b_conv[j], a_ln_g[j], a_ln_b[j]], bf(a_w_out[j]), _a_pre, _a_post,
                                    grp["sb"], grp["tm"], grp["lv"], "a_mixer")
                outs["a"][gi].append(nst)
            elif kind == 1:
                qn, kn, knb, v, vb, qi, kin, kinb, wi = b_proj(xf, norm_mix[i], b_w_in[j], b_q_norm[j],
                                                               b_k_norm[j], b_kidx_norm[j])
                kw = KV_B * DH_B
                r3 = lambda a: a.reshape(n_seq, length, a.shape[-1])
                if prompt:
                    vtb = vb.reshape(n_seq, length // PAGE_SIZE, PAGE_SIZE, kw).transpose(0, 1, 3, 2)
                    o = dsa_prompt(r3(qn), r3(qi), r3(wi), r3(knb), vtb, r3(kinb), btt)
                else:
                    am = dsa_sample_select(page_table, r3(qi), r3(wi), r3(kin),
                                           cache_b_kidx[j].transpose(0, 2, 1), grp["lv"])
                    o = dsa_sample_attend(page_table, r3(qn), r3(kn), r3(v), am, bt,
                                          cache_b_k[j].transpose(0, 2, 3, 1), cache_b_v[j].transpose(0, 2, 3, 1))
                outs["bk"][gi].append(r3(kn))
                outs["bv"][gi].append(r3(v))
                outs["bki"][gi].append(r3(kin))
                x = _unflat(linear(_flat(o), bf(b_w_out[j]), res=xf, name="b_out"), x)
            elif kind == 2:
                w_in = c_w_in[j]
                conv_c = D_IN_C + 2 * G_C * N_SSM
                st = zeros_state(W_C, conv_c) if prompt else state_c_conv[j]
                xbc, z, nst = c_in_conv(x, st, norm_mix[i], bf(w_in[:, D_IN_C:D_IN_C + conv_c]), bf(w_in[:, :D_IN_C]),
                                        c_w_conv[j], c_b_conv[j], grp["sb"], grp["tm"], grp["lv"])
                h0 = jnp.zeros((n_seq, H_C, P_C, N_SSM), F32) if prompt else state_c_ssm[j]
                x, hst = ssd_mix(x, xbc, z, 0, h0, norm_mix[i], w_in[:, D_IN_C + conv_c:],
                                 c_dt_bias[j], c_a_log[j], c_d_skip[j], c_norm[j], bf(c_w_out[j]),
                                 grp["qin"], grp["lvq"])
                outs["cc"][gi].append(nst)
                outs["cs"][gi].append(hst)
            else:
                st = zeros_state(W_D, d) if prompt else state_d_conv[j]
                x, nst = conv_mixer(x, st, norm_mix[i], bf(d_w_in[j]), None, d_w_conv[j], [], bf(d_w_out[j]),
                                    _d_pre, _d_post, grp["sb"], grp["tm"], grp["lv"], "d_mixer")
                outs["d"][gi].append(nst)

            if prompt:
                x = mem_fused(x, mk, mv, norm_mem[i], bf(m_w_q[i]), m_q_norm[i], bf(m_w_o[i]), 512)
            else:
                xf = _flat(x)
                qm = _unflat(linear(xf, bf(m_w_q[i]), g=norm_mem[i], name="mem_q"), x)
                om = mem_attn_cached(qm, cache_mem_k.reshape(DEPTH * n_seq, N_MEM, H_M, DH_M),
                                     cache_mem_v.reshape(DEPTH * n_seq, N_MEM, H_M, DH_M),
                                     m_q_norm[i], MEM_SEQS_PER_STEP, i * n_seq)
                x = _unflat(linear(_flat(om), bf(m_w_o[i]), res=xf, name="mem_o"), x)

            st = zeros_state(W_F, D_FF) if prompt else state_ffn_conv[i]
            x, nst = ffn_fused(x, st, norm_ffn[i], bf(f_w_in[i]), f_w_conv[i], f_b_conv[i], bf(f_w_out[i]),
                               1 if prompt else n_seq, grp["tmf"], grp["tmf"] if prompt else grp["lv"])
            outs["f"][gi].append(nst)
            grp["x"] = x

    yp = groups[0]["x"]
    ys = groups[1]["x"][:, :l_s]
    st = lambda key, gi: jnp.stack(outs[key][gi])
    kvshape = lambda a, n, l: a.reshape(a.shape[0], n, -1, a.shape[-1])[:, :, :l]
    b_k_p = kvshape(st("bk", 0), n_p, l_p).reshape(-1, n_p, l_p, KV_B, DH_B)
    b_v_p = kvshape(st("bv", 0), n_p, l_p).reshape(-1, n_p, l_p, KV_B, DH_B)
    b_ki_p = kvshape(st("bki", 0), n_p, l_p)
    b_k_s = kvshape(st("bk", 1), n_s, l_s).reshape(-1, n_s, l_s, KV_B, DH_B)
    b_v_s = kvshape(st("bv", 1), n_s, l_s).reshape(-1, n_s, l_s, KV_B, DH_B)
    b_ki_s = kvshape(st("bki", 1), n_s, l_s)
    return (yp, ys, st("a", 0), st("a", 1), b_k_p, b_v_p, b_ki_p, b_k_s, b_v_s, b_ki_s,
            st("cc", 0), st("cs", 0), st("cc", 1), st("cs", 1), st("d", 0), st("d", 1),
            st("f", 0), st("f", 1), jnp.stack(m_kp), jnp.stack(m_vp))
```
